```python
import math
import jax, jax.numpy as jnp
from jax import lax
import numpy as np

D_MODEL = 1024
BATCH = 2
SEQ = 8192
DEPTH = 2
DEC_BATCH = 32
DEC_SEQ = 8
PAST_LEN = 8192
PAGE_SIZE = 128

N_MIXERS = 2
N_ATTN_LAYERS = (DEPTH + 1) // 2
N_SSM_LAYERS = DEPTH // 2
HEAD_DIM = 64
HEADS_PER_GROUP = 4
DILATED_GROUPS = ((128, 1), (512, 4), (2048, 16))
N_DIL_GROUPS = len(DILATED_GROUPS)
N_HEADS = N_DIL_GROUPS * HEADS_PER_GROUP
ATTN_WIDTH = N_HEADS * HEAD_DIM
BAND_BLOCK = 128
SSM_CH = 16
SSM_GROUPS = D_MODEL // SSM_CH
SSM_STATE = 64
SSM_CHUNK = 128
DT_MIN = 1e-3
DT_MAX = 1e-1
D_FF = 2816
NORM_EPS = 1e-6

kernel_name = 'hybrid_dilated_attn_s5_macaron_step'


def rms_norm(x, g):
    x32 = x.astype(jnp.float32)
    y = x32 * lax.rsqrt(jnp.mean(x32 * x32, axis=-1, keepdims=True) + NORM_EPS)
    return (y * g.astype(jnp.float32)).astype(x.dtype)


def swiglu(x, w_gate, w_up, w_down):
    return (jax.nn.silu(x @ w_gate) * (x @ w_up)) @ w_down


def alibi_slopes():
    h = jnp.arange(1, N_HEADS + 1, dtype=jnp.float32)
    return (2.0 ** (-8.0 * h / N_HEADS)).reshape(N_DIL_GROUPS, HEADS_PER_GROUP)


def dilated_attn_prompt(q, k, v, window, dil, slopes):
    bt, s_len, nh, dh = q.shape
    sub = s_len // dil
    nw = window // dil
    nb = -(-sub // BAND_BLOCK)
    sub_p = nb * BAND_BLOCK

    def to_sub(t, front):
        t = t.astype(jnp.float32).reshape(bt, sub, dil, nh, dh).transpose(0, 2, 1, 3, 4).reshape(bt * dil, sub, nh, dh)
        return jnp.pad(t, ((0, 0), (front, sub_p - sub), (0, 0), (0, 0)))

    qb = to_sub(q, 0).reshape(bt * dil, nb, BAND_BLOCK, nh, dh)
    kb = to_sub(k, BAND_BLOCK).reshape(bt * dil, nb + 1, BAND_BLOCK, nh, dh)
    vb = to_sub(v, BAND_BLOCK).reshape(bt * dil, nb + 1, BAND_BLOCK, nh, dh)
    kc = jnp.concatenate([kb[:, :-1], kb[:, 1:]], axis=2)
    vc = jnp.concatenate([vb[:, :-1], vb[:, 1:]], axis=2)
    s = jnp.einsum('bnqhd,bnkhd->bnhqk', qb, kc) * (HEAD_DIM ** -0.5)
    delta = (jnp.arange(BAND_BLOCK)[:, None] + BAND_BLOCK) - jnp.arange(2 * BAND_BLOCK)[None, :]
    key_sub = jnp.arange(nb)[:, None] * BAND_BLOCK + jnp.arange(2 * BAND_BLOCK)[None, :] - BAND_BLOCK
    valid = ((delta >= 0) & (delta <= nw))[None] & (key_sub >= 0)[:, None, :]
    bias = -slopes[:, None, None] * (delta * dil).astype(jnp.float32)[None]
    s = jnp.where(valid[None, :, None], s + bias, -jnp.inf)
    m = jnp.max(s, axis=-1, keepdims=True)
    p = jnp.exp(s - m)
    den = jnp.sum(p, axis=-1)
    o = jnp.einsum('bnhqk,bnkhd->bnqhd', p, vc) / jnp.swapaxes(den, 2, 3)[..., None]
    lse = jnp.swapaxes(m[..., 0] + jnp.log(den), 2, 3)

    def from_sub(t):
        t = t.reshape((bt * dil, sub_p) + t.shape[3:])[:, :sub]
        t = jnp.swapaxes(t.reshape((bt, dil, sub) + t.shape[2:]), 1, 2)
        return t.reshape((bt, s_len) + t.shape[3:])

    return from_sub(o), from_sub(lse)


def dilated_attn_sample(q, kv_buf, kv_new, window, dil, slopes):
    t_len = q.shape[1]
    wb = kv_buf.shape[1]
    kv_all = jnp.concatenate([kv_buf.astype(kv_new.dtype), kv_new], axis=1)
    steps = jnp.arange(window // dil + 1)
    idx = wb + jnp.arange(t_len)[:, None] - steps[None, :] * dil
    valid = idx >= 0
    kv_sel = jnp.take(kv_all, jnp.maximum(idx, 0), axis=1).astype(jnp.float32)
    s = jnp.einsum('bthd,btkhd->bthk', q.astype(jnp.float32), kv_sel[:, :, :, 0]) * (HEAD_DIM ** -0.5)
    s = s - slopes[:, None] * (steps * dil).astype(jnp.float32)[None, :]
    s = jnp.where(valid[None, :, None, :], s, -jnp.inf)
    m = jnp.max(s, axis=-1, keepdims=True)
    p = jnp.exp(s - m)
    den = jnp.sum(p, axis=-1)
    o = jnp.einsum('bthk,btkhd->bthd', p, kv_sel[:, :, :, 1]) / den[..., None]
    lse = m[..., 0] + jnp.log(den)
    return o, lse, kv_all[:, t_len:]


def dilated_mixer(h, w_qkv, w_o, bufs):
    bt, l_len, _ = h.shape
    qkv = (h @ w_qkv).reshape(bt, l_len, 3, N_DIL_GROUPS, HEADS_PER_GROUP, HEAD_DIM)
    slopes = alibi_slopes()
    outs, lses, new_bufs = [], [], []
    for g, (win, dil) in enumerate(DILATED_GROUPS):
        q, k, v = qkv[:, :, 0, g], qkv[:, :, 1, g], qkv[:, :, 2, g]
        kv = jnp.stack([k, v], axis=2)
        if bufs is None:
            o, lse = dilated_attn_prompt(q, k, v, win, dil, slopes[g])
            new_bufs.append(kv[:, l_len - min(win, l_len):])
        else:
            o, lse, nbuf = dilated_attn_sample(q, bufs[g], kv, win, dil, slopes[g])
            new_bufs.append(nbuf)
        outs.append(o)
        lses.append(lse)
    alpha = jax.nn.softmax(jnp.stack(lses, axis=0), axis=0)
    o = jnp.stack(outs, axis=0) * alpha[..., None]
    o = jnp.moveaxis(o, 0, 2).reshape(bt, l_len, ATTN_WIDTH).astype(h.dtype)
    return o @ w_o, new_bufs


def ssm_discretize(a_re, a_im, log_dt, b_re, b_im):
    a_re = a_re.astype(jnp.float32)
    a_im = a_im.astype(jnp.float32)
    b_re = b_re.astype(jnp.float32)
    b_im = b_im.astype(jnp.float32)
    dt = jnp.exp(log_dt.astype(jnp.float32))[:, None]
    mag = jnp.exp(a_re * dt)
    ang = a_im * dt
    lam_re = mag * jnp.cos(ang)
    lam_im = mag * jnp.sin(ang)
    den = a_re * a_re + a_im * a_im
    num_re = lam_re - 1.0
    coef_re = ((num_re * a_re + lam_im * a_im) / den)[..., None]
    coef_im = ((lam_im * a_re - num_re * a_im) / den)[..., None]
    bb_re = coef_re * b_re - coef_im * b_im
    bb_im = coef_re * b_im + coef_im * b_re
    return lam_re, lam_im, bb_re, bb_im


def complex_affine_combine(e1, e2):
    ar1, ai1, br1, bi1 = e1
    ar2, ai2, br2, bi2 = e2
    return (ar2 * ar1 - ai2 * ai1,
            ar2 * ai1 + ai2 * ar1,
            ar2 * br1 - ai2 * bi1 + br2,
            ar2 * bi1 + ai2 * br1 + bi2)


def ssm_scan(u, h0, lam_re, lam_im, bb_re, bb_im, c_re, c_im):
    bt, l_len = u.shape[:2]
    chunk = SSM_CHUNK if l_len % SSM_CHUNK == 0 else l_len
    nc = l_len // chunk
    u_blocks = jnp.moveaxis(u.reshape(bt, nc, chunk, SSM_GROUPS, SSM_CH), 1, 0)

    def step(carry, u_blk):
        h_re, h_im = carry
        bu_re = jnp.einsum('btgc,gnc->btgn', u_blk, bb_re)
        bu_im = jnp.einsum('btgc,gnc->btgn', u_blk, bb_im)
        a_re = jnp.broadcast_to(lam_re, bu_re.shape)
        a_im = jnp.broadcast_to(lam_im, bu_im.shape)
        p_re, p_im, x_re, x_im = lax.associative_scan(complex_affine_combine, (a_re, a_im, bu_re, bu_im), axis=1)
        x_re = x_re + p_re * h_re[:, None] - p_im * h_im[:, None]
        x_im = x_im + p_re * h_im[:, None] + p_im * h_re[:, None]
        y = jnp.einsum('gcn,btgn->btgc', c_re, x_re) - jnp.einsum('gcn,btgn->btgc', c_im, x_im)
        return (x_re[:, -1], x_im[:, -1]), y

    (h_re, h_im), ys = lax.scan(step, (h0[..., 0], h0[..., 1]), u_blocks)
    y = jnp.moveaxis(ys, 0, 1).reshape(bt, l_len, SSM_GROUPS, SSM_CH)
    return y, jnp.stack([h_re, h_im], axis=-1)


def ssm_mixer(h, state, w_in, a_re, a_im, log_dt, b_re, b_im, c_re, c_im, d_skip, w_glu, b_glu, w_out):
    bt, l_len, _ = h.shape
    u = (h @ w_in).astype(jnp.float32).reshape(bt, l_len, SSM_GROUPS, SSM_CH)
    lam_re, lam_im, bb_re, bb_im = ssm_discretize(a_re, a_im, log_dt, b_re, b_im)
    if state is None:
        h0 = jnp.zeros((bt, SSM_GROUPS, SSM_STATE, 2), jnp.float32)
    else:
        h0 = state.astype(jnp.float32)
    y, new_state = ssm_scan(u, h0, lam_re, lam_im, bb_re, bb_im, c_re.astype(jnp.float32), c_im.astype(jnp.float32))
    y = (y + d_skip.astype(jnp.float32) * u).reshape(bt, l_len, D_MODEL)
    g = jax.nn.gelu(y)
    z = g * jax.nn.sigmoid(g @ w_glu.astype(jnp.float32) + b_glu.astype(jnp.float32))
    return z.astype(h.dtype) @ w_out, new_state


def trunk(x, kv_caches, ssm_states, P):
    kv_out = [[] for _ in range(N_DIL_GROUPS)]
    ssm_out = []
    for i in range(DEPTH):
        h = x + 0.5 * swiglu(rms_norm(x, P['norm_g'][i, 0]), P['ffn_w_gate'][i, 0], P['ffn_w_up'][i, 0], P['ffn_w_down'][i, 0])
        hn = rms_norm(h, P['norm_g'][i, 1])
        j = i // N_MIXERS
        if i % N_MIXERS == 0:
            bufs = None if kv_caches is None else [c[j] for c in kv_caches]
            mix, nbufs = dilated_mixer(hn, P['attn_w_qkv'][j], P['attn_w_o'][j], bufs)
            for g in range(N_DIL_GROUPS):
                kv_out[g].append(nbufs[g])
        else:
            st = None if ssm_states is None else ssm_states[j]
            mix, nst = ssm_mixer(hn, st, P['ssm_w_in'][j], P['ssm_a_re'][j], P['ssm_a_im'][j], P['ssm_log_dt'][j],
                                 P['ssm_b_re'][j], P['ssm_b_im'][j], P['ssm_c_re'][j], P['ssm_c_im'][j], P['ssm_d'][j],
                                 P['ssm_w_glu'][j], P['ssm_b_glu'][j], P['ssm_w_out'][j])
            ssm_out.append(nst)
        h = h + mix
        x = h + 0.5 * swiglu(rms_norm(h, P['norm_g'][i, 2]), P['ffn_w_gate'][i, 1], P['ffn_w_up'][i, 1], P['ffn_w_down'][i, 1])
    y = rms_norm(x, P['final_norm_g'])
    return y, [jnp.stack(b, axis=0) for b in kv_out], jnp.stack(ssm_out, axis=0)


def setup_inputs(seed: int = 0) -> dict:
    key = jax.random.key(seed)
    ks = jax.random.split(key, 32)
    f32 = jnp.float32

    def nrm(k, shape, scale):
        return jax.random.normal(k, shape, f32) * scale

    wb = [min(w, PAST_LEN) for (w, _) in DILATED_GROUPS]
    kvs = (HEADS_PER_GROUP, HEAD_DIM)
    return {
        'x_prompt': nrm(ks[0], (BATCH, SEQ, D_MODEL), 1.0),
        'x_sample': nrm(ks[1], (DEC_BATCH, DEC_SEQ, D_MODEL), 1.0),
        'cache_kv_g0': nrm(ks[2], (N_ATTN_LAYERS, DEC_BATCH, wb[0], 2) + kvs, 1.0),
        'cache_kv_g1': nrm(ks[3], (N_ATTN_LAYERS, DEC_BATCH, wb[1], 2) + kvs, 1.0),
        'cache_kv_g2': nrm(ks[4], (N_ATTN_LAYERS, DEC_BATCH, wb[2], 2) + kvs, 1.0),
        'state_ssm': nrm(ks[5], (N_SSM_LAYERS, DEC_BATCH, SSM_GROUPS, SSM_STATE, 2), 0.1),
        'norm_g': 1.0 + nrm(ks[6], (DEPTH, 3, D_MODEL), 0.01),
        'final_norm_g': 1.0 + nrm(ks[7], (D_MODEL,), 0.01),
        'ffn_w_gate': nrm(ks[8], (DEPTH, 2, D_MODEL, D_FF), D_MODEL ** -0.5),
        'ffn_w_up': nrm(ks[9], (DEPTH, 2, D_MODEL, D_FF), D_MODEL ** -0.5),
        'ffn_w_down': nrm(ks[10], (DEPTH, 2, D_FF, D_MODEL), D_FF ** -0.5),
        'attn_w_qkv': nrm(ks[11], (N_ATTN_LAYERS, D_MODEL, 3 * ATTN_WIDTH), D_MODEL ** -0.5),
        'attn_w_o': nrm(ks[12], (N_ATTN_LAYERS, ATTN_WIDTH, D_MODEL), ATTN_WIDTH ** -0.5),
        'ssm_w_in': nrm(ks[13], (N_SSM_LAYERS, D_MODEL, D_MODEL), D_MODEL ** -0.5),
        'ssm_a_re': -0.5 + nrm(ks[14], (N_SSM_LAYERS, SSM_GROUPS, SSM_STATE), 0.01),
        'ssm_a_im': math.pi * jnp.arange(SSM_STATE, dtype=f32)[None, None, :] + nrm(ks[15], (N_SSM_LAYERS, SSM_GROUPS, SSM_STATE), 0.01),
        'ssm_log_dt': jax.random.uniform(ks[16], (N_SSM_LAYERS, SSM_GROUPS), f32, math.log(DT_MIN), math.log(DT_MAX)),
        'ssm_b_re': nrm(ks[17], (N_SSM_LAYERS, SSM_GROUPS, SSM_STATE, SSM_CH), (2 * SSM_CH) ** -0.5),
        'ssm_b_im': nrm(ks[18], (N_SSM_LAYERS, SSM_GROUPS, SSM_STATE, SSM_CH), (2 * SSM_CH) ** -0.5),
        'ssm_c_re': nrm(ks[19], (N_SSM_LAYERS, SSM_GROUPS, SSM_CH, SSM_STATE), (0.5 * SSM_STATE) ** -0.5),
        'ssm_c_im': nrm(ks[20], (N_SSM_LAYERS, SSM_GROUPS, SSM_CH, SSM_STATE), (0.5 * SSM_STATE) ** -0.5),
        'ssm_d': nrm(ks[21], (N_SSM_LAYERS, SSM_GROUPS, SSM_CH), 1.0),
        'ssm_w_glu': nrm(ks[22], (N_SSM_LAYERS, D_MODEL, D_MODEL), D_MODEL ** -0.5),
        'ssm_b_glu': nrm(ks[23], (N_SSM_LAYERS, D_MODEL), 0.01),
        'ssm_w_out': nrm(ks[24], (N_SSM_LAYERS, D_MODEL, D_MODEL), D_MODEL ** -0.5),
    }


def reference(x_prompt, x_sample, cache_kv_g0, cache_kv_g1, cache_kv_g2, state_ssm, norm_g, final_norm_g,
              ffn_w_gate, ffn_w_up, ffn_w_down, attn_w_qkv, attn_w_o, ssm_w_in, ssm_a_re, ssm_a_im, ssm_log_dt,
              ssm_b_re, ssm_b_im, ssm_c_re, ssm_c_im, ssm_d, ssm_w_glu, ssm_b_glu, ssm_w_out):
    params = dict(norm_g=norm_g, final_norm_g=final_norm_g, ffn_w_gate=ffn_w_gate, ffn_w_up=ffn_w_up,
                  ffn_w_down=ffn_w_down, attn_w_qkv=attn_w_qkv, attn_w_o=attn_w_o, ssm_w_in=ssm_w_in,
                  ssm_a_re=ssm_a_re, ssm_a_im=ssm_a_im, ssm_log_dt=ssm_log_dt, ssm_b_re=ssm_b_re, ssm_b_im=ssm_b_im,
                  ssm_c_re=ssm_c_re, ssm_c_im=ssm_c_im, ssm_d=ssm_d, ssm_w_glu=ssm_w_glu, ssm_b_glu=ssm_b_glu,
                  ssm_w_out=ssm_w_out)
    y_prompt, kv_p, ssm_prompt = trunk(x_prompt, None, None, params)
    y_sample, kv_s, ssm_sample = trunk(x_sample, (cache_kv_g0, cache_kv_g1, cache_kv_g2), state_ssm, params)
    kv_g0_prompt, kv_g1_prompt, kv_g2_prompt = kv_p
    kv_g0_sample, kv_g1_sample, kv_g2_sample = kv_s
    return (y_prompt, y_sample, kv_g0_prompt, kv_g1_prompt, kv_g2_prompt, ssm_prompt, kv_g0_sample, kv_g1_sample, kv_g2_sample, ssm_sample)
```

```python
import functools
import math

import jax
import jax.numpy as jnp
from jax import lax
from jax.experimental import pallas as pl
from jax.experimental.pallas import tpu as pltpu

F32 = jnp.float32
BF16 = jnp.bfloat16

D_MODEL = 1024
D_FF = 2816
HEAD_DIM = 64
HEADS_PER_GROUP = 4
DILATED_GROUPS = ((128, 1), (512, 4), (2048, 16))
N_GROUPS = len(DILATED_GROUPS)
N_HEADS = N_GROUPS * HEADS_PER_GROUP
GROUP_WIDTH = HEADS_PER_GROUP * HEAD_DIM
ATTN_WIDTH = N_HEADS * HEAD_DIM
BAND = 128
SSM_CH = 16
SSM_GROUPS = D_MODEL // SSM_CH
SSM_STATE = 64
NORM_EPS = 1e-6

LANES = 128
V7X_VMEM_BYTES = 64 * 1024 * 1024
VMEM_LIMIT = V7X_VMEM_BYTES - 8 * 1024 * 1024

SSM_LANE_BLOCKS = D_MODEL // LANES
GROUPS_PER_BLOCK = LANES // SSM_CH
STATE_LANES = GROUPS_PER_BLOCK * SSM_STATE

assert all(w // d == BAND for w, d in DILATED_GROUPS)
assert all(d & (d - 1) == 0 for _, d in DILATED_GROUPS)


def _params(sem, vmem=VMEM_LIMIT):
    return pltpu.CompilerParams(dimension_semantics=sem, vmem_limit_bytes=vmem)


def _const_spec(shape):
    nd = len(shape)
    return pl.BlockSpec(shape, lambda *_: (0,) * nd, pipeline_mode=pl.Buffered(1))


def _rms(x, g):
    return x * lax.rsqrt(jnp.mean(x * x, axis=-1, keepdims=True) + NORM_EPS) * g


FFN_CHUNKS = 2


def _ffn_kernel(x_ref, g_ref, wg_ref, wu_ref, wd_ref, gf_ref, o_ref, *, final_norm):
    x = x_ref[...]
    xb = _rms(x, g_ref[...]).astype(BF16)
    fc = D_FF // FFN_CHUNKS
    acc = None
    for c in range(FFN_CHUNKS):
        gate = jnp.dot(xb, wg_ref[:, c * fc:(c + 1) * fc], preferred_element_type=F32)
        up = jnp.dot(xb, wu_ref[:, c * fc:(c + 1) * fc], preferred_element_type=F32)
        hmid = (gate * jax.nn.sigmoid(gate) * up).astype(BF16)
        part = jnp.dot(hmid, wd_ref[c * fc:(c + 1) * fc, :], preferred_element_type=F32)
        acc = part if acc is None else acc + part
    out = x + 0.5 * acc
    if final_norm:
        out = _rms(out, gf_ref[...])
    o_ref[...] = out


def _ffn(x, g, wg, wu, wd, gf, *, tm, final_norm):
    rows = x.shape[0]
    return pl.pallas_call(
        functools.partial(_ffn_kernel, final_norm=final_norm),
        grid=(rows // tm,),
        in_specs=[
            pl.BlockSpec((tm, D_MODEL), lambda i: (i, 0)),
            _const_spec((1, D_MODEL)),
            _const_spec((D_MODEL, D_FF)),
            _const_spec((D_MODEL, D_FF)),
            _const_spec((D_FF, D_MODEL)),
            _const_spec((1, D_MODEL)),
        ],
        out_specs=pl.BlockSpec((tm, D_MODEL), lambda i: (i, 0)),
        out_shape=jax.ShapeDtypeStruct((rows, D_MODEL), F32),
        compiler_params=_params(("parallel",)),
        name="ffn",
    )(x, g, wg, wu, wd, gf)


def _proj_kernel(x_ref, g_ref, w_ref, o_ref, *, slabs):
    xb = _rms(x_ref[...], g_ref[...]).astype(BF16)
    y = jnp.dot(xb, w_ref[...], preferred_element_type=F32)
    if slabs:
        for k in range(y.shape[1] // LANES):
            o_ref[k] = y[:, k * LANES:(k + 1) * LANES]
    else:
        o_ref[...] = y


def _proj(x, g, w, *, tm, slabs):
    rows, n = x.shape[0], w.shape[1]
    if slabs:
        out_spec = pl.BlockSpec((n // LANES, tm, LANES), lambda i: (0, i, 0))
        out_shape = jax.ShapeDtypeStruct((n // LANES, rows, LANES), F32)
    else:
        out_spec = pl.BlockSpec((tm, n), lambda i: (i, 0))
        out_shape = jax.ShapeDtypeStruct((rows, n), F32)
    return pl.pallas_call(
        functools.partial(_proj_kernel, slabs=slabs),
        grid=(rows // tm,),
        in_specs=[
            pl.BlockSpec((tm, D_MODEL), lambda i: (i, 0)),
            _const_spec((1, D_MODEL)),
            _const_spec(w.shape),
        ],
        out_specs=out_spec,
        out_shape=out_shape,
        compiler_params=_params(("parallel",)),
        name="norm_proj",
    )(x, g, w)


def _alibi_slope(head):
    return 2.0 ** (-8.0 * (head + 1) / N_HEADS)


def _head_lane_mask(hh):
    lane = lax.broadcasted_iota(jnp.int32, (1, LANES), 1)
    return (lane >= hh * HEAD_DIM) & (lane < (hh + 1) * HEAD_DIM)


def _attn_kernel(q_ref, kp_ref, kc_ref, vp_ref, vc_ref, o_ref, l_ref, *, group, tiles_per_seq):
    dil = DILATED_GROUPS[group][1]
    i = pl.program_id(0)
    c = pl.program_id(1)
    has_prev = (i % tiles_per_seq) != 0

    qi = lax.broadcasted_iota(jnp.int32, (BAND, 2 * BAND), 0)
    kj = lax.broadcasted_iota(jnp.int32, (BAND, 2 * BAND), 1)
    delta = BAND + qi - kj
    valid = (delta >= 0) & (delta <= BAND) & ((kj >= BAND) | has_prev)
    dist = (delta * dil).astype(F32)

    def rows(ref, r):
        return ref[...] if dil == 1 else ref[pl.ds(r, BAND, stride=dil), :]

    def one_residue(r):
        q = rows(q_ref, r) * (HEAD_DIM ** -0.5)
        k = jnp.concatenate([rows(kp_ref, r), rows(kc_ref, r)], axis=0).astype(BF16)
        v = jnp.concatenate([rows(vp_ref, r), rows(vc_ref, r)], axis=0).astype(BF16)
        o_heads, l_heads = [], []
        for hh in range(2):
            head0 = group * HEADS_PER_GROUP + hh
            slope = jnp.where(c == 0, _alibi_slope(head0), _alibi_slope(head0 + 2)).astype(F32)
            qm = jnp.where(_head_lane_mask(hh), q, 0.0).astype(BF16)
            s = lax.dot_general(qm, k, (((1,), (1,)), ((), ())), preferred_element_type=F32)
            s = jnp.where(valid, s - slope * dist, -jnp.inf)
            m = jnp.max(s, axis=-1, keepdims=True)
            p = jnp.exp(s - m)
            den = jnp.sum(p, axis=-1, keepdims=True)
            o_heads.append(jnp.dot(p.astype(BF16), v, preferred_element_type=F32) / den)
            l_heads.append(m + jnp.log(den))
        first = _head_lane_mask(0)
        o = jnp.where(first, o_heads[0], o_heads[1])
        lse = jnp.where(first, l_heads[0], l_heads[1])
        if dil == 1:
            o_ref[...] = o
            l_ref[...] = lse
        else:
            o_ref[pl.ds(r, BAND, stride=dil), :] = o
            l_ref[pl.ds(r, BAND, stride=dil), :] = lse

    for r in range(dil):
        one_residue(r)


def _attn_prompt(qkv, *, group, seq):
    rows = qkv.shape[0]
    dil = DILATED_GROUPS[group][1]
    tq = BAND * dil
    qcol = group * GROUP_WIDTH // LANES
    kcol = (ATTN_WIDTH + group * GROUP_WIDTH) // LANES
    vcol = (2 * ATTN_WIDTH + group * GROUP_WIDTH) // LANES
    cur = lambda col: pl.BlockSpec((tq, LANES), lambda i, c: (i, col + c))
    prev = lambda col: pl.BlockSpec((tq, LANES), lambda i, c: (jnp.maximum(i - 1, 0), col + c))
    out_spec = pl.BlockSpec((tq, LANES), lambda i, c: (i, c))
    out_shape = jax.ShapeDtypeStruct((rows, GROUP_WIDTH), F32)
    return pl.pallas_call(
        functools.partial(_attn_kernel, group=group, tiles_per_seq=seq // tq),
        grid=(rows // tq, GROUP_WIDTH // LANES),
        in_specs=[cur(qcol), prev(kcol), cur(kcol), prev(vcol), cur(vcol)],
        out_specs=[out_spec, out_spec],
        out_shape=[out_shape, out_shape],
        compiler_params=_params(("parallel", "parallel")),
        name=f"attn_prompt_g{group}",
    )(qkv, qkv, qkv, qkv, qkv)


def _attn_sample_kernel(qkv_ref, c0_ref, c1_ref, c2_ref,
                        o0_ref, l0_ref, o1_ref, l1_ref, o2_ref, l2_ref, n0_ref, n1_ref, n2_ref, *, t_len):
    cache_refs = (c0_ref, c1_ref, c2_ref)
    o_refs = (o0_ref, o1_ref, o2_ref)
    l_refs = (l0_ref, l1_ref, l2_ref)
    n_refs = (n0_ref, n1_ref, n2_ref)
    for g, (win, dil) in enumerate(DILATED_GROUPS):
        cache_ref = cache_refs[g]
        wb = cache_ref.shape[0]
        q0 = g * GROUP_WIDTH
        k0 = ATTN_WIDTH + g * GROUP_WIDTH
        v0 = 2 * ATTN_WIDTH + g * GROUP_WIDTH
        k_new = qkv_ref[:, k0:k0 + GROUP_WIDTH]
        v_new = qkv_ref[:, v0:v0 + GROUP_WIDTH]

        t1 = lax.broadcasted_iota(jnp.int32, (t_len, wb), 0)
        r1 = lax.broadcasted_iota(jnp.int32, (t_len, wb), 1)
        dist1 = wb + t1 - r1
        valid1 = (dist1 <= win) & ((dist1 & (dil - 1)) == 0)
        t2 = lax.broadcasted_iota(jnp.int32, (t_len, t_len), 0)
        r2 = lax.broadcasted_iota(jnp.int32, (t_len, t_len), 1)
        dist2 = t2 - r2
        valid2 = (dist2 >= 0) & ((dist2 & (dil - 1)) == 0)

        o_blocks, l_blocks = [], []
        for c in range(GROUP_WIDTH // LANES):
            q = qkv_ref[:, q0 + c * LANES:q0 + (c + 1) * LANES] * (HEAD_DIM ** -0.5)
            kc = cache_ref[:, c * LANES:(c + 1) * LANES].astype(BF16)
            vc = cache_ref[:, GROUP_WIDTH + c * LANES:GROUP_WIDTH + (c + 1) * LANES].astype(BF16)
            kn = k_new[:, c * LANES:(c + 1) * LANES].astype(BF16)
            vn = v_new[:, c * LANES:(c + 1) * LANES].astype(BF16)
            o_heads, l_heads = [], []
            for hh in range(2):
                slope = _alibi_slope(g * HEADS_PER_GROUP + 2 * c + hh)
                qm = jnp.where(_head_lane_mask(hh), q, 0.0).astype(BF16)
                nt = (((1,), (1,)), ((), ()))
                s1 = lax.dot_general(qm, kc, nt, preferred_element_type=F32)
                s2 = lax.dot_general(qm, kn, nt, preferred_element_type=F32)
                s1 = jnp.where(valid1, s1 - slope * dist1.astype(F32), -jnp.inf)
                s2 = jnp.where(valid2, s2 - slope * dist2.astype(F32), -jnp.inf)
                m = jnp.maximum(jnp.max(s1, axis=-1, keepdims=True), jnp.max(s2, axis=-1, keepdims=True))
                p1 = jnp.exp(s1 - m)
                p2 = jnp.exp(s2 - m)
                den = jnp.sum(p1, axis=-1, keepdims=True) + jnp.sum(p2, axis=-1, keepdims=True)
                o = (jnp.dot(p1.astype(BF16), vc, preferred_element_type=F32)
                     + jnp.dot(p2.astype(BF16), vn, preferred_element_type=F32)) / den
                o_heads.append(o)
                l_heads.append(m + jnp.log(den))
            first = _head_lane_mask(0)
            o_blocks.append(jnp.where(first, o_heads[0], o_heads[1]))
            l_blocks.append(jnp.where(first, l_heads[0], l_heads[1]))
        o_refs[g][...] = jnp.concatenate(o_blocks, axis=1)
        l_refs[g][...] = jnp.concatenate(l_blocks, axis=1)

        n_refs[g][0:wb - t_len, :] = cache_ref[t_len:wb, :]
        n_refs[g][wb - t_len:wb, :] = jnp.concatenate([k_new, v_new], axis=1)


def _attn_sample(qkv, caches, *, t_len):
    nseq = caches[0].shape[0]
    rows = qkv.shape[0]
    ol_spec = pl.BlockSpec((t_len, GROUP_WIDTH), lambda b: (b, 0))
    ol_shape = jax.ShapeDtypeStruct((rows, GROUP_WIDTH), F32)
    cache_specs = [pl.BlockSpec((None,) + c.shape[1:], lambda b: (b, 0, 0)) for c in caches]
    outs = pl.pallas_call(
        functools.partial(_attn_sample_kernel, t_len=t_len),
        grid=(nseq,),
        in_specs=[pl.BlockSpec((t_len, 3 * ATTN_WIDTH), lambda b: (b, 0))] + cache_specs,
        out_specs=[ol_spec] * 6 + cache_specs,
        out_shape=[ol_shape] * 6 + [jax.ShapeDtypeStruct(c.shape, F32) for c in caches],
        compiler_params=_params(("parallel",)),
        name="attn_sample",
    )(qkv, *caches)
    return outs[:6], outs[6:]


def _merge_kernel(o0_ref, l0_ref, o1_ref, l1_ref, o2_ref, l2_ref, h_ref, wo_ref, out_ref):
    lses = (l0_ref[...], l1_ref[...], l2_ref[...])
    outs = (o0_ref[...], o1_ref[...], o2_ref[...])
    m = jnp.maximum(jnp.maximum(lses[0], lses[1]), lses[2])
    es = [jnp.exp(l - m) for l in lses]
    tot = es[0] + es[1] + es[2]
    merged = jnp.concatenate([o * (e / tot) for o, e in zip(outs, es)], axis=1).astype(BF16)
    out_ref[...] = h_ref[...] + jnp.dot(merged, wo_ref[...], preferred_element_type=F32)


def _merge(ols, h, wo, *, tm):
    rows = h.shape[0]
    ol_spec = pl.BlockSpec((tm, GROUP_WIDTH), lambda i: (i, 0))
    row_spec = pl.BlockSpec((tm, D_MODEL), lambda i: (i, 0))
    return pl.pallas_call(
        _merge_kernel,
        grid=(rows // tm,),
        in_specs=[ol_spec] * 6 + [row_spec, _const_spec(wo.shape)],
        out_specs=row_spec,
        out_shape=jax.ShapeDtypeStruct((rows, D_MODEL), F32),
        compiler_params=_params(("parallel",)),
        name="attn_merge",
    )(*ols, h, wo)


LAM_EXPONENTS = (1, 2, 4, 8, 16)


def _ssm_prep_kernel(a_re_ref, a_im_ref, logdt_ref, bre_ref, bim_ref, pre_ref, pim_ref, bbre_ref, bbim_ref):
    a_re = a_re_ref[...]
    a_im = a_im_ref[...]
    dt = jnp.exp(logdt_ref[...])
    mag = jnp.exp(a_re * dt)
    ang = a_im * dt
    lam_re = mag * jnp.cos(ang)
    lam_im = mag * jnp.sin(ang)
    den = a_re * a_re + a_im * a_im
    num_re = lam_re - 1.0
    coef_re = ((num_re * a_re + lam_im * a_im) / den)[None]
    coef_im = ((lam_im * a_re - num_re * a_im) / den)[None]
    b_re = bre_ref[...]
    b_im = bim_ref[...]
    bbre_ref[...] = coef_re * b_re - coef_im * b_im
    bbim_ref[...] = coef_re * b_im + coef_im * b_re
    p_re, p_im = lam_re, lam_im
    for idx in range(len(LAM_EXPONENTS)):
        pre_ref[idx] = p_re
        pim_ref[idx] = p_im
        p_re, p_im = p_re * p_re - p_im * p_im, 2.0 * p_re * p_im


def _ssm_prep(a_re, a_im, log_dt, b_re_t, b_im_t):
    g, n = a_re.shape
    pw = jax.ShapeDtypeStruct((len(LAM_EXPONENTS), g, n), F32)
    bb = jax.ShapeDtypeStruct(b_re_t.shape, F32)
    return pl.pallas_call(
        _ssm_prep_kernel,
        out_shape=[pw, pw, bb, bb],
        name="ssm_prep",
    )(a_re, a_im, log_dt.reshape(g, 1), b_re_t, b_im_t)


def _ssm_scan_kernel(u_ref, bb_ref, cre_ref, cim_ref, lam_ref, lamc_ref, d_ref, h0_ref, y_ref, fs_ref,
                     bu_scr, xre_scr, xim_scr, ud_scr, e_scr, hp_scr, carry_scr, *, chunk, nchunk, chained):
    sl = STATE_LANES
    tile = pl.program_id(2)

    for j in range(chunk):
        ud_scr[j * nchunk:(j + 1) * nchunk, :] = u_ref[pl.ds(j, nchunk, stride=chunk), :].astype(BF16)
    bu_scr[...] = jnp.dot(ud_scr[...], bb_ref[...], preferred_element_type=F32)

    lam_re = jnp.broadcast_to(lam_ref[0:1, :], (nchunk, sl))
    lam_im = jnp.broadcast_to(lam_ref[1:2, :], (nchunk, sl))

    def step(x_re, x_im, j):
        b_re = bu_scr[j * nchunk:(j + 1) * nchunk, 0:sl]
        b_im = bu_scr[j * nchunk:(j + 1) * nchunk, sl:2 * sl]
        return (lam_re * x_re - lam_im * x_im + b_re, lam_re * x_im + lam_im * x_re + b_im)

    if chained:
        @pl.when(tile == 0)
        def _():
            carry_scr[...] = h0_ref[...]

        x_re = jnp.zeros((nchunk, sl), F32)
        x_im = jnp.zeros((nchunk, sl), F32)
        for j in range(chunk):
            x_re, x_im = step(x_re, x_im, j)
        e_scr[:, 0:sl] = x_re
        e_scr[:, sl:2 * sl] = x_im
        lc_re = lamc_ref[0:1, :]
        lc_im = lamc_ref[1:2, :]
        h_re = carry_scr[:, 0:sl]
        h_im = carry_scr[:, sl:2 * sl]
        for c in range(nchunk):
            hp_scr[c:c + 1, 0:sl] = h_re
            hp_scr[c:c + 1, sl:2 * sl] = h_im
            e_re = e_scr[c:c + 1, 0:sl]
            e_im = e_scr[c:c + 1, sl:2 * sl]
            h_re, h_im = (lc_re * h_re - lc_im * h_im + e_re, lc_re * h_im + lc_im * h_re + e_im)
        carry_scr[:, 0:sl] = h_re
        carry_scr[:, sl:2 * sl] = h_im
        fs_ref[:, 0:sl] = h_re
        fs_ref[:, sl:2 * sl] = h_im
        x_re = hp_scr[:, 0:sl]
        x_im = hp_scr[:, sl:2 * sl]
    else:
        x_re = h0_ref[:, 0:sl]
        x_im = h0_ref[:, sl:2 * sl]

    for j in range(chunk):
        x_re, x_im = step(x_re, x_im, j)
        xre_scr[j * nchunk:(j + 1) * nchunk, :] = x_re.astype(BF16)
        xim_scr[j * nchunk:(j + 1) * nchunk, :] = x_im.astype(BF16)
    if not chained:
        fs_ref[:, 0:sl] = x_re
        fs_ref[:, sl:2 * sl] = x_im

    y = (jnp.dot(xre_scr[...], cre_ref[...], preferred_element_type=F32)
         - jnp.dot(xim_scr[...], cim_ref[...], preferred_element_type=F32))
    d = d_ref[...]
    for j in range(chunk):
        y_ref[pl.ds(j, nchunk, stride=chunk), :] = (
            y[j * nchunk:(j + 1) * nchunk, :] + d * u_ref[pl.ds(j, nchunk, stride=chunk), :])


def _ssm_scan(u, bb, cre, cim, lam, lamc, dvec, h0, *, nseq, chunk, nchunk, chained):
    nblk, rows, _ = u.shape
    tm = chunk * nchunk
    tiles = rows // (nseq * tm)
    r0 = h0.shape[2]
    sl2 = 2 * STATE_LANES
    blk = lambda shape: pl.BlockSpec((None,) + shape, lambda k, b, t: (k,) + (0,) * len(shape))
    row_spec = pl.BlockSpec((None, tm, LANES), lambda k, b, t: (k, b * tiles + t, 0))
    st_spec = pl.BlockSpec((None, None, r0, sl2), lambda k, b, t: (k, b, 0, 0))
    return pl.pallas_call(
        functools.partial(_ssm_scan_kernel, chunk=chunk, nchunk=nchunk, chained=chained),
        grid=(nblk, nseq, tiles),
        in_specs=[row_spec, blk((LANES, sl2)), blk((STATE_LANES, LANES)), blk((STATE_LANES, LANES)),
                  blk((2, STATE_LANES)), blk((2, STATE_LANES)), blk((1, LANES)), st_spec],
        out_specs=[row_spec, st_spec],
        out_shape=[jax.ShapeDtypeStruct(u.shape, F32), jax.ShapeDtypeStruct(h0.shape, F32)],
        scratch_shapes=[
            pltpu.VMEM((tm, sl2), F32),
            pltpu.VMEM((tm, STATE_LANES), BF16),
            pltpu.VMEM((tm, STATE_LANES), BF16),
            pltpu.VMEM((tm, LANES), BF16),
            pltpu.VMEM((nchunk, sl2), F32),
            pltpu.VMEM((nchunk, sl2), F32),
            pltpu.VMEM((1, sl2), F32),
        ],
        compiler_params=_params(("parallel", "parallel", "arbitrary")),
        name="ssm_scan",
    )(u, bb, cre, cim, lam, lamc, dvec, h0)


def _ssm_out_kernel(y_ref, h_ref, wglu_ref, bglu_ref, wout_ref, o_ref):
    y = jnp.concatenate([y_ref[k] for k in range(y_ref.shape[0])], axis=1)
    g = jax.nn.gelu(y, approximate=True)
    gate = jnp.dot(g.astype(BF16), wglu_ref[...], preferred_element_type=F32) + bglu_ref[...]
    z = (g * jax.nn.sigmoid(gate)).astype(BF16)
    o_ref[...] = h_ref[...] + jnp.dot(z, wout_ref[...], preferred_element_type=F32)


def _ssm_out(y, h, wglu, bglu, wout, *, tm):
    rows = h.shape[0]
    row_spec = pl.BlockSpec((tm, D_MODEL), lambda i: (i, 0))
    return pl.pallas_call(
        _ssm_out_kernel,
        grid=(rows // tm,),
        in_specs=[pl.BlockSpec((y.shape[0], tm, LANES), lambda i: (0, i, 0)), row_spec,
                  _const_spec(wglu.shape), _const_spec((1, D_MODEL)), _const_spec(wout.shape)],
        out_specs=row_spec,
        out_shape=jax.ShapeDtypeStruct((rows, D_MODEL), F32),
        compiler_params=_params(("parallel",)),
        name="ssm_out",
    )(y, h, wglu, bglu, wout)


def _block_diag(x):
    kg, a, b = x.shape
    k = kg // GROUPS_PER_BLOCK
    eye = jnp.eye(GROUPS_PER_BLOCK, dtype=x.dtype)
    x = x.reshape(k, GROUPS_PER_BLOCK, a, b)
    return jnp.einsum("kgab,gh->kgahb", x, eye).reshape(k, GROUPS_PER_BLOCK * a, GROUPS_PER_BLOCK * b)


def _lane_blocks(x):
    lead = x.shape[:-2]
    x = x.reshape(lead + (SSM_LANE_BLOCKS, STATE_LANES))
    return jnp.moveaxis(x, -2, 0)


def _ssm_tables(P, j):
    b_re_t = jnp.transpose(P["ssm_b_re"][j], (2, 0, 1))
    b_im_t = jnp.transpose(P["ssm_b_im"][j], (2, 0, 1))
    p_re, p_im, bb_re, bb_im = _ssm_prep(P["ssm_a_re"][j], P["ssm_a_im"][j], P["ssm_log_dt"][j], b_re_t, b_im_t)
    bb_re = jnp.swapaxes(bb_re, 0, 1)
    bb_im = jnp.swapaxes(bb_im, 0, 1)
    bb = jnp.concatenate([_block_diag(bb_re), _block_diag(bb_im)], axis=2).astype(BF16)
    cre = _block_diag(jnp.swapaxes(P["ssm_c_re"][j], 1, 2)).astype(BF16)
    cim = _block_diag(jnp.swapaxes(P["ssm_c_im"][j], 1, 2)).astype(BF16)
    pows = jnp.stack([_lane_blocks(p_re), _lane_blocks(p_im)], axis=2)
    dvec = P["ssm_d"][j].reshape(SSM_LANE_BLOCKS, 1, LANES)
    return dict(bb=bb, cre=cre, cim=cim, pows=pows, dvec=dvec)


def _ssm_mixer(hres, g, P, j, tabs, h0, *, nseq, chunk, nchunk, chained, tm):
    u = _proj(hres, g, P["ssm_w_in"][j], tm=tm, slabs=True)
    lam = tabs["pows"][:, LAM_EXPONENTS.index(1)]
    lamc = tabs["pows"][:, LAM_EXPONENTS.index(chunk)]
    y, fs = _ssm_scan(u, tabs["bb"], tabs["cre"], tabs["cim"], lam, lamc, tabs["dvec"], h0,
                      nseq=nseq, chunk=chunk, nchunk=nchunk, chained=chained)
    out = _ssm_out(y, hres, P["ssm_w_glu"][j], P["ssm_b_glu"][j].reshape(1, D_MODEL), P["ssm_w_out"][j], tm=tm)
    return out, fs


def _state_from_blocks(fs):
    nblk, nseq, r0, _ = fs.shape
    fs = fs.reshape(nblk, nseq * r0, 2, GROUPS_PER_BLOCK, SSM_STATE)
    fs = jnp.transpose(fs, (1, 0, 3, 4, 2))
    return fs.reshape(nseq * r0, SSM_GROUPS, SSM_STATE, 2)


def _state_to_blocks(state):
    nseq = state.shape[0]
    s = state.reshape(nseq, SSM_LANE_BLOCKS, GROUPS_PER_BLOCK, SSM_STATE, 2)
    s = jnp.transpose(s, (1, 0, 4, 2, 3)).reshape(SSM_LANE_BLOCKS, nseq, 2 * STATE_LANES)
    return s[:, None]


PROMPT_TM = 512
SSM_CHUNK = 16
SSM_NCHUNK = 32


def _trunk(x, caches, ssm_state, P, W, *, nseq, seq, tm):
    row = lambda v: v.reshape(1, D_MODEL)
    ffn = lambda v, i, k, final=False: _ffn(
        v, row(P["norm_g"][i, 2 * k]), W["gate"][i][k], W["up"][i][k], W["down"][i][k],
        row(P["final_norm_g"]), tm=tm, final_norm=final)

    h = ffn(x, 0, 0)
    qkv = _proj(h, row(P["norm_g"][0, 1]), W["qkv"], tm=tm, slabs=False)
    if caches is None:
        ols = []
        for g in range(N_GROUPS):
            ols.extend(_attn_prompt(qkv, group=g, seq=seq))
        new_caches = None
    else:
        ols, new_caches = _attn_sample(qkv, caches, t_len=seq)
    h = _merge(ols, h, W["o"], tm=tm)
    x = ffn(h, 0, 1)

    h = ffn(x, 1, 0)
    if ssm_state is None:
        h0 = jnp.zeros((SSM_LANE_BLOCKS, nseq, 1, 2 * STATE_LANES), F32)
        h, fs = _ssm_mixer(h, row(P["norm_g"][1, 1]), W, 0, W["ssm"], h0, nseq=nseq,
                           chunk=SSM_CHUNK, nchunk=SSM_NCHUNK, chained=True, tm=tm)
    else:
        h0 = _state_to_blocks(ssm_state)
        h, fs = _ssm_mixer(h, row(P["norm_g"][1, 1]), W, 0, W["ssm"], h0, nseq=1,
                           chunk=seq, nchunk=nseq, chained=False, tm=tm)
    y = ffn(h, 1, 1, final=True)
    return y, qkv, new_caches, _state_from_blocks(fs)


def kernel(x_prompt, x_sample, cache_kv_g0, cache_kv_g1, cache_kv_g2, state_ssm, norm_g, final_norm_g,
           ffn_w_gate, ffn_w_up, ffn_w_down, attn_w_qkv, attn_w_o, ssm_w_in, ssm_a_re, ssm_a_im, ssm_log_dt,
           ssm_b_re, ssm_b_im, ssm_c_re, ssm_c_im, ssm_d, ssm_w_glu, ssm_b_glu, ssm_w_out):
    batch, seq, _ = x_prompt.shape
    dec_batch, dec_seq, _ = x_sample.shape
    P = dict(norm_g=norm_g, final_norm_g=final_norm_g, ssm_a_re=ssm_a_re, ssm_a_im=ssm_a_im,
             ssm_log_dt=ssm_log_dt, ssm_b_re=ssm_b_re, ssm_b_im=ssm_b_im, ssm_c_re=ssm_c_re,
             ssm_c_im=ssm_c_im, ssm_d=ssm_d, ssm_b_glu=ssm_b_glu)
    W = dict(
        gate=ffn_w_gate.astype(BF16), up=ffn_w_up.astype(BF16), down=ffn_w_down.astype(BF16),
        qkv=attn_w_qkv[0].astype(BF16), o=attn_w_o[0].astype(BF16),
        ssm_w_in=ssm_w_in.astype(BF16), ssm_w_glu=ssm_w_glu.astype(BF16), ssm_w_out=ssm_w_out.astype(BF16),
        ssm_b_glu=ssm_b_glu,
    )
    W["ssm"] = _ssm_tables(P, 0)

    y_p, qkv_p, _, ssm_p = _trunk(x_prompt.reshape(batch * seq, D_MODEL), None, None, P, W,
                                  nseq=batch, seq=seq, tm=PROMPT_TM)
    qkv_p = qkv_p.reshape(batch, seq, 3 * ATTN_WIDTH)
    kv_prompt = []
    for g, (win, _) in enumerate(DILATED_GROUPS):
        wb = min(win, seq)
        tail = qkv_p[:, seq - wb:]
        k0 = ATTN_WIDTH + g * GROUP_WIDTH
        v0 = 2 * ATTN_WIDTH + g * GROUP_WIDTH
        kv = jnp.stack([tail[:, :, k0:k0 + GROUP_WIDTH], tail[:, :, v0:v0 + GROUP_WIDTH]], axis=2)
        kv_prompt.append(kv.reshape(1, batch, wb, 2, HEADS_PER_GROUP, HEAD_DIM))

    caches = [c[0].reshape(dec_batch, c.shape[2], 2 * GROUP_WIDTH)
              for c in (cache_kv_g0, cache_kv_g1, cache_kv_g2)]
    y_s, _, new_caches, ssm_s = _trunk(x_sample.reshape(dec_batch * dec_seq, D_MODEL), caches, state_ssm[0], P, W,
                                       nseq=dec_batch, seq=dec_seq, tm=dec_batch * dec_seq)
    kv_sample = [nc.reshape(1, dec_batch, nc.shape[1], 2, HEADS_PER_GROUP, HEAD_DIM) for nc in new_caches]

    return (y_p.reshape(batch, seq, D_MODEL), y_s.reshape(dec_batch, dec_seq, D_MODEL),
            kv_prompt[0], kv_prompt[1], kv_prompt[2], ssm_p[None],
            kv_sample[0], kv_sample[1], kv_sample[2], ssm_s[None])
```

```python
import functools
import math

import jax
import jax.numpy as jnp
from jax import lax
from jax.experimental import pallas as pl
from jax.experimental.pallas import tpu as pltpu

F32 = jnp.float32
BF16 = jnp.bfloat16

D_MODEL = 1024
D_FF = 2816
HEAD_DIM = 64
HEADS_PER_GROUP = 4
DILATED_GROUPS = ((128, 1), (512, 4), (2048, 16))
N_GROUPS = len(DILATED_GROUPS)
N_HEADS = N_GROUPS * HEADS_PER_GROUP
GROUP_WIDTH = HEADS_PER_GROUP * HEAD_DIM
ATTN_WIDTH = N_HEADS * HEAD_DIM
BAND = 128
SSM_CH = 16
SSM_GROUPS = D_MODEL // SSM_CH
SSM_STATE = 64
NORM_EPS = 1e-6

LANES = 128
V7X_VMEM_BYTES = 64 * 1024 * 1024
VMEM_LIMIT = V7X_VMEM_BYTES - 8 * 1024 * 1024

SSM_LANE_BLOCKS = D_MODEL // LANES
GROUPS_PER_BLOCK = LANES // SSM_CH
STATE_LANES = GROUPS_PER_BLOCK * SSM_STATE

assert all(w // d == BAND for w, d in DILATED_GROUPS)
assert all(d & (d - 1) == 0 for _, d in DILATED_GROUPS)


def _params(sem, vmem=VMEM_LIMIT):
    return pltpu.CompilerParams(dimension_semantics=sem, vmem_limit_bytes=vmem)


def _const_spec(shape):
    nd = len(shape)
    return pl.BlockSpec(shape, lambda *_: (0,) * nd, pipeline_mode=pl.Buffered(1))


def _rms(x, g):
    return x * lax.rsqrt(jnp.mean(x * x, axis=-1, keepdims=True) + NORM_EPS) * g


MXU_TILE = 256
FFN_CHUNK_BOUNDS = (0, 6 * MXU_TILE, D_FF)


def _ffn_kernel(x_ref, g_ref, wg_ref, wu_ref, wd_ref, gf_ref, o_ref, *, final_norm):
    x = x_ref[...]
    xb = _rms(x, g_ref[...]).astype(BF16)
    acc = None
    for lo, hi in zip(FFN_CHUNK_BOUNDS[:-1], FFN_CHUNK_BOUNDS[1:]):
        gate = jnp.dot(xb, wg_ref[:, lo:hi], preferred_element_type=F32)
        up = jnp.dot(xb, wu_ref[:, lo:hi], preferred_element_type=F32)
        hmid = (gate * jax.nn.sigmoid(gate) * up).astype(BF16)
        part = jnp.dot(hmid, wd_ref[lo:hi, :], preferred_element_type=F32)
        acc = part if acc is None else acc + part
    out = x + 0.5 * acc
    if final_norm:
        out = _rms(out, gf_ref[...])
    o_ref[...] = out


def _pick_spec(arr, *idx):
    tail = arr.shape[len(idx):]
    index = tuple(idx) + (0,) * len(tail)
    return pl.BlockSpec((None,) * len(idx) + tail, lambda *_: index, pipeline_mode=pl.Buffered(1))


def _ffn(x, norms, W, layer, half, *, tm, final_norm):
    rows = x.shape[0]
    return pl.pallas_call(
        functools.partial(_ffn_kernel, final_norm=final_norm),
        grid=(rows // tm,),
        in_specs=[
            pl.BlockSpec((tm, D_MODEL), lambda i: (i, 0)),
            _pick_spec(norms, layer, 2 * half),
            _pick_spec(W["gate"], layer, half),
            _pick_spec(W["up"], layer, half),
            _pick_spec(W["down"], layer, half),
            _const_spec((1, D_MODEL)),
        ],
        out_specs=pl.BlockSpec((tm, D_MODEL), lambda i: (i, 0)),
        out_shape=jax.ShapeDtypeStruct((rows, D_MODEL), F32),
        compiler_params=_params(("parallel",)),
        name="ffn",
    )(x, norms, W["gate"], W["up"], W["down"], W["final_norm"])


def _proj_kernel(x_ref, g_ref, w_ref, o_ref, *, slabs):
    xb = _rms(x_ref[...], g_ref[...]).astype(BF16)
    y = jnp.dot(xb, w_ref[...], preferred_element_type=F32)
    if slabs:
        for k in range(y.shape[1] // LANES):
            o_ref[k] = y[:, k * LANES:(k + 1) * LANES]
    else:
        o_ref[...] = y


def _proj(x, norms, layer, w, *, tm, slabs):
    rows, n = x.shape[0], w.shape[-1]
    if slabs:
        out_spec = pl.BlockSpec((n // LANES, tm, LANES), lambda i: (0, i, 0))
        out_shape = jax.ShapeDtypeStruct((n // LANES, rows, LANES), F32)
    else:
        out_spec = pl.BlockSpec((tm, n), lambda i: (i, 0))
        out_shape = jax.ShapeDtypeStruct((rows, n), F32)
    return pl.pallas_call(
        functools.partial(_proj_kernel, slabs=slabs),
        grid=(rows // tm,),
        in_specs=[
            pl.BlockSpec((tm, D_MODEL), lambda i: (i, 0)),
            _pick_spec(norms, layer, 1),
            _pick_spec(w, 0),
        ],
        out_specs=out_spec,
        out_shape=out_shape,
        compiler_params=_params(("parallel",)),
        name="norm_proj",
    )(x, norms, w)


def _alibi_slope(head):
    return 2.0 ** (-8.0 * (head + 1) / N_HEADS)


def _head_lane_mask(hh):
    lane = lax.broadcasted_iota(jnp.int32, (1, LANES), 1)
    return (lane >= hh * HEAD_DIM) & (lane < (hh + 1) * HEAD_DIM)


ATTN_TILE_TOKENS = 2048
_NT = (((1,), (1,)), ((), ()))


def _attn_kernel(q_ref, kp_ref, kc_ref, vp_ref, vc_ref, o_ref, l_ref, *, group, nblk, tiles_per_seq):
    dil = DILATED_GROUPS[group][1]
    span = BAND * dil
    i = pl.program_id(0)
    c = pl.program_id(1)
    has_prev = (i % tiles_per_seq) != 0

    row = lax.broadcasted_iota(jnp.int32, (2 * BAND, 2 * BAND), 0)
    kj = lax.broadcasted_iota(jnp.int32, (2 * BAND, 2 * BAND), 1)
    delta = BAND + (row & (BAND - 1)) - kj
    head0 = group * HEADS_PER_GROUP
    slope_a = jnp.where(c == 0, _alibi_slope(head0), _alibi_slope(head0 + 2)).astype(F32)
    slope_b = jnp.where(c == 0, _alibi_slope(head0 + 1), _alibi_slope(head0 + 3)).astype(F32)
    slope = jnp.where(row < BAND, slope_a, slope_b)
    band = (delta >= 0) & (delta <= BAND)
    bias = jnp.where(band, -slope * (delta * dil).astype(F32), -jnp.inf)
    bias_first = jnp.where((kj >= BAND) | has_prev, bias, -jnp.inf)
    first_head = _head_lane_mask(0)

    def rows(ref, start):
        return ref[pl.ds(start, BAND), :] if dil == 1 else ref[pl.ds(start, BAND, stride=dil), :]

    for r in range(dil):
        k_prev = rows(kp_ref, r).astype(BF16)
        v_prev = rows(vp_ref, r).astype(BF16)
        for b in range(nblk):
            start = b * span + r
            q = rows(q_ref, start) * (HEAD_DIM ** -0.5)
            k_cur = rows(kc_ref, start).astype(BF16)
            v_cur = rows(vc_ref, start).astype(BF16)
            k = jnp.concatenate([k_prev, k_cur], axis=0)
            v = jnp.concatenate([v_prev, v_cur], axis=0)
            q2 = jnp.concatenate([jnp.where(first_head, q, 0.0), jnp.where(first_head, 0.0, q)], axis=0)
            s = lax.dot_general(q2.astype(BF16), k, _NT, preferred_element_type=F32)
            s = s + (bias_first if b == 0 else bias)
            m = jnp.max(s, axis=-1, keepdims=True)
            p = jnp.exp(s - m)
            den = jnp.sum(p, axis=-1, keepdims=True)
            o2 = jnp.dot(p.astype(BF16), v, preferred_element_type=F32) / den
            l2 = m + jnp.log(den)
            o = jnp.where(first_head, o2[:BAND], o2[BAND:])
            lse = jnp.where(first_head, l2[:BAND], l2[BAND:])
            if dil == 1:
                o_ref[pl.ds(start, BAND), :] = o
                l_ref[pl.ds(start, BAND), :] = lse
            else:
                o_ref[pl.ds(start, BAND, stride=dil), :] = o
                l_ref[pl.ds(start, BAND, stride=dil), :] = lse
            k_prev, v_prev = k_cur, v_cur


def _attn_prompt(qkv, *, group, seq):
    rows = qkv.shape[0]
    dil = DILATED_GROUPS[group][1]
    span = BAND * dil
    tt = ATTN_TILE_TOKENS
    nblk = tt // span
    qcol = group * GROUP_WIDTH // LANES
    kcol = (ATTN_WIDTH + group * GROUP_WIDTH) // LANES
    vcol = (2 * ATTN_WIDTH + group * GROUP_WIDTH) // LANES
    cur = lambda col: pl.BlockSpec((tt, LANES), lambda i, c: (i, col + c))
    prev = lambda col: pl.BlockSpec((span, LANES), lambda i, c: (jnp.maximum(i * nblk - 1, 0), col + c))
    out_spec = pl.BlockSpec((tt, LANES), lambda i, c: (i, c))
    out_shape = jax.ShapeDtypeStruct((rows, GROUP_WIDTH), F32)
    return pl.pallas_call(
        functools.partial(_attn_kernel, group=group, nblk=nblk, tiles_per_seq=seq // tt),
        grid=(rows // tt, GROUP_WIDTH // LANES),
        in_specs=[cur(qcol), prev(kcol), cur(kcol), prev(vcol), cur(vcol)],
        out_specs=[out_spec, out_spec],
        out_shape=[out_shape, out_shape],
        compiler_params=_params(("parallel", "parallel")),
        name=f"attn_prompt_g{group}",
    )(qkv, qkv, qkv, qkv, qkv)


def _attn_sample_kernel(qkv_ref, c0_ref, c1_ref, c2_ref,
                        o0_ref, l0_ref, o1_ref, l1_ref, o2_ref, l2_ref, n0_ref, n1_ref, n2_ref, *, t_len):
    cache_refs = (c0_ref, c1_ref, c2_ref)
    o_refs = (o0_ref, o1_ref, o2_ref)
    l_refs = (l0_ref, l1_ref, l2_ref)
    n_refs = (n0_ref, n1_ref, n2_ref)
    for g, (win, dil) in enumerate(DILATED_GROUPS):
        cache_ref = cache_refs[g]
        wb = cache_ref.shape[0]
        q0 = g * GROUP_WIDTH
        k0 = ATTN_WIDTH + g * GROUP_WIDTH
        v0 = 2 * ATTN_WIDTH + g * GROUP_WIDTH
        k_new = qkv_ref[:, k0:k0 + GROUP_WIDTH]
        v_new = qkv_ref[:, v0:v0 + GROUP_WIDTH]

        t1 = lax.broadcasted_iota(jnp.int32, (t_len, wb), 0)
        r1 = lax.broadcasted_iota(jnp.int32, (t_len, wb), 1)
        dist1 = wb + t1 - r1
        valid1 = (dist1 <= win) & ((dist1 & (dil - 1)) == 0)
        t2 = lax.broadcasted_iota(jnp.int32, (t_len, t_len), 0)
        r2 = lax.broadcasted_iota(jnp.int32, (t_len, t_len), 1)
        dist2 = t2 - r2
        valid2 = (dist2 >= 0) & ((dist2 & (dil - 1)) == 0)

        o_blocks, l_blocks = [], []
        for c in range(GROUP_WIDTH // LANES):
            q = qkv_ref[:, q0 + c * LANES:q0 + (c + 1) * LANES] * (HEAD_DIM ** -0.5)
            kc = cache_ref[:, c * LANES:(c + 1) * LANES].astype(BF16)
            vc = cache_ref[:, GROUP_WIDTH + c * LANES:GROUP_WIDTH + (c + 1) * LANES].astype(BF16)
            kn = k_new[:, c * LANES:(c + 1) * LANES].astype(BF16)
            vn = v_new[:, c * LANES:(c + 1) * LANES].astype(BF16)
            o_heads, l_heads = [], []
            for hh in range(2):
                slope = _alibi_slope(g * HEADS_PER_GROUP + 2 * c + hh)
                qm = jnp.where(_head_lane_mask(hh), q, 0.0).astype(BF16)
                nt = (((1,), (1,)), ((), ()))
                s1 = lax.dot_general(qm, kc, nt, preferred_element_type=F32)
                s2 = lax.dot_general(qm, kn, nt, preferred_element_type=F32)
                s1 = jnp.where(valid1, s1 - slope * dist1.astype(F32), -jnp.inf)
                s2 = jnp.where(valid2, s2 - slope * dist2.astype(F32), -jnp.inf)
                m = jnp.maximum(jnp.max(s1, axis=-1, keepdims=True), jnp.max(s2, axis=-1, keepdims=True))
                p1 = jnp.exp(s1 - m)
                p2 = jnp.exp(s2 - m)
                den = jnp.sum(p1, axis=-1, keepdims=True) + jnp.sum(p2, axis=-1, keepdims=True)
                o = (jnp.dot(p1.astype(BF16), vc, preferred_element_type=F32)
                     + jnp.dot(p2.astype(BF16), vn, preferred_element_type=F32)) / den
                o_heads.append(o)
                l_heads.append(m + jnp.log(den))
            first = _head_lane_mask(0)
            o_blocks.append(jnp.where(first, o_heads[0], o_heads[1]))
            l_blocks.append(jnp.where(first, l_heads[0], l_heads[1]))
        o_refs[g][...] = jnp.concatenate(o_blocks, axis=1)
        l_refs[g][...] = jnp.concatenate(l_blocks, axis=1)

        n_refs[g][0:wb - t_len, :] = cache_ref[t_len:wb, :]
        n_refs[g][wb - t_len:wb, :] = jnp.concatenate([k_new, v_new], axis=1)


def _attn_sample(qkv, caches, *, t_len):
    nseq = caches[0].shape[0]
    rows = qkv.shape[0]
    ol_spec = pl.BlockSpec((t_len, GROUP_WIDTH), lambda b: (b, 0))
    ol_shape = jax.ShapeDtypeStruct((rows, GROUP_WIDTH), F32)
    cache_specs = [pl.BlockSpec((None,) + c.shape[1:], lambda b: (b, 0, 0)) for c in caches]
    outs = pl.pallas_call(
        functools.partial(_attn_sample_kernel, t_len=t_len),
        grid=(nseq,),
        in_specs=[pl.BlockSpec((t_len, 3 * ATTN_WIDTH), lambda b: (b, 0))] + cache_specs,
        out_specs=[ol_spec] * 6 + cache_specs,
        out_shape=[ol_shape] * 6 + [jax.ShapeDtypeStruct(c.shape, F32) for c in caches],
        compiler_params=_params(("parallel",)),
        name="attn_sample",
    )(qkv, *caches)
    return outs[:6], outs[6:]


def _merge_kernel(o0_ref, l0_ref, o1_ref, l1_ref, o2_ref, l2_ref, h_ref, wo_ref, out_ref):
    lses = (l0_ref[...], l1_ref[...], l2_ref[...])
    outs = (o0_ref[...], o1_ref[...], o2_ref[...])
    m = jnp.maximum(jnp.maximum(lses[0], lses[1]), lses[2])
    es = [jnp.exp(l - m) for l in lses]
    tot = es[0] + es[1] + es[2]
    merged = jnp.concatenate([o * (e / tot) for o, e in zip(outs, es)], axis=1).astype(BF16)
    out_ref[...] = h_ref[...] + jnp.dot(merged, wo_ref[...], preferred_element_type=F32)


def _merge(ols, h, wo, *, tm):
    rows = h.shape[0]
    ol_spec = pl.BlockSpec((tm, GROUP_WIDTH), lambda i: (i, 0))
    row_spec = pl.BlockSpec((tm, D_MODEL), lambda i: (i, 0))
    return pl.pallas_call(
        _merge_kernel,
        grid=(rows // tm,),
        in_specs=[ol_spec] * 6 + [row_spec, _pick_spec(wo, 0)],
        out_specs=row_spec,
        out_shape=jax.ShapeDtypeStruct((rows, D_MODEL), F32),
        compiler_params=_params(("parallel",)),
        name="attn_merge",
    )(*ols, h, wo)


LAM_EXPONENTS = (1, 2, 4, 8, 16)


def _ssm_prep_kernel(a_re_ref, a_im_ref, logdt_ref, bre_ref, bim_ref, pre_ref, pim_ref, bbre_ref, bbim_ref):
    a_re = a_re_ref[...]
    a_im = a_im_ref[...]
    dt = jnp.exp(logdt_ref[...])
    mag = jnp.exp(a_re * dt)
    ang = a_im * dt
    lam_re = mag * jnp.cos(ang)
    lam_im = mag * jnp.sin(ang)
    den = a_re * a_re + a_im * a_im
    num_re = lam_re - 1.0
    coef_re = ((num_re * a_re + lam_im * a_im) / den)[None]
    coef_im = ((lam_im * a_re - num_re * a_im) / den)[None]
    b_re = bre_ref[...]
    b_im = bim_ref[...]
    bbre_ref[...] = coef_re * b_re - coef_im * b_im
    bbim_ref[...] = coef_re * b_im + coef_im * b_re
    p_re, p_im = lam_re, lam_im
    for idx in range(len(LAM_EXPONENTS)):
        pre_ref[idx] = p_re
        pim_ref[idx] = p_im
        p_re, p_im = p_re * p_re - p_im * p_im, 2.0 * p_re * p_im


def _ssm_prep(a_re, a_im, log_dt, b_re_t, b_im_t):
    g, n = a_re.shape
    pw = jax.ShapeDtypeStruct((len(LAM_EXPONENTS), g, n), F32)
    bb = jax.ShapeDtypeStruct(b_re_t.shape, F32)
    return pl.pallas_call(
        _ssm_prep_kernel,
        out_shape=[pw, pw, bb, bb],
        name="ssm_prep",
    )(a_re, a_im, log_dt.reshape(g, 1), b_re_t, b_im_t)


def _ssm_scan_kernel(u_ref, bb_ref, cre_ref, cim_ref, lam_ref, lamc_ref, d_ref, h0_ref, y_ref, fs_ref,
                     bu_scr, xre_scr, xim_scr, ud_scr, e_scr, hp_scr, carry_scr, *, chunk, nchunk, chained):
    sl = STATE_LANES
    tile = pl.program_id(2)

    for j in range(chunk):
        ud_scr[j * nchunk:(j + 1) * nchunk, :] = u_ref[pl.ds(j, nchunk, stride=chunk), :].astype(BF16)
    bu_scr[...] = jnp.dot(ud_scr[...], bb_ref[...], preferred_element_type=F32)

    lam_re = jnp.broadcast_to(lam_ref[0:1, :], (nchunk, sl))
    lam_im = jnp.broadcast_to(lam_ref[1:2, :], (nchunk, sl))

    def step(x_re, x_im, j):
        b_re = bu_scr[j * nchunk:(j + 1) * nchunk, 0:sl]
        b_im = bu_scr[j * nchunk:(j + 1) * nchunk, sl:2 * sl]
        return (lam_re * x_re - lam_im * x_im + b_re, lam_re * x_im + lam_im * x_re + b_im)

    if chained:
        @pl.when(tile == 0)
        def _():
            carry_scr[...] = h0_ref[...]

        x_re = jnp.zeros((nchunk, sl), F32)
        x_im = jnp.zeros((nchunk, sl), F32)
        for j in range(chunk):
            x_re, x_im = step(x_re, x_im, j)
        e_scr[:, 0:sl] = x_re
        e_scr[:, sl:2 * sl] = x_im
        lc_re = lamc_ref[0:1, :]
        lc_im = lamc_ref[1:2, :]
        h_re = carry_scr[:, 0:sl]
        h_im = carry_scr[:, sl:2 * sl]
        for c in range(nchunk):
            hp_scr[c:c + 1, 0:sl] = h_re
            hp_scr[c:c + 1, sl:2 * sl] = h_im
            e_re = e_scr[c:c + 1, 0:sl]
            e_im = e_scr[c:c + 1, sl:2 * sl]
            h_re, h_im = (lc_re * h_re - lc_im * h_im + e_re, lc_re * h_im + lc_im * h_re + e_im)
        carry_scr[:, 0:sl] = h_re
        carry_scr[:, sl:2 * sl] = h_im
        fs_ref[:, 0:sl] = h_re
        fs_ref[:, sl:2 * sl] = h_im
        x_re = hp_scr[:, 0:sl]
        x_im = hp_scr[:, sl:2 * sl]
    else:
        x_re = h0_ref[:, 0:sl]
        x_im = h0_ref[:, sl:2 * sl]

    for j in range(chunk):
        x_re, x_im = step(x_re, x_im, j)
        xre_scr[j * nchunk:(j + 1) * nchunk, :] = x_re.astype(BF16)
        xim_scr[j * nchunk:(j + 1) * nchunk, :] = x_im.astype(BF16)
    if not chained:
        fs_ref[:, 0:sl] = x_re
        fs_ref[:, sl:2 * sl] = x_im

    y = (jnp.dot(xre_scr[...], cre_ref[...], preferred_element_type=F32)
         - jnp.dot(xim_scr[...], cim_ref[...], preferred_element_type=F32))
    d = d_ref[...]
    for j in range(chunk):
        y_ref[pl.ds(j, nchunk, stride=chunk), :] = (
            y[j * nchunk:(j + 1) * nchunk, :] + d * u_ref[pl.ds(j, nchunk, stride=chunk), :])


def _ssm_scan(u, bb, cre, cim, lam, lamc, dvec, h0, *, nseq, chunk, nchunk, chained):
    nblk, rows, _ = u.shape
    tm = chunk * nchunk
    tiles = rows // (nseq * tm)
    r0 = h0.shape[2]
    sl2 = 2 * STATE_LANES
    blk = lambda shape: pl.BlockSpec((None,) + shape, lambda k, b, t: (k,) + (0,) * len(shape))
    row_spec = pl.BlockSpec((None, tm, LANES), lambda k, b, t: (k, b * tiles + t, 0))
    st_spec = pl.BlockSpec((None, None, r0, sl2), lambda k, b, t: (k, b, 0, 0))
    return pl.pallas_call(
        functools.partial(_ssm_scan_kernel, chunk=chunk, nchunk=nchunk, chained=chained),
        grid=(nblk, nseq, tiles),
        in_specs=[row_spec, blk((LANES, sl2)), blk((STATE_LANES, LANES)), blk((STATE_LANES, LANES)),
                  blk((2, STATE_LANES)), blk((2, STATE_LANES)), blk((1, LANES)), st_spec],
        out_specs=[row_spec, st_spec],
        out_shape=[jax.ShapeDtypeStruct(u.shape, F32), jax.ShapeDtypeStruct(h0.shape, F32)],
        scratch_shapes=[
            pltpu.VMEM((tm, sl2), F32),
            pltpu.VMEM((tm, STATE_LANES), BF16),
            pltpu.VMEM((tm, STATE_LANES), BF16),
            pltpu.VMEM((tm, LANES), BF16),
            pltpu.VMEM((nchunk, sl2), F32),
            pltpu.VMEM((nchunk, sl2), F32),
            pltpu.VMEM((1, sl2), F32),
        ],
        compiler_params=_params(("parallel", "parallel", "arbitrary")),
        name="ssm_scan",
    )(u, bb, cre, cim, lam, lamc, dvec, h0)


def _ssm_out_kernel(y_ref, h_ref, wglu_ref, bglu_ref, wout_ref, o_ref):
    y = jnp.concatenate([y_ref[k] for k in range(y_ref.shape[0])], axis=1)
    g = jax.nn.gelu(y, approximate=True)
    gate = jnp.dot(g.astype(BF16), wglu_ref[...], preferred_element_type=F32) + bglu_ref[...]
    z = (g * jax.nn.sigmoid(gate)).astype(BF16)
    o_ref[...] = h_ref[...] + jnp.dot(z, wout_ref[...], preferred_element_type=F32)


def _ssm_out(y, h, wglu, bglu, wout, *, tm):
    rows = h.shape[0]
    row_spec = pl.BlockSpec((tm, D_MODEL), lambda i: (i, 0))
    return pl.pallas_call(
        _ssm_out_kernel,
        grid=(rows // tm,),
        in_specs=[pl.BlockSpec((y.shape[0], tm, LANES), lambda i: (0, i, 0)), row_spec,
                  _pick_spec(wglu, 0), _pick_spec(bglu, 0), _pick_spec(wout, 0)],
        out_specs=row_spec,
        out_shape=jax.ShapeDtypeStruct((rows, D_MODEL), F32),
        compiler_params=_params(("parallel",)),
        name="ssm_out",
    )(y, h, wglu, bglu, wout)


def _block_diag(x):
    kg, a, b = x.shape
    k = kg // GROUPS_PER_BLOCK
    eye = jnp.eye(GROUPS_PER_BLOCK, dtype=x.dtype)
    x = x.reshape(k, GROUPS_PER_BLOCK, a, b)
    return jnp.einsum("kgab,gh->kgahb", x, eye).reshape(k, GROUPS_PER_BLOCK * a, GROUPS_PER_BLOCK * b)


def _lane_blocks(x):
    lead = x.shape[:-2]
    x = x.reshape(lead + (SSM_LANE_BLOCKS, STATE_LANES))
    return jnp.moveaxis(x, -2, 0)


def _ssm_tables(P, j):
    b_re_t = jnp.transpose(P["ssm_b_re"][j], (2, 0, 1))
    b_im_t = jnp.transpose(P["ssm_b_im"][j], (2, 0, 1))
    p_re, p_im, bb_re, bb_im = _ssm_prep(P["ssm_a_re"][j], P["ssm_a_im"][j], P["ssm_log_dt"][j], b_re_t, b_im_t)
    bb_re = jnp.swapaxes(bb_re, 0, 1)
    bb_im = jnp.swapaxes(bb_im, 0, 1)
    bb = jnp.concatenate([_block_diag(bb_re), _block_diag(bb_im)], axis=2).astype(BF16)
    cre = _block_diag(jnp.swapaxes(P["ssm_c_re"][j], 1, 2)).astype(BF16)
    cim = _block_diag(jnp.swapaxes(P["ssm_c_im"][j], 1, 2)).astype(BF16)
    pows = jnp.stack([_lane_blocks(p_re), _lane_blocks(p_im)], axis=2)
    dvec = P["ssm_d"][j].reshape(SSM_LANE_BLOCKS, 1, LANES)
    return dict(bb=bb, cre=cre, cim=cim, pows=pows, dvec=dvec)


def _ssm_mixer(hres, W, layer, h0, *, nseq, chunk, nchunk, chained, tm):
    tabs = W["ssm"]
    u = _proj(hres, W["norms"], layer, W["ssm_w_in"], tm=tm, slabs=True)
    lam = tabs["pows"][:, LAM_EXPONENTS.index(1)]
    lamc = tabs["pows"][:, LAM_EXPONENTS.index(chunk)]
    y, fs = _ssm_scan(u, tabs["bb"], tabs["cre"], tabs["cim"], lam, lamc, tabs["dvec"], h0,
                      nseq=nseq, chunk=chunk, nchunk=nchunk, chained=chained)
    out = _ssm_out(y, hres, W["ssm_w_glu"], W["ssm_b_glu"], W["ssm_w_out"], tm=tm)
    return out, fs


def _state_from_blocks(fs):
    nblk, nseq, r0, _ = fs.shape
    fs = fs.reshape(nblk, nseq * r0, 2, GROUPS_PER_BLOCK, SSM_STATE)
    fs = jnp.transpose(fs, (1, 0, 3, 4, 2))
    return fs.reshape(nseq * r0, SSM_GROUPS, SSM_STATE, 2)


def _state_to_blocks(state):
    nseq = state.shape[0]
    s = state.reshape(nseq, SSM_LANE_BLOCKS, GROUPS_PER_BLOCK, SSM_STATE, 2)
    s = jnp.transpose(s, (1, 0, 4, 2, 3)).reshape(SSM_LANE_BLOCKS, nseq, 2 * STATE_LANES)
    return s[:, None]


PROMPT_TM = 512
SSM_CHUNK = 16
SSM_NCHUNK = 32


def _trunk(x, caches, ssm_state, W, *, nseq, seq, tm):
    ffn = lambda v, i, k, final=False: _ffn(v, W["norms"], W, i, k, tm=tm, final_norm=final)

    h = ffn(x, 0, 0)
    qkv = _proj(h, W["norms"], 0, W["qkv"], tm=tm, slabs=False)
    if caches is None:
        ols = []
        for g in range(N_GROUPS):
            ols.extend(_attn_prompt(qkv, group=g, seq=seq))
        new_caches = None
    else:
        ols, new_caches = _attn_sample(qkv, caches, t_len=seq)
    h = _merge(ols, h, W["o"], tm=tm)
    x = ffn(h, 0, 1)

    h = ffn(x, 1, 0)
    if ssm_state is None:
        h0 = jnp.zeros((SSM_LANE_BLOCKS, nseq, 1, 2 * STATE_LANES), F32)
        h, fs = _ssm_mixer(h, W, 1, h0, nseq=nseq, chunk=SSM_CHUNK, nchunk=SSM_NCHUNK, chained=True, tm=tm)
    else:
        h0 = _state_to_blocks(ssm_state)
        h, fs = _ssm_mixer(h, W, 1, h0, nseq=1, chunk=seq, nchunk=nseq, chained=False, tm=tm)
    y = ffn(h, 1, 1, final=True)
    return y, qkv, new_caches, _state_from_blocks(fs)


def kernel(x_prompt, x_sample, cache_kv_g0, cache_kv_g1, cache_kv_g2, state_ssm, norm_g, final_norm_g,
           ffn_w_gate, ffn_w_up, ffn_w_down, attn_w_qkv, attn_w_o, ssm_w_in, ssm_a_re, ssm_a_im, ssm_log_dt,
           ssm_b_re, ssm_b_im, ssm_c_re, ssm_c_im, ssm_d, ssm_w_glu, ssm_b_glu, ssm_w_out):
    batch, seq, _ = x_prompt.shape
    dec_batch, dec_seq, _ = x_sample.shape
    P = dict(ssm_a_re=ssm_a_re, ssm_a_im=ssm_a_im, ssm_log_dt=ssm_log_dt, ssm_b_re=ssm_b_re,
             ssm_b_im=ssm_b_im, ssm_c_re=ssm_c_re, ssm_c_im=ssm_c_im, ssm_d=ssm_d)
    W = dict(
        norms=norm_g.reshape(norm_g.shape[0], norm_g.shape[1], 1, D_MODEL),
        final_norm=final_norm_g.reshape(1, D_MODEL),
        gate=ffn_w_gate.astype(BF16), up=ffn_w_up.astype(BF16), down=ffn_w_down.astype(BF16),
        qkv=attn_w_qkv.astype(BF16), o=attn_w_o.astype(BF16),
        ssm_w_in=ssm_w_in.astype(BF16), ssm_w_glu=ssm_w_glu.astype(BF16), ssm_w_out=ssm_w_out.astype(BF16),
        ssm_b_glu=ssm_b_glu.reshape(ssm_b_glu.shape[0], 1, D_MODEL),
        ssm=_ssm_tables(P, 0),
    )

    y_p, qkv_p, _, ssm_p = _trunk(x_prompt.reshape(batch * seq, D_MODEL), None, None, W,
                                  nseq=batch, seq=seq, tm=PROMPT_TM)
    qkv_p = qkv_p.reshape(batch, seq, 3 * ATTN_WIDTH)
    kv_prompt = []
    for g, (win, _) in enumerate(DILATED_GROUPS):
        wb = min(win, seq)
        tail = qkv_p[:, seq - wb:]
        k0 = ATTN_WIDTH + g * GROUP_WIDTH
        v0 = 2 * ATTN_WIDTH + g * GROUP_WIDTH
        kv = jnp.stack([tail[:, :, k0:k0 + GROUP_WIDTH], tail[:, :, v0:v0 + GROUP_WIDTH]], axis=2)
        kv_prompt.append(kv.reshape(1, batch, wb, 2, HEADS_PER_GROUP, HEAD_DIM))

    caches = [c[0].reshape(dec_batch, c.shape[2], 2 * GROUP_WIDTH)
              for c in (cache_kv_g0, cache_kv_g1, cache_kv_g2)]
    y_s, _, new_caches, ssm_s = _trunk(x_sample.reshape(dec_batch * dec_seq, D_MODEL), caches, state_ssm[0], W,
                                       nseq=dec_batch, seq=dec_seq, tm=dec_batch * dec_seq)
    kv_sample = [nc.reshape(1, dec_batch, nc.shape[1], 2, HEADS_PER_GROUP, HEAD_DIM) for nc in new_caches]

    return (y_p.reshape(batch, seq, D_MODEL), y_s.reshape(dec_batch, dec_seq, D_MODEL),
            kv_prompt[0], kv_prompt[1], kv_prompt[2], ssm_p[None],
            kv_sample[0], kv_sample[1], kv_sample[2], ssm_s[None])
```

```python
import functools
import math

import jax
import jax.numpy as jnp
from jax import lax
from jax.experimental import pallas as pl
from jax.experimental.pallas import tpu as pltpu

F32 = jnp.float32
BF16 = jnp.bfloat16

D_MODEL = 1024
D_FF = 2816
HEAD_DIM = 64
HEADS_PER_GROUP = 4
DILATED_GROUPS = ((128, 1), (512, 4), (2048, 16))
N_GROUPS = len(DILATED_GROUPS)
N_HEADS = N_GROUPS * HEADS_PER_GROUP
GROUP_WIDTH = HEADS_PER_GROUP * HEAD_DIM
ATTN_WIDTH = N_HEADS * HEAD_DIM
BAND = 128
SSM_CH = 16
SSM_GROUPS = D_MODEL // SSM_CH
SSM_STATE = 64
NORM_EPS = 1e-6

LANES = 128
V7X_VMEM_BYTES = 64 * 1024 * 1024
VMEM_LIMIT = V7X_VMEM_BYTES - 8 * 1024 * 1024

SSM_LANE_BLOCKS = D_MODEL // LANES
GROUPS_PER_BLOCK = LANES // SSM_CH
STATE_LANES = GROUPS_PER_BLOCK * SSM_STATE

assert all(w // d == BAND for w, d in DILATED_GROUPS)
assert all(d & (d - 1) == 0 for _, d in DILATED_GROUPS)


def _params(sem, vmem=VMEM_LIMIT):
    return pltpu.CompilerParams(dimension_semantics=sem, vmem_limit_bytes=vmem)


def _const_spec(shape):
    nd = len(shape)
    return pl.BlockSpec(shape, lambda *_: (0,) * nd, pipeline_mode=pl.Buffered(1))


def _rms(x, g):
    return x * lax.rsqrt(jnp.mean(x * x, axis=-1, keepdims=True) + NORM_EPS) * g


MXU_TILE = 256
FFN_CHUNK_BOUNDS = (0, 6 * MXU_TILE, D_FF)


def _ffn_kernel(x_ref, g_ref, wg_ref, wu_ref, wd_ref, gf_ref, o_ref, *, final_norm):
    x = x_ref[...]
    xb = _rms(x, g_ref[...]).astype(BF16)
    acc = None
    for lo, hi in zip(FFN_CHUNK_BOUNDS[:-1], FFN_CHUNK_BOUNDS[1:]):
        gate = jnp.dot(xb, wg_ref[:, lo:hi], preferred_element_type=F32)
        up = jnp.dot(xb, wu_ref[:, lo:hi], preferred_element_type=F32)
        hmid = (gate * jax.nn.sigmoid(gate) * up).astype(BF16)
        part = jnp.dot(hmid, wd_ref[lo:hi, :], preferred_element_type=F32)
        acc = part if acc is None else acc + part
    out = x + 0.5 * acc
    if final_norm:
        out = _rms(out, gf_ref[...])
    o_ref[...] = out


def _pick_spec(arr, *idx):
    tail = arr.shape[len(idx):]
    index = tuple(idx) + (0,) * len(tail)
    return pl.BlockSpec((None,) * len(idx) + tail, lambda *_: index, pipeline_mode=pl.Buffered(1))


def _ffn(x, norms, W, layer, half, *, tm, final_norm):
    rows = x.shape[0]
    return pl.pallas_call(
        functools.partial(_ffn_kernel, final_norm=final_norm),
        grid=(rows // tm,),
        in_specs=[
            pl.BlockSpec((tm, D_MODEL), lambda i: (i, 0)),
            _pick_spec(norms, layer, 2 * half),
            _pick_spec(W["gate"], layer, half),
            _pick_spec(W["up"], layer, half),
            _pick_spec(W["down"], layer, half),
            _const_spec((1, D_MODEL)),
        ],
        out_specs=pl.BlockSpec((tm, D_MODEL), lambda i: (i, 0)),
        out_shape=jax.ShapeDtypeStruct((rows, D_MODEL), F32),
        compiler_params=_params(("parallel",)),
        name="ffn",
    )(x, norms, W["gate"], W["up"], W["down"], W["final_norm"])


def _proj_kernel(x_ref, g_ref, w_ref, o_ref, *, slabs):
    xb = _rms(x_ref[...], g_ref[...]).astype(BF16)
    y = jnp.dot(xb, w_ref[...], preferred_element_type=F32)
    if slabs:
        for k in range(y.shape[1] // LANES):
            o_ref[k] = y[:, k * LANES:(k + 1) * LANES]
    else:
        o_ref[...] = y


def _proj(x, norms, layer, w, *, tm, slabs):
    rows, n = x.shape[0], w.shape[-1]
    if slabs:
        out_spec = pl.BlockSpec((n // LANES, tm, LANES), lambda i: (0, i, 0))
        out_shape = jax.ShapeDtypeStruct((n // LANES, rows, LANES), F32)
    else:
        out_spec = pl.BlockSpec((tm, n), lambda i: (i, 0))
        out_shape = jax.ShapeDtypeStruct((rows, n), F32)
    return pl.pallas_call(
        functools.partial(_proj_kernel, slabs=slabs),
        grid=(rows // tm,),
        in_specs=[
            pl.BlockSpec((tm, D_MODEL), lambda i: (i, 0)),
            _pick_spec(norms, layer, 1),
            _pick_spec(w, 0),
        ],
        out_specs=out_spec,
        out_shape=out_shape,
        compiler_params=_params(("parallel",)),
        name="norm_proj",
    )(x, norms, w)


def _alibi_slope(head):
    return 2.0 ** (-8.0 * (head + 1) / N_HEADS)


def _head_lane_mask(hh):
    lane = lax.broadcasted_iota(jnp.int32, (1, LANES), 1)
    return (lane >= hh * HEAD_DIM) & (lane < (hh + 1) * HEAD_DIM)


ATTN_TILE_TOKENS = 2048
_NT = (((1,), (1,)), ((), ()))


def _attn_kernel(q_ref, kp_ref, kc_ref, vp_ref, vc_ref, o_ref, l_ref, *, group, nblk, tiles_per_seq):
    dil = DILATED_GROUPS[group][1]
    span = BAND * dil
    i = pl.program_id(0)
    c = pl.program_id(1)
    has_prev = (i % tiles_per_seq) != 0

    row = lax.broadcasted_iota(jnp.int32, (2 * BAND, 2 * BAND), 0)
    kj = lax.broadcasted_iota(jnp.int32, (2 * BAND, 2 * BAND), 1)
    delta = BAND + (row & (BAND - 1)) - kj
    head0 = group * HEADS_PER_GROUP
    slope_a = jnp.where(c == 0, _alibi_slope(head0), _alibi_slope(head0 + 2)).astype(F32)
    slope_b = jnp.where(c == 0, _alibi_slope(head0 + 1), _alibi_slope(head0 + 3)).astype(F32)
    slope = jnp.where(row < BAND, slope_a, slope_b)
    band = (delta >= 0) & (delta <= BAND)
    bias = jnp.where(band, -slope * (delta * dil).astype(F32), -jnp.inf)
    bias_first = jnp.where((kj >= BAND) | has_prev, bias, -jnp.inf)
    first_head = _head_lane_mask(0)

    def rows(ref, start):
        return ref[pl.ds(start, BAND), :] if dil == 1 else ref[pl.ds(start, BAND, stride=dil), :]

    for r in range(dil):
        k_prev = rows(kp_ref, r).astype(BF16)
        v_prev = rows(vp_ref, r).astype(BF16)
        for b in range(nblk):
            start = b * span + r
            q = rows(q_ref, start) * (HEAD_DIM ** -0.5)
            k_cur = rows(kc_ref, start).astype(BF16)
            v_cur = rows(vc_ref, start).astype(BF16)
            k = jnp.concatenate([k_prev, k_cur], axis=0)
            v = jnp.concatenate([v_prev, v_cur], axis=0)
            q2 = jnp.concatenate([jnp.where(first_head, q, 0.0), jnp.where(first_head, 0.0, q)], axis=0)
            s = lax.dot_general(q2.astype(BF16), k, _NT, preferred_element_type=F32)
            s = s + (bias_first if b == 0 else bias)
            m = jnp.max(s, axis=-1, keepdims=True)
            p = jnp.exp(s - m)
            den = jnp.sum(p, axis=-1, keepdims=True)
            o2 = jnp.dot(p.astype(BF16), v, preferred_element_type=F32) / den
            l2 = m + jnp.log(den)
            o = jnp.where(first_head, o2[:BAND], o2[BAND:])
            lse = jnp.where(first_head, l2[:BAND], l2[BAND:])
            if dil == 1:
                o_ref[pl.ds(start, BAND), :] = o
                l_ref[pl.ds(start, BAND), :] = lse
            else:
                o_ref[pl.ds(start, BAND, stride=dil), :] = o
                l_ref[pl.ds(start, BAND, stride=dil), :] = lse
            k_prev, v_prev = k_cur, v_cur


def _attn_prompt(qkv, *, group, seq):
    rows = qkv.shape[0]
    dil = DILATED_GROUPS[group][1]
    span = BAND * dil
    tt = ATTN_TILE_TOKENS
    nblk = tt // span
    qcol = group * GROUP_WIDTH // LANES
    kcol = (ATTN_WIDTH + group * GROUP_WIDTH) // LANES
    vcol = (2 * ATTN_WIDTH + group * GROUP_WIDTH) // LANES
    cur = lambda col: pl.BlockSpec((tt, LANES), lambda i, c: (i, col + c))
    prev = lambda col: pl.BlockSpec((span, LANES), lambda i, c: (jnp.maximum(i * nblk - 1, 0), col + c))
    out_spec = pl.BlockSpec((tt, LANES), lambda i, c: (i, c))
    out_shape = jax.ShapeDtypeStruct((rows, GROUP_WIDTH), F32)
    return pl.pallas_call(
        functools.partial(_attn_kernel, group=group, nblk=nblk, tiles_per_seq=seq // tt),
        grid=(rows // tt, GROUP_WIDTH // LANES),
        in_specs=[cur(qcol), prev(kcol), cur(kcol), prev(vcol), cur(vcol)],
        out_specs=[out_spec, out_spec],
        out_shape=[out_shape, out_shape],
        compiler_params=_params(("parallel", "parallel")),
        name=f"attn_prompt_g{group}",
    )(qkv, qkv, qkv, qkv, qkv)


def _attn_sample_kernel(qkv_ref, c0_ref, c1_ref, c2_ref,
                        o0_ref, l0_ref, o1_ref, l1_ref, o2_ref, l2_ref, n0_ref, n1_ref, n2_ref, *, t_len):
    cache_refs = (c0_ref, c1_ref, c2_ref)
    o_refs = (o0_ref, o1_ref, o2_ref)
    l_refs = (l0_ref, l1_ref, l2_ref)
    n_refs = (n0_ref, n1_ref, n2_ref)
    gw = GROUP_WIDTH
    nrow = HEADS_PER_GROUP * t_len
    row_head = lax.broadcasted_iota(jnp.int32, (nrow, gw), 0) // t_len
    col_head = lax.broadcasted_iota(jnp.int32, (nrow, gw), 1) // HEAD_DIM
    out_col_head = lax.broadcasted_iota(jnp.int32, (t_len, gw), 1) // HEAD_DIM

    def gather_heads(x):
        acc = jnp.zeros((t_len, gw), F32)
        for h in range(HEADS_PER_GROUP):
            acc = acc + jnp.where(out_col_head == h, x[h * t_len:(h + 1) * t_len, :], 0.0)
        return acc

    for g, (win, dil) in enumerate(DILATED_GROUPS):
        cache_ref = cache_refs[g]
        wb = cache_ref.shape[1]
        q0 = g * gw
        k0 = ATTN_WIDTH + g * gw
        v0 = 2 * ATTN_WIDTH + g * gw

        kv_new = jnp.concatenate([qkv_ref[:, k0:k0 + gw], qkv_ref[:, v0:v0 + gw]], axis=1)
        kv_pad = jnp.concatenate([kv_new, jnp.zeros((LANES - t_len, 2 * gw), F32)], axis=0)
        new_t = kv_pad.T

        q = qkv_ref[:, q0:q0 + gw] * (HEAD_DIM ** -0.5)
        q_bd = jnp.where(row_head == col_head, jnp.concatenate([q] * HEADS_PER_GROUP, axis=0), 0.0).astype(BF16)

        def bias(ncol, base):
            r = lax.broadcasted_iota(jnp.int32, (nrow, ncol), 0)
            w = lax.broadcasted_iota(jnp.int32, (nrow, ncol), 1)
            dist = base + (r % t_len) - w
            ok = (dist >= 0) & (dist <= win) & ((dist & (dil - 1)) == 0)
            slope = jnp.zeros((nrow, ncol), F32)
            for h in range(HEADS_PER_GROUP):
                slope = jnp.where(r // t_len == h, _alibi_slope(g * HEADS_PER_GROUP + h), slope)
            return jnp.where(ok, -slope * dist.astype(F32), -jnp.inf)

        s1 = jnp.dot(q_bd, cache_ref[0:gw, :].astype(BF16), preferred_element_type=F32) + bias(wb, wb)
        pad_ok = lax.broadcasted_iota(jnp.int32, (nrow, LANES), 1) < t_len
        s2 = jnp.dot(q_bd, new_t[0:gw, :].astype(BF16), preferred_element_type=F32)
        s2 = jnp.where(pad_ok, s2 + bias(LANES, 0), -jnp.inf)
        m = jnp.maximum(jnp.max(s1, axis=-1, keepdims=True), jnp.max(s2, axis=-1, keepdims=True))
        p1 = jnp.exp(s1 - m)
        p2 = jnp.exp(s2 - m)
        den = jnp.sum(p1, axis=-1, keepdims=True) + jnp.sum(p2, axis=-1, keepdims=True)
        o = (lax.dot_general(p1.astype(BF16), cache_ref[gw:2 * gw, :].astype(BF16), _NT, preferred_element_type=F32)
             + lax.dot_general(p2.astype(BF16), new_t[gw:2 * gw, :].astype(BF16), _NT, preferred_element_type=F32))
        o_refs[g][...] = gather_heads(o / den)
        l_refs[g][...] = gather_heads(jnp.broadcast_to(m + jnp.log(den), (nrow, gw)))

        ext = jnp.concatenate([cache_ref[...], new_t], axis=1)
        n_refs[g][...] = pltpu.roll(ext, wb + LANES - t_len, axis=1)[:, 0:wb]


def _attn_sample(qkv, caches, *, t_len):
    nseq = caches[0].shape[0]
    rows = qkv.shape[0]
    ol_spec = pl.BlockSpec((t_len, GROUP_WIDTH), lambda b: (b, 0))
    ol_shape = jax.ShapeDtypeStruct((rows, GROUP_WIDTH), F32)
    cache_specs = [pl.BlockSpec((None,) + c.shape[1:], lambda b: (b, 0, 0)) for c in caches]
    outs = pl.pallas_call(
        functools.partial(_attn_sample_kernel, t_len=t_len),
        grid=(nseq,),
        in_specs=[pl.BlockSpec((t_len, 3 * ATTN_WIDTH), lambda b: (b, 0))] + cache_specs,
        out_specs=[ol_spec] * 6 + cache_specs,
        out_shape=[ol_shape] * 6 + [jax.ShapeDtypeStruct(c.shape, F32) for c in caches],
        compiler_params=_params(("parallel",)),
        name="attn_sample",
    )(qkv, *caches)
    return outs[:6], outs[6:]


def _merge_kernel(o0_ref, l0_ref, o1_ref, l1_ref, o2_ref, l2_ref, h_ref, wo_ref, out_ref):
    lses = (l0_ref[...], l1_ref[...], l2_ref[...])
    outs = (o0_ref[...], o1_ref[...], o2_ref[...])
    m = jnp.maximum(jnp.maximum(lses[0], lses[1]), lses[2])
    es = [jnp.exp(l - m) for l in lses]
    tot = es[0] + es[1] + es[2]
    merged = jnp.concatenate([o * (e / tot) for o, e in zip(outs, es)], axis=1).astype(BF16)
    out_ref[...] = h_ref[...] + jnp.dot(merged, wo_ref[...], preferred_element_type=F32)


def _merge(ols, h, wo, *, tm):
    rows = h.shape[0]
    ol_spec = pl.BlockSpec((tm, GROUP_WIDTH), lambda i: (i, 0))
    row_spec = pl.BlockSpec((tm, D_MODEL), lambda i: (i, 0))
    return pl.pallas_call(
        _merge_kernel,
        grid=(rows // tm,),
        in_specs=[ol_spec] * 6 + [row_spec, _pick_spec(wo, 0)],
        out_specs=row_spec,
        out_shape=jax.ShapeDtypeStruct((rows, D_MODEL), F32),
        compiler_params=_params(("parallel",)),
        name="attn_merge",
    )(*ols, h, wo)


LAM_EXPONENTS = (1, 2, 4, 8, 16)


def _ssm_prep_kernel(a_re_ref, a_im_ref, logdt_ref, bre_ref, bim_ref, pre_ref, pim_ref, bbre_ref, bbim_ref):
    a_re = a_re_ref[...]
    a_im = a_im_ref[...]
    dt = jnp.exp(logdt_ref[...])
    mag = jnp.exp(a_re * dt)
    ang = a_im * dt
    lam_re = mag * jnp.cos(ang)
    lam_im = mag * jnp.sin(ang)
    den = a_re * a_re + a_im * a_im
    num_re = lam_re - 1.0
    coef_re = ((num_re * a_re + lam_im * a_im) / den)[None]
    coef_im = ((lam_im * a_re - num_re * a_im) / den)[None]
    b_re = bre_ref[...]
    b_im = bim_ref[...]
    bbre_ref[...] = coef_re * b_re - coef_im * b_im
    bbim_ref[...] = coef_re * b_im + coef_im * b_re
    p_re, p_im = lam_re, lam_im
    for idx in range(len(LAM_EXPONENTS)):
        pre_ref[idx] = p_re
        pim_ref[idx] = p_im
        p_re, p_im = p_re * p_re - p_im * p_im, 2.0 * p_re * p_im


def _ssm_prep(a_re, a_im, log_dt, b_re_t, b_im_t):
    g, n = a_re.shape
    pw = jax.ShapeDtypeStruct((len(LAM_EXPONENTS), g, n), F32)
    bb = jax.ShapeDtypeStruct(b_re_t.shape, F32)
    return pl.pallas_call(
        _ssm_prep_kernel,
        out_shape=[pw, pw, bb, bb],
        name="ssm_prep",
    )(a_re, a_im, log_dt.reshape(g, 1), b_re_t, b_im_t)


def _ssm_scan_kernel(u_ref, bb_ref, cre_ref, cim_ref, lam_ref, lamc_ref, d_ref, h0_ref, y_ref, fs_ref,
                     bu_scr, xre_scr, xim_scr, ud_scr, e_scr, hp_scr, carry_scr, *, chunk, nchunk, chained):
    sl = STATE_LANES
    tile = pl.program_id(2)

    for j in range(chunk):
        ud_scr[j * nchunk:(j + 1) * nchunk, :] = u_ref[pl.ds(j, nchunk, stride=chunk), :].astype(BF16)
    bu_scr[...] = jnp.dot(ud_scr[...], bb_ref[...], preferred_element_type=F32)

    lam_re = jnp.broadcast_to(lam_ref[0:1, :], (nchunk, sl))
    lam_im = jnp.broadcast_to(lam_ref[1:2, :], (nchunk, sl))

    def step(x_re, x_im, j):
        b_re = bu_scr[j * nchunk:(j + 1) * nchunk, 0:sl]
        b_im = bu_scr[j * nchunk:(j + 1) * nchunk, sl:2 * sl]
        return (lam_re * x_re - lam_im * x_im + b_re, lam_re * x_im + lam_im * x_re + b_im)

    if chained:
        @pl.when(tile == 0)
        def _():
            carry_scr[...] = h0_ref[...]

        x_re = jnp.zeros((nchunk, sl), F32)
        x_im = jnp.zeros((nchunk, sl), F32)
        for j in range(chunk):
            x_re, x_im = step(x_re, x_im, j)
        e_scr[:, 0:sl] = x_re
        e_scr[:, sl:2 * sl] = x_im
        lc_re = lamc_ref[0:1, :]
        lc_im = lamc_ref[1:2, :]
        h_re = carry_scr[:, 0:sl]
        h_im = carry_scr[:, sl:2 * sl]
        for c in range(nchunk):
            hp_scr[c:c + 1, 0:sl] = h_re
            hp_scr[c:c + 1, sl:2 * sl] = h_im
            e_re = e_scr[c:c + 1, 0:sl]
            e_im = e_scr[c:c + 1, sl:2 * sl]
            h_re, h_im = (lc_re * h_re - lc_im * h_im + e_re, lc_re * h_im + lc_im * h_re + e_im)
        carry_scr[:, 0:sl] = h_re
        carry_scr[:, sl:2 * sl] = h_im
        fs_ref[:, 0:sl] = h_re
        fs_ref[:, sl:2 * sl] = h_im
        x_re = hp_scr[:, 0:sl]
        x_im = hp_scr[:, sl:2 * sl]
    else:
        x_re = h0_ref[:, 0:sl]
        x_im = h0_ref[:, sl:2 * sl]

    for j in range(chunk):
        x_re, x_im = step(x_re, x_im, j)
        xre_scr[j * nchunk:(j + 1) * nchunk, :] = x_re.astype(BF16)
        xim_scr[j * nchunk:(j + 1) * nchunk, :] = x_im.astype(BF16)
    if not chained:
        fs_ref[:, 0:sl] = x_re
        fs_ref[:, sl:2 * sl] = x_im

    y = (jnp.dot(xre_scr[...], cre_ref[...], preferred_element_type=F32)
         - jnp.dot(xim_scr[...], cim_ref[...], preferred_element_type=F32))
    d = d_ref[...]
    for j in range(chunk):
        y_ref[pl.ds(j, nchunk, stride=chunk), :] = (
            y[j * nchunk:(j + 1) * nchunk, :] + d * u_ref[pl.ds(j, nchunk, stride=chunk), :])


def _ssm_scan(u, bb, cre, cim, lam, lamc, dvec, h0, *, nseq, chunk, nchunk, chained):
    nblk, rows, _ = u.shape
    tm = chunk * nchunk
    tiles = rows // (nseq * tm)
    r0 = h0.shape[2]
    sl2 = 2 * STATE_LANES
    blk = lambda shape: pl.BlockSpec((None,) + shape, lambda k, b, t: (k,) + (0,) * len(shape))
    row_spec = pl.BlockSpec((None, tm, LANES), lambda k, b, t: (k, b * tiles + t, 0))
    st_spec = pl.BlockSpec((None, None, r0, sl2), lambda k, b, t: (k, b, 0, 0))
    return pl.pallas_call(
        functools.partial(_ssm_scan_kernel, chunk=chunk, nchunk=nchunk, chained=chained),
        grid=(nblk, nseq, tiles),
        in_specs=[row_spec, blk((LANES, sl2)), blk((STATE_LANES, LANES)), blk((STATE_LANES, LANES)),
                  blk((2, STATE_LANES)), blk((2, STATE_LANES)), blk((1, LANES)), st_spec],
        out_specs=[row_spec, st_spec],
        out_shape=[jax.ShapeDtypeStruct(u.shape, F32), jax.ShapeDtypeStruct(h0.shape, F32)],
        scratch_shapes=[
            pltpu.VMEM((tm, sl2), F32),
            pltpu.VMEM((tm, STATE_LANES), BF16),
            pltpu.VMEM((tm, STATE_LANES), BF16),
            pltpu.VMEM((tm, LANES), BF16),
            pltpu.VMEM((nchunk, sl2), F32),
            pltpu.VMEM((nchunk, sl2), F32),
            pltpu.VMEM((1, sl2), F32),
        ],
        compiler_params=_params(("parallel", "parallel", "arbitrary")),
        name="ssm_scan",
    )(u, bb, cre, cim, lam, lamc, dvec, h0)


def _ssm_out_kernel(y_ref, h_ref, wglu_ref, bglu_ref, wout_ref, o_ref):
    y = jnp.concatenate([y_ref[k] for k in range(y_ref.shape[0])], axis=1)
    g = jax.nn.gelu(y, approximate=True)
    gate = jnp.dot(g.astype(BF16), wglu_ref[...], preferred_element_type=F32) + bglu_ref[...]
    z = (g * jax.nn.sigmoid(gate)).astype(BF16)
    o_ref[...] = h_ref[...] + jnp.dot(z, wout_ref[...], preferred_element_type=F32)


def _ssm_out(y, h, wglu, bglu, wout, *, tm):
    rows = h.shape[0]
    row_spec = pl.BlockSpec((tm, D_MODEL), lambda i: (i, 0))
    return pl.pallas_call(
        _ssm_out_kernel,
        grid=(rows // tm,),
        in_specs=[pl.BlockSpec((y.shape[0], tm, LANES), lambda i: (0, i, 0)), row_spec,
                  _pick_spec(wglu, 0), _pick_spec(bglu, 0), _pick_spec(wout, 0)],
        out_specs=row_spec,
        out_shape=jax.ShapeDtypeStruct((rows, D_MODEL), F32),
        compiler_params=_params(("parallel",)),
        name="ssm_out",
    )(y, h, wglu, bglu, wout)


def _block_diag(x):
    kg, a, b = x.shape
    k = kg // GROUPS_PER_BLOCK
    eye = jnp.eye(GROUPS_PER_BLOCK, dtype=x.dtype)
    x = x.reshape(k, GROUPS_PER_BLOCK, a, b)
    return jnp.einsum("kgab,gh->kgahb", x, eye).reshape(k, GROUPS_PER_BLOCK * a, GROUPS_PER_BLOCK * b)


def _lane_blocks(x):
    lead = x.shape[:-2]
    x = x.reshape(lead + (SSM_LANE_BLOCKS, STATE_LANES))
    return jnp.moveaxis(x, -2, 0)


def _ssm_tables(P, j):
    b_re_t = jnp.transpose(P["ssm_b_re"][j], (2, 0, 1))
    b_im_t = jnp.transpose(P["ssm_b_im"][j], (2, 0, 1))
    p_re, p_im, bb_re, bb_im = _ssm_prep(P["ssm_a_re"][j], P["ssm_a_im"][j], P["ssm_log_dt"][j], b_re_t, b_im_t)
    bb_re = jnp.swapaxes(bb_re, 0, 1)
    bb_im = jnp.swapaxes(bb_im, 0, 1)
    bb = jnp.concatenate([_block_diag(bb_re), _block_diag(bb_im)], axis=2).astype(BF16)
    cre = _block_diag(jnp.swapaxes(P["ssm_c_re"][j], 1, 2)).astype(BF16)
    cim = _block_diag(jnp.swapaxes(P["ssm_c_im"][j], 1, 2)).astype(BF16)
    pows = jnp.stack([_lane_blocks(p_re), _lane_blocks(p_im)], axis=2)
    dvec = P["ssm_d"][j].reshape(SSM_LANE_BLOCKS, 1, LANES)
    return dict(bb=bb, cre=cre, cim=cim, pows=pows, dvec=dvec)


def _ssm_mixer(hres, W, layer, h0, *, nseq, chunk, nchunk, chained, tm):
    tabs = W["ssm"]
    u = _proj(hres, W["norms"], layer, W["ssm_w_in"], tm=tm, slabs=True)
    lam = tabs["pows"][:, LAM_EXPONENTS.index(1)]
    lamc = tabs["pows"][:, LAM_EXPONENTS.index(chunk)]
    y, fs = _ssm_scan(u, tabs["bb"], tabs["cre"], tabs["cim"], lam, lamc, tabs["dvec"], h0,
                      nseq=nseq, chunk=chunk, nchunk=nchunk, chained=chained)
    out = _ssm_out(y, hres, W["ssm_w_glu"], W["ssm_b_glu"], W["ssm_w_out"], tm=tm)
    return out, fs


def _state_from_blocks(fs):
    nblk, nseq, r0, _ = fs.shape
    fs = fs.reshape(nblk, nseq * r0, 2, GROUPS_PER_BLOCK, SSM_STATE)
    fs = jnp.transpose(fs, (1, 0, 3, 4, 2))
    return fs.reshape(nseq * r0, SSM_GROUPS, SSM_STATE, 2)


def _state_to_blocks(state):
    nseq = state.shape[0]
    s = state.reshape(nseq, SSM_LANE_BLOCKS, GROUPS_PER_BLOCK, SSM_STATE, 2)
    s = jnp.transpose(s, (1, 0, 4, 2, 3)).reshape(SSM_LANE_BLOCKS, nseq, 2 * STATE_LANES)
    return s[:, None]


PROMPT_TM = 512
SSM_CHUNK = 16
SSM_NCHUNK = 32


def _trunk(x, caches, ssm_state, W, *, nseq, seq, tm):
    ffn = lambda v, i, k, final=False: _ffn(v, W["norms"], W, i, k, tm=tm, final_norm=final)

    h = ffn(x, 0, 0)
    qkv = _proj(h, W["norms"], 0, W["qkv"], tm=tm, slabs=False)
    if caches is None:
        ols = []
        for g in range(N_GROUPS):
            ols.extend(_attn_prompt(qkv, group=g, seq=seq))
        new_caches = None
    else:
        ols, new_caches = _attn_sample(qkv, caches, t_len=seq)
    h = _merge(ols, h, W["o"], tm=tm)
    x = ffn(h, 0, 1)

    h = ffn(x, 1, 0)
    if ssm_state is None:
        h0 = jnp.zeros((SSM_LANE_BLOCKS, nseq, 1, 2 * STATE_LANES), F32)
        h, fs = _ssm_mixer(h, W, 1, h0, nseq=nseq, chunk=SSM_CHUNK, nchunk=SSM_NCHUNK, chained=True, tm=tm)
    else:
        h0 = _state_to_blocks(ssm_state)
        h, fs = _ssm_mixer(h, W, 1, h0, nseq=1, chunk=seq, nchunk=nseq, chained=False, tm=tm)
    y = ffn(h, 1, 1, final=True)
    return y, qkv, new_caches, _state_from_blocks(fs)


def kernel(x_prompt, x_sample, cache_kv_g0, cache_kv_g1, cache_kv_g2, state_ssm, norm_g, final_norm_g,
           ffn_w_gate, ffn_w_up, ffn_w_down, attn_w_qkv, attn_w_o, ssm_w_in, ssm_a_re, ssm_a_im, ssm_log_dt,
           ssm_b_re, ssm_b_im, ssm_c_re, ssm_c_im, ssm_d, ssm_w_glu, ssm_b_glu, ssm_w_out):
    batch, seq, _ = x_prompt.shape
    dec_batch, dec_seq, _ = x_sample.shape
    P = dict(ssm_a_re=ssm_a_re, ssm_a_im=ssm_a_im, ssm_log_dt=ssm_log_dt, ssm_b_re=ssm_b_re,
             ssm_b_im=ssm_b_im, ssm_c_re=ssm_c_re, ssm_c_im=ssm_c_im, ssm_d=ssm_d)
    W = dict(
        norms=norm_g.reshape(norm_g.shape[0], norm_g.shape[1], 1, D_MODEL),
        final_norm=final_norm_g.reshape(1, D_MODEL),
        gate=ffn_w_gate.astype(BF16), up=ffn_w_up.astype(BF16), down=ffn_w_down.astype(BF16),
        qkv=attn_w_qkv.astype(BF16), o=attn_w_o.astype(BF16),
        ssm_w_in=ssm_w_in.astype(BF16), ssm_w_glu=ssm_w_glu.astype(BF16), ssm_w_out=ssm_w_out.astype(BF16),
        ssm_b_glu=ssm_b_glu.reshape(ssm_b_glu.shape[0], 1, D_MODEL),
        ssm=_ssm_tables(P, 0),
    )

    y_p, qkv_p, _, ssm_p = _trunk(x_prompt.reshape(batch * seq, D_MODEL), None, None, W,
                                  nseq=batch, seq=seq, tm=PROMPT_TM)
    qkv_p = qkv_p.reshape(batch, seq, 3 * ATTN_WIDTH)
    kv_prompt = []
    for g, (win, _) in enumerate(DILATED_GROUPS):
        wb = min(win, seq)
        tail = qkv_p[:, seq - wb:]
        k0 = ATTN_WIDTH + g * GROUP_WIDTH
        v0 = 2 * ATTN_WIDTH + g * GROUP_WIDTH
        kv = jnp.stack([tail[:, :, k0:k0 + GROUP_WIDTH], tail[:, :, v0:v0 + GROUP_WIDTH]], axis=2)
        kv_prompt.append(kv.reshape(1, batch, wb, 2, HEADS_PER_GROUP, HEAD_DIM))

    caches = [jnp.transpose(c[0], (0, 2, 3, 4, 1)).reshape(dec_batch, 2 * GROUP_WIDTH, c.shape[2])
              for c in (cache_kv_g0, cache_kv_g1, cache_kv_g2)]
    y_s, _, new_caches, ssm_s = _trunk(x_sample.reshape(dec_batch * dec_seq, D_MODEL), caches, state_ssm[0], W,
                                       nseq=dec_batch, seq=dec_seq, tm=dec_batch * dec_seq)
    kv_sample = [jnp.transpose(nc.reshape(dec_batch, 2, HEADS_PER_GROUP, HEAD_DIM, nc.shape[2]), (0, 4, 1, 2, 3))[None]
                 for nc in new_caches]

    return (y_p.reshape(batch, seq, D_MODEL), y_s.reshape(dec_batch, dec_seq, D_MODEL),
            kv_prompt[0], kv_prompt[1], kv_prompt[2], ssm_p[None],
            kv_sample[0], kv_sample[1], kv_sample[2], ssm_s[None])
```

```python
import functools
import math

import jax
import jax.numpy as jnp
from jax import lax
from jax.experimental import pallas as pl
from jax.experimental.pallas import tpu as pltpu

F32 = jnp.float32
BF16 = jnp.bfloat16

D_MODEL = 1024
D_FF = 2816
HEAD_DIM = 64
HEADS_PER_GROUP = 4
DILATED_GROUPS = ((128, 1), (512, 4), (2048, 16))
N_GROUPS = len(DILATED_GROUPS)
N_HEADS = N_GROUPS * HEADS_PER_GROUP
GROUP_WIDTH = HEADS_PER_GROUP * HEAD_DIM
ATTN_WIDTH = N_HEADS * HEAD_DIM
BAND = 128
SSM_CH = 16
SSM_GROUPS = D_MODEL // SSM_CH
SSM_STATE = 64
NORM_EPS = 1e-6

LANES = 128
V7X_VMEM_BYTES = 64 * 1024 * 1024
VMEM_LIMIT = V7X_VMEM_BYTES - 8 * 1024 * 1024

SSM_LANE_BLOCKS = D_MODEL // LANES
GROUPS_PER_BLOCK = LANES // SSM_CH
STATE_LANES = GROUPS_PER_BLOCK * SSM_STATE

assert all(w // d == BAND for w, d in DILATED_GROUPS)
assert all(d & (d - 1) == 0 for _, d in DILATED_GROUPS)


def _params(sem, vmem=VMEM_LIMIT):
    return pltpu.CompilerParams(dimension_semantics=sem, vmem_limit_bytes=vmem)


def _const_spec(shape):
    nd = len(shape)
    return pl.BlockSpec(shape, lambda *_: (0,) * nd, pipeline_mode=pl.Buffered(1))


def _rms(x, g):
    return x * lax.rsqrt(jnp.mean(x * x, axis=-1, keepdims=True) + NORM_EPS) * g


MXU_TILE = 256
FFN_CHUNK_BOUNDS = (0, 6 * MXU_TILE, D_FF)


def _ffn_kernel(x_ref, g_ref, wg_ref, wu_ref, wd_ref, gf_ref, o_ref, *, final_norm):
    x = x_ref[...]
    xb = _rms(x, g_ref[...]).astype(BF16)
    acc = None
    for lo, hi in zip(FFN_CHUNK_BOUNDS[:-1], FFN_CHUNK_BOUNDS[1:]):
        gate = jnp.dot(xb, wg_ref[:, lo:hi], preferred_element_type=F32)
        up = jnp.dot(xb, wu_ref[:, lo:hi], preferred_element_type=F32)
        hmid = (gate * jax.nn.sigmoid(gate) * up).astype(BF16)
        part = jnp.dot(hmid, wd_ref[lo:hi, :], preferred_element_type=F32)
        acc = part if acc is None else acc + part
    out = x + 0.5 * acc
    if final_norm:
        out = _rms(out, gf_ref[...])
    o_ref[...] = out


def _pick_spec(arr, *idx):
    tail = arr.shape[len(idx):]
    index = tuple(idx) + (0,) * len(tail)
    return pl.BlockSpec((None,) * len(idx) + tail, lambda *_: index, pipeline_mode=pl.Buffered(1))


def _ffn(x, norms, W, layer, half, *, tm, final_norm):
    rows = x.shape[0]
    return pl.pallas_call(
        functools.partial(_ffn_kernel, final_norm=final_norm),
        grid=(rows // tm,),
        in_specs=[
            pl.BlockSpec((tm, D_MODEL), lambda i: (i, 0)),
            _pick_spec(norms, layer, 2 * half),
            _pick_spec(W["gate"], layer, half),
            _pick_spec(W["up"], layer, half),
            _pick_spec(W["down"], layer, half),
            _const_spec((1, D_MODEL)),
        ],
        out_specs=pl.BlockSpec((tm, D_MODEL), lambda i: (i, 0)),
        out_shape=jax.ShapeDtypeStruct((rows, D_MODEL), F32),
        compiler_params=_params(("parallel",)),
        name="ffn",
    )(x, norms, W["gate"], W["up"], W["down"], W["final_norm"])


def _proj_kernel(x_ref, g_ref, w_ref, o_ref, *, slabs):
    xb = _rms(x_ref[...], g_ref[...]).astype(BF16)
    y = jnp.dot(xb, w_ref[...], preferred_element_type=F32)
    if slabs:
        for k in range(y.shape[1] // LANES):
            o_ref[k] = y[:, k * LANES:(k + 1) * LANES]
    else:
        o_ref[...] = y


def _proj(x, norms, layer, w, *, tm, slabs):
    rows, n = x.shape[0], w.shape[-1]
    if slabs:
        out_spec = pl.BlockSpec((n // LANES, tm, LANES), lambda i: (0, i, 0))
        out_shape = jax.ShapeDtypeStruct((n // LANES, rows, LANES), F32)
    else:
        out_spec = pl.BlockSpec((tm, n), lambda i: (i, 0))
        out_shape = jax.ShapeDtypeStruct((rows, n), F32)
    return pl.pallas_call(
        functools.partial(_proj_kernel, slabs=slabs),
        grid=(rows // tm,),
        in_specs=[
            pl.BlockSpec((tm, D_MODEL), lambda i: (i, 0)),
            _pick_spec(norms, layer, 1),
            _pick_spec(w, 0),
        ],
        out_specs=out_spec,
        out_shape=out_shape,
        compiler_params=_params(("parallel",)),
        name="norm_proj",
    )(x, norms, w)


def _alibi_slope(head):
    return 2.0 ** (-8.0 * (head + 1) / N_HEADS)


def _head_lane_mask(hh):
    lane = lax.broadcasted_iota(jnp.int32, (1, LANES), 1)
    return (lane >= hh * HEAD_DIM) & (lane < (hh + 1) * HEAD_DIM)


ATTN_TILE_TOKENS = 2048
_NT = (((1,), (1,)), ((), ()))


def _attn_kernel(q_ref, kp_ref, kc_ref, vp_ref, vc_ref, o_ref, l_ref, *, group, nblk, tiles_per_seq):
    dil = DILATED_GROUPS[group][1]
    span = BAND * dil
    i = pl.program_id(0)
    c = pl.program_id(1)
    has_prev = (i % tiles_per_seq) != 0

    row = lax.broadcasted_iota(jnp.int32, (2 * BAND, 2 * BAND), 0)
    kj = lax.broadcasted_iota(jnp.int32, (2 * BAND, 2 * BAND), 1)
    delta = BAND + (row & (BAND - 1)) - kj
    head0 = group * HEADS_PER_GROUP
    slope_a = jnp.where(c == 0, _alibi_slope(head0), _alibi_slope(head0 + 2)).astype(F32)
    slope_b = jnp.where(c == 0, _alibi_slope(head0 + 1), _alibi_slope(head0 + 3)).astype(F32)
    slope = jnp.where(row < BAND, slope_a, slope_b)
    band = (delta >= 0) & (delta <= BAND)
    bias = jnp.where(band, -slope * (delta * dil).astype(F32), -jnp.inf)
    bias_first = jnp.where((kj >= BAND) | has_prev, bias, -jnp.inf)
    first_head = _head_lane_mask(0)

    def rows(ref, start):
        return ref[pl.ds(start, BAND), :] if dil == 1 else ref[pl.ds(start, BAND, stride=dil), :]

    for r in range(dil):
        k_prev = rows(kp_ref, r).astype(BF16)
        v_prev = rows(vp_ref, r).astype(BF16)
        for b in range(nblk):
            start = b * span + r
            q = rows(q_ref, start) * (HEAD_DIM ** -0.5)
            k_cur = rows(kc_ref, start).astype(BF16)
            v_cur = rows(vc_ref, start).astype(BF16)
            k = jnp.concatenate([k_prev, k_cur], axis=0)
            v = jnp.concatenate([v_prev, v_cur], axis=0)
            q2 = jnp.concatenate([jnp.where(first_head, q, 0.0), jnp.where(first_head, 0.0, q)], axis=0)
            s = lax.dot_general(q2.astype(BF16), k, _NT, preferred_element_type=F32)
            s = s + (bias_first if b == 0 else bias)
            m = jnp.max(s, axis=-1, keepdims=True)
            p = jnp.exp(s - m)
            den = jnp.sum(p, axis=-1, keepdims=True)
            o2 = jnp.dot(p.astype(BF16), v, preferred_element_type=F32) / den
            l2 = m + jnp.log(den)
            o = jnp.where(first_head, o2[:BAND], o2[BAND:])
            lse = jnp.where(first_head, l2[:BAND], l2[BAND:])
            if dil == 1:
                o_ref[pl.ds(start, BAND), :] = o
                l_ref[pl.ds(start, BAND), :] = lse
            else:
                o_ref[pl.ds(start, BAND, stride=dil), :] = o
                l_ref[pl.ds(start, BAND, stride=dil), :] = lse
            k_prev, v_prev = k_cur, v_cur


def _attn_prompt(qkv, *, group, seq):
    rows = qkv.shape[0]
    dil = DILATED_GROUPS[group][1]
    span = BAND * dil
    tt = ATTN_TILE_TOKENS
    nblk = tt // span
    qcol = group * GROUP_WIDTH // LANES
    kcol = (ATTN_WIDTH + group * GROUP_WIDTH) // LANES
    vcol = (2 * ATTN_WIDTH + group * GROUP_WIDTH) // LANES
    cur = lambda col: pl.BlockSpec((tt, LANES), lambda i, c: (i, col + c))
    prev = lambda col: pl.BlockSpec((span, LANES), lambda i, c: (jnp.maximum(i * nblk - 1, 0), col + c))
    out_spec = pl.BlockSpec((tt, LANES), lambda i, c: (i, c))
    out_shape = jax.ShapeDtypeStruct((rows, GROUP_WIDTH), F32)
    return pl.pallas_call(
        functools.partial(_attn_kernel, group=group, nblk=nblk, tiles_per_seq=seq // tt),
        grid=(rows // tt, GROUP_WIDTH // LANES),
        in_specs=[cur(qcol), prev(kcol), cur(kcol), prev(vcol), cur(vcol)],
        out_specs=[out_spec, out_spec],
        out_shape=[out_shape, out_shape],
        compiler_params=_params(("parallel", "parallel")),
        name=f"attn_prompt_g{group}",
    )(qkv, qkv, qkv, qkv, qkv)


def _attn_sample_kernel(qkv_ref, c0_ref, c1_ref, c2_ref,
                        o0_ref, l0_ref, o1_ref, l1_ref, o2_ref, l2_ref, n0_ref, n1_ref, n2_ref, *, t_len):
    cache_refs = (c0_ref, c1_ref, c2_ref)
    o_refs = (o0_ref, o1_ref, o2_ref)
    l_refs = (l0_ref, l1_ref, l2_ref)
    n_refs = (n0_ref, n1_ref, n2_ref)
    gw = GROUP_WIDTH
    nrow = HEADS_PER_GROUP * t_len
    row_head = lax.broadcasted_iota(jnp.int32, (nrow, gw), 0) // t_len
    col_head = lax.broadcasted_iota(jnp.int32, (nrow, gw), 1) // HEAD_DIM
    out_col_head = lax.broadcasted_iota(jnp.int32, (t_len, gw), 1) // HEAD_DIM

    def gather_heads(x):
        acc = jnp.zeros((t_len, gw), F32)
        for h in range(HEADS_PER_GROUP):
            acc = acc + jnp.where(out_col_head == h, x[h * t_len:(h + 1) * t_len, :], 0.0)
        return acc

    for g, (win, dil) in enumerate(DILATED_GROUPS):
        cache_ref = cache_refs[g]
        wb = cache_ref.shape[1]
        q0 = g * gw
        k0 = ATTN_WIDTH + g * gw
        v0 = 2 * ATTN_WIDTH + g * gw

        kv_new = jnp.concatenate([qkv_ref[:, k0:k0 + gw], qkv_ref[:, v0:v0 + gw]], axis=1)
        kv_pad = jnp.concatenate([kv_new, jnp.zeros((LANES - t_len, 2 * gw), F32)], axis=0)
        new_t = kv_pad.T

        q = qkv_ref[:, q0:q0 + gw] * (HEAD_DIM ** -0.5)
        q_bd = jnp.where(row_head == col_head, jnp.concatenate([q] * HEADS_PER_GROUP, axis=0), 0.0).astype(BF16)

        def bias(ncol, base):
            r = lax.broadcasted_iota(jnp.int32, (nrow, ncol), 0)
            w = lax.broadcasted_iota(jnp.int32, (nrow, ncol), 1)
            dist = base + (r % t_len) - w
            ok = (dist >= 0) & (dist <= win) & ((dist & (dil - 1)) == 0)
            slope = jnp.zeros((nrow, ncol), F32)
            for h in range(HEADS_PER_GROUP):
                slope = jnp.where(r // t_len == h, _alibi_slope(g * HEADS_PER_GROUP + h), slope)
            return jnp.where(ok, -slope * dist.astype(F32), -jnp.inf)

        s1 = jnp.dot(q_bd, cache_ref[0:gw, :].astype(BF16), preferred_element_type=F32) + bias(wb, wb)
        pad_ok = lax.broadcasted_iota(jnp.int32, (nrow, LANES), 1) < t_len
        s2 = jnp.dot(q_bd, new_t[0:gw, :].astype(BF16), preferred_element_type=F32)
        s2 = jnp.where(pad_ok, s2 + bias(LANES, 0), -jnp.inf)
        m = jnp.maximum(jnp.max(s1, axis=-1, keepdims=True), jnp.max(s2, axis=-1, keepdims=True))
        p1 = jnp.exp(s1 - m)
        p2 = jnp.exp(s2 - m)
        den = jnp.sum(p1, axis=-1, keepdims=True) + jnp.sum(p2, axis=-1, keepdims=True)
        o = (lax.dot_general(p1.astype(BF16), cache_ref[gw:2 * gw, :].astype(BF16), _NT, preferred_element_type=F32)
             + lax.dot_general(p2.astype(BF16), new_t[gw:2 * gw, :].astype(BF16), _NT, preferred_element_type=F32))
        o_refs[g][...] = gather_heads(o / den)
        l_refs[g][...] = gather_heads(jnp.broadcast_to(m + jnp.log(den), (nrow, gw)))

        ext = jnp.concatenate([cache_ref[...], new_t], axis=1)
        n_refs[g][...] = pltpu.roll(ext, wb + LANES - t_len, axis=1)[:, 0:wb]


def _attn_sample(qkv, caches, *, t_len):
    nseq = caches[0].shape[0]
    rows = qkv.shape[0]
    ol_spec = pl.BlockSpec((t_len, GROUP_WIDTH), lambda b: (b, 0))
    ol_shape = jax.ShapeDtypeStruct((rows, GROUP_WIDTH), F32)
    cache_specs = [pl.BlockSpec((None,) + c.shape[1:], lambda b: (b, 0, 0)) for c in caches]
    outs = pl.pallas_call(
        functools.partial(_attn_sample_kernel, t_len=t_len),
        grid=(nseq,),
        in_specs=[pl.BlockSpec((t_len, 3 * ATTN_WIDTH), lambda b: (b, 0))] + cache_specs,
        out_specs=[ol_spec] * 6 + cache_specs,
        out_shape=[ol_shape] * 6 + [jax.ShapeDtypeStruct(c.shape, F32) for c in caches],
        compiler_params=_params(("parallel",)),
        name="attn_sample",
    )(qkv, *caches)
    return outs[:6], outs[6:]


def _merge_kernel(o0_ref, l0_ref, o1_ref, l1_ref, o2_ref, l2_ref, h_ref, wo_ref, out_ref):
    lses = (l0_ref[...], l1_ref[...], l2_ref[...])
    outs = (o0_ref[...], o1_ref[...], o2_ref[...])
    m = jnp.maximum(jnp.maximum(lses[0], lses[1]), lses[2])
    es = [jnp.exp(l - m) for l in lses]
    tot = es[0] + es[1] + es[2]
    merged = jnp.concatenate([o * (e / tot) for o, e in zip(outs, es)], axis=1).astype(BF16)
    out_ref[...] = h_ref[...] + jnp.dot(merged, wo_ref[...], preferred_element_type=F32)


def _merge(ols, h, wo, *, tm):
    rows = h.shape[0]
    ol_spec = pl.BlockSpec((tm, GROUP_WIDTH), lambda i: (i, 0))
    row_spec = pl.BlockSpec((tm, D_MODEL), lambda i: (i, 0))
    return pl.pallas_call(
        _merge_kernel,
        grid=(rows // tm,),
        in_specs=[ol_spec] * 6 + [row_spec, _pick_spec(wo, 0)],
        out_specs=row_spec,
        out_shape=jax.ShapeDtypeStruct((rows, D_MODEL), F32),
        compiler_params=_params(("parallel",)),
        name="attn_merge",
    )(*ols, h, wo)


LAM_EXPONENTS = (1, 2, 4, 8, 16)


def _ssm_prep_kernel(a_re_ref, a_im_ref, logdt_ref, bre_ref, bim_ref, pre_ref, pim_ref, bbre_ref, bbim_ref):
    a_re = a_re_ref[...]
    a_im = a_im_ref[...]
    dt = jnp.exp(logdt_ref[...])
    mag = jnp.exp(a_re * dt)
    ang = a_im * dt
    lam_re = mag * jnp.cos(ang)
    lam_im = mag * jnp.sin(ang)
    den = a_re * a_re + a_im * a_im
    num_re = lam_re - 1.0
    coef_re = ((num_re * a_re + lam_im * a_im) / den)[None]
    coef_im = ((lam_im * a_re - num_re * a_im) / den)[None]
    b_re = bre_ref[...]
    b_im = bim_ref[...]
    bbre_ref[...] = coef_re * b_re - coef_im * b_im
    bbim_ref[...] = coef_re * b_im + coef_im * b_re
    p_re, p_im = lam_re, lam_im
    for idx in range(len(LAM_EXPONENTS)):
        pre_ref[idx] = p_re
        pim_ref[idx] = p_im
        p_re, p_im = p_re * p_re - p_im * p_im, 2.0 * p_re * p_im


def _ssm_prep(a_re, a_im, log_dt, b_re_t, b_im_t):
    g, n = a_re.shape
    pw = jax.ShapeDtypeStruct((len(LAM_EXPONENTS), g, n), F32)
    bb = jax.ShapeDtypeStruct(b_re_t.shape, F32)
    return pl.pallas_call(
        _ssm_prep_kernel,
        out_shape=[pw, pw, bb, bb],
        name="ssm_prep",
    )(a_re, a_im, log_dt.reshape(g, 1), b_re_t, b_im_t)


SCAN_BLOCKS_PER_STEP = 4


def _ssm_scan_kernel(*refs, chunk, nchunk, chained):
    h0_ref, carry_scr = refs[7], refs[16]
    if chained:
        @pl.when(pl.program_id(2) == 0)
        def _():
            carry_scr[...] = h0_ref[...]

    for k in range(refs[0].shape[0]):
        _ssm_scan_block(*[r.at[k] for r in refs], chunk=chunk, nchunk=nchunk, chained=chained)


def _ssm_scan_block(u_ref, bb_ref, cre_ref, cim_ref, lam_ref, lamc_ref, d_ref, h0_ref, y_ref, fs_ref,
                    bu_scr, xre_scr, xim_scr, ud_scr, e_scr, hp_scr, carry_scr, *, chunk, nchunk, chained):
    sl = STATE_LANES

    for j in range(chunk):
        ud_scr[j * nchunk:(j + 1) * nchunk, :] = u_ref[pl.ds(j, nchunk, stride=chunk), :].astype(BF16)
    bu_scr[...] = jnp.dot(ud_scr[...], bb_ref[...], preferred_element_type=F32)

    lam_re = jnp.broadcast_to(lam_ref[0:1, :], (nchunk, sl))
    lam_im = jnp.broadcast_to(lam_ref[1:2, :], (nchunk, sl))

    def step(x_re, x_im, j):
        b_re = bu_scr[j * nchunk:(j + 1) * nchunk, 0:sl]
        b_im = bu_scr[j * nchunk:(j + 1) * nchunk, sl:2 * sl]
        return (lam_re * x_re - lam_im * x_im + b_re, lam_re * x_im + lam_im * x_re + b_im)

    if chained:
        x_re = jnp.zeros((nchunk, sl), F32)
        x_im = jnp.zeros((nchunk, sl), F32)
        for j in range(chunk):
            x_re, x_im = step(x_re, x_im, j)
        e_scr[:, 0:sl] = x_re
        e_scr[:, sl:2 * sl] = x_im
        lc_re = lamc_ref[0:1, :]
        lc_im = lamc_ref[1:2, :]
        h_re = carry_scr[:, 0:sl]
        h_im = carry_scr[:, sl:2 * sl]
        for c in range(nchunk):
            hp_scr[c:c + 1, 0:sl] = h_re
            hp_scr[c:c + 1, sl:2 * sl] = h_im
            e_re = e_scr[c:c + 1, 0:sl]
            e_im = e_scr[c:c + 1, sl:2 * sl]
            h_re, h_im = (lc_re * h_re - lc_im * h_im + e_re, lc_re * h_im + lc_im * h_re + e_im)
        carry_scr[:, 0:sl] = h_re
        carry_scr[:, sl:2 * sl] = h_im
        fs_ref[:, 0:sl] = h_re
        fs_ref[:, sl:2 * sl] = h_im
        x_re = hp_scr[:, 0:sl]
        x_im = hp_scr[:, sl:2 * sl]
    else:
        x_re = h0_ref[:, 0:sl]
        x_im = h0_ref[:, sl:2 * sl]

    for j in range(chunk):
        x_re, x_im = step(x_re, x_im, j)
        xre_scr[j * nchunk:(j + 1) * nchunk, :] = x_re.astype(BF16)
        xim_scr[j * nchunk:(j + 1) * nchunk, :] = x_im.astype(BF16)
    if not chained:
        fs_ref[:, 0:sl] = x_re
        fs_ref[:, sl:2 * sl] = x_im

    y = (jnp.dot(xre_scr[...], cre_ref[...], preferred_element_type=F32)
         - jnp.dot(xim_scr[...], cim_ref[...], preferred_element_type=F32))
    d = d_ref[...]
    for j in range(chunk):
        y_ref[pl.ds(j, nchunk, stride=chunk), :] = (
            y[j * nchunk:(j + 1) * nchunk, :] + d * u_ref[pl.ds(j, nchunk, stride=chunk), :])


def _ssm_scan(u, bb, cre, cim, lam, lamc, dvec, h0, *, nseq, chunk, nchunk, chained):
    nblk, rows, _ = u.shape
    kb = SCAN_BLOCKS_PER_STEP
    tm = chunk * nchunk
    tiles = rows // (nseq * tm)
    r0 = h0.shape[2]
    sl2 = 2 * STATE_LANES
    blk = lambda shape: pl.BlockSpec((kb,) + shape, lambda k, b, t: (k,) + (0,) * len(shape))
    row_spec = pl.BlockSpec((kb, tm, LANES), lambda k, b, t: (k, b * tiles + t, 0))
    st_spec = pl.BlockSpec((kb, None, r0, sl2), lambda k, b, t: (k, b, 0, 0))
    return pl.pallas_call(
        functools.partial(_ssm_scan_kernel, chunk=chunk, nchunk=nchunk, chained=chained),
        grid=(nblk // kb, nseq, tiles),
        in_specs=[row_spec, blk((LANES, sl2)), blk((STATE_LANES, LANES)), blk((STATE_LANES, LANES)),
                  blk((2, STATE_LANES)), blk((2, STATE_LANES)), blk((1, LANES)), st_spec],
        out_specs=[row_spec, st_spec],
        out_shape=[jax.ShapeDtypeStruct(u.shape, F32), jax.ShapeDtypeStruct(h0.shape, F32)],
        scratch_shapes=[
            pltpu.VMEM((kb, tm, sl2), F32),
            pltpu.VMEM((kb, tm, STATE_LANES), BF16),
            pltpu.VMEM((kb, tm, STATE_LANES), BF16),
            pltpu.VMEM((kb, tm, LANES), BF16),
            pltpu.VMEM((kb, nchunk, sl2), F32),
            pltpu.VMEM((kb, nchunk, sl2), F32),
            pltpu.VMEM((kb, 1, sl2), F32),
        ],
        compiler_params=_params(("parallel", "parallel", "arbitrary")),
        name="ssm_scan",
    )(u, bb, cre, cim, lam, lamc, dvec, h0)


def _ssm_out_kernel(y_ref, h_ref, wglu_ref, bglu_ref, wout_ref, o_ref):
    y = jnp.concatenate([y_ref[k] for k in range(y_ref.shape[0])], axis=1)
    g = jax.nn.gelu(y, approximate=True)
    gate = jnp.dot(g.astype(BF16), wglu_ref[...], preferred_element_type=F32) + bglu_ref[...]
    z = (g * jax.nn.sigmoid(gate)).astype(BF16)
    o_ref[...] = h_ref[...] + jnp.dot(z, wout_ref[...], preferred_element_type=F32)


def _ssm_out(y, h, wglu, bglu, wout, *, tm):
    rows = h.shape[0]
    row_spec = pl.BlockSpec((tm, D_MODEL), lambda i: (i, 0))
    return pl.pallas_call(
        _ssm_out_kernel,
        grid=(rows // tm,),
        in_specs=[pl.BlockSpec((y.shape[0], tm, LANES), lambda i: (0, i, 0)), row_spec,
                  _pick_spec(wglu, 0), _pick_spec(bglu, 0), _pick_spec(wout, 0)],
        out_specs=row_spec,
        out_shape=jax.ShapeDtypeStruct((rows, D_MODEL), F32),
        compiler_params=_params(("parallel",)),
        name="ssm_out",
    )(y, h, wglu, bglu, wout)


def _block_diag(x):
    kg, a, b = x.shape
    k = kg // GROUPS_PER_BLOCK
    eye = jnp.eye(GROUPS_PER_BLOCK, dtype=x.dtype)
    x = x.reshape(k, GROUPS_PER_BLOCK, a, b)
    return jnp.einsum("kgab,gh->kgahb", x, eye).reshape(k, GROUPS_PER_BLOCK * a, GROUPS_PER_BLOCK * b)


def _lane_blocks(x):
    lead = x.shape[:-2]
    x = x.reshape(lead + (SSM_LANE_BLOCKS, STATE_LANES))
    return jnp.moveaxis(x, -2, 0)


def _ssm_tables(P, j):
    b_re_t = jnp.transpose(P["ssm_b_re"][j], (2, 0, 1))
    b_im_t = jnp.transpose(P["ssm_b_im"][j], (2, 0, 1))
    p_re, p_im, bb_re, bb_im = _ssm_prep(P["ssm_a_re"][j], P["ssm_a_im"][j], P["ssm_log_dt"][j], b_re_t, b_im_t)
    bb_re = jnp.swapaxes(bb_re, 0, 1)
    bb_im = jnp.swapaxes(bb_im, 0, 1)
    bb = jnp.concatenate([_block_diag(bb_re), _block_diag(bb_im)], axis=2).astype(BF16)
    cre = _block_diag(jnp.swapaxes(P["ssm_c_re"][j], 1, 2)).astype(BF16)
    cim = _block_diag(jnp.swapaxes(P["ssm_c_im"][j], 1, 2)).astype(BF16)
    pows = jnp.stack([_lane_blocks(p_re), _lane_blocks(p_im)], axis=2)
    dvec = P["ssm_d"][j].reshape(SSM_LANE_BLOCKS, 1, LANES)
    return dict(bb=bb, cre=cre, cim=cim, pows=pows, dvec=dvec)


def _ssm_mixer(hres, W, layer, h0, *, nseq, chunk, nchunk, chained, tm):
    tabs = W["ssm"]
    u = _proj(hres, W["norms"], layer, W["ssm_w_in"], tm=tm, slabs=True)
    lam = tabs["pows"][:, LAM_EXPONENTS.index(1)]
    lamc = tabs["pows"][:, LAM_EXPONENTS.index(chunk)]
    y, fs = _ssm_scan(u, tabs["bb"], tabs["cre"], tabs["cim"], lam, lamc, tabs["dvec"], h0,
                      nseq=nseq, chunk=chunk, nchunk=nchunk, chained=chained)
    out = _ssm_out(y, hres, W["ssm_w_glu"], W["ssm_b_glu"], W["ssm_w_out"], tm=tm)
    return out, fs


def _state_from_blocks(fs):
    nblk, nseq, r0, _ = fs.shape
    fs = fs.reshape(nblk, nseq * r0, 2, GROUPS_PER_BLOCK, SSM_STATE)
    fs = jnp.transpose(fs, (1, 0, 3, 4, 2))
    return fs.reshape(nseq * r0, SSM_GROUPS, SSM_STATE, 2)


def _state_to_blocks(state):
    nseq = state.shape[0]
    s = state.reshape(nseq, SSM_LANE_BLOCKS, GROUPS_PER_BLOCK, SSM_STATE, 2)
    s = jnp.transpose(s, (1, 0, 4, 2, 3)).reshape(SSM_LANE_BLOCKS, nseq, 2 * STATE_LANES)
    return s[:, None]


PROMPT_TM = 512
SSM_CHUNK = 16
SSM_NCHUNK = 32


def _trunk(x, caches, ssm_state, W, *, nseq, seq, tm):
    ffn = lambda v, i, k, final=False: _ffn(v, W["norms"], W, i, k, tm=tm, final_norm=final)

    h = ffn(x, 0, 0)
    qkv = _proj(h, W["norms"], 0, W["qkv"], tm=tm, slabs=False)
    if caches is None:
        ols = []
        for g in range(N_GROUPS):
            ols.extend(_attn_prompt(qkv, group=g, seq=seq))
        new_caches = None
    else:
        ols, new_caches = _attn_sample(qkv, caches, t_len=seq)
    h = _merge(ols, h, W["o"], tm=tm)
    x = ffn(h, 0, 1)

    h = ffn(x, 1, 0)
    if ssm_state is None:
        h0 = jnp.zeros((SSM_LANE_BLOCKS, nseq, 1, 2 * STATE_LANES), F32)
        h, fs = _ssm_mixer(h, W, 1, h0, nseq=nseq, chunk=SSM_CHUNK, nchunk=SSM_NCHUNK, chained=True, tm=tm)
    else:
        h0 = _state_to_blocks(ssm_state)
        h, fs = _ssm_mixer(h, W, 1, h0, nseq=1, chunk=seq, nchunk=nseq, chained=False, tm=tm)
    y = ffn(h, 1, 1, final=True)
    return y, qkv, new_caches, _state_from_blocks(fs)


def kernel(x_prompt, x_sample, cache_kv_g0, cache_kv_g1, cache_kv_g2, state_ssm, norm_g, final_norm_g,
           ffn_w_gate, ffn_w_up, ffn_w_down, attn_w_qkv, attn_w_o, ssm_w_in, ssm_a_re, ssm_a_im, ssm_log_dt,
           ssm_b_re, ssm_b_im, ssm_c_re, ssm_c_im, ssm_d, ssm_w_glu, ssm_b_glu, ssm_w_out):
    batch, seq, _ = x_prompt.shape
    dec_batch, dec_seq, _ = x_sample.shape
    P = dict(ssm_a_re=ssm_a_re, ssm_a_im=ssm_a_im, ssm_log_dt=ssm_log_dt, ssm_b_re=ssm_b_re,
             ssm_b_im=ssm_b_im, ssm_c_re=ssm_c_re, ssm_c_im=ssm_c_im, ssm_d=ssm_d)
    W = dict(
        norms=norm_g.reshape(norm_g.shape[0], norm_g.shape[1], 1, D_MODEL),
        final_norm=final_norm_g.reshape(1, D_MODEL),
        gate=ffn_w_gate.astype(BF16), up=ffn_w_up.astype(BF16), down=ffn_w_down.astype(BF16),
        qkv=attn_w_qkv.astype(BF16), o=attn_w_o.astype(BF16),
        ssm_w_in=ssm_w_in.astype(BF16), ssm_w_glu=ssm_w_glu.astype(BF16), ssm_w_out=ssm_w_out.astype(BF16),
        ssm_b_glu=ssm_b_glu.reshape(ssm_b_glu.shape[0], 1, D_MODEL),
        ssm=_ssm_tables(P, 0),
    )

    y_p, qkv_p, _, ssm_p = _trunk(x_prompt.reshape(batch * seq, D_MODEL), None, None, W,
                                  nseq=batch, seq=seq, tm=PROMPT_TM)
    qkv_p = qkv_p.reshape(batch, seq, 3 * ATTN_WIDTH)
    kv_prompt = []
    for g, (win, _) in enumerate(DILATED_GROUPS):
        wb = min(win, seq)
        tail = qkv_p[:, seq - wb:]
        k0 = ATTN_WIDTH + g * GROUP_WIDTH
        v0 = 2 * ATTN_WIDTH + g * GROUP_WIDTH
        kv = jnp.stack([tail[:, :, k0:k0 + GROUP_WIDTH], tail[:, :, v0:v0 + GROUP_WIDTH]], axis=2)
        kv_prompt.append(kv.reshape(1, batch, wb, 2, HEADS_PER_GROUP, HEAD_DIM))

    caches = [jnp.transpose(c[0], (0, 2, 3, 4, 1)).reshape(dec_batch, 2 * GROUP_WIDTH, c.shape[2])
              for c in (cache_kv_g0, cache_kv_g1, cache_kv_g2)]
    y_s, _, new_caches, ssm_s = _trunk(x_sample.reshape(dec_batch * dec_seq, D_MODEL), caches, state_ssm[0], W,
                                       nseq=dec_batch, seq=dec_seq, tm=dec_batch * dec_seq)
    kv_sample = [jnp.transpose(nc.reshape(dec_batch, 2, HEADS_PER_GROUP, HEAD_DIM, nc.shape[2]), (0, 4, 1, 2, 3))[None]
                 for nc in new_caches]

    return (y_p.reshape(batch, seq, D_MODEL), y_s.reshape(dec_batch, dec_seq, D_MODEL),
            kv_prompt[0], kv_prompt[1], kv_prompt[2], ssm_p[None],
            kv_sample[0], kv_sample[1], kv_sample[2], ssm_s[None])
```

```python
import functools
import math

import jax
import jax.numpy as jnp
from jax import lax
from jax.experimental import pallas as pl
from jax.experimental.pallas import tpu as pltpu

F32 = jnp.float32
BF16 = jnp.bfloat16

D_MODEL = 1024
D_FF = 2816
HEAD_DIM = 64
HEADS_PER_GROUP = 4
DILATED_GROUPS = ((128, 1), (512, 4), (2048, 16))
N_GROUPS = len(DILATED_GROUPS)
N_HEADS = N_GROUPS * HEADS_PER_GROUP
GROUP_WIDTH = HEADS_PER_GROUP * HEAD_DIM
ATTN_WIDTH = N_HEADS * HEAD_DIM
BAND = 128
SSM_CH = 16
SSM_GROUPS = D_MODEL // SSM_CH
SSM_STATE = 64
NORM_EPS = 1e-6

LANES = 128
V7X_VMEM_BYTES = 64 * 1024 * 1024
VMEM_LIMIT = V7X_VMEM_BYTES - 8 * 1024 * 1024

SSM_LANE_BLOCKS = D_MODEL // LANES
GROUPS_PER_BLOCK = LANES // SSM_CH
STATE_LANES = GROUPS_PER_BLOCK * SSM_STATE

assert all(w // d == BAND for w, d in DILATED_GROUPS)
assert all(d & (d - 1) == 0 for _, d in DILATED_GROUPS)


def _params(sem, vmem=VMEM_LIMIT):
    return pltpu.CompilerParams(dimension_semantics=sem, vmem_limit_bytes=vmem)


def _const_spec(shape):
    nd = len(shape)
    return pl.BlockSpec(shape, lambda *_: (0,) * nd, pipeline_mode=pl.Buffered(1))


def _rms(x, g):
    return x * lax.rsqrt(jnp.mean(x * x, axis=-1, keepdims=True) + NORM_EPS) * g


MXU_TILE = 256
FFN_CHUNK_BOUNDS = (0, 6 * MXU_TILE, D_FF)


def _ffn_kernel(x_ref, g_ref, wg_ref, wu_ref, wd_ref, gf_ref, o_ref, *, final_norm):
    x = x_ref[...]
    xb = _rms(x, g_ref[...]).astype(BF16)
    acc = None
    for lo, hi in zip(FFN_CHUNK_BOUNDS[:-1], FFN_CHUNK_BOUNDS[1:]):
        gate = jnp.dot(xb, wg_ref[:, lo:hi], preferred_element_type=F32)
        up = jnp.dot(xb, wu_ref[:, lo:hi], preferred_element_type=F32)
        hmid = (gate * jax.nn.sigmoid(gate) * up).astype(BF16)
        part = jnp.dot(hmid, wd_ref[lo:hi, :], preferred_element_type=F32)
        acc = part if acc is None else acc + part
    out = x + 0.5 * acc
    if final_norm:
        out = _rms(out, gf_ref[...])
    o_ref[...] = out


def _pick_spec(arr, *idx):
    tail = arr.shape[len(idx):]
    index = tuple(idx) + (0,) * len(tail)
    return pl.BlockSpec((None,) * len(idx) + tail, lambda *_: index, pipeline_mode=pl.Buffered(1))


def _ffn(x, norms, W, layer, half, *, tm, final_norm):
    rows = x.shape[0]
    return pl.pallas_call(
        functools.partial(_ffn_kernel, final_norm=final_norm),
        grid=(rows // tm,),
        in_specs=[
            pl.BlockSpec((tm, D_MODEL), lambda i: (i, 0)),
            _pick_spec(norms, layer, 2 * half),
            _pick_spec(W["gate"], layer, half),
            _pick_spec(W["up"], layer, half),
            _pick_spec(W["down"], layer, half),
            _const_spec((1, D_MODEL)),
        ],
        out_specs=pl.BlockSpec((tm, D_MODEL), lambda i: (i, 0)),
        out_shape=jax.ShapeDtypeStruct((rows, D_MODEL), F32),
        compiler_params=_params(("parallel",)),
        name="ffn",
    )(x, norms, W["gate"], W["up"], W["down"], W["final_norm"])


def _proj_kernel(x_ref, g_ref, w_ref, o_ref, *, slabs):
    xb = _rms(x_ref[...], g_ref[...]).astype(BF16)
    y = jnp.dot(xb, w_ref[...], preferred_element_type=F32)
    if slabs:
        for k in range(y.shape[1] // LANES):
            o_ref[k] = y[:, k * LANES:(k + 1) * LANES]
    else:
        o_ref[...] = y


def _proj(x, norms, layer, w, *, tm, slabs):
    rows, n = x.shape[0], w.shape[-1]
    if slabs:
        out_spec = pl.BlockSpec((n // LANES, tm, LANES), lambda i: (0, i, 0))
        out_shape = jax.ShapeDtypeStruct((n // LANES, rows, LANES), F32)
    else:
        out_spec = pl.BlockSpec((tm, n), lambda i: (i, 0))
        out_shape = jax.ShapeDtypeStruct((rows, n), F32)
    return pl.pallas_call(
        functools.partial(_proj_kernel, slabs=slabs),
        grid=(rows // tm,),
        in_specs=[
            pl.BlockSpec((tm, D_MODEL), lambda i: (i, 0)),
            _pick_spec(norms, layer, 1),
            _pick_spec(w, 0),
        ],
        out_specs=out_spec,
        out_shape=out_shape,
        compiler_params=_params(("parallel",)),
        name="norm_proj",
    )(x, norms, w)


def _alibi_slope(head):
    return 2.0 ** (-8.0 * (head + 1) / N_HEADS)


def _head_lane_mask(hh):
    lane = lax.broadcasted_iota(jnp.int32, (1, LANES), 1)
    return (lane >= hh * HEAD_DIM) & (lane < (hh + 1) * HEAD_DIM)


ATTN_TILE_TOKENS = 2048
_NT = (((1,), (1,)), ((), ()))


def _attn_kernel(q_ref, kp_ref, kc_ref, vp_ref, vc_ref, o_ref, l_ref, *, group, nblk, tiles_per_seq):
    dil = DILATED_GROUPS[group][1]
    span = BAND * dil
    i = pl.program_id(0)
    c = pl.program_id(1)
    has_prev = (i % tiles_per_seq) != 0

    row = lax.broadcasted_iota(jnp.int32, (2 * BAND, 2 * BAND), 0)
    kj = lax.broadcasted_iota(jnp.int32, (2 * BAND, 2 * BAND), 1)
    delta = BAND + (row & (BAND - 1)) - kj
    head0 = group * HEADS_PER_GROUP
    slope_a = jnp.where(c == 0, _alibi_slope(head0), _alibi_slope(head0 + 2)).astype(F32)
    slope_b = jnp.where(c == 0, _alibi_slope(head0 + 1), _alibi_slope(head0 + 3)).astype(F32)
    slope = jnp.where(row < BAND, slope_a, slope_b)
    band = (delta >= 0) & (delta <= BAND)
    bias = jnp.where(band, -slope * (delta * dil).astype(F32), -jnp.inf)
    bias_first = jnp.where((kj >= BAND) | has_prev, bias, -jnp.inf)
    first_head = _head_lane_mask(0)

    def rows(ref, start):
        return ref[pl.ds(start, BAND), :] if dil == 1 else ref[pl.ds(start, BAND, stride=dil), :]

    for r in range(dil):
        k_prev = rows(kp_ref, r).astype(BF16)
        v_prev = rows(vp_ref, r).astype(BF16)
        for b in range(nblk):
            start = b * span + r
            q = rows(q_ref, start) * (HEAD_DIM ** -0.5)
            k_cur = rows(kc_ref, start).astype(BF16)
            v_cur = rows(vc_ref, start).astype(BF16)
            k = jnp.concatenate([k_prev, k_cur], axis=0)
            v = jnp.concatenate([v_prev, v_cur], axis=0)
            q2 = jnp.concatenate([jnp.where(first_head, q, 0.0), jnp.where(first_head, 0.0, q)], axis=0)
            s = lax.dot_general(q2.astype(BF16), k, _NT, preferred_element_type=F32)
            s = s + (bias_first if b == 0 else bias)
            m = jnp.max(s, axis=-1, keepdims=True)
            p = jnp.exp(s - m)
            den = jnp.sum(p, axis=-1, keepdims=True)
            o2 = jnp.dot(p.astype(BF16), v, preferred_element_type=F32) / den
            l2 = m + jnp.log(den)
            o = jnp.where(first_head, o2[:BAND], o2[BAND:])
            lse = jnp.where(first_head, l2[:BAND], l2[BAND:])
            if dil == 1:
                o_ref[pl.ds(start, BAND), :] = o
                l_ref[pl.ds(start, BAND), :] = lse
            else:
                o_ref[pl.ds(start, BAND, stride=dil), :] = o
                l_ref[pl.ds(start, BAND, stride=dil), :] = lse
            k_prev, v_prev = k_cur, v_cur


def _attn_prompt(qkv, *, group, seq):
    rows = qkv.shape[0]
    dil = DILATED_GROUPS[group][1]
    span = BAND * dil
    tt = ATTN_TILE_TOKENS
    nblk = tt // span
    qcol = group * GROUP_WIDTH // LANES
    kcol = (ATTN_WIDTH + group * GROUP_WIDTH) // LANES
    vcol = (2 * ATTN_WIDTH + group * GROUP_WIDTH) // LANES
    cur = lambda col: pl.BlockSpec((tt, LANES), lambda i, c: (i, col + c))
    prev = lambda col: pl.BlockSpec((span, LANES), lambda i, c: (jnp.maximum(i * nblk - 1, 0), col + c))
    out_spec = pl.BlockSpec((tt, LANES), lambda i, c: (i, c))
    out_shape = jax.ShapeDtypeStruct((rows, GROUP_WIDTH), F32)
    return pl.pallas_call(
        functools.partial(_attn_kernel, group=group, nblk=nblk, tiles_per_seq=seq // tt),
        grid=(rows // tt, GROUP_WIDTH // LANES),
        in_specs=[cur(qcol), prev(kcol), cur(kcol), prev(vcol), cur(vcol)],
        out_specs=[out_spec, out_spec],
        out_shape=[out_shape, out_shape],
        compiler_params=_params(("parallel", "parallel")),
        name=f"attn_prompt_g{group}",
    )(qkv, qkv, qkv, qkv, qkv)


def _attn_sample_kernel(qkv_ref, c0_ref, c1_ref, c2_ref,
                        o0_ref, l0_ref, o1_ref, l1_ref, o2_ref, l2_ref, n0_ref, n1_ref, n2_ref, *, t_len):
    cache_refs = (c0_ref, c1_ref, c2_ref)
    o_refs = (o0_ref, o1_ref, o2_ref)
    l_refs = (l0_ref, l1_ref, l2_ref)
    n_refs = (n0_ref, n1_ref, n2_ref)
    gw = GROUP_WIDTH
    nrow = HEADS_PER_GROUP * t_len
    row_head = lax.broadcasted_iota(jnp.int32, (nrow, gw), 0) // t_len
    col_head = lax.broadcasted_iota(jnp.int32, (nrow, gw), 1) // HEAD_DIM
    out_col_head = lax.broadcasted_iota(jnp.int32, (t_len, gw), 1) // HEAD_DIM

    def gather_heads(x):
        acc = jnp.zeros((t_len, gw), F32)
        for h in range(HEADS_PER_GROUP):
            acc = acc + jnp.where(out_col_head == h, x[h * t_len:(h + 1) * t_len, :], 0.0)
        return acc

    for g, (win, dil) in enumerate(DILATED_GROUPS):
        cache_ref = cache_refs[g]
        wb = cache_ref.shape[1]
        q0 = g * gw
        k0 = ATTN_WIDTH + g * gw
        v0 = 2 * ATTN_WIDTH + g * gw

        kv_new = jnp.concatenate([qkv_ref[:, k0:k0 + gw], qkv_ref[:, v0:v0 + gw]], axis=1)
        kv_pad = jnp.concatenate([kv_new, jnp.zeros((LANES - t_len, 2 * gw), F32)], axis=0)
        new_t = kv_pad.T

        q = qkv_ref[:, q0:q0 + gw] * (HEAD_DIM ** -0.5)
        q_bd = jnp.where(row_head == col_head, jnp.concatenate([q] * HEADS_PER_GROUP, axis=0), 0.0).astype(BF16)

        def bias(ncol, base):
            r = lax.broadcasted_iota(jnp.int32, (nrow, ncol), 0)
            w = lax.broadcasted_iota(jnp.int32, (nrow, ncol), 1)
            dist = base + (r % t_len) - w
            ok = (dist >= 0) & (dist <= win) & ((dist & (dil - 1)) == 0)
            slope = jnp.zeros((nrow, ncol), F32)
            for h in range(HEADS_PER_GROUP):
                slope = jnp.where(r // t_len == h, _alibi_slope(g * HEADS_PER_GROUP + h), slope)
            return jnp.where(ok, -slope * dist.astype(F32), -jnp.inf)

        s1 = jnp.dot(q_bd, cache_ref[0:gw, :].astype(BF16), preferred_element_type=F32) + bias(wb, wb)
        pad_ok = lax.broadcasted_iota(jnp.int32, (nrow, LANES), 1) < t_len
        s2 = jnp.dot(q_bd, new_t[0:gw, :].astype(BF16), preferred_element_type=F32)
        s2 = jnp.where(pad_ok, s2 + bias(LANES, 0), -jnp.inf)
        m = jnp.maximum(jnp.max(s1, axis=-1, keepdims=True), jnp.max(s2, axis=-1, keepdims=True))
        p1 = jnp.exp(s1 - m)
        p2 = jnp.exp(s2 - m)
        den = jnp.sum(p1, axis=-1, keepdims=True) + jnp.sum(p2, axis=-1, keepdims=True)
        o = (lax.dot_general(p1.astype(BF16), cache_ref[gw:2 * gw, :].astype(BF16), _NT, preferred_element_type=F32)
             + lax.dot_general(p2.astype(BF16), new_t[gw:2 * gw, :].astype(BF16), _NT, preferred_element_type=F32))
        o_refs[g][...] = gather_heads(o / den)
        l_refs[g][...] = gather_heads(jnp.broadcast_to(m + jnp.log(den), (nrow, gw)))

        ext = jnp.concatenate([cache_ref[...], new_t], axis=1)
        n_refs[g][...] = pltpu.roll(ext, wb + LANES - t_len, axis=1)[:, 0:wb]


def _attn_sample(qkv, caches, *, t_len):
    nseq = caches[0].shape[0]
    rows = qkv.shape[0]
    ol_spec = pl.BlockSpec((t_len, GROUP_WIDTH), lambda b: (b, 0))
    ol_shape = jax.ShapeDtypeStruct((rows, GROUP_WIDTH), F32)
    cache_specs = [pl.BlockSpec((None,) + c.shape[1:], lambda b: (b, 0, 0)) for c in caches]
    outs = pl.pallas_call(
        functools.partial(_attn_sample_kernel, t_len=t_len),
        grid=(nseq,),
        in_specs=[pl.BlockSpec((t_len, 3 * ATTN_WIDTH), lambda b: (b, 0))] + cache_specs,
        out_specs=[ol_spec] * 6 + cache_specs,
        out_shape=[ol_shape] * 6 + [jax.ShapeDtypeStruct(c.shape, F32) for c in caches],
        compiler_params=_params(("parallel",)),
        name="attn_sample",
    )(qkv, *caches)
    return outs[:6], outs[6:]


def _merge_kernel(o0_ref, l0_ref, o1_ref, l1_ref, o2_ref, l2_ref, h_ref, wo_ref, out_ref):
    lses = (l0_ref[...], l1_ref[...], l2_ref[...])
    outs = (o0_ref[...], o1_ref[...], o2_ref[...])
    m = jnp.maximum(jnp.maximum(lses[0], lses[1]), lses[2])
    es = [jnp.exp(l - m) for l in lses]
    tot = es[0] + es[1] + es[2]
    merged = jnp.concatenate([o * (e / tot) for o, e in zip(outs, es)], axis=1).astype(BF16)
    out_ref[...] = h_ref[...] + jnp.dot(merged, wo_ref[...], preferred_element_type=F32)


def _merge(ols, h, wo, *, tm):
    rows = h.shape[0]
    ol_spec = pl.BlockSpec((tm, GROUP_WIDTH), lambda i: (i, 0))
    row_spec = pl.BlockSpec((tm, D_MODEL), lambda i: (i, 0))
    return pl.pallas_call(
        _merge_kernel,
        grid=(rows // tm,),
        in_specs=[ol_spec] * 6 + [row_spec, _pick_spec(wo, 0)],
        out_specs=row_spec,
        out_shape=jax.ShapeDtypeStruct((rows, D_MODEL), F32),
        compiler_params=_params(("parallel",)),
        name="attn_merge",
    )(*ols, h, wo)


SSM_CHUNK = 16
N_LAM_POWERS = SSM_CHUNK + 1


def _ssm_prep_kernel(a_re_ref, a_im_ref, logdt_ref, bre_ref, bim_ref, pre_ref, pim_ref, bbre_ref, bbim_ref):
    a_re = a_re_ref[...]
    a_im = a_im_ref[...]
    dt = jnp.exp(logdt_ref[...])
    mag = jnp.exp(a_re * dt)
    ang = a_im * dt
    lam_re = mag * jnp.cos(ang)
    lam_im = mag * jnp.sin(ang)
    den = a_re * a_re + a_im * a_im
    num_re = lam_re - 1.0
    coef_re = ((num_re * a_re + lam_im * a_im) / den)[None]
    coef_im = ((lam_im * a_re - num_re * a_im) / den)[None]
    b_re = bre_ref[...]
    b_im = bim_ref[...]
    bbre_ref[...] = coef_re * b_re - coef_im * b_im
    bbim_ref[...] = coef_re * b_im + coef_im * b_re
    p_re, p_im = jnp.ones_like(lam_re), jnp.zeros_like(lam_im)
    for m in range(N_LAM_POWERS):
        pre_ref[m] = p_re
        pim_ref[m] = p_im
        p_re, p_im = p_re * lam_re - p_im * lam_im, p_re * lam_im + p_im * lam_re


def _ssm_prep(a_re, a_im, log_dt, b_re_t, b_im_t):
    g, n = a_re.shape
    pw = jax.ShapeDtypeStruct((N_LAM_POWERS, g, n), F32)
    bb = jax.ShapeDtypeStruct(b_re_t.shape, F32)
    return pl.pallas_call(
        _ssm_prep_kernel,
        out_shape=[pw, pw, bb, bb],
        name="ssm_prep",
    )(a_re, a_im, log_dt.reshape(g, 1), b_re_t, b_im_t)


SCAN_BLOCKS_PER_STEP = 4


def _ssm_rows_kernel(*refs, t_len, nseq):
    for k in range(refs[0].shape[0]):
        _ssm_rows_block(*[r.at[k] for r in refs], t_len=t_len, nseq=nseq)


def _ssm_rows_block(u_ref, bb_ref, ctre_ref, ctim_ref, pw_ref, d_ref, h0_ref, y_ref, fs_ref,
                    bu_scr, xre_scr, xim_scr, ud_scr, *, t_len, nseq):
    sl = STATE_LANES
    for j in range(t_len):
        ud_scr[j * nseq:(j + 1) * nseq, :] = u_ref[pl.ds(j, nseq, stride=t_len), :].astype(BF16)
    bu_scr[...] = jnp.dot(ud_scr[...], bb_ref[...].astype(BF16), preferred_element_type=F32)

    lam_re = jnp.broadcast_to(pw_ref[1:2, :], (nseq, sl))
    lam_im = jnp.broadcast_to(pw_ref[N_LAM_POWERS + 1:N_LAM_POWERS + 2, :], (nseq, sl))
    x_re = h0_ref[:, 0:sl]
    x_im = h0_ref[:, sl:2 * sl]
    for j in range(t_len):
        b_re = bu_scr[j * nseq:(j + 1) * nseq, 0:sl]
        b_im = bu_scr[j * nseq:(j + 1) * nseq, sl:2 * sl]
        x_re, x_im = (lam_re * x_re - lam_im * x_im + b_re, lam_re * x_im + lam_im * x_re + b_im)
        xre_scr[j * nseq:(j + 1) * nseq, :] = x_re.astype(BF16)
        xim_scr[j * nseq:(j + 1) * nseq, :] = x_im.astype(BF16)
    fs_ref[:, 0:sl] = x_re
    fs_ref[:, sl:2 * sl] = x_im

    y = (lax.dot_general(xre_scr[...], ctre_ref[...].astype(BF16), _NT, preferred_element_type=F32)
         - lax.dot_general(xim_scr[...], ctim_ref[...].astype(BF16), _NT, preferred_element_type=F32))
    d = d_ref[...]
    for j in range(t_len):
        y_ref[pl.ds(j, nseq, stride=t_len), :] = (
            y[j * nseq:(j + 1) * nseq, :] + d * u_ref[pl.ds(j, nseq, stride=t_len), :])


def _ssm_scan_rows(u, tabs, h0, *, t_len, nseq):
    nblk, rows, _ = u.shape
    kb = SCAN_BLOCKS_PER_STEP
    sl2 = 2 * STATE_LANES
    blk = lambda arr: pl.BlockSpec((kb,) + arr.shape[1:], lambda k: (k,) + (0,) * (arr.ndim - 1))
    ins = (u, tabs["bb"], tabs["ct_re"], tabs["ct_im"], tabs["pw_rows"], tabs["dvec"], h0)
    return pl.pallas_call(
        functools.partial(_ssm_rows_kernel, t_len=t_len, nseq=nseq),
        grid=(nblk // kb,),
        in_specs=[blk(a) for a in ins],
        out_specs=[blk(u), blk(h0)],
        out_shape=[jax.ShapeDtypeStruct(u.shape, F32), jax.ShapeDtypeStruct(h0.shape, F32)],
        scratch_shapes=[
            pltpu.VMEM((kb, rows, sl2), F32),
            pltpu.VMEM((kb, rows, STATE_LANES), BF16),
            pltpu.VMEM((kb, rows, STATE_LANES), BF16),
            pltpu.VMEM((kb, rows, LANES), BF16),
        ],
        compiler_params=_params(("parallel",)),
        name="ssm_scan_rows",
    )(*ins)


GATHER_STRIDE = 4
assert SSM_CHUNK == GATHER_STRIDE * GATHER_STRIDE


def _ssm_seq_kernel(u_ref, bb_ref, ctre_ref, ctim_ref, pw_ref, d_ref, h0_ref, y_ref, fs_ref,
                    g_scr, kt_scr, clt_scr, ucat_scr, e_scr, hp_scr, s1_scr, s2_scr):
    L = SSM_CHUNK
    sl = STATE_LANES
    seq = u_ref.shape[0]
    nch = seq // L

    @pl.when(pl.program_id(1) == 0)
    def _build_tables():
        lam_re = pw_ref[1:2, :]
        lam_im = pw_ref[N_LAM_POWERS + 1:N_LAM_POWERS + 2, :]
        q_re = ctre_ref[...]
        q_im = ctim_ref[...]
        ccs_t = jnp.concatenate([q_re, -q_im], axis=1).astype(BF16)
        kt_scr[...] = jnp.zeros(kt_scr.shape, BF16)
        g_re = bb_ref[:, 0:sl]
        g_im = bb_ref[:, sl:2 * sl]
        for m in range(L):
            j = L - 1 - m
            g_m = jnp.concatenate([g_re, g_im], axis=1).astype(BF16)
            g_scr[j * LANES:(j + 1) * LANES, :] = g_m
            k_m = lax.dot_general(g_m, ccs_t, _NT, preferred_element_type=F32).astype(BF16)
            for jj in range(L - m):
                i = jj + m
                kt_scr[jj * LANES:(jj + 1) * LANES, i * LANES:(i + 1) * LANES] = k_m
            g_re, g_im = g_re * lam_re - g_im * lam_im, g_re * lam_im + g_im * lam_re
            q_re, q_im = q_re * lam_re - q_im * lam_im, q_re * lam_im + q_im * lam_re
            clt_scr[m * LANES:(m + 1) * LANES, 0:sl] = q_re.astype(BF16)
            clt_scr[m * LANES:(m + 1) * LANES, sl:2 * sl] = (-q_im).astype(BF16)

    qs = GATHER_STRIDE
    for b in range(qs):
        s1_scr[b] = u_ref[pl.ds(b, seq // qs, stride=qs), :]
    for j in range(L):
        a, b = divmod(j, qs)
        ucat_scr[:, j * LANES:(j + 1) * LANES] = s1_scr[b, pl.ds(a, nch, stride=qs), :].astype(BF16)
    e_scr[...] = jnp.dot(ucat_scr[...], g_scr[...], preferred_element_type=F32)

    lc_re = pw_ref[L:L + 1, :]
    lc_im = pw_ref[N_LAM_POWERS + L:N_LAM_POWERS + L + 1, :]

    def chunk_step(c, h):
        h_re, h_im = h
        hp_scr[pl.ds(c, 1), 0:sl] = h_re
        hp_scr[pl.ds(c, 1), sl:2 * sl] = h_im
        e_re = e_scr[pl.ds(c, 1), 0:sl]
        e_im = e_scr[pl.ds(c, 1), sl:2 * sl]
        return (lc_re * h_re - lc_im * h_im + e_re, lc_re * h_im + lc_im * h_re + e_im)

    h_re, h_im = lax.fori_loop(0, nch, chunk_step, (h0_ref[:, 0:sl], h0_ref[:, sl:2 * sl]), unroll=8)
    fs_ref[:, 0:sl] = h_re
    fs_ref[:, sl:2 * sl] = h_im

    h_prev = hp_scr[...].astype(BF16)
    pair = 2 * LANES
    for p in range(L // 2):
        y2 = (jnp.dot(ucat_scr[:, 0:(p + 1) * pair], kt_scr[0:(p + 1) * pair, p * pair:(p + 1) * pair],
                      preferred_element_type=F32)
              + lax.dot_general(h_prev, clt_scr[p * pair:(p + 1) * pair, :], _NT, preferred_element_type=F32))
        for half in range(2):
            a, b = divmod(2 * p + half, qs)
            s2_scr[b, pl.ds(a, nch, stride=qs), :] = y2[:, half * LANES:(half + 1) * LANES]
    d = d_ref[...]
    for b in range(qs):
        y_ref[pl.ds(b, seq // qs, stride=qs), :] = s2_scr[b] + d * s1_scr[b]


def _ssm_scan_seq(u, tabs, h0, *, nseq, seq):
    nblk = u.shape[0]
    L = SSM_CHUNK
    nch = seq // L
    sl2 = 2 * STATE_LANES
    blk = lambda arr: pl.BlockSpec((None,) + arr.shape[1:], lambda k, b: (k,) + (0,) * (arr.ndim - 1))
    row_spec = pl.BlockSpec((None, seq, LANES), lambda k, b: (k, b, 0))
    st_spec = pl.BlockSpec((None, None, 1, sl2), lambda k, b: (k, b, 0, 0))
    return pl.pallas_call(
        _ssm_seq_kernel,
        grid=(nblk, nseq),
        in_specs=[row_spec, blk(tabs["bb"]), blk(tabs["ct_re"]), blk(tabs["ct_im"]), blk(tabs["pw_rows"]),
                  blk(tabs["dvec"]), st_spec],
        out_specs=[row_spec, st_spec],
        out_shape=[jax.ShapeDtypeStruct(u.shape, F32), jax.ShapeDtypeStruct(h0.shape, F32)],
        scratch_shapes=[
            pltpu.VMEM((L * LANES, sl2), BF16),
            pltpu.VMEM((L * LANES, L * LANES), BF16),
            pltpu.VMEM((L * LANES, sl2), BF16),
            pltpu.VMEM((nch, L * LANES), BF16),
            pltpu.VMEM((nch, sl2), F32),
            pltpu.VMEM((nch, sl2), F32),
            pltpu.VMEM((GATHER_STRIDE, seq // GATHER_STRIDE, LANES), F32),
            pltpu.VMEM((GATHER_STRIDE, seq // GATHER_STRIDE, LANES), F32),
        ],
        compiler_params=_params(("arbitrary", "arbitrary")),
        name="ssm_scan_seq",
    )(u, tabs["bb"], tabs["ct_re"], tabs["ct_im"], tabs["pw_rows"], tabs["dvec"], h0)


def _ssm_out_kernel(y_ref, h_ref, wglu_ref, bglu_ref, wout_ref, o_ref):
    y = jnp.concatenate([y_ref[k] for k in range(y_ref.shape[0])], axis=1)
    g = jax.nn.gelu(y, approximate=True)
    gate = jnp.dot(g.astype(BF16), wglu_ref[...], preferred_element_type=F32) + bglu_ref[...]
    z = (g * jax.nn.sigmoid(gate)).astype(BF16)
    o_ref[...] = h_ref[...] + jnp.dot(z, wout_ref[...], preferred_element_type=F32)


def _ssm_out(y, h, wglu, bglu, wout, *, tm):
    rows = h.shape[0]
    row_spec = pl.BlockSpec((tm, D_MODEL), lambda i: (i, 0))
    return pl.pallas_call(
        _ssm_out_kernel,
        grid=(rows // tm,),
        in_specs=[pl.BlockSpec((y.shape[0], tm, LANES), lambda i: (0, i, 0)), row_spec,
                  _pick_spec(wglu, 0), _pick_spec(bglu, 0), _pick_spec(wout, 0)],
        out_specs=row_spec,
        out_shape=jax.ShapeDtypeStruct((rows, D_MODEL), F32),
        compiler_params=_params(("parallel",)),
        name="ssm_out",
    )(y, h, wglu, bglu, wout)


def _block_diag(x):
    kg, a, b = x.shape
    k = kg // GROUPS_PER_BLOCK
    eye = jnp.eye(GROUPS_PER_BLOCK, dtype=x.dtype)
    x = x.reshape(k, GROUPS_PER_BLOCK, a, b)
    return jnp.einsum("kgab,gh->kgahb", x, eye).reshape(k, GROUPS_PER_BLOCK * a, GROUPS_PER_BLOCK * b)


def _lane_blocks(x):
    lead = x.shape[:-2]
    x = x.reshape(lead + (SSM_LANE_BLOCKS, STATE_LANES))
    return jnp.moveaxis(x, -2, 0)


def _ssm_tables(P, j):
    b_re_t = jnp.transpose(P["ssm_b_re"][j], (2, 0, 1))
    b_im_t = jnp.transpose(P["ssm_b_im"][j], (2, 0, 1))
    p_re, p_im, bb_re, bb_im = _ssm_prep(P["ssm_a_re"][j], P["ssm_a_im"][j], P["ssm_log_dt"][j], b_re_t, b_im_t)
    bb_re = jnp.swapaxes(bb_re, 0, 1)
    bb_im = jnp.swapaxes(bb_im, 0, 1)
    bb = jnp.concatenate([_block_diag(bb_re), _block_diag(bb_im)], axis=2)
    ct_re = _block_diag(P["ssm_c_re"][j])
    ct_im = _block_diag(P["ssm_c_im"][j])
    pw_rows = jnp.concatenate([_lane_blocks(p_re), _lane_blocks(p_im)], axis=1)
    dvec = P["ssm_d"][j].reshape(SSM_LANE_BLOCKS, 1, LANES)
    return dict(bb=bb, ct_re=ct_re, ct_im=ct_im, pw_rows=pw_rows, dvec=dvec)


def _ssm_mixer(hres, W, layer, state, *, nseq, seq, tm):
    tabs = W["ssm"]
    u = _proj(hres, W["norms"], layer, W["ssm_w_in"], tm=tm, slabs=True)
    if state is None:
        h0 = jnp.zeros((SSM_LANE_BLOCKS, nseq, 1, 2 * STATE_LANES), F32)
        y, fs = _ssm_scan_seq(u, tabs, h0, nseq=nseq, seq=seq)
        fs = fs.reshape(SSM_LANE_BLOCKS, nseq, 2 * STATE_LANES)
    else:
        y, fs = _ssm_scan_rows(u, tabs, _state_to_blocks(state), t_len=seq, nseq=nseq)
    out = _ssm_out(y, hres, W["ssm_w_glu"], W["ssm_b_glu"], W["ssm_w_out"], tm=tm)
    return out, _state_from_blocks(fs)


def _state_from_blocks(fs):
    nblk, nseq, _ = fs.shape
    fs = fs.reshape(nblk, nseq, 2, GROUPS_PER_BLOCK, SSM_STATE)
    fs = jnp.transpose(fs, (1, 0, 3, 4, 2))
    return fs.reshape(nseq, SSM_GROUPS, SSM_STATE, 2)


def _state_to_blocks(state):
    nseq = state.shape[0]
    s = state.reshape(nseq, SSM_LANE_BLOCKS, GROUPS_PER_BLOCK, SSM_STATE, 2)
    return jnp.transpose(s, (1, 0, 4, 2, 3)).reshape(SSM_LANE_BLOCKS, nseq, 2 * STATE_LANES)


PROMPT_TM = 512


def _trunk(x, caches, ssm_state, W, *, nseq, seq, tm):
    ffn = lambda v, i, k, final=False: _ffn(v, W["norms"], W, i, k, tm=tm, final_norm=final)

    h = ffn(x, 0, 0)
    qkv = _proj(h, W["norms"], 0, W["qkv"], tm=tm, slabs=False)
    if caches is None:
        ols = []
        for g in range(N_GROUPS):
            ols.extend(_attn_prompt(qkv, group=g, seq=seq))
        new_caches = None
    else:
        ols, new_caches = _attn_sample(qkv, caches, t_len=seq)
    h = _merge(ols, h, W["o"], tm=tm)
    x = ffn(h, 0, 1)

    h = ffn(x, 1, 0)
    h, new_state = _ssm_mixer(h, W, 1, ssm_state, nseq=nseq, seq=seq, tm=tm)
    y = ffn(h, 1, 1, final=True)
    return y, qkv, new_caches, new_state


def kernel(x_prompt, x_sample, cache_kv_g0, cache_kv_g1, cache_kv_g2, state_ssm, norm_g, final_norm_g,
           ffn_w_gate, ffn_w_up, ffn_w_down, attn_w_qkv, attn_w_o, ssm_w_in, ssm_a_re, ssm_a_im, ssm_log_dt,
           ssm_b_re, ssm_b_im, ssm_c_re, ssm_c_im, ssm_d, ssm_w_glu, ssm_b_glu, ssm_w_out):
    batch, seq, _ = x_prompt.shape
    dec_batch, dec_seq, _ = x_sample.shape
    P = dict(ssm_a_re=ssm_a_re, ssm_a_im=ssm_a_im, ssm_log_dt=ssm_log_dt, ssm_b_re=ssm_b_re,
             ssm_b_im=ssm_b_im, ssm_c_re=ssm_c_re, ssm_c_im=ssm_c_im, ssm_d=ssm_d)
    W = dict(
        norms=norm_g.reshape(norm_g.shape[0], norm_g.shape[1], 1, D_MODEL),
        final_norm=final_norm_g.reshape(1, D_MODEL),
        gate=ffn_w_gate.astype(BF16), up=ffn_w_up.astype(BF16), down=ffn_w_down.astype(BF16),
        qkv=attn_w_qkv.astype(BF16), o=attn_w_o.astype(BF16),
        ssm_w_in=ssm_w_in.astype(BF16), ssm_w_glu=ssm_w_glu.astype(BF16), ssm_w_out=ssm_w_out.astype(BF16),
        ssm_b_glu=ssm_b_glu.reshape(ssm_b_glu.shape[0], 1, D_MODEL),
        ssm=_ssm_tables(P, 0),
    )

    y_p, qkv_p, _, ssm_p = _trunk(x_prompt.reshape(batch * seq, D_MODEL), None, None, W,
                                  nseq=batch, seq=seq, tm=PROMPT_TM)
    qkv_p = qkv_p.reshape(batch, seq, 3 * ATTN_WIDTH)
    kv_prompt = []
    for g, (win, _) in enumerate(DILATED_GROUPS):
        wb = min(win, seq)
        tail = qkv_p[:, seq - wb:]
        k0 = ATTN_WIDTH + g * GROUP_WIDTH
        v0 = 2 * ATTN_WIDTH + g * GROUP_WIDTH
        kv = jnp.stack([tail[:, :, k0:k0 + GROUP_WIDTH], tail[:, :, v0:v0 + GROUP_WIDTH]], axis=2)
        kv_prompt.append(kv.reshape(1, batch, wb, 2, HEADS_PER_GROUP, HEAD_DIM))

    caches = [jnp.transpose(c[0], (0, 2, 3, 4, 1)).reshape(dec_batch, 2 * GROUP_WIDTH, c.shape[2])
              for c in (cache_kv_g0, cache_kv_g1, cache_kv_g2)]
    y_s, _, new_caches, ssm_s = _trunk(x_sample.reshape(dec_batch * dec_seq, D_MODEL), caches, state_ssm[0], W,
                                       nseq=dec_batch, seq=dec_seq, tm=dec_batch * dec_seq)
    kv_sample = [jnp.transpose(nc.reshape(dec_batch, 2, HEADS_PER_GROUP, HEAD_DIM, nc.shape[2]), (0, 4, 1, 2, 3))[None]
                 for nc in new_caches]

    return (y_p.reshape(batch, seq, D_MODEL), y_s.reshape(dec_batch, dec_seq, D_MODEL),
            kv_prompt[0], kv_prompt[1], kv_prompt[2], ssm_p[None],
            kv_sample[0], kv_sample[1], kv_sample[2], ssm_s[None])
```

```python
import functools
import math

import jax
import jax.numpy as jnp
from jax import lax
from jax.experimental import pallas as pl
from jax.experimental.pallas import tpu as pltpu

F32 = jnp.float32
BF16 = jnp.bfloat16

D_MODEL = 1024
D_FF = 2816
HEAD_DIM = 64
HEADS_PER_GROUP = 4
DILATED_GROUPS = ((128, 1), (512, 4), (2048, 16))
N_GROUPS = len(DILATED_GROUPS)
N_HEADS = N_GROUPS * HEADS_PER_GROUP
GROUP_WIDTH = HEADS_PER_GROUP * HEAD_DIM
ATTN_WIDTH = N_HEADS * HEAD_DIM
BAND = 128
SSM_CH = 16
SSM_GROUPS = D_MODEL // SSM_CH
SSM_STATE = 64
NORM_EPS = 1e-6

LANES = 128
V7X_VMEM_BYTES = 64 * 1024 * 1024
VMEM_LIMIT = V7X_VMEM_BYTES - 8 * 1024 * 1024

SSM_LANE_BLOCKS = D_MODEL // LANES
GROUPS_PER_BLOCK = LANES // SSM_CH
STATE_LANES = GROUPS_PER_BLOCK * SSM_STATE

assert all(w // d == BAND for w, d in DILATED_GROUPS)
assert all(d & (d - 1) == 0 for _, d in DILATED_GROUPS)


def _params(sem, vmem=VMEM_LIMIT):
    return pltpu.CompilerParams(dimension_semantics=sem, vmem_limit_bytes=vmem)


def _const_spec(shape):
    nd = len(shape)
    return pl.BlockSpec(shape, lambda *_: (0,) * nd, pipeline_mode=pl.Buffered(1))


def _rms(x, g):
    return x * lax.rsqrt(jnp.mean(x * x, axis=-1, keepdims=True) + NORM_EPS) * g


MXU_TILE = 256
FFN_CHUNK_BOUNDS = (0, 6 * MXU_TILE, D_FF)


def _stage_kernel(*refs, pre, post):
    refs = list(refs)
    take = lambda n: [refs.pop(0) for _ in range(n)]
    (x_ref,) = take(1)
    x = x_ref[...]
    if pre == "attn":
        om_ref, wo_ref = take(2)
        x = x + jnp.dot(om_ref[...].astype(BF16), wo_ref[...], preferred_element_type=F32)
    elif pre == "ssm":
        y_ref, wglu_ref, bglu_ref, wout_ref = take(4)
        y = jnp.concatenate([y_ref[k] for k in range(y_ref.shape[0])], axis=1)
        g = jax.nn.gelu(y, approximate=True)
        gate = jnp.dot(g.astype(BF16), wglu_ref[...], preferred_element_type=F32) + bglu_ref[...]
        z = (g * jax.nn.sigmoid(gate)).astype(BF16)
        x = x + jnp.dot(z, wout_ref[...], preferred_element_type=F32)

    g_ref, wg_ref, wu_ref, wd_ref = take(4)
    xb = _rms(x, g_ref[...]).astype(BF16)
    acc = None
    for lo, hi in zip(FFN_CHUNK_BOUNDS[:-1], FFN_CHUNK_BOUNDS[1:]):
        gate = jnp.dot(xb, wg_ref[:, lo:hi], preferred_element_type=F32)
        up = jnp.dot(xb, wu_ref[:, lo:hi], preferred_element_type=F32)
        hmid = (gate * jax.nn.sigmoid(gate) * up).astype(BF16)
        part = jnp.dot(hmid, wd_ref[lo:hi, :], preferred_element_type=F32)
        acc = part if acc is None else acc + part
    out = x + 0.5 * acc

    if post == "final":
        (gf_ref,) = take(1)
        out = _rms(out, gf_ref[...])
    elif post == "proj_slabs":
        gp_ref, wp_ref = take(2)
    (o_ref,) = take(1)
    o_ref[...] = out
    if post == "proj_slabs":
        (p_ref,) = take(1)
        proj = jnp.dot(_rms(out, gp_ref[...]).astype(BF16), wp_ref[...], preferred_element_type=F32)
        for k in range(p_ref.shape[0]):
            p_ref[k] = proj[:, k * LANES:(k + 1) * LANES]


def _pick_spec(arr, *idx):
    tail = arr.shape[len(idx):]
    index = tuple(idx) + (0,) * len(tail)
    return pl.BlockSpec((None,) * len(idx) + tail, lambda *_: index, pipeline_mode=pl.Buffered(1))


def _stage(x, W, layer, half, *, tm, pre=None, pre_in=None, post=None):
    rows = x.shape[0]
    row_spec = pl.BlockSpec((tm, D_MODEL), lambda i: (i, 0))
    slab_spec = pl.BlockSpec((SSM_LANE_BLOCKS, tm, LANES), lambda i: (0, i, 0))
    ins, specs = [x], [row_spec]
    if pre == "attn":
        ins += [pre_in, W["o"]]
        specs += [pl.BlockSpec((tm, ATTN_WIDTH), lambda i: (i, 0)), _pick_spec(W["o"], 0)]
    elif pre == "ssm":
        ins += [pre_in, W["ssm_w_glu"], W["ssm_b_glu"], W["ssm_w_out"]]
        specs += [slab_spec, _pick_spec(W["ssm_w_glu"], 0), _pick_spec(W["ssm_b_glu"], 0),
                  _pick_spec(W["ssm_w_out"], 0)]
    ins += [W["norms"], W["gate"], W["up"], W["down"]]
    specs += [_pick_spec(W["norms"], layer, 2 * half), _pick_spec(W["gate"], layer, half),
              _pick_spec(W["up"], layer, half), _pick_spec(W["down"], layer, half)]
    out_specs, out_shape = [row_spec], [jax.ShapeDtypeStruct((rows, D_MODEL), F32)]
    if post == "final":
        ins += [W["final_norm"]]
        specs += [_const_spec((1, D_MODEL))]
    elif post == "proj_slabs":
        ins += [W["norms"], W["ssm_w_in"]]
        specs += [_pick_spec(W["norms"], layer, 1), _pick_spec(W["ssm_w_in"], 0)]
        out_specs.append(slab_spec)
        out_shape.append(jax.ShapeDtypeStruct((SSM_LANE_BLOCKS, rows, LANES), F32))
    outs = pl.pallas_call(
        functools.partial(_stage_kernel, pre=pre, post=post),
        grid=(rows // tm,),
        in_specs=specs,
        out_specs=out_specs,
        out_shape=out_shape,
        compiler_params=_params(("parallel",)),
        name="stage_" + "_".join(s for s in (pre, "ffn", post) if s),
    )(*ins)
    return outs if post == "proj_slabs" else outs[0]


def _proj_kernel(x_ref, g_ref, w_ref, o_ref):
    xb = _rms(x_ref[...], g_ref[...]).astype(BF16)
    o_ref[...] = jnp.dot(xb, w_ref[...], preferred_element_type=F32)


def _proj(x, norms, layer, w, *, tm):
    rows, n = x.shape[0], w.shape[-1]
    return pl.pallas_call(
        _proj_kernel,
        grid=(rows // tm,),
        in_specs=[
            pl.BlockSpec((tm, D_MODEL), lambda i: (i, 0)),
            _pick_spec(norms, layer, 1),
            _pick_spec(w, 0),
        ],
        out_specs=pl.BlockSpec((tm, n), lambda i: (i, 0)),
        out_shape=jax.ShapeDtypeStruct((rows, n), F32),
        compiler_params=_params(("parallel",)),
        name="norm_proj",
    )(x, norms, w)


def _alibi_slope(head):
    return 2.0 ** (-8.0 * (head + 1) / N_HEADS)


def _head_lane_mask(hh):
    lane = lax.broadcasted_iota(jnp.int32, (1, LANES), 1)
    return (lane >= hh * HEAD_DIM) & (lane < (hh + 1) * HEAD_DIM)


ATTN_TILE_TOKENS = 2048
_NT = (((1,), (1,)), ((), ()))


def _merge_groups(outs, lses):
    m = jnp.maximum(jnp.maximum(lses[0], lses[1]), lses[2])
    es = [jnp.exp(l - m) for l in lses]
    tot = es[0] + es[1] + es[2]
    return [o * (e / tot) for o, e in zip(outs, es)]


def _attn_kernel(*refs, tiles_per_seq):
    om_ref, o_scr, l_scr = refs[5 * N_GROUPS:]
    has_prev = (pl.program_id(0) % tiles_per_seq) != 0
    for g in range(N_GROUPS):
        _attn_group(*refs[5 * g:5 * g + 5], o_scr.at[g], l_scr.at[g], group=g, has_prev=has_prev)
    merged = _merge_groups([o_scr[g] for g in range(N_GROUPS)], [l_scr[g] for g in range(N_GROUPS)])
    for g in range(N_GROUPS):
        om_ref[:, g * LANES:(g + 1) * LANES] = merged[g].astype(BF16)


def _attn_group(q_ref, kp_ref, kc_ref, vp_ref, vc_ref, o_ref, l_ref, *, group, has_prev):
    dil = DILATED_GROUPS[group][1]
    span = BAND * dil
    nblk = q_ref.shape[0] // span
    c = pl.program_id(1)

    row = lax.broadcasted_iota(jnp.int32, (2 * BAND, 2 * BAND), 0)
    kj = lax.broadcasted_iota(jnp.int32, (2 * BAND, 2 * BAND), 1)
    delta = BAND + (row & (BAND - 1)) - kj
    head0 = group * HEADS_PER_GROUP
    slope_a = jnp.where(c == 0, _alibi_slope(head0), _alibi_slope(head0 + 2)).astype(F32)
    slope_b = jnp.where(c == 0, _alibi_slope(head0 + 1), _alibi_slope(head0 + 3)).astype(F32)
    slope = jnp.where(row < BAND, slope_a, slope_b)
    band = (delta >= 0) & (delta <= BAND)
    bias = jnp.where(band, -slope * (delta * dil).astype(F32), -jnp.inf)
    bias_first = jnp.where((kj >= BAND) | has_prev, bias, -jnp.inf)
    first_head = _head_lane_mask(0)

    def rows(ref, start):
        return ref[pl.ds(start, BAND), :] if dil == 1 else ref[pl.ds(start, BAND, stride=dil), :]

    for r in range(dil):
        k_prev = rows(kp_ref, r).astype(BF16)
        v_prev = rows(vp_ref, r).astype(BF16)
        for b in range(nblk):
            start = b * span + r
            q = rows(q_ref, start) * (HEAD_DIM ** -0.5)
            k_cur = rows(kc_ref, start).astype(BF16)
            v_cur = rows(vc_ref, start).astype(BF16)
            k = jnp.concatenate([k_prev, k_cur], axis=0)
            v = jnp.concatenate([v_prev, v_cur], axis=0)
            q2 = jnp.concatenate([jnp.where(first_head, q, 0.0), jnp.where(first_head, 0.0, q)], axis=0)
            s = lax.dot_general(q2.astype(BF16), k, _NT, preferred_element_type=F32)
            s = s + (bias_first if b == 0 else bias)
            m = jnp.max(s, axis=-1, keepdims=True)
            p = jnp.exp(s - m)
            den = jnp.sum(p, axis=-1, keepdims=True)
            o2 = jnp.dot(p.astype(BF16), v, preferred_element_type=F32) / den
            l2 = m + jnp.log(den)
            o = jnp.where(first_head, o2[:BAND], o2[BAND:])
            lse = jnp.where(first_head, l2[:BAND], l2[BAND:])
            if dil == 1:
                o_ref[pl.ds(start, BAND), :] = o
                l_ref[pl.ds(start, BAND), :] = lse
            else:
                o_ref[pl.ds(start, BAND, stride=dil), :] = o
                l_ref[pl.ds(start, BAND, stride=dil), :] = lse
            k_prev, v_prev = k_cur, v_cur


MERGED_BLOCK = N_GROUPS * LANES


def _attn_prompt(qkv, *, seq):
    rows = qkv.shape[0]
    tt = ATTN_TILE_TOKENS
    specs = []
    for g, (_, dil) in enumerate(DILATED_GROUPS):
        span = BAND * dil
        nblk = tt // span
        cur = lambda col: pl.BlockSpec((tt, LANES), lambda i, c: (i, col + c))
        prev = lambda col, span=span, nblk=nblk: pl.BlockSpec(
            (span, LANES), lambda i, c: (jnp.maximum(i * nblk - 1, 0), col + c))
        qcol = g * GROUP_WIDTH // LANES
        kcol = (ATTN_WIDTH + g * GROUP_WIDTH) // LANES
        vcol = (2 * ATTN_WIDTH + g * GROUP_WIDTH) // LANES
        specs += [cur(qcol), prev(kcol), cur(kcol), prev(vcol), cur(vcol)]
    return pl.pallas_call(
        functools.partial(_attn_kernel, tiles_per_seq=seq // tt),
        grid=(rows // tt, GROUP_WIDTH // LANES),
        in_specs=specs,
        out_specs=pl.BlockSpec((tt, MERGED_BLOCK), lambda i, c: (i, c)),
        out_shape=jax.ShapeDtypeStruct((rows, ATTN_WIDTH), BF16),
        scratch_shapes=[pltpu.VMEM((N_GROUPS, tt, LANES), F32), pltpu.VMEM((N_GROUPS, tt, LANES), F32)],
        compiler_params=_params(("parallel", "parallel")),
        name="attn_prompt",
    )(*([qkv] * len(specs)))


def _attn_sample_kernel(qkv_ref, c0_ref, c1_ref, c2_ref, om_ref, n0_ref, n1_ref, n2_ref, *, t_len):
    cache_refs = (c0_ref, c1_ref, c2_ref)
    n_refs = (n0_ref, n1_ref, n2_ref)
    outs, lses = [], []
    gw = GROUP_WIDTH
    nrow = HEADS_PER_GROUP * t_len
    row_head = lax.broadcasted_iota(jnp.int32, (nrow, gw), 0) // t_len
    col_head = lax.broadcasted_iota(jnp.int32, (nrow, gw), 1) // HEAD_DIM
    out_col_head = lax.broadcasted_iota(jnp.int32, (t_len, gw), 1) // HEAD_DIM

    def gather_heads(x):
        acc = jnp.zeros((t_len, gw), F32)
        for h in range(HEADS_PER_GROUP):
            acc = acc + jnp.where(out_col_head == h, x[h * t_len:(h + 1) * t_len, :], 0.0)
        return acc

    for g, (win, dil) in enumerate(DILATED_GROUPS):
        cache_ref = cache_refs[g]
        wb = cache_ref.shape[1]
        q0 = g * gw
        k0 = ATTN_WIDTH + g * gw
        v0 = 2 * ATTN_WIDTH + g * gw

        kv_new = jnp.concatenate([qkv_ref[:, k0:k0 + gw], qkv_ref[:, v0:v0 + gw]], axis=1)
        kv_pad = jnp.concatenate([kv_new, jnp.zeros((LANES - t_len, 2 * gw), F32)], axis=0)
        new_t = kv_pad.T

        q = qkv_ref[:, q0:q0 + gw] * (HEAD_DIM ** -0.5)
        q_bd = jnp.where(row_head == col_head, jnp.concatenate([q] * HEADS_PER_GROUP, axis=0), 0.0).astype(BF16)

        def bias(ncol, base):
            r = lax.broadcasted_iota(jnp.int32, (nrow, ncol), 0)
            w = lax.broadcasted_iota(jnp.int32, (nrow, ncol), 1)
            dist = base + (r % t_len) - w
            ok = (dist >= 0) & (dist <= win) & ((dist & (dil - 1)) == 0)
            slope = jnp.zeros((nrow, ncol), F32)
            for h in range(HEADS_PER_GROUP):
                slope = jnp.where(r // t_len == h, _alibi_slope(g * HEADS_PER_GROUP + h), slope)
            return jnp.where(ok, -slope * dist.astype(F32), -jnp.inf)

        s1 = jnp.dot(q_bd, cache_ref[0:gw, :].astype(BF16), preferred_element_type=F32) + bias(wb, wb)
        pad_ok = lax.broadcasted_iota(jnp.int32, (nrow, LANES), 1) < t_len
        s2 = jnp.dot(q_bd, new_t[0:gw, :].astype(BF16), preferred_element_type=F32)
        s2 = jnp.where(pad_ok, s2 + bias(LANES, 0), -jnp.inf)
        m = jnp.maximum(jnp.max(s1, axis=-1, keepdims=True), jnp.max(s2, axis=-1, keepdims=True))
        p1 = jnp.exp(s1 - m)
        p2 = jnp.exp(s2 - m)
        den = jnp.sum(p1, axis=-1, keepdims=True) + jnp.sum(p2, axis=-1, keepdims=True)
        o = (lax.dot_general(p1.astype(BF16), cache_ref[gw:2 * gw, :].astype(BF16), _NT, preferred_element_type=F32)
             + lax.dot_general(p2.astype(BF16), new_t[gw:2 * gw, :].astype(BF16), _NT, preferred_element_type=F32))
        outs.append(gather_heads(o / den))
        lses.append(gather_heads(jnp.broadcast_to(m + jnp.log(den), (nrow, gw))))

        ext = jnp.concatenate([cache_ref[...], new_t], axis=1)
        n_refs[g][...] = pltpu.roll(ext, wb + LANES - t_len, axis=1)[:, 0:wb]

    merged = _merge_groups(outs, lses)
    for g in range(N_GROUPS):
        for c in range(gw // LANES):
            col = c * MERGED_BLOCK + g * LANES
            om_ref[:, col:col + LANES] = merged[g][:, c * LANES:(c + 1) * LANES]


def _attn_sample(qkv, caches, *, t_len):
    nseq = caches[0].shape[0]
    rows = qkv.shape[0]
    cache_specs = [pl.BlockSpec((None,) + c.shape[1:], lambda b: (b, 0, 0)) for c in caches]
    outs = pl.pallas_call(
        functools.partial(_attn_sample_kernel, t_len=t_len),
        grid=(nseq,),
        in_specs=[pl.BlockSpec((t_len, 3 * ATTN_WIDTH), lambda b: (b, 0))] + cache_specs,
        out_specs=[pl.BlockSpec((t_len, ATTN_WIDTH), lambda b: (b, 0))] + cache_specs,
        out_shape=[jax.ShapeDtypeStruct((rows, ATTN_WIDTH), F32)]
        + [jax.ShapeDtypeStruct(c.shape, F32) for c in caches],
        compiler_params=_params(("parallel",)),
        name="attn_sample",
    )(qkv, *caches)
    return outs[0], outs[1:]


SSM_CHUNK = 16
N_LAM_POWERS = SSM_CHUNK + 1


def _ssm_prep_kernel(a_re_ref, a_im_ref, logdt_ref, bre_ref, bim_ref, pre_ref, pim_ref, bbre_ref, bbim_ref):
    a_re = a_re_ref[...]
    a_im = a_im_ref[...]
    dt = jnp.exp(logdt_ref[...])
    mag = jnp.exp(a_re * dt)
    ang = a_im * dt
    lam_re = mag * jnp.cos(ang)
    lam_im = mag * jnp.sin(ang)
    den = a_re * a_re + a_im * a_im
    num_re = lam_re - 1.0
    coef_re = ((num_re * a_re + lam_im * a_im) / den)[None]
    coef_im = ((lam_im * a_re - num_re * a_im) / den)[None]
    b_re = bre_ref[...]
    b_im = bim_ref[...]
    bbre_ref[...] = coef_re * b_re - coef_im * b_im
    bbim_ref[...] = coef_re * b_im + coef_im * b_re
    p_re, p_im = jnp.ones_like(lam_re), jnp.zeros_like(lam_im)
    for m in range(N_LAM_POWERS):
        pre_ref[m] = p_re
        pim_ref[m] = p_im
        p_re, p_im = p_re * lam_re - p_im * lam_im, p_re * lam_im + p_im * lam_re


def _ssm_prep(a_re, a_im, log_dt, b_re_t, b_im_t):
    g, n = a_re.shape
    pw = jax.ShapeDtypeStruct((N_LAM_POWERS, g, n), F32)
    bb = jax.ShapeDtypeStruct(b_re_t.shape, F32)
    return pl.pallas_call(
        _ssm_prep_kernel,
        out_shape=[pw, pw, bb, bb],
        name="ssm_prep",
    )(a_re, a_im, log_dt.reshape(g, 1), b_re_t, b_im_t)


SCAN_BLOCKS_PER_STEP = 4


def _ssm_rows_kernel(*refs, t_len, nseq):
    for k in range(refs[0].shape[0]):
        _ssm_rows_block(*[r.at[k] for r in refs], t_len=t_len, nseq=nseq)


def _ssm_rows_block(u_ref, bb_ref, ctre_ref, ctim_ref, pw_ref, d_ref, h0_ref, y_ref, fs_ref,
                    bu_scr, xre_scr, xim_scr, ud_scr, *, t_len, nseq):
    sl = STATE_LANES
    for j in range(t_len):
        ud_scr[j * nseq:(j + 1) * nseq, :] = u_ref[pl.ds(j, nseq, stride=t_len), :].astype(BF16)
    bu_scr[...] = jnp.dot(ud_scr[...], bb_ref[...].astype(BF16), preferred_element_type=F32)

    lam_re = jnp.broadcast_to(pw_ref[1:2, :], (nseq, sl))
    lam_im = jnp.broadcast_to(pw_ref[N_LAM_POWERS + 1:N_LAM_POWERS + 2, :], (nseq, sl))
    x_re = h0_ref[:, 0:sl]
    x_im = h0_ref[:, sl:2 * sl]
    for j in range(t_len):
        b_re = bu_scr[j * nseq:(j + 1) * nseq, 0:sl]
        b_im = bu_scr[j * nseq:(j + 1) * nseq, sl:2 * sl]
        x_re, x_im = (lam_re * x_re - lam_im * x_im + b_re, lam_re * x_im + lam_im * x_re + b_im)
        xre_scr[j * nseq:(j + 1) * nseq, :] = x_re.astype(BF16)
        xim_scr[j * nseq:(j + 1) * nseq, :] = x_im.astype(BF16)
    fs_ref[:, 0:sl] = x_re
    fs_ref[:, sl:2 * sl] = x_im

    y = (lax.dot_general(xre_scr[...], ctre_ref[...].astype(BF16), _NT, preferred_element_type=F32)
         - lax.dot_general(xim_scr[...], ctim_ref[...].astype(BF16), _NT, preferred_element_type=F32))
    d = d_ref[...]
    for j in range(t_len):
        y_ref[pl.ds(j, nseq, stride=t_len), :] = (
            y[j * nseq:(j + 1) * nseq, :] + d * u_ref[pl.ds(j, nseq, stride=t_len), :])


def _ssm_scan_rows(u, tabs, h0, *, t_len, nseq):
    nblk, rows, _ = u.shape
    kb = SCAN_BLOCKS_PER_STEP
    sl2 = 2 * STATE_LANES
    blk = lambda arr: pl.BlockSpec((kb,) + arr.shape[1:], lambda k: (k,) + (0,) * (arr.ndim - 1))
    ins = (u, tabs["bb"], tabs["ct_re"], tabs["ct_im"], tabs["pw_rows"], tabs["dvec"], h0)
    return pl.pallas_call(
        functools.partial(_ssm_rows_kernel, t_len=t_len, nseq=nseq),
        grid=(nblk // kb,),
        in_specs=[blk(a) for a in ins],
        out_specs=[blk(u), blk(h0)],
        out_shape=[jax.ShapeDtypeStruct(u.shape, F32), jax.ShapeDtypeStruct(h0.shape, F32)],
        scratch_shapes=[
            pltpu.VMEM((kb, rows, sl2), F32),
            pltpu.VMEM((kb, rows, STATE_LANES), BF16),
            pltpu.VMEM((kb, rows, STATE_LANES), BF16),
            pltpu.VMEM((kb, rows, LANES), BF16),
        ],
        compiler_params=_params(("parallel",)),
        name="ssm_scan_rows",
    )(*ins)


GATHER_STRIDE = 4
assert SSM_CHUNK == GATHER_STRIDE * GATHER_STRIDE


def _ssm_seq_kernel(u_ref, bb_ref, ctre_ref, ctim_ref, pw_ref, d_ref, h0_ref, y_ref, fs_ref,
                    g_scr, kt_scr, clt_scr, ucat_scr, e_scr, hp_scr, s1_scr, s2_scr):
    L = SSM_CHUNK
    sl = STATE_LANES
    seq = u_ref.shape[0]
    nch = seq // L

    @pl.when(pl.program_id(1) == 0)
    def _build_tables():
        lam_re = pw_ref[1:2, :]
        lam_im = pw_ref[N_LAM_POWERS + 1:N_LAM_POWERS + 2, :]
        q_re = ctre_ref[...]
        q_im = ctim_ref[...]
        ccs_t = jnp.concatenate([q_re, -q_im], axis=1).astype(BF16)
        kt_scr[...] = jnp.zeros(kt_scr.shape, BF16)
        g_re = bb_ref[:, 0:sl]
        g_im = bb_ref[:, sl:2 * sl]
        for m in range(L):
            j = L - 1 - m
            g_m = jnp.concatenate([g_re, g_im], axis=1).astype(BF16)
            g_scr[j * LANES:(j + 1) * LANES, :] = g_m
            k_m = lax.dot_general(g_m, ccs_t, _NT, preferred_element_type=F32).astype(BF16)
            for jj in range(L - m):
                i = jj + m
                kt_scr[jj * LANES:(jj + 1) * LANES, i * LANES:(i + 1) * LANES] = k_m
            g_re, g_im = g_re * lam_re - g_im * lam_im, g_re * lam_im + g_im * lam_re
            q_re, q_im = q_re * lam_re - q_im * lam_im, q_re * lam_im + q_im * lam_re
            clt_scr[m * LANES:(m + 1) * LANES, 0:sl] = q_re.astype(BF16)
            clt_scr[m * LANES:(m + 1) * LANES, sl:2 * sl] = (-q_im).astype(BF16)

    qs = GATHER_STRIDE
    for b in range(qs):
        s1_scr[b] = u_ref[pl.ds(b, seq // qs, stride=qs), :]
    for j in range(L):
        a, b = divmod(j, qs)
        ucat_scr[:, j * LANES:(j + 1) * LANES] = s1_scr[b, pl.ds(a, nch, stride=qs), :].astype(BF16)
    e_scr[...] = jnp.dot(ucat_scr[...], g_scr[...], preferred_element_type=F32)

    lc_re = pw_ref[L:L + 1, :]
    lc_im = pw_ref[N_LAM_POWERS + L:N_LAM_POWERS + L + 1, :]

    def chunk_step(c, h):
        h_re, h_im = h
        hp_scr[pl.ds(c, 1), 0:sl] = h_re
        hp_scr[pl.ds(c, 1), sl:2 * sl] = h_im
        e_re = e_scr[pl.ds(c, 1), 0:sl]
        e_im = e_scr[pl.ds(c, 1), sl:2 * sl]
        return (lc_re * h_re - lc_im * h_im + e_re, lc_re * h_im + lc_im * h_re + e_im)

    h_re, h_im = lax.fori_loop(0, nch, chunk_step, (h0_ref[:, 0:sl], h0_ref[:, sl:2 * sl]), unroll=8)
    fs_ref[:, 0:sl] = h_re
    fs_ref[:, sl:2 * sl] = h_im

    h_prev = hp_scr[...].astype(BF16)
    pair = 2 * LANES
    for p in range(L // 2):
        y2 = (jnp.dot(ucat_scr[:, 0:(p + 1) * pair], kt_scr[0:(p + 1) * pair, p * pair:(p + 1) * pair],
                      preferred_element_type=F32)
              + lax.dot_general(h_prev, clt_scr[p * pair:(p + 1) * pair, :], _NT, preferred_element_type=F32))
        for half in range(2):
            a, b = divmod(2 * p + half, qs)
            s2_scr[b, pl.ds(a, nch, stride=qs), :] = y2[:, half * LANES:(half + 1) * LANES]
    d = d_ref[...]
    for b in range(qs):
        y_ref[pl.ds(b, seq // qs, stride=qs), :] = s2_scr[b] + d * s1_scr[b]


def _ssm_scan_seq(u, tabs, h0, *, nseq, seq):
    nblk = u.shape[0]
    L = SSM_CHUNK
    nch = seq // L
    sl2 = 2 * STATE_LANES
    blk = lambda arr: pl.BlockSpec((None,) + arr.shape[1:], lambda k, b: (k,) + (0,) * (arr.ndim - 1))
    row_spec = pl.BlockSpec((None, seq, LANES), lambda k, b: (k, b, 0))
    st_spec = pl.BlockSpec((None, None, 1, sl2), lambda k, b: (k, b, 0, 0))
    return pl.pallas_call(
        _ssm_seq_kernel,
        grid=(nblk, nseq),
        in_specs=[row_spec, blk(tabs["bb"]), blk(tabs["ct_re"]), blk(tabs["ct_im"]), blk(tabs["pw_rows"]),
                  blk(tabs["dvec"]), st_spec],
        out_specs=[row_spec, st_spec],
        out_shape=[jax.ShapeDtypeStruct(u.shape, F32), jax.ShapeDtypeStruct(h0.shape, F32)],
        scratch_shapes=[
            pltpu.VMEM((L * LANES, sl2), BF16),
            pltpu.VMEM((L * LANES, L * LANES), BF16),
            pltpu.VMEM((L * LANES, sl2), BF16),
            pltpu.VMEM((nch, L * LANES), BF16),
            pltpu.VMEM((nch, sl2), F32),
            pltpu.VMEM((nch, sl2), F32),
            pltpu.VMEM((GATHER_STRIDE, seq // GATHER_STRIDE, LANES), F32),
            pltpu.VMEM((GATHER_STRIDE, seq // GATHER_STRIDE, LANES), F32),
        ],
        compiler_params=_params(("arbitrary", "arbitrary")),
        name="ssm_scan_seq",
    )(u, tabs["bb"], tabs["ct_re"], tabs["ct_im"], tabs["pw_rows"], tabs["dvec"], h0)


def _block_diag(x):
    kg, a, b = x.shape
    k = kg // GROUPS_PER_BLOCK
    eye = jnp.eye(GROUPS_PER_BLOCK, dtype=x.dtype)
    x = x.reshape(k, GROUPS_PER_BLOCK, a, b)
    return jnp.einsum("kgab,gh->kgahb", x, eye).reshape(k, GROUPS_PER_BLOCK * a, GROUPS_PER_BLOCK * b)


def _lane_blocks(x):
    lead = x.shape[:-2]
    x = x.reshape(lead + (SSM_LANE_BLOCKS, STATE_LANES))
    return jnp.moveaxis(x, -2, 0)


def _ssm_tables(P, j):
    b_re_t = jnp.transpose(P["ssm_b_re"][j], (2, 0, 1))
    b_im_t = jnp.transpose(P["ssm_b_im"][j], (2, 0, 1))
    p_re, p_im, bb_re, bb_im = _ssm_prep(P["ssm_a_re"][j], P["ssm_a_im"][j], P["ssm_log_dt"][j], b_re_t, b_im_t)
    bb_re = jnp.swapaxes(bb_re, 0, 1)
    bb_im = jnp.swapaxes(bb_im, 0, 1)
    bb = jnp.concatenate([_block_diag(bb_re), _block_diag(bb_im)], axis=2)
    ct_re = _block_diag(P["ssm_c_re"][j])
    ct_im = _block_diag(P["ssm_c_im"][j])
    pw_rows = jnp.concatenate([_lane_blocks(p_re), _lane_blocks(p_im)], axis=1)
    dvec = P["ssm_d"][j].reshape(SSM_LANE_BLOCKS, 1, LANES)
    return dict(bb=bb, ct_re=ct_re, ct_im=ct_im, pw_rows=pw_rows, dvec=dvec)


def _ssm_scan(u, tabs, state, *, nseq, seq):
    if state is None:
        h0 = jnp.zeros((SSM_LANE_BLOCKS, nseq, 1, 2 * STATE_LANES), F32)
        y, fs = _ssm_scan_seq(u, tabs, h0, nseq=nseq, seq=seq)
        fs = fs.reshape(SSM_LANE_BLOCKS, nseq, 2 * STATE_LANES)
    else:
        y, fs = _ssm_scan_rows(u, tabs, _state_to_blocks(state), t_len=seq, nseq=nseq)
    return y, _state_from_blocks(fs)


def _state_from_blocks(fs):
    nblk, nseq, _ = fs.shape
    fs = fs.reshape(nblk, nseq, 2, GROUPS_PER_BLOCK, SSM_STATE)
    fs = jnp.transpose(fs, (1, 0, 3, 4, 2))
    return fs.reshape(nseq, SSM_GROUPS, SSM_STATE, 2)


def _state_to_blocks(state):
    nseq = state.shape[0]
    s = state.reshape(nseq, SSM_LANE_BLOCKS, GROUPS_PER_BLOCK, SSM_STATE, 2)
    return jnp.transpose(s, (1, 0, 4, 2, 3)).reshape(SSM_LANE_BLOCKS, nseq, 2 * STATE_LANES)


PROMPT_TM = 512


def _trunk(x, caches, ssm_state, W, *, nseq, seq, tm):
    h = _stage(x, W, 0, 0, tm=tm)
    qkv = _proj(h, W["norms"], 0, W["qkv"], tm=tm)
    if caches is None:
        merged, new_caches = _attn_prompt(qkv, seq=seq), None
    else:
        merged, new_caches = _attn_sample(qkv, caches, t_len=seq)
    x = _stage(h, W, 0, 1, tm=tm, pre="attn", pre_in=merged)

    h, u = _stage(x, W, 1, 0, tm=tm, post="proj_slabs")
    y, new_state = _ssm_scan(u, W["ssm"], ssm_state, nseq=nseq, seq=seq)
    out = _stage(h, W, 1, 1, tm=tm, pre="ssm", pre_in=y, post="final")
    return out, qkv, new_caches, new_state


def kernel(x_prompt, x_sample, cache_kv_g0, cache_kv_g1, cache_kv_g2, state_ssm, norm_g, final_norm_g,
           ffn_w_gate, ffn_w_up, ffn_w_down, attn_w_qkv, attn_w_o, ssm_w_in, ssm_a_re, ssm_a_im, ssm_log_dt,
           ssm_b_re, ssm_b_im, ssm_c_re, ssm_c_im, ssm_d, ssm_w_glu, ssm_b_glu, ssm_w_out):
    batch, seq, _ = x_prompt.shape
    dec_batch, dec_seq, _ = x_sample.shape
    P = dict(ssm_a_re=ssm_a_re, ssm_a_im=ssm_a_im, ssm_log_dt=ssm_log_dt, ssm_b_re=ssm_b_re,
             ssm_b_im=ssm_b_im, ssm_c_re=ssm_c_re, ssm_c_im=ssm_c_im, ssm_d=ssm_d)
    W = dict(
        norms=norm_g.reshape(norm_g.shape[0], norm_g.shape[1], 1, D_MODEL),
        final_norm=final_norm_g.reshape(1, D_MODEL),
        gate=ffn_w_gate.astype(BF16), up=ffn_w_up.astype(BF16), down=ffn_w_down.astype(BF16),
        qkv=attn_w_qkv.astype(BF16),
        o=jnp.swapaxes(attn_w_o.reshape(-1, N_GROUPS, GROUP_WIDTH // LANES, LANES, D_MODEL), 1, 2)
        .reshape(attn_w_o.shape).astype(BF16),
        ssm_w_in=ssm_w_in.astype(BF16), ssm_w_glu=ssm_w_glu.astype(BF16), ssm_w_out=ssm_w_out.astype(BF16),
        ssm_b_glu=ssm_b_glu.reshape(ssm_b_glu.shape[0], 1, D_MODEL),
        ssm=_ssm_tables(P, 0),
    )

    y_p, qkv_p, _, ssm_p = _trunk(x_prompt.reshape(batch * seq, D_MODEL), None, None, W,
                                  nseq=batch, seq=seq, tm=PROMPT_TM)
    qkv_p = qkv_p.reshape(batch, seq, 3 * ATTN_WIDTH)
    kv_prompt = []
    for g, (win, _) in enumerate(DILATED_GROUPS):
        wb = min(win, seq)
        tail = qkv_p[:, seq - wb:]
        k0 = ATTN_WIDTH + g * GROUP_WIDTH
        v0 = 2 * ATTN_WIDTH + g * GROUP_WIDTH
        kv = jnp.stack([tail[:, :, k0:k0 + GROUP_WIDTH], tail[:, :, v0:v0 + GROUP_WIDTH]], axis=2)
        kv_prompt.append(kv.reshape(1, batch, wb, 2, HEADS_PER_GROUP, HEAD_DIM))

    caches = [jnp.transpose(c[0], (0, 2, 3, 4, 1)).reshape(dec_batch, 2 * GROUP_WIDTH, c.shape[2])
              for c in (cache_kv_g0, cache_kv_g1, cache_kv_g2)]
    y_s, _, new_caches, ssm_s = _trunk(x_sample.reshape(dec_batch * dec_seq, D_MODEL), caches, state_ssm[0], W,
                                       nseq=dec_batch, seq=dec_seq, tm=dec_batch * dec_seq)
    kv_sample = [jnp.transpose(nc.reshape(dec_batch, 2, HEADS_PER_GROUP, HEAD_DIM, nc.shape[2]), (0, 4, 1, 2, 3))[None]
                 for nc in new_caches]

    return (y_p.reshape(batch, seq, D_MODEL), y_s.reshape(dec_batch, dec_seq, D_MODEL),
            kv_prompt[0], kv_prompt[1], kv_prompt[2], ssm_p[None],
            kv_sample[0], kv_sample[1], kv_sample[2], ssm_s[None])
```

```python
import functools
import math

import jax
import jax.numpy as jnp
from jax import lax
from jax.experimental import pallas as pl
from jax.experimental.pallas import tpu as pltpu

F32 = jnp.float32
BF16 = jnp.bfloat16

D_MODEL = 1024
D_FF = 2816
HEAD_DIM = 64
HEADS_PER_GROUP = 4
DILATED_GROUPS = ((128, 1), (512, 4), (2048, 16))
N_GROUPS = len(DILATED_GROUPS)
N_HEADS = N_GROUPS * HEADS_PER_GROUP
GROUP_WIDTH = HEADS_PER_GROUP * HEAD_DIM
ATTN_WIDTH = N_HEADS * HEAD_DIM
BAND = 128
SSM_CH = 16
SSM_GROUPS = D_MODEL // SSM_CH
SSM_STATE = 64
NORM_EPS = 1e-6

LANES = 128
V7X_VMEM_BYTES = 64 * 1024 * 1024
VMEM_LIMIT = V7X_VMEM_BYTES - 8 * 1024 * 1024

SSM_LANE_BLOCKS = D_MODEL // LANES
GROUPS_PER_BLOCK = LANES // SSM_CH
STATE_LANES = GROUPS_PER_BLOCK * SSM_STATE

assert all(w // d == BAND for w, d in DILATED_GROUPS)
assert all(d & (d - 1) == 0 for _, d in DILATED_GROUPS)


def _params(sem, vmem=VMEM_LIMIT):
    return pltpu.CompilerParams(dimension_semantics=sem, vmem_limit_bytes=vmem)


def _const_spec(shape):
    nd = len(shape)
    return pl.BlockSpec(shape, lambda *_: (0,) * nd, pipeline_mode=pl.Buffered(1))


def _rms(x, g):
    return x * lax.rsqrt(jnp.mean(x * x, axis=-1, keepdims=True) + NORM_EPS) * g


MXU_TILE = 256
FFN_CHUNK_BOUNDS = (0, 6 * MXU_TILE, D_FF)


N_STREAMS = 2


def _stage_kernel(*refs, pre, post):
    refs = list(refs)
    take = lambda n: [refs.pop(0) for _ in range(n)]
    x_refs = take(N_STREAMS)
    if pre == "attn":
        om_refs = take(N_STREAMS)
        (wo_ref,) = take(1)
    elif pre == "ssm":
        y_refs = take(N_STREAMS)
        wglu_ref, bglu_ref, wout_ref = take(3)
    g_ref, wg_ref, wu_ref, wd_ref = take(4)
    if post == "final":
        (gf_ref,) = take(1)
    elif post == "proj_slabs":
        gp_ref, wp_ref = take(2)
    o_refs = take(N_STREAMS)
    if post == "proj_slabs":
        p_refs = take(N_STREAMS)

    def run(s):
        x = x_refs[s][...]
        if pre == "attn":
            x = x + jnp.dot(om_refs[s][...].astype(BF16), wo_ref[...], preferred_element_type=F32)
        elif pre == "ssm":
            y = jnp.concatenate([y_refs[s][k] for k in range(y_refs[s].shape[0])], axis=1)
            g = jax.nn.gelu(y, approximate=True)
            gate = jnp.dot(g.astype(BF16), wglu_ref[...], preferred_element_type=F32) + bglu_ref[...]
            z = (g * jax.nn.sigmoid(gate)).astype(BF16)
            x = x + jnp.dot(z, wout_ref[...], preferred_element_type=F32)

        xb = _rms(x, g_ref[...]).astype(BF16)
        acc = None
        for lo, hi in zip(FFN_CHUNK_BOUNDS[:-1], FFN_CHUNK_BOUNDS[1:]):
            gate = jnp.dot(xb, wg_ref[:, lo:hi], preferred_element_type=F32)
            up = jnp.dot(xb, wu_ref[:, lo:hi], preferred_element_type=F32)
            hmid = (gate * jax.nn.sigmoid(gate) * up).astype(BF16)
            part = jnp.dot(hmid, wd_ref[lo:hi, :], preferred_element_type=F32)
            acc = part if acc is None else acc + part
        out = x + 0.5 * acc

        if post == "final":
            out = _rms(out, gf_ref[...])
        o_refs[s][...] = out
        if post == "proj_slabs":
            proj = jnp.dot(_rms(out, gp_ref[...]).astype(BF16), wp_ref[...], preferred_element_type=F32)
            for k in range(p_refs[s].shape[0]):
                p_refs[s][k] = proj[:, k * LANES:(k + 1) * LANES]

    run(0)
    pl.when(pl.program_id(0) == 0)(lambda: run(1))


def _pick_spec(arr, *idx):
    tail = arr.shape[len(idx):]
    index = tuple(idx) + (0,) * len(tail)
    return pl.BlockSpec((None,) * len(idx) + tail, lambda *_: index, pipeline_mode=pl.Buffered(1))


STAGE_TM = 512


def _whole_spec(shape):
    nd = len(shape)
    return pl.BlockSpec(shape, lambda *_: (0,) * nd)


def _row_specs(xs):
    return [pl.BlockSpec((STAGE_TM, xs[0].shape[1]), lambda i: (i, 0)), _whole_spec(xs[1].shape)]


def _slab_specs(ys):
    return [pl.BlockSpec((SSM_LANE_BLOCKS, STAGE_TM, LANES), lambda i: (0, i, 0)), _whole_spec(ys[1].shape)]


def _stage(xs, W, layer, half, *, pre=None, pre_ins=None, post=None):
    ins, specs = list(xs), _row_specs(xs)
    if pre == "attn":
        ins += list(pre_ins) + [W["o"]]
        specs += _row_specs(pre_ins) + [_pick_spec(W["o"], 0)]
    elif pre == "ssm":
        ins += list(pre_ins) + [W["ssm_w_glu"], W["ssm_b_glu"], W["ssm_w_out"]]
        specs += _slab_specs(pre_ins) + [_pick_spec(W["ssm_w_glu"], 0), _pick_spec(W["ssm_b_glu"], 0),
                                         _pick_spec(W["ssm_w_out"], 0)]
    ins += [W["norms"], W["gate"], W["up"], W["down"]]
    specs += [_pick_spec(W["norms"], layer, 2 * half), _pick_spec(W["gate"], layer, half),
              _pick_spec(W["up"], layer, half), _pick_spec(W["down"], layer, half)]
    out_specs = _row_specs(xs)
    out_shape = [jax.ShapeDtypeStruct(x.shape, F32) for x in xs]
    if post == "final":
        ins += [W["final_norm"]]
        specs += [_const_spec((1, D_MODEL))]
    elif post == "proj_slabs":
        ins += [W["norms"], W["ssm_w_in"]]
        specs += [_pick_spec(W["norms"], layer, 1), _pick_spec(W["ssm_w_in"], 0)]
        slabs = [jax.ShapeDtypeStruct((SSM_LANE_BLOCKS, x.shape[0], LANES), F32) for x in xs]
        out_specs += _slab_specs(slabs)
        out_shape += slabs
    outs = pl.pallas_call(
        functools.partial(_stage_kernel, pre=pre, post=post),
        grid=(xs[0].shape[0] // STAGE_TM,),
        in_specs=specs,
        out_specs=out_specs,
        out_shape=out_shape,
        compiler_params=_params(("arbitrary",)),
        name="stage_" + "_".join(s for s in (pre, "ffn", post) if s),
    )(*ins)
    return (outs[:N_STREAMS], outs[N_STREAMS:]) if post == "proj_slabs" else outs


def _proj_kernel(x0_ref, x1_ref, g_ref, w_ref, o0_ref, o1_ref):
    def run(x_ref, o_ref):
        xb = _rms(x_ref[...], g_ref[...]).astype(BF16)
        o_ref[...] = jnp.dot(xb, w_ref[...], preferred_element_type=F32)

    run(x0_ref, o0_ref)
    pl.when(pl.program_id(0) == 0)(lambda: run(x1_ref, o1_ref))


def _proj(xs, norms, layer, w):
    n = w.shape[-1]
    out_shape = [jax.ShapeDtypeStruct((x.shape[0], n), F32) for x in xs]
    return pl.pallas_call(
        _proj_kernel,
        grid=(xs[0].shape[0] // STAGE_TM,),
        in_specs=_row_specs(xs) + [_pick_spec(norms, layer, 1), _pick_spec(w, 0)],
        out_specs=_row_specs(out_shape),
        out_shape=out_shape,
        compiler_params=_params(("arbitrary",)),
        name="norm_proj",
    )(*xs, norms, w)


def _alibi_slope(head):
    return 2.0 ** (-8.0 * (head + 1) / N_HEADS)


def _head_lane_mask(hh):
    lane = lax.broadcasted_iota(jnp.int32, (1, LANES), 1)
    return (lane >= hh * HEAD_DIM) & (lane < (hh + 1) * HEAD_DIM)


ATTN_TILE_TOKENS = 2048
_NT = (((1,), (1,)), ((), ()))


def _merge_groups(outs, lses):
    m = jnp.maximum(jnp.maximum(lses[0], lses[1]), lses[2])
    es = [jnp.exp(l - m) for l in lses]
    tot = es[0] + es[1] + es[2]
    return [o * (e / tot) for o, e in zip(outs, es)]


def _attn_kernel(*refs, tiles_per_seq):
    om_ref, o_scr, l_scr = refs[5 * N_GROUPS:]
    has_prev = (pl.program_id(0) % tiles_per_seq) != 0
    for g in range(N_GROUPS):
        _attn_group(*refs[5 * g:5 * g + 5], o_scr.at[g], l_scr.at[g], group=g, has_prev=has_prev)
    merged = _merge_groups([o_scr[g] for g in range(N_GROUPS)], [l_scr[g] for g in range(N_GROUPS)])
    for g in range(N_GROUPS):
        om_ref[:, g * LANES:(g + 1) * LANES] = merged[g].astype(BF16)


def _attn_group(q_ref, kp_ref, kc_ref, vp_ref, vc_ref, o_ref, l_ref, *, group, has_prev):
    dil = DILATED_GROUPS[group][1]
    span = BAND * dil
    nblk = q_ref.shape[0] // span
    c = pl.program_id(1)

    row = lax.broadcasted_iota(jnp.int32, (2 * BAND, 2 * BAND), 0)
    kj = lax.broadcasted_iota(jnp.int32, (2 * BAND, 2 * BAND), 1)
    delta = BAND + (row & (BAND - 1)) - kj
    head0 = group * HEADS_PER_GROUP
    slope_a = jnp.where(c == 0, _alibi_slope(head0), _alibi_slope(head0 + 2)).astype(F32)
    slope_b = jnp.where(c == 0, _alibi_slope(head0 + 1), _alibi_slope(head0 + 3)).astype(F32)
    slope = jnp.where(row < BAND, slope_a, slope_b)
    band = (delta >= 0) & (delta <= BAND)
    bias = jnp.where(band, -slope * (delta * dil).astype(F32), -jnp.inf)
    bias_first = jnp.where((kj >= BAND) | has_prev, bias, -jnp.inf)
    first_head = _head_lane_mask(0)

    def rows(ref, start):
        return ref[pl.ds(start, BAND), :] if dil == 1 else ref[pl.ds(start, BAND, stride=dil), :]

    for r in range(dil):
        k_prev = rows(kp_ref, r).astype(BF16)
        v_prev = rows(vp_ref, r).astype(BF16)
        for b in range(nblk):
            start = b * span + r
            q = rows(q_ref, start) * (HEAD_DIM ** -0.5)
            k_cur = rows(kc_ref, start).astype(BF16)
            v_cur = rows(vc_ref, start).astype(BF16)
            k = jnp.concatenate([k_prev, k_cur], axis=0)
            v = jnp.concatenate([v_prev, v_cur], axis=0)
            q2 = jnp.concatenate([jnp.where(first_head, q, 0.0), jnp.where(first_head, 0.0, q)], axis=0)
            s = lax.dot_general(q2.astype(BF16), k, _NT, preferred_element_type=F32)
            s = s + (bias_first if b == 0 else bias)
            m = jnp.max(s, axis=-1, keepdims=True)
            p = jnp.exp(s - m)
            den = jnp.sum(p, axis=-1, keepdims=True)
            o2 = jnp.dot(p.astype(BF16), v, preferred_element_type=F32) / den
            l2 = m + jnp.log(den)
            o = jnp.where(first_head, o2[:BAND], o2[BAND:])
            lse = jnp.where(first_head, l2[:BAND], l2[BAND:])
            if dil == 1:
                o_ref[pl.ds(start, BAND), :] = o
                l_ref[pl.ds(start, BAND), :] = lse
            else:
                o_ref[pl.ds(start, BAND, stride=dil), :] = o
                l_ref[pl.ds(start, BAND, stride=dil), :] = lse
            k_prev, v_prev = k_cur, v_cur


MERGED_BLOCK = N_GROUPS * LANES


def _attn_prompt(qkv, *, seq):
    rows = qkv.shape[0]
    tt = ATTN_TILE_TOKENS
    specs = []
    for g, (_, dil) in enumerate(DILATED_GROUPS):
        span = BAND * dil
        nblk = tt // span
        cur = lambda col: pl.BlockSpec((tt, LANES), lambda i, c: (i, col + c))
        prev = lambda col, span=span, nblk=nblk: pl.BlockSpec(
            (span, LANES), lambda i, c: (jnp.maximum(i * nblk - 1, 0), col + c))
        qcol = g * GROUP_WIDTH // LANES
        kcol = (ATTN_WIDTH + g * GROUP_WIDTH) // LANES
        vcol = (2 * ATTN_WIDTH + g * GROUP_WIDTH) // LANES
        specs += [cur(qcol), prev(kcol), cur(kcol), prev(vcol), cur(vcol)]
    return pl.pallas_call(
        functools.partial(_attn_kernel, tiles_per_seq=seq // tt),
        grid=(rows // tt, GROUP_WIDTH // LANES),
        in_specs=specs,
        out_specs=pl.BlockSpec((tt, MERGED_BLOCK), lambda i, c: (i, c)),
        out_shape=jax.ShapeDtypeStruct((rows, ATTN_WIDTH), BF16),
        scratch_shapes=[pltpu.VMEM((N_GROUPS, tt, LANES), F32), pltpu.VMEM((N_GROUPS, tt, LANES), F32)],
        compiler_params=_params(("parallel", "parallel")),
        name="attn_prompt",
    )(*([qkv] * len(specs)))


def _attn_sample_kernel(qkv_ref, c0_ref, c1_ref, c2_ref, om_ref, n0_ref, n1_ref, n2_ref, *, t_len):
    cache_refs = (c0_ref, c1_ref, c2_ref)
    n_refs = (n0_ref, n1_ref, n2_ref)
    outs, lses = [], []
    gw = GROUP_WIDTH
    nrow = HEADS_PER_GROUP * t_len
    row_head = lax.broadcasted_iota(jnp.int32, (nrow, gw), 0) // t_len
    col_head = lax.broadcasted_iota(jnp.int32, (nrow, gw), 1) // HEAD_DIM
    out_col_head = lax.broadcasted_iota(jnp.int32, (t_len, gw), 1) // HEAD_DIM

    def gather_heads(x):
        acc = jnp.zeros((t_len, gw), F32)
        for h in range(HEADS_PER_GROUP):
            acc = acc + jnp.where(out_col_head == h, x[h * t_len:(h + 1) * t_len, :], 0.0)
        return acc

    for g, (win, dil) in enumerate(DILATED_GROUPS):
        cache_ref = cache_refs[g]
        wb = cache_ref.shape[1]
        q0 = g * gw
        k0 = ATTN_WIDTH + g * gw
        v0 = 2 * ATTN_WIDTH + g * gw

        kv_new = jnp.concatenate([qkv_ref[:, k0:k0 + gw], qkv_ref[:, v0:v0 + gw]], axis=1)
        kv_pad = jnp.concatenate([kv_new, jnp.zeros((LANES - t_len, 2 * gw), F32)], axis=0)
        new_t = kv_pad.T

        q = qkv_ref[:, q0:q0 + gw] * (HEAD_DIM ** -0.5)
        q_bd = jnp.where(row_head == col_head, jnp.concatenate([q] * HEADS_PER_GROUP, axis=0), 0.0).astype(BF16)

        def bias(ncol, base):
            r = lax.broadcasted_iota(jnp.int32, (nrow, ncol), 0)
            w = lax.broadcasted_iota(jnp.int32, (nrow, ncol), 1)
            dist = base + (r % t_len) - w
            ok = (dist >= 0) & (dist <= win) & ((dist & (dil - 1)) == 0)
            slope = jnp.zeros((nrow, ncol), F32)
            for h in range(HEADS_PER_GROUP):
                slope = jnp.where(r // t_len == h, _alibi_slope(g * HEADS_PER_GROUP + h), slope)
            return jnp.where(ok, -slope * dist.astype(F32), -jnp.inf)

        s1 = jnp.dot(q_bd, cache_ref[0:gw, :].astype(BF16), preferred_element_type=F32) + bias(wb, wb)
        pad_ok = lax.broadcasted_iota(jnp.int32, (nrow, LANES), 1) < t_len
        s2 = jnp.dot(q_bd, new_t[0:gw, :].astype(BF16), preferred_element_type=F32)
        s2 = jnp.where(pad_ok, s2 + bias(LANES, 0), -jnp.inf)
        m = jnp.maximum(jnp.max(s1, axis=-1, keepdims=True), jnp.max(s2, axis=-1, keepdims=True))
        p1 = jnp.exp(s1 - m)
        p2 = jnp.exp(s2 - m)
        den = jnp.sum(p1, axis=-1, keepdims=True) + jnp.sum(p2, axis=-1, keepdims=True)
        o = (lax.dot_general(p1.astype(BF16), cache_ref[gw:2 * gw, :].astype(BF16), _NT, preferred_element_type=F32)
             + lax.dot_general(p2.astype(BF16), new_t[gw:2 * gw, :].astype(BF16), _NT, preferred_element_type=F32))
        outs.append(gather_heads(o / den))
        lses.append(gather_heads(jnp.broadcast_to(m + jnp.log(den), (nrow, gw))))

        ext = jnp.concatenate([cache_ref[...], new_t], axis=1)
        n_refs[g][...] = pltpu.roll(ext, wb + LANES - t_len, axis=1)[:, 0:wb]

    merged = _merge_groups(outs, lses)
    for g in range(N_GROUPS):
        for c in range(gw // LANES):
            col = c * MERGED_BLOCK + g * LANES
            om_ref[:, col:col + LANES] = merged[g][:, c * LANES:(c + 1) * LANES]


def _attn_sample(qkv, caches, *, t_len):
    nseq = caches[0].shape[0]
    rows = qkv.shape[0]
    cache_specs = [pl.BlockSpec((None,) + c.shape[1:], lambda b: (b, 0, 0)) for c in caches]
    outs = pl.pallas_call(
        functools.partial(_attn_sample_kernel, t_len=t_len),
        grid=(nseq,),
        in_specs=[pl.BlockSpec((t_len, 3 * ATTN_WIDTH), lambda b: (b, 0))] + cache_specs,
        out_specs=[pl.BlockSpec((t_len, ATTN_WIDTH), lambda b: (b, 0))] + cache_specs,
        out_shape=[jax.ShapeDtypeStruct((rows, ATTN_WIDTH), F32)]
        + [jax.ShapeDtypeStruct(c.shape, F32) for c in caches],
        compiler_params=_params(("parallel",)),
        name="attn_sample",
    )(qkv, *caches)
    return outs[0], outs[1:]


SSM_CHUNK = 16
N_LAM_POWERS = SSM_CHUNK + 1


def _ssm_prep_kernel(a_re_ref, a_im_ref, logdt_ref, bre_ref, bim_ref, pre_ref, pim_ref, bbre_ref, bbim_ref):
    a_re = a_re_ref[...]
    a_im = a_im_ref[...]
    dt = jnp.exp(logdt_ref[...])
    mag = jnp.exp(a_re * dt)
    ang = a_im * dt
    lam_re = mag * jnp.cos(ang)
    lam_im = mag * jnp.sin(ang)
    den = a_re * a_re + a_im * a_im
    num_re = lam_re - 1.0
    coef_re = ((num_re * a_re + lam_im * a_im) / den)[None]
    coef_im = ((lam_im * a_re - num_re * a_im) / den)[None]
    b_re = bre_ref[...]
    b_im = bim_ref[...]
    bbre_ref[...] = coef_re * b_re - coef_im * b_im
    bbim_ref[...] = coef_re * b_im + coef_im * b_re
    p_re, p_im = jnp.ones_like(lam_re), jnp.zeros_like(lam_im)
    for m in range(N_LAM_POWERS):
        pre_ref[m] = p_re
        pim_ref[m] = p_im
        p_re, p_im = p_re * lam_re - p_im * lam_im, p_re * lam_im + p_im * lam_re


def _ssm_prep(a_re, a_im, log_dt, b_re_t, b_im_t):
    g, n = a_re.shape
    pw = jax.ShapeDtypeStruct((N_LAM_POWERS, g, n), F32)
    bb = jax.ShapeDtypeStruct(b_re_t.shape, F32)
    return pl.pallas_call(
        _ssm_prep_kernel,
        out_shape=[pw, pw, bb, bb],
        name="ssm_prep",
    )(a_re, a_im, log_dt.reshape(g, 1), b_re_t, b_im_t)


SCAN_BLOCKS_PER_STEP = 4


def _ssm_rows_kernel(*refs, t_len, nseq):
    for k in range(refs[0].shape[0]):
        _ssm_rows_block(*[r.at[k] for r in refs], t_len=t_len, nseq=nseq)


def _ssm_rows_block(u_ref, bb_ref, ctre_ref, ctim_ref, pw_ref, d_ref, h0_ref, y_ref, fs_ref,
                    bu_scr, xre_scr, xim_scr, ud_scr, *, t_len, nseq):
    sl = STATE_LANES
    for j in range(t_len):
        ud_scr[j * nseq:(j + 1) * nseq, :] = u_ref[pl.ds(j, nseq, stride=t_len), :].astype(BF16)
    bu_scr[...] = jnp.dot(ud_scr[...], bb_ref[...].astype(BF16), preferred_element_type=F32)

    lam_re = jnp.broadcast_to(pw_ref[1:2, :], (nseq, sl))
    lam_im = jnp.broadcast_to(pw_ref[N_LAM_POWERS + 1:N_LAM_POWERS + 2, :], (nseq, sl))
    x_re = h0_ref[:, 0:sl]
    x_im = h0_ref[:, sl:2 * sl]
    for j in range(t_len):
        b_re = bu_scr[j * nseq:(j + 1) * nseq, 0:sl]
        b_im = bu_scr[j * nseq:(j + 1) * nseq, sl:2 * sl]
        x_re, x_im = (lam_re * x_re - lam_im * x_im + b_re, lam_re * x_im + lam_im * x_re + b_im)
        xre_scr[j * nseq:(j + 1) * nseq, :] = x_re.astype(BF16)
        xim_scr[j * nseq:(j + 1) * nseq, :] = x_im.astype(BF16)
    fs_ref[:, 0:sl] = x_re
    fs_ref[:, sl:2 * sl] = x_im

    y = (lax.dot_general(xre_scr[...], ctre_ref[...].astype(BF16), _NT, preferred_element_type=F32)
         - lax.dot_general(xim_scr[...], ctim_ref[...].astype(BF16), _NT, preferred_element_type=F32))
    d = d_ref[...]
    for j in range(t_len):
        y_ref[pl.ds(j, nseq, stride=t_len), :] = (
            y[j * nseq:(j + 1) * nseq, :] + d * u_ref[pl.ds(j, nseq, stride=t_len), :])


def _ssm_scan_rows(u, tabs, h0, *, t_len, nseq):
    nblk, rows, _ = u.shape
    kb = SCAN_BLOCKS_PER_STEP
    sl2 = 2 * STATE_LANES
    blk = lambda arr: pl.BlockSpec((kb,) + arr.shape[1:], lambda k: (k,) + (0,) * (arr.ndim - 1))
    ins = (u, tabs["bb"], tabs["ct_re"], tabs["ct_im"], tabs["pw_rows"], tabs["dvec"], h0)
    return pl.pallas_call(
        functools.partial(_ssm_rows_kernel, t_len=t_len, nseq=nseq),
        grid=(nblk // kb,),
        in_specs=[blk(a) for a in ins],
        out_specs=[blk(u), blk(h0)],
        out_shape=[jax.ShapeDtypeStruct(u.shape, F32), jax.ShapeDtypeStruct(h0.shape, F32)],
        scratch_shapes=[
            pltpu.VMEM((kb, rows, sl2), F32),
            pltpu.VMEM((kb, rows, STATE_LANES), BF16),
            pltpu.VMEM((kb, rows, STATE_LANES), BF16),
            pltpu.VMEM((kb, rows, LANES), BF16),
        ],
        compiler_params=_params(("parallel",)),
        name="ssm_scan_rows",
    )(*ins)


GATHER_STRIDE = 4
assert SSM_CHUNK == GATHER_STRIDE * GATHER_STRIDE


def _ssm_seq_kernel(u_ref, bb_ref, ctre_ref, ctim_ref, pw_ref, d_ref, h0_ref, y_ref, fs_ref,
                    g_scr, kt_scr, clt_scr, ucat_scr, e_scr, hp_scr, s1_scr, s2_scr):
    L = SSM_CHUNK
    sl = STATE_LANES
    seq = u_ref.shape[0]
    nch = seq // L

    @pl.when(pl.program_id(1) == 0)
    def _build_tables():
        lam_re = pw_ref[1:2, :]
        lam_im = pw_ref[N_LAM_POWERS + 1:N_LAM_POWERS + 2, :]
        q_re = ctre_ref[...]
        q_im = ctim_ref[...]
        ccs_t = jnp.concatenate([q_re, -q_im], axis=1).astype(BF16)
        kt_scr[...] = jnp.zeros(kt_scr.shape, BF16)
        g_re = bb_ref[:, 0:sl]
        g_im = bb_ref[:, sl:2 * sl]
        for m in range(L):
            j = L - 1 - m
            g_m = jnp.concatenate([g_re, g_im], axis=1).astype(BF16)
            g_scr[j * LANES:(j + 1) * LANES, :] = g_m
            k_m = lax.dot_general(g_m, ccs_t, _NT, preferred_element_type=F32).astype(BF16)
            for jj in range(L - m):
                i = jj + m
                kt_scr[jj * LANES:(jj + 1) * LANES, i * LANES:(i + 1) * LANES] = k_m
            g_re, g_im = g_re * lam_re - g_im * lam_im, g_re * lam_im + g_im * lam_re
            q_re, q_im = q_re * lam_re - q_im * lam_im, q_re * lam_im + q_im * lam_re
            clt_scr[m * LANES:(m + 1) * LANES, 0:sl] = q_re.astype(BF16)
            clt_scr[m * LANES:(m + 1) * LANES, sl:2 * sl] = (-q_im).astype(BF16)

    qs = GATHER_STRIDE
    for b in range(qs):
        s1_scr[b] = u_ref[pl.ds(b, seq // qs, stride=qs), :]
    for j in range(L):
        a, b = divmod(j, qs)
        ucat_scr[:, j * LANES:(j + 1) * LANES] = s1_scr[b, pl.ds(a, nch, stride=qs), :].astype(BF16)
    e_scr[...] = jnp.dot(ucat_scr[...], g_scr[...], preferred_element_type=F32)

    lc_re = pw_ref[L:L + 1, :]
    lc_im = pw_ref[N_LAM_POWERS + L:N_LAM_POWERS + L + 1, :]

    def chunk_step(c, h):
        h_re, h_im = h
        hp_scr[pl.ds(c, 1), 0:sl] = h_re
        hp_scr[pl.ds(c, 1), sl:2 * sl] = h_im
        e_re = e_scr[pl.ds(c, 1), 0:sl]
        e_im = e_scr[pl.ds(c, 1), sl:2 * sl]
        return (lc_re * h_re - lc_im * h_im + e_re, lc_re * h_im + lc_im * h_re + e_im)

    h_re, h_im = lax.fori_loop(0, nch, chunk_step, (h0_ref[:, 0:sl], h0_ref[:, sl:2 * sl]), unroll=8)
    fs_ref[:, 0:sl] = h_re
    fs_ref[:, sl:2 * sl] = h_im

    h_prev = hp_scr[...].astype(BF16)
    pair = 2 * LANES
    for p in range(L // 2):
        y2 = (jnp.dot(ucat_scr[:, 0:(p + 1) * pair], kt_scr[0:(p + 1) * pair, p * pair:(p + 1) * pair],
                      preferred_element_type=F32)
              + lax.dot_general(h_prev, clt_scr[p * pair:(p + 1) * pair, :], _NT, preferred_element_type=F32))
        for half in range(2):
            a, b = divmod(2 * p + half, qs)
            s2_scr[b, pl.ds(a, nch, stride=qs), :] = y2[:, half * LANES:(half + 1) * LANES]
    d = d_ref[...]
    for b in range(qs):
        y_ref[pl.ds(b, seq // qs, stride=qs), :] = s2_scr[b] + d * s1_scr[b]


def _ssm_scan_seq(u, tabs, h0, *, nseq, seq):
    nblk = u.shape[0]
    L = SSM_CHUNK
    nch = seq // L
    sl2 = 2 * STATE_LANES
    blk = lambda arr: pl.BlockSpec((None,) + arr.shape[1:], lambda k, b: (k,) + (0,) * (arr.ndim - 1))
    row_spec = pl.BlockSpec((None, seq, LANES), lambda k, b: (k, b, 0))
    st_spec = pl.BlockSpec((None, None, 1, sl2), lambda k, b: (k, b, 0, 0))
    return pl.pallas_call(
        _ssm_seq_kernel,
        grid=(nblk, nseq),
        in_specs=[row_spec, blk(tabs["bb"]), blk(tabs["ct_re"]), blk(tabs["ct_im"]), blk(tabs["pw_rows"]),
                  blk(tabs["dvec"]), st_spec],
        out_specs=[row_spec, st_spec],
        out_shape=[jax.ShapeDtypeStruct(u.shape, F32), jax.ShapeDtypeStruct(h0.shape, F32)],
        scratch_shapes=[
            pltpu.VMEM((L * LANES, sl2), BF16),
            pltpu.VMEM((L * LANES, L * LANES), BF16),
            pltpu.VMEM((L * LANES, sl2), BF16),
            pltpu.VMEM((nch, L * LANES), BF16),
            pltpu.VMEM((nch, sl2), F32),
            pltpu.VMEM((nch, sl2), F32),
            pltpu.VMEM((GATHER_STRIDE, seq // GATHER_STRIDE, LANES), F32),
            pltpu.VMEM((GATHER_STRIDE, seq // GATHER_STRIDE, LANES), F32),
        ],
        compiler_params=_params(("arbitrary", "arbitrary")),
        name="ssm_scan_seq",
    )(u, tabs["bb"], tabs["ct_re"], tabs["ct_im"], tabs["pw_rows"], tabs["dvec"], h0)


def _block_diag(x):
    kg, a, b = x.shape
    k = kg // GROUPS_PER_BLOCK
    tiled = jnp.tile(x.reshape(k, GROUPS_PER_BLOCK * a, b), (1, 1, GROUPS_PER_BLOCK))
    row_blk = lax.broadcasted_iota(jnp.int32, tiled.shape, 1) // a
    col_blk = lax.broadcasted_iota(jnp.int32, tiled.shape, 2) // b
    return jnp.where(row_blk == col_blk, tiled, 0)


def _lane_blocks(x):
    lead = x.shape[:-2]
    x = x.reshape(lead + (SSM_LANE_BLOCKS, STATE_LANES))
    return jnp.moveaxis(x, -2, 0)


def _ssm_tables(P, j):
    b_re_t = jnp.transpose(P["ssm_b_re"][j], (2, 0, 1))
    b_im_t = jnp.transpose(P["ssm_b_im"][j], (2, 0, 1))
    p_re, p_im, bb_re, bb_im = _ssm_prep(P["ssm_a_re"][j], P["ssm_a_im"][j], P["ssm_log_dt"][j], b_re_t, b_im_t)
    bb_re = jnp.swapaxes(bb_re, 0, 1)
    bb_im = jnp.swapaxes(bb_im, 0, 1)
    bb = jnp.concatenate([_block_diag(bb_re), _block_diag(bb_im)], axis=2)
    ct_re = _block_diag(P["ssm_c_re"][j])
    ct_im = _block_diag(P["ssm_c_im"][j])
    pw_rows = jnp.concatenate([_lane_blocks(p_re), _lane_blocks(p_im)], axis=1)
    dvec = P["ssm_d"][j].reshape(SSM_LANE_BLOCKS, 1, LANES)
    return dict(bb=bb, ct_re=ct_re, ct_im=ct_im, pw_rows=pw_rows, dvec=dvec)


def _ssm_scan(u, tabs, state, *, nseq, seq):
    if state is None:
        h0 = jnp.zeros((SSM_LANE_BLOCKS, nseq, 1, 2 * STATE_LANES), F32)
        y, fs = _ssm_scan_seq(u, tabs, h0, nseq=nseq, seq=seq)
        fs = fs.reshape(SSM_LANE_BLOCKS, nseq, 2 * STATE_LANES)
    else:
        y, fs = _ssm_scan_rows(u, tabs, _state_to_blocks(state), t_len=seq, nseq=nseq)
    return y, _state_from_blocks(fs)


def _state_from_blocks(fs):
    nblk, nseq, _ = fs.shape
    fs = fs.reshape(nblk, nseq, 2, GROUPS_PER_BLOCK, SSM_STATE)
    fs = jnp.transpose(fs, (1, 0, 3, 4, 2))
    return fs.reshape(nseq, SSM_GROUPS, SSM_STATE, 2)


def _state_to_blocks(state):
    nseq = state.shape[0]
    s = state.reshape(nseq, SSM_LANE_BLOCKS, GROUPS_PER_BLOCK, SSM_STATE, 2)
    return jnp.transpose(s, (1, 0, 4, 2, 3)).reshape(SSM_LANE_BLOCKS, nseq, 2 * STATE_LANES)


def _to_window_minor(c):
    return jnp.transpose(c[0], (0, 2, 3, 4, 1)).reshape(c.shape[1], 2 * GROUP_WIDTH, c.shape[2])


def _from_window_minor(c):
    c = c.reshape(c.shape[0], 2, HEADS_PER_GROUP, HEAD_DIM, c.shape[2])
    return jnp.transpose(c, (0, 4, 1, 2, 3))[None]


def _kv_tail_kernel(k_ref, v_ref, o_ref):
    o_ref[0:GROUP_WIDTH, :] = k_ref[...].T
    o_ref[GROUP_WIDTH:2 * GROUP_WIDTH, :] = v_ref[...].T


def _kv_tail(qkv, *, group, nseq, seq):
    wb = min(DILATED_GROUPS[group][0], seq)
    kcol = (ATTN_WIDTH + group * GROUP_WIDTH) // GROUP_WIDTH
    vcol = (2 * ATTN_WIDTH + group * GROUP_WIDTH) // GROUP_WIDTH
    tail = lambda col: pl.BlockSpec((wb, GROUP_WIDTH), lambda b: ((b + 1) * (seq // wb) - 1, col))
    return pl.pallas_call(
        _kv_tail_kernel,
        grid=(nseq,),
        in_specs=[tail(kcol), tail(vcol)],
        out_specs=pl.BlockSpec((None, 2 * GROUP_WIDTH, wb), lambda b: (b, 0, 0)),
        out_shape=jax.ShapeDtypeStruct((nseq, 2 * GROUP_WIDTH, wb), F32),
        compiler_params=_params(("parallel",)),
        name="kv_tail",
    )(qkv, qkv)


def _trunks(xs, caches, ssm_state, W, *, nseqs, seqs):
    hs = _stage(xs, W, 0, 0)
    qkvs = _proj(hs, W["norms"], 0, W["qkv"])
    merged_p = _attn_prompt(qkvs[0], seq=seqs[0])
    merged_s, new_caches = _attn_sample(qkvs[1], caches, t_len=seqs[1])
    xs = _stage(hs, W, 0, 1, pre="attn", pre_ins=(merged_p, merged_s))

    hs, us = _stage(xs, W, 1, 0, post="proj_slabs")
    y_p, state_p = _ssm_scan(us[0], W["ssm"], None, nseq=nseqs[0], seq=seqs[0])
    y_s, state_s = _ssm_scan(us[1], W["ssm"], ssm_state, nseq=nseqs[1], seq=seqs[1])
    outs = _stage(hs, W, 1, 1, pre="ssm", pre_ins=(y_p, y_s), post="final")
    return outs, qkvs[0], new_caches, (state_p, state_s)


def kernel(x_prompt, x_sample, cache_kv_g0, cache_kv_g1, cache_kv_g2, state_ssm, norm_g, final_norm_g,
           ffn_w_gate, ffn_w_up, ffn_w_down, attn_w_qkv, attn_w_o, ssm_w_in, ssm_a_re, ssm_a_im, ssm_log_dt,
           ssm_b_re, ssm_b_im, ssm_c_re, ssm_c_im, ssm_d, ssm_w_glu, ssm_b_glu, ssm_w_out):
    batch, seq, _ = x_prompt.shape
    dec_batch, dec_seq, _ = x_sample.shape
    P = dict(ssm_a_re=ssm_a_re, ssm_a_im=ssm_a_im, ssm_log_dt=ssm_log_dt, ssm_b_re=ssm_b_re,
             ssm_b_im=ssm_b_im, ssm_c_re=ssm_c_re, ssm_c_im=ssm_c_im, ssm_d=ssm_d)
    W = dict(
        norms=norm_g.reshape(norm_g.shape[0], norm_g.shape[1], 1, D_MODEL),
        final_norm=final_norm_g.reshape(1, D_MODEL),
        gate=ffn_w_gate.astype(BF16), up=ffn_w_up.astype(BF16), down=ffn_w_down.astype(BF16),
        qkv=attn_w_qkv.astype(BF16),
        o=jnp.swapaxes(attn_w_o.reshape(-1, N_GROUPS, GROUP_WIDTH // LANES, LANES, D_MODEL), 1, 2)
        .reshape(attn_w_o.shape).astype(BF16),
        ssm_w_in=ssm_w_in.astype(BF16), ssm_w_glu=ssm_w_glu.astype(BF16), ssm_w_out=ssm_w_out.astype(BF16),
        ssm_b_glu=ssm_b_glu.reshape(ssm_b_glu.shape[0], 1, D_MODEL),
        ssm=_ssm_tables(P, 0),
    )

    xs = (x_prompt.reshape(batch * seq, D_MODEL), x_sample.reshape(dec_batch * dec_seq, D_MODEL))
    caches = [_to_window_minor(c) for c in (cache_kv_g0, cache_kv_g1, cache_kv_g2)]
    (y_p, y_s), qkv_p, new_caches, (ssm_p, ssm_s) = _trunks(
        xs, caches, state_ssm[0], W, nseqs=(batch, dec_batch), seqs=(seq, dec_seq))
    kv_prompt = [_from_window_minor(_kv_tail(qkv_p, group=g, nseq=batch, seq=seq)) for g in range(N_GROUPS)]
    kv_sample = [_from_window_minor(nc) for nc in new_caches]

    return (y_p.reshape(batch, seq, D_MODEL), y_s.reshape(dec_batch, dec_seq, D_MODEL),
            kv_prompt[0], kv_prompt[1], kv_prompt[2], ssm_p[None],
            kv_sample[0], kv_sample[1], kv_sample[2], ssm_s[None])
```

```python
import functools
import math

import jax
import jax.numpy as jnp
from jax import lax
from jax.experimental import pallas as pl
from jax.experimental.pallas import tpu as pltpu

F32 = jnp.float32
BF16 = jnp.bfloat16

D_MODEL = 1024
D_FF = 2816
HEAD_DIM = 64
HEADS_PER_GROUP = 4
DILATED_GROUPS = ((128, 1), (512, 4), (2048, 16))
N_GROUPS = len(DILATED_GROUPS)
N_HEADS = N_GROUPS * HEADS_PER_GROUP
GROUP_WIDTH = HEADS_PER_GROUP * HEAD_DIM
ATTN_WIDTH = N_HEADS * HEAD_DIM
BAND = 128
SSM_CH = 16
SSM_GROUPS = D_MODEL // SSM_CH
SSM_STATE = 64
NORM_EPS = 1e-6

LANES = 128
V7X_VMEM_BYTES = 64 * 1024 * 1024
VMEM_LIMIT = V7X_VMEM_BYTES - 8 * 1024 * 1024

SSM_LANE_BLOCKS = D_MODEL // LANES
GROUPS_PER_BLOCK = LANES // SSM_CH
STATE_LANES = GROUPS_PER_BLOCK * SSM_STATE

assert all(w // d == BAND for w, d in DILATED_GROUPS)
assert all(d & (d - 1) == 0 for _, d in DILATED_GROUPS)


def _params(sem, vmem=VMEM_LIMIT):
    return pltpu.CompilerParams(dimension_semantics=sem, vmem_limit_bytes=vmem)


def _const_spec(shape):
    nd = len(shape)
    return pl.BlockSpec(shape, lambda *_: (0,) * nd, pipeline_mode=pl.Buffered(1))


def _rms(x, g):
    return x * lax.rsqrt(jnp.mean(x * x, axis=-1, keepdims=True) + NORM_EPS) * g


MXU_TILE = 256
FFN_CHUNK_BOUNDS = (0, 6 * MXU_TILE, D_FF)


N_STREAMS = 2


FFN_WEIGHTS = ("gate", "up", "down")


def _stage_kernel(*refs, pre, post, cast_next):
    refs = list(refs)
    take = lambda n: [refs.pop(0) for _ in range(n)]
    x_refs = take(N_STREAMS)
    if pre == "attn":
        om_refs = take(N_STREAMS)
        (wo_ref,) = take(1)
    elif pre == "ssm":
        y_refs = take(N_STREAMS)
        wglu_ref, bglu_ref, wout_ref = take(3)
    g_ref, wg_ref, wu_ref, wd_ref = take(4)
    if post == "final":
        (gf_ref,) = take(1)
    elif post == "proj_slabs":
        gp_ref, wp_ref = take(2)
    next_f32 = take(len(FFN_WEIGHTS)) if cast_next else []
    o_refs = take(N_STREAMS)
    if post == "proj_slabs":
        p_refs = take(N_STREAMS)
    for src, dst in zip(next_f32, take(len(next_f32))):
        dst[...] = src[...].astype(BF16)

    def run(s):
        x = x_refs[s][...]
        if pre == "attn":
            x = x + jnp.dot(om_refs[s][...].astype(BF16), wo_ref[...], preferred_element_type=F32)
        elif pre == "ssm":
            y = jnp.concatenate([y_refs[s][k] for k in range(y_refs[s].shape[0])], axis=1)
            g = jax.nn.gelu(y, approximate=True)
            gate = jnp.dot(g.astype(BF16), wglu_ref[...], preferred_element_type=F32) + bglu_ref[...]
            z = (g * jax.nn.sigmoid(gate)).astype(BF16)
            x = x + jnp.dot(z, wout_ref[...], preferred_element_type=F32)

        xb = _rms(x, g_ref[...]).astype(BF16)
        acc = None
        for lo, hi in zip(FFN_CHUNK_BOUNDS[:-1], FFN_CHUNK_BOUNDS[1:]):
            gate = jnp.dot(xb, wg_ref[:, lo:hi], preferred_element_type=F32)
            up = jnp.dot(xb, wu_ref[:, lo:hi], preferred_element_type=F32)
            hmid = (gate * jax.nn.sigmoid(gate) * up).astype(BF16)
            part = jnp.dot(hmid, wd_ref[lo:hi, :], preferred_element_type=F32)
            acc = part if acc is None else acc + part
        out = x + 0.5 * acc

        if post == "final":
            out = _rms(out, gf_ref[...])
        o_refs[s][...] = out
        if post == "proj_slabs":
            proj = jnp.dot(_rms(out, gp_ref[...]).astype(BF16), wp_ref[...], preferred_element_type=F32)
            for k in range(p_refs[s].shape[0]):
                p_refs[s][k] = proj[:, k * LANES:(k + 1) * LANES]

    run(0)
    pl.when(pl.program_id(0) == 0)(lambda: run(1))


def _pick_spec(arr, *idx):
    tail = arr.shape[len(idx):]
    index = tuple(idx) + (0,) * len(tail)
    return pl.BlockSpec((None,) * len(idx) + tail, lambda *_: index, pipeline_mode=pl.Buffered(1))


STAGE_TM = 512


def _whole_spec(shape):
    nd = len(shape)
    return pl.BlockSpec(shape, lambda *_: (0,) * nd)


def _row_specs(xs):
    return [pl.BlockSpec((STAGE_TM, xs[0].shape[1]), lambda i: (i, 0)), _whole_spec(xs[1].shape)]


def _slab_specs(ys):
    return [pl.BlockSpec((SSM_LANE_BLOCKS, STAGE_TM, LANES), lambda i: (0, i, 0)), _whole_spec(ys[1].shape)]


def _stage(xs, W, ffn_w, layer, half, *, pre=None, pre_ins=None, post=None, cast_next=None):
    steps = xs[0].shape[0] // STAGE_TM
    ins, specs = list(xs), _row_specs(xs)
    if pre == "attn":
        ins += list(pre_ins) + [W["o"]]
        specs += _row_specs(pre_ins) + [_pick_spec(W["o"], 0)]
    elif pre == "ssm":
        ins += list(pre_ins) + [W["ssm_w_glu"], W["ssm_b_glu"], W["ssm_w_out"]]
        specs += _slab_specs(pre_ins) + [_pick_spec(W["ssm_w_glu"], 0), _pick_spec(W["ssm_b_glu"], 0),
                                         _pick_spec(W["ssm_w_out"], 0)]
    ins += [W["norms"]] + list(ffn_w)
    specs += [_pick_spec(W["norms"], layer, 2 * half)] + [_pick_spec(w) for w in ffn_w]
    out_specs = _row_specs(xs)
    out_shape = [jax.ShapeDtypeStruct(x.shape, F32) for x in xs]
    if post == "final":
        ins += [W["final_norm"]]
        specs += [_const_spec((1, D_MODEL))]
    elif post == "proj_slabs":
        ins += [W["norms"], W["ssm_w_in"]]
        specs += [_pick_spec(W["norms"], layer, 1), _pick_spec(W["ssm_w_in"], 0)]
    if cast_next:
        nl, nh = cast_next
        rows_per_step = D_MODEL // steps
        steps_per_col = steps // (D_MODEL // LANES)
        assert rows_per_step * steps == D_MODEL and steps_per_col * (D_MODEL // LANES) == steps
        ins += [W[name + "_f32"] for name in FFN_WEIGHTS]
        specs += [pl.BlockSpec((None, None, rows_per_step, D_FF), lambda i: (nl, nh, i, 0))] * 2
        specs += [pl.BlockSpec((None, None, D_FF, LANES), lambda i: (nl, nh, 0, i // steps_per_col))]
    if post == "proj_slabs":
        slabs = [jax.ShapeDtypeStruct((SSM_LANE_BLOCKS, x.shape[0], LANES), F32) for x in xs]
        out_specs += _slab_specs(slabs)
        out_shape += slabs
    if cast_next:
        out_specs += [pl.BlockSpec((rows_per_step, D_FF), lambda i: (i, 0))] * 2
        out_specs += [pl.BlockSpec((D_FF, LANES), lambda i: (0, i // steps_per_col))]
        out_shape += [jax.ShapeDtypeStruct(s, BF16) for s in ((D_MODEL, D_FF), (D_MODEL, D_FF), (D_FF, D_MODEL))]
    outs = list(pl.pallas_call(
        functools.partial(_stage_kernel, pre=pre, post=post, cast_next=bool(cast_next)),
        grid=(steps,),
        in_specs=specs,
        out_specs=out_specs,
        out_shape=out_shape,
        compiler_params=_params(("arbitrary",)),
        name="stage_" + "_".join(s for s in (pre, "ffn", post) if s),
    )(*ins))
    take = lambda n: [outs.pop(0) for _ in range(n)]
    return dict(x=take(N_STREAMS), proj=take(N_STREAMS) if post == "proj_slabs" else None,
                next_w=take(len(FFN_WEIGHTS)) if cast_next else None)


def _proj_kernel(x0_ref, x1_ref, g_ref, w_ref, o0_ref, o1_ref):
    def run(x_ref, o_ref):
        xb = _rms(x_ref[...], g_ref[...]).astype(BF16)
        o_ref[...] = jnp.dot(xb, w_ref[...], preferred_element_type=F32)

    run(x0_ref, o0_ref)
    pl.when(pl.program_id(0) == 0)(lambda: run(x1_ref, o1_ref))


def _proj(xs, norms, layer, w):
    n = w.shape[-1]
    out_shape = [jax.ShapeDtypeStruct((x.shape[0], n), F32) for x in xs]
    return pl.pallas_call(
        _proj_kernel,
        grid=(xs[0].shape[0] // STAGE_TM,),
        in_specs=_row_specs(xs) + [_pick_spec(norms, layer, 1), _pick_spec(w, 0)],
        out_specs=_row_specs(out_shape),
        out_shape=out_shape,
        compiler_params=_params(("arbitrary",)),
        name="norm_proj",
    )(*xs, norms, w)


def _alibi_slope(head):
    return 2.0 ** (-8.0 * (head + 1) / N_HEADS)


def _head_lane_mask(hh):
    lane = lax.broadcasted_iota(jnp.int32, (1, LANES), 1)
    return (lane >= hh * HEAD_DIM) & (lane < (hh + 1) * HEAD_DIM)


ATTN_TILE_TOKENS = 2048
_NT = (((1,), (1,)), ((), ()))


def _merge_groups(outs, lses):
    m = jnp.maximum(jnp.maximum(lses[0], lses[1]), lses[2])
    es = [jnp.exp(l - m) for l in lses]
    tot = es[0] + es[1] + es[2]
    return [o * (e / tot) for o, e in zip(outs, es)]


def _attn_kernel(*refs, tiles_per_seq):
    om_ref, o_scr, l_scr = refs[5 * N_GROUPS:]
    has_prev = (pl.program_id(0) % tiles_per_seq) != 0
    for g in range(N_GROUPS):
        _attn_group(*refs[5 * g:5 * g + 5], o_scr.at[g], l_scr.at[g], group=g, has_prev=has_prev)
    merged = _merge_groups([o_scr[g] for g in range(N_GROUPS)], [l_scr[g] for g in range(N_GROUPS)])
    for g in range(N_GROUPS):
        om_ref[:, g * LANES:(g + 1) * LANES] = merged[g].astype(BF16)


def _attn_group(q_ref, kp_ref, kc_ref, vp_ref, vc_ref, o_ref, l_ref, *, group, has_prev):
    dil = DILATED_GROUPS[group][1]
    span = BAND * dil
    nblk = q_ref.shape[0] // span
    c = pl.program_id(1)

    row = lax.broadcasted_iota(jnp.int32, (2 * BAND, 2 * BAND), 0)
    kj = lax.broadcasted_iota(jnp.int32, (2 * BAND, 2 * BAND), 1)
    delta = BAND + (row & (BAND - 1)) - kj
    head0 = group * HEADS_PER_GROUP
    slope_a = jnp.where(c == 0, _alibi_slope(head0), _alibi_slope(head0 + 2)).astype(F32)
    slope_b = jnp.where(c == 0, _alibi_slope(head0 + 1), _alibi_slope(head0 + 3)).astype(F32)
    slope = jnp.where(row < BAND, slope_a, slope_b)
    band = (delta >= 0) & (delta <= BAND)
    bias = jnp.where(band, -slope * (delta * dil).astype(F32), -jnp.inf)
    bias_first = jnp.where((kj >= BAND) | has_prev, bias, -jnp.inf)
    first_head = _head_lane_mask(0)

    def rows(ref, start):
        return ref[pl.ds(start, BAND), :] if dil == 1 else ref[pl.ds(start, BAND, stride=dil), :]

    for r in range(dil):
        k_prev = rows(kp_ref, r).astype(BF16)
        v_prev = rows(vp_ref, r).astype(BF16)
        for b in range(nblk):
            start = b * span + r
            q = rows(q_ref, start) * (HEAD_DIM ** -0.5)
            k_cur = rows(kc_ref, start).astype(BF16)
            v_cur = rows(vc_ref, start).astype(BF16)
            k = jnp.concatenate([k_prev, k_cur], axis=0)
            v = jnp.concatenate([v_prev, v_cur], axis=0)
            q2 = jnp.concatenate([jnp.where(first_head, q, 0.0), jnp.where(first_head, 0.0, q)], axis=0)
            s = lax.dot_general(q2.astype(BF16), k, _NT, preferred_element_type=F32)
            s = s + (bias_first if b == 0 else bias)
            m = jnp.max(s, axis=-1, keepdims=True)
            p = jnp.exp(s - m)
            den = jnp.sum(p, axis=-1, keepdims=True)
            o2 = jnp.dot(p.astype(BF16), v, preferred_element_type=F32) / den
            l2 = m + jnp.log(den)
            o = jnp.where(first_head, o2[:BAND], o2[BAND:])
            lse = jnp.where(first_head, l2[:BAND], l2[BAND:])
            if dil == 1:
                o_ref[pl.ds(start, BAND), :] = o
                l_ref[pl.ds(start, BAND), :] = lse
            else:
                o_ref[pl.ds(start, BAND, stride=dil), :] = o
                l_ref[pl.ds(start, BAND, stride=dil), :] = lse
            k_prev, v_prev = k_cur, v_cur


MERGED_BLOCK = N_GROUPS * LANES


def _attn_prompt(qkv, *, seq):
    rows = qkv.shape[0]
    tt = ATTN_TILE_TOKENS
    specs = []
    for g, (_, dil) in enumerate(DILATED_GROUPS):
        span = BAND * dil
        nblk = tt // span
        cur = lambda col: pl.BlockSpec((tt, LANES), lambda i, c: (i, col + c))
        prev = lambda col, span=span, nblk=nblk: pl.BlockSpec(
            (span, LANES), lambda i, c: (jnp.maximum(i * nblk - 1, 0), col + c))
        qcol = g * GROUP_WIDTH // LANES
        kcol = (ATTN_WIDTH + g * GROUP_WIDTH) // LANES
        vcol = (2 * ATTN_WIDTH + g * GROUP_WIDTH) // LANES
        specs += [cur(qcol), prev(kcol), cur(kcol), prev(vcol), cur(vcol)]
    return pl.pallas_call(
        functools.partial(_attn_kernel, tiles_per_seq=seq // tt),
        grid=(rows // tt, GROUP_WIDTH // LANES),
        in_specs=specs,
        out_specs=pl.BlockSpec((tt, MERGED_BLOCK), lambda i, c: (i, c)),
        out_shape=jax.ShapeDtypeStruct((rows, ATTN_WIDTH), BF16),
        scratch_shapes=[pltpu.VMEM((N_GROUPS, tt, LANES), F32), pltpu.VMEM((N_GROUPS, tt, LANES), F32)],
        compiler_params=_params(("parallel", "parallel")),
        name="attn_prompt",
    )(*([qkv] * len(specs)))


def _attn_sample_kernel(qkv_ref, c0_ref, c1_ref, c2_ref, om_ref, n0_ref, n1_ref, n2_ref, *, t_len):
    cache_refs = (c0_ref, c1_ref, c2_ref)
    n_refs = (n0_ref, n1_ref, n2_ref)
    outs, lses = [], []
    gw = GROUP_WIDTH
    nrow = HEADS_PER_GROUP * t_len
    row_head = lax.broadcasted_iota(jnp.int32, (nrow, gw), 0) // t_len
    col_head = lax.broadcasted_iota(jnp.int32, (nrow, gw), 1) // HEAD_DIM
    out_col_head = lax.broadcasted_iota(jnp.int32, (t_len, gw), 1) // HEAD_DIM

    def gather_heads(x):
        acc = jnp.zeros((t_len, gw), F32)
        for h in range(HEADS_PER_GROUP):
            acc = acc + jnp.where(out_col_head == h, x[h * t_len:(h + 1) * t_len, :], 0.0)
        return acc

    for g, (win, dil) in enumerate(DILATED_GROUPS):
        cache_ref = cache_refs[g]
        wb = cache_ref.shape[1]
        q0 = g * gw
        k0 = ATTN_WIDTH + g * gw
        v0 = 2 * ATTN_WIDTH + g * gw

        kv_new = jnp.concatenate([qkv_ref[:, k0:k0 + gw], qkv_ref[:, v0:v0 + gw]], axis=1)
        kv_pad = jnp.concatenate([kv_new, jnp.zeros((LANES - t_len, 2 * gw), F32)], axis=0)
        new_t = kv_pad.T

        q = qkv_ref[:, q0:q0 + gw] * (HEAD_DIM ** -0.5)
        q_bd = jnp.where(row_head == col_head, jnp.concatenate([q] * HEADS_PER_GROUP, axis=0), 0.0).astype(BF16)

        def bias(ncol, base):
            r = lax.broadcasted_iota(jnp.int32, (nrow, ncol), 0)
            w = lax.broadcasted_iota(jnp.int32, (nrow, ncol), 1)
            dist = base + (r % t_len) - w
            ok = (dist >= 0) & (dist <= win) & ((dist & (dil - 1)) == 0)
            slope = jnp.zeros((nrow, ncol), F32)
            for h in range(HEADS_PER_GROUP):
                slope = jnp.where(r // t_len == h, _alibi_slope(g * HEADS_PER_GROUP + h), slope)
            return jnp.where(ok, -slope * dist.astype(F32), -jnp.inf)

        s1 = jnp.dot(q_bd, cache_ref[0:gw, :].astype(BF16), preferred_element_type=F32) + bias(wb, wb)
        pad_ok = lax.broadcasted_iota(jnp.int32, (nrow, LANES), 1) < t_len
        s2 = jnp.dot(q_bd, new_t[0:gw, :].astype(BF16), preferred_element_type=F32)
        s2 = jnp.where(pad_ok, s2 + bias(LANES, 0), -jnp.inf)
        m = jnp.maximum(jnp.max(s1, axis=-1, keepdims=True), jnp.max(s2, axis=-1, keepdims=True))
        p1 = jnp.exp(s1 - m)
        p2 = jnp.exp(s2 - m)
        den = jnp.sum(p1, axis=-1, keepdims=True) + jnp.sum(p2, axis=-1, keepdims=True)
        o = (lax.dot_general(p1.astype(BF16), cache_ref[gw:2 * gw, :].astype(BF16), _NT, preferred_element_type=F32)
             + lax.dot_general(p2.astype(BF16), new_t[gw:2 * gw, :].astype(BF16), _NT, preferred_element_type=F32))
        outs.append(gather_heads(o / den))
        lses.append(gather_heads(jnp.broadcast_to(m + jnp.log(den), (nrow, gw))))

        ext = jnp.concatenate([cache_ref[...], new_t], axis=1)
        n_refs[g][...] = pltpu.roll(ext, wb + LANES - t_len, axis=1)[:, 0:wb]

    merged = _merge_groups(outs, lses)
    for g in range(N_GROUPS):
        for c in range(gw // LANES):
            col = c * MERGED_BLOCK + g * LANES
            om_ref[:, col:col + LANES] = merged[g][:, c * LANES:(c + 1) * LANES]


def _attn_sample(qkv, caches, *, t_len):
    nseq = caches[0].shape[0]
    rows = qkv.shape[0]
    cache_specs = [pl.BlockSpec((None,) + c.shape[1:], lambda b: (b, 0, 0)) for c in caches]
    outs = pl.pallas_call(
        functools.partial(_attn_sample_kernel, t_len=t_len),
        grid=(nseq,),
        in_specs=[pl.BlockSpec((t_len, 3 * ATTN_WIDTH), lambda b: (b, 0))] + cache_specs,
        out_specs=[pl.BlockSpec((t_len, ATTN_WIDTH), lambda b: (b, 0))] + cache_specs,
        out_shape=[jax.ShapeDtypeStruct((rows, ATTN_WIDTH), F32)]
        + [jax.ShapeDtypeStruct(c.shape, F32) for c in caches],
        compiler_params=_params(("parallel",)),
        name="attn_sample",
    )(qkv, *caches)
    return outs[0], outs[1:]


SSM_CHUNK = 16
N_LAM_POWERS = SSM_CHUNK + 1


def _ssm_prep_kernel(a_re_ref, a_im_ref, logdt_ref, bre_ref, bim_ref, pre_ref, pim_ref, bbre_ref, bbim_ref):
    a_re = a_re_ref[...]
    a_im = a_im_ref[...]
    dt = jnp.exp(logdt_ref[...])
    mag = jnp.exp(a_re * dt)
    ang = a_im * dt
    lam_re = mag * jnp.cos(ang)
    lam_im = mag * jnp.sin(ang)
    den = a_re * a_re + a_im * a_im
    num_re = lam_re - 1.0
    coef_re = ((num_re * a_re + lam_im * a_im) / den)[None]
    coef_im = ((lam_im * a_re - num_re * a_im) / den)[None]
    b_re = bre_ref[...]
    b_im = bim_ref[...]
    bbre_ref[...] = coef_re * b_re - coef_im * b_im
    bbim_ref[...] = coef_re * b_im + coef_im * b_re
    p_re, p_im = jnp.ones_like(lam_re), jnp.zeros_like(lam_im)
    for m in range(N_LAM_POWERS):
        pre_ref[m] = p_re
        pim_ref[m] = p_im
        p_re, p_im = p_re * lam_re - p_im * lam_im, p_re * lam_im + p_im * lam_re


def _ssm_prep(a_re, a_im, log_dt, b_re_t, b_im_t):
    g, n = a_re.shape
    pw = jax.ShapeDtypeStruct((N_LAM_POWERS, g, n), F32)
    bb = jax.ShapeDtypeStruct(b_re_t.shape, F32)
    return pl.pallas_call(
        _ssm_prep_kernel,
        out_shape=[pw, pw, bb, bb],
        name="ssm_prep",
    )(a_re, a_im, log_dt.reshape(g, 1), b_re_t, b_im_t)


SCAN_BLOCKS_PER_STEP = 4


def _ssm_rows_kernel(*refs, t_len, nseq):
    for k in range(refs[0].shape[0]):
        _ssm_rows_block(*[r.at[k] for r in refs], t_len=t_len, nseq=nseq)


def _ssm_rows_block(u_ref, bb_ref, ctre_ref, ctim_ref, pw_ref, d_ref, h0_ref, y_ref, fs_ref,
                    bu_scr, xre_scr, xim_scr, ud_scr, *, t_len, nseq):
    sl = STATE_LANES
    for j in range(t_len):
        ud_scr[j * nseq:(j + 1) * nseq, :] = u_ref[pl.ds(j, nseq, stride=t_len), :].astype(BF16)
    bu_scr[...] = jnp.dot(ud_scr[...], bb_ref[...].astype(BF16), preferred_element_type=F32)

    lam_re = jnp.broadcast_to(pw_ref[1:2, :], (nseq, sl))
    lam_im = jnp.broadcast_to(pw_ref[N_LAM_POWERS + 1:N_LAM_POWERS + 2, :], (nseq, sl))
    x_re = h0_ref[:, 0:sl]
    x_im = h0_ref[:, sl:2 * sl]
    for j in range(t_len):
        b_re = bu_scr[j * nseq:(j + 1) * nseq, 0:sl]
        b_im = bu_scr[j * nseq:(j + 1) * nseq, sl:2 * sl]
        x_re, x_im = (lam_re * x_re - lam_im * x_im + b_re, lam_re * x_im + lam_im * x_re + b_im)
        xre_scr[j * nseq:(j + 1) * nseq, :] = x_re.astype(BF16)
        xim_scr[j * nseq:(j + 1) * nseq, :] = x_im.astype(BF16)
    fs_ref[:, 0:sl] = x_re
    fs_ref[:, sl:2 * sl] = x_im

    y = (lax.dot_general(xre_scr[...], ctre_ref[...].astype(BF16), _NT, preferred_element_type=F32)
         - lax.dot_general(xim_scr[...], ctim_ref[...].astype(BF16), _NT, preferred_element_type=F32))
    d = d_ref[...]
    for j in range(t_len):
        y_ref[pl.ds(j, nseq, stride=t_len), :] = (
            y[j * nseq:(j + 1) * nseq, :] + d * u_ref[pl.ds(j, nseq, stride=t_len), :])


def _ssm_scan_rows(u, tabs, h0, *, t_len, nseq):
    nblk, rows, _ = u.shape
    kb = SCAN_BLOCKS_PER_STEP
    sl2 = 2 * STATE_LANES
    blk = lambda arr: pl.BlockSpec((kb,) + arr.shape[1:], lambda k: (k,) + (0,) * (arr.ndim - 1))
    ins = (u, tabs["bb"], tabs["ct_re"], tabs["ct_im"], tabs["pw_rows"], tabs["dvec"], h0)
    return pl.pallas_call(
        functools.partial(_ssm_rows_kernel, t_len=t_len, nseq=nseq),
        grid=(nblk // kb,),
        in_specs=[blk(a) for a in ins],
        out_specs=[blk(u), blk(h0)],
        out_shape=[jax.ShapeDtypeStruct(u.shape, F32), jax.ShapeDtypeStruct(h0.shape, F32)],
        scratch_shapes=[
            pltpu.VMEM((kb, rows, sl2), F32),
            pltpu.VMEM((kb, rows, STATE_LANES), BF16),
            pltpu.VMEM((kb, rows, STATE_LANES), BF16),
            pltpu.VMEM((kb, rows, LANES), BF16),
        ],
        compiler_params=_params(("parallel",)),
        name="ssm_scan_rows",
    )(*ins)


GATHER_STRIDE = 4
assert SSM_CHUNK == GATHER_STRIDE * GATHER_STRIDE


def _ssm_seq_kernel(u_ref, bb_ref, ctre_ref, ctim_ref, pw_ref, d_ref, h0_ref, y_ref, fs_ref,
                    g_scr, kt_scr, clt_scr, ucat_scr, e_scr, hp_scr, s1_scr, s2_scr):
    L = SSM_CHUNK
    sl = STATE_LANES
    seq = u_ref.shape[0]
    nch = seq // L

    @pl.when(pl.program_id(1) == 0)
    def _build_tables():
        lam_re = pw_ref[1:2, :]
        lam_im = pw_ref[N_LAM_POWERS + 1:N_LAM_POWERS + 2, :]
        q_re = ctre_ref[...]
        q_im = ctim_ref[...]
        ccs_t = jnp.concatenate([q_re, -q_im], axis=1).astype(BF16)
        kt_scr[...] = jnp.zeros(kt_scr.shape, BF16)
        g_re = bb_ref[:, 0:sl]
        g_im = bb_ref[:, sl:2 * sl]
        for m in range(L):
            j = L - 1 - m
            g_m = jnp.concatenate([g_re, g_im], axis=1).astype(BF16)
            g_scr[j * LANES:(j + 1) * LANES, :] = g_m
            k_m = lax.dot_general(g_m, ccs_t, _NT, preferred_element_type=F32).astype(BF16)
            for jj in range(L - m):
                i = jj + m
                kt_scr[jj * LANES:(jj + 1) * LANES, i * LANES:(i + 1) * LANES] = k_m
            g_re, g_im = g_re * lam_re - g_im * lam_im, g_re * lam_im + g_im * lam_re
            q_re, q_im = q_re * lam_re - q_im * lam_im, q_re * lam_im + q_im * lam_re
            clt_scr[m * LANES:(m + 1) * LANES, 0:sl] = q_re.astype(BF16)
            clt_scr[m * LANES:(m + 1) * LANES, sl:2 * sl] = (-q_im).astype(BF16)

    qs = GATHER_STRIDE
    for b in range(qs):
        s1_scr[b] = u_ref[pl.ds(b, seq // qs, stride=qs), :]
    for j in range(L):
        a, b = divmod(j, qs)
        ucat_scr[:, j * LANES:(j + 1) * LANES] = s1_scr[b, pl.ds(a, nch, stride=qs), :].astype(BF16)
    e_scr[...] = jnp.dot(ucat_scr[...], g_scr[...], preferred_element_type=F32)

    lc_re = pw_ref[L:L + 1, :]
    lc_im = pw_ref[N_LAM_POWERS + L:N_LAM_POWERS + L + 1, :]

    def chunk_step(c, h):
        h_re, h_im = h
        hp_scr[pl.ds(c, 1), 0:sl] = h_re
        hp_scr[pl.ds(c, 1), sl:2 * sl] = h_im
        e_re = e_scr[pl.ds(c, 1), 0:sl]
        e_im = e_scr[pl.ds(c, 1), sl:2 * sl]
        return (lc_re * h_re - lc_im * h_im + e_re, lc_re * h_im + lc_im * h_re + e_im)

    h_re, h_im = lax.fori_loop(0, nch, chunk_step, (h0_ref[:, 0:sl], h0_ref[:, sl:2 * sl]), unroll=8)
    fs_ref[:, 0:sl] = h_re
    fs_ref[:, sl:2 * sl] = h_im

    h_prev = hp_scr[...].astype(BF16)
    pair = 2 * LANES
    for p in range(L // 2):
        y2 = (jnp.dot(ucat_scr[:, 0:(p + 1) * pair], kt_scr[0:(p + 1) * pair, p * pair:(p + 1) * pair],
                      preferred_element_type=F32)
              + lax.dot_general(h_prev, clt_scr[p * pair:(p + 1) * pair, :], _NT, preferred_element_type=F32))
        for half in range(2):
            a, b = divmod(2 * p + half, qs)
            s2_scr[b, pl.ds(a, nch, stride=qs), :] = y2[:, half * LANES:(half + 1) * LANES]
    d = d_ref[...]
    for b in range(qs):
        y_ref[pl.ds(b, seq // qs, stride=qs), :] = s2_scr[b] + d * s1_scr[b]


def _ssm_scan_seq(u, tabs, h0, *, nseq, seq):
    nblk = u.shape[0]
    L = SSM_CHUNK
    nch = seq // L
    sl2 = 2 * STATE_LANES
    blk = lambda arr: pl.BlockSpec((None,) + arr.shape[1:], lambda k, b: (k,) + (0,) * (arr.ndim - 1))
    row_spec = pl.BlockSpec((None, seq, LANES), lambda k, b: (k, b, 0))
    st_spec = pl.BlockSpec((None, None, 1, sl2), lambda k, b: (k, b, 0, 0))
    return pl.pallas_call(
        _ssm_seq_kernel,
        grid=(nblk, nseq),
        in_specs=[row_spec, blk(tabs["bb"]), blk(tabs["ct_re"]), blk(tabs["ct_im"]), blk(tabs["pw_rows"]),
                  blk(tabs["dvec"]), st_spec],
        out_specs=[row_spec, st_spec],
        out_shape=[jax.ShapeDtypeStruct(u.shape, F32), jax.ShapeDtypeStruct(h0.shape, F32)],
        scratch_shapes=[
            pltpu.VMEM((L * LANES, sl2), BF16),
            pltpu.VMEM((L * LANES, L * LANES), BF16),
            pltpu.VMEM((L * LANES, sl2), BF16),
            pltpu.VMEM((nch, L * LANES), BF16),
            pltpu.VMEM((nch, sl2), F32),
            pltpu.VMEM((nch, sl2), F32),
            pltpu.VMEM((GATHER_STRIDE, seq // GATHER_STRIDE, LANES), F32),
            pltpu.VMEM((GATHER_STRIDE, seq // GATHER_STRIDE, LANES), F32),
        ],
        compiler_params=_params(("arbitrary", "arbitrary")),
        name="ssm_scan_seq",
    )(u, tabs["bb"], tabs["ct_re"], tabs["ct_im"], tabs["pw_rows"], tabs["dvec"], h0)


def _block_diag(x):
    kg, a, b = x.shape
    k = kg // GROUPS_PER_BLOCK
    tiled = jnp.tile(x.reshape(k, GROUPS_PER_BLOCK * a, b), (1, 1, GROUPS_PER_BLOCK))
    row_blk = lax.broadcasted_iota(jnp.int32, tiled.shape, 1) // a
    col_blk = lax.broadcasted_iota(jnp.int32, tiled.shape, 2) // b
    return jnp.where(row_blk == col_blk, tiled, 0)


def _lane_blocks(x):
    lead = x.shape[:-2]
    x = x.reshape(lead + (SSM_LANE_BLOCKS, STATE_LANES))
    return jnp.moveaxis(x, -2, 0)


def _ssm_tables(P, j):
    b_re_t = jnp.transpose(P["ssm_b_re"][j], (2, 0, 1))
    b_im_t = jnp.transpose(P["ssm_b_im"][j], (2, 0, 1))
    p_re, p_im, bb_re, bb_im = _ssm_prep(P["ssm_a_re"][j], P["ssm_a_im"][j], P["ssm_log_dt"][j], b_re_t, b_im_t)
    bb_re = jnp.swapaxes(bb_re, 0, 1)
    bb_im = jnp.swapaxes(bb_im, 0, 1)
    bb = jnp.concatenate([_block_diag(bb_re), _block_diag(bb_im)], axis=2)
    ct_re = _block_diag(P["ssm_c_re"][j])
    ct_im = _block_diag(P["ssm_c_im"][j])
    pw_rows = jnp.concatenate([_lane_blocks(p_re), _lane_blocks(p_im)], axis=1)
    dvec = P["ssm_d"][j].reshape(SSM_LANE_BLOCKS, 1, LANES)
    return dict(bb=bb, ct_re=ct_re, ct_im=ct_im, pw_rows=pw_rows, dvec=dvec)


def _ssm_scan(u, tabs, state, *, nseq, seq):
    if state is None:
        h0 = jnp.zeros((SSM_LANE_BLOCKS, nseq, 1, 2 * STATE_LANES), F32)
        y, fs = _ssm_scan_seq(u, tabs, h0, nseq=nseq, seq=seq)
        fs = fs.reshape(SSM_LANE_BLOCKS, nseq, 2 * STATE_LANES)
    else:
        y, fs = _ssm_scan_rows(u, tabs, _state_to_blocks(state), t_len=seq, nseq=nseq)
    return y, _state_from_blocks(fs)


def _state_from_blocks(fs):
    nblk, nseq, _ = fs.shape
    fs = fs.reshape(nblk, nseq, 2, GROUPS_PER_BLOCK, SSM_STATE)
    fs = jnp.transpose(fs, (1, 0, 3, 4, 2))
    return fs.reshape(nseq, SSM_GROUPS, SSM_STATE, 2)


def _state_to_blocks(state):
    nseq = state.shape[0]
    s = state.reshape(nseq, SSM_LANE_BLOCKS, GROUPS_PER_BLOCK, SSM_STATE, 2)
    return jnp.transpose(s, (1, 0, 4, 2, 3)).reshape(SSM_LANE_BLOCKS, nseq, 2 * STATE_LANES)


def _to_window_minor(c):
    return jnp.transpose(c[0], (0, 2, 3, 4, 1)).reshape(c.shape[1], 2 * GROUP_WIDTH, c.shape[2])


def _from_window_minor(c):
    c = c.reshape(c.shape[0], 2, HEADS_PER_GROUP, HEAD_DIM, c.shape[2])
    return jnp.transpose(c, (0, 4, 1, 2, 3))[None]


def _kv_tail_kernel(k_ref, v_ref, o_ref):
    o_ref[0:GROUP_WIDTH, :] = k_ref[...].T
    o_ref[GROUP_WIDTH:2 * GROUP_WIDTH, :] = v_ref[...].T


def _kv_tail(qkv, *, group, nseq, seq):
    wb = min(DILATED_GROUPS[group][0], seq)
    kcol = (ATTN_WIDTH + group * GROUP_WIDTH) // GROUP_WIDTH
    vcol = (2 * ATTN_WIDTH + group * GROUP_WIDTH) // GROUP_WIDTH
    tail = lambda col: pl.BlockSpec((wb, GROUP_WIDTH), lambda b: ((b + 1) * (seq // wb) - 1, col))
    return pl.pallas_call(
        _kv_tail_kernel,
        grid=(nseq,),
        in_specs=[tail(kcol), tail(vcol)],
        out_specs=pl.BlockSpec((None, 2 * GROUP_WIDTH, wb), lambda b: (b, 0, 0)),
        out_shape=jax.ShapeDtypeStruct((nseq, 2 * GROUP_WIDTH, wb), F32),
        compiler_params=_params(("parallel",)),
        name="kv_tail",
    )(qkv, qkv)


def _trunks(xs, caches, ssm_state, W, *, nseqs, seqs):
    ffn_w = [W[name + "_f32"][0, 0].astype(BF16) for name in FFN_WEIGHTS]

    st = _stage(xs, W, ffn_w, 0, 0, cast_next=(0, 1))
    hs = st["x"]
    qkvs = _proj(hs, W["norms"], 0, W["qkv"])
    merged_p = _attn_prompt(qkvs[0], seq=seqs[0])
    merged_s, new_caches = _attn_sample(qkvs[1], caches, t_len=seqs[1])
    st = _stage(hs, W, st["next_w"], 0, 1, pre="attn", pre_ins=(merged_p, merged_s), cast_next=(1, 0))

    st = _stage(st["x"], W, st["next_w"], 1, 0, post="proj_slabs", cast_next=(1, 1))
    us = st["proj"]
    y_p, state_p = _ssm_scan(us[0], W["ssm"], None, nseq=nseqs[0], seq=seqs[0])
    y_s, state_s = _ssm_scan(us[1], W["ssm"], ssm_state, nseq=nseqs[1], seq=seqs[1])
    outs = _stage(st["x"], W, st["next_w"], 1, 1, pre="ssm", pre_ins=(y_p, y_s), post="final")["x"]
    return outs, qkvs[0], new_caches, (state_p, state_s)


def kernel(x_prompt, x_sample, cache_kv_g0, cache_kv_g1, cache_kv_g2, state_ssm, norm_g, final_norm_g,
           ffn_w_gate, ffn_w_up, ffn_w_down, attn_w_qkv, attn_w_o, ssm_w_in, ssm_a_re, ssm_a_im, ssm_log_dt,
           ssm_b_re, ssm_b_im, ssm_c_re, ssm_c_im, ssm_d, ssm_w_glu, ssm_b_glu, ssm_w_out):
    batch, seq, _ = x_prompt.shape
    dec_batch, dec_seq, _ = x_sample.shape
    P = dict(ssm_a_re=ssm_a_re, ssm_a_im=ssm_a_im, ssm_log_dt=ssm_log_dt, ssm_b_re=ssm_b_re,
             ssm_b_im=ssm_b_im, ssm_c_re=ssm_c_re, ssm_c_im=ssm_c_im, ssm_d=ssm_d)
    W = dict(
        norms=norm_g.reshape(norm_g.shape[0], norm_g.shape[1], 1, D_MODEL),
        final_norm=final_norm_g.reshape(1, D_MODEL),
        gate_f32=ffn_w_gate, up_f32=ffn_w_up, down_f32=ffn_w_down,
        qkv=attn_w_qkv.astype(BF16),
        o=jnp.swapaxes(attn_w_o.reshape(-1, N_GROUPS, GROUP_WIDTH // LANES, LANES, D_MODEL), 1, 2)
        .reshape(attn_w_o.shape).astype(BF16),
        ssm_w_in=ssm_w_in.astype(BF16), ssm_w_glu=ssm_w_glu.astype(BF16), ssm_w_out=ssm_w_out.astype(BF16),
        ssm_b_glu=ssm_b_glu.reshape(ssm_b_glu.shape[0], 1, D_MODEL),
        ssm=_ssm_tables(P, 0),
    )

    xs = (x_prompt.reshape(batch * seq, D_MODEL), x_sample.reshape(dec_batch * dec_seq, D_MODEL))
    caches = [_to_window_minor(c) for c in (cache_kv_g0, cache_kv_g1, cache_kv_g2)]
    (y_p, y_s), qkv_p, new_caches, (ssm_p, ssm_s) = _trunks(
        xs, caches, state_ssm[0], W, nseqs=(batch, dec_batch), seqs=(seq, dec_seq))
    kv_prompt = [_from_window_minor(_kv_tail(qkv_p, group=g, nseq=batch, seq=seq)) for g in range(N_GROUPS)]
    kv_sample = [_from_window_minor(nc) for nc in new_caches]

    return (y_p.reshape(batch, seq, D_MODEL), y_s.reshape(dec_batch, dec_seq, D_MODEL),
            kv_prompt[0], kv_prompt[1], kv_prompt[2], ssm_p[None],
            kv_sample[0], kv_sample[1], kv_sample[2], ssm_s[None])
```

```python
import functools
import math

import jax
import jax.numpy as jnp
from jax import lax
from jax.experimental import pallas as pl
from jax.experimental.pallas import tpu as pltpu

F32 = jnp.float32
BF16 = jnp.bfloat16

D_MODEL = 1024
D_FF = 2816
HEAD_DIM = 64
HEADS_PER_GROUP = 4
DILATED_GROUPS = ((128, 1), (512, 4), (2048, 16))
N_GROUPS = len(DILATED_GROUPS)
N_HEADS = N_GROUPS * HEADS_PER_GROUP
GROUP_WIDTH = HEADS_PER_GROUP * HEAD_DIM
ATTN_WIDTH = N_HEADS * HEAD_DIM
BAND = 128
SSM_CH = 16
SSM_GROUPS = D_MODEL // SSM_CH
SSM_STATE = 64
NORM_EPS = 1e-6

LANES = 128
V7X_VMEM_BYTES = 64 * 1024 * 1024
VMEM_LIMIT = V7X_VMEM_BYTES - 8 * 1024 * 1024

SSM_LANE_BLOCKS = D_MODEL // LANES
GROUPS_PER_BLOCK = LANES // SSM_CH
STATE_LANES = GROUPS_PER_BLOCK * SSM_STATE

assert all(w // d == BAND for w, d in DILATED_GROUPS)
assert all(d & (d - 1) == 0 for _, d in DILATED_GROUPS)


def _params(sem, vmem=VMEM_LIMIT):
    return pltpu.CompilerParams(dimension_semantics=sem, vmem_limit_bytes=vmem)


def _const_spec(shape):
    nd = len(shape)
    return pl.BlockSpec(shape, lambda *_: (0,) * nd, pipeline_mode=pl.Buffered(1))


def _rms(x, g):
    return x * lax.rsqrt(jnp.mean(x * x, axis=-1, keepdims=True) + NORM_EPS) * g


MXU_TILE = 256
FFN_CHUNK_BOUNDS = (0, 6 * MXU_TILE, D_FF)


N_STREAMS = 2


FFN_WEIGHTS = ("gate", "up", "down")


def _stage_kernel(*refs, pre, post, cast_next):
    refs = list(refs)
    take = lambda n: [refs.pop(0) for _ in range(n)]
    x_refs = take(N_STREAMS)
    if pre == "attn":
        om_refs = take(N_STREAMS)
        (wo_ref,) = take(1)
    elif pre == "ssm":
        y_refs = take(N_STREAMS)
        wglu_ref, bglu_ref, wout_ref = take(3)
    g_ref, wg_ref, wu_ref, wd_ref = take(4)
    if post == "final":
        (gf_ref,) = take(1)
    elif post == "proj_slabs":
        gp_ref, wp_ref = take(2)
    next_f32 = take(len(FFN_WEIGHTS)) if cast_next else []
    o_refs = take(N_STREAMS)
    if post == "proj_slabs":
        p_refs = take(N_STREAMS)
    for src, dst in zip(next_f32, take(len(next_f32))):
        dst[...] = src[...].astype(BF16)

    def run(s):
        x = x_refs[s][...]
        if pre == "attn":
            x = x + jnp.dot(om_refs[s][...].astype(BF16), wo_ref[...], preferred_element_type=F32)
        elif pre == "ssm":
            y = jnp.concatenate([y_refs[s][k] for k in range(y_refs[s].shape[0])], axis=1)
            g = jax.nn.gelu(y, approximate=True)
            gate = jnp.dot(g.astype(BF16), wglu_ref[...], preferred_element_type=F32) + bglu_ref[...]
            z = (g * jax.nn.sigmoid(gate)).astype(BF16)
            x = x + jnp.dot(z, wout_ref[...], preferred_element_type=F32)

        xb = _rms(x, g_ref[...]).astype(BF16)
        acc = None
        for lo, hi in zip(FFN_CHUNK_BOUNDS[:-1], FFN_CHUNK_BOUNDS[1:]):
            gate = jnp.dot(xb, wg_ref[:, lo:hi], preferred_element_type=F32)
            up = jnp.dot(xb, wu_ref[:, lo:hi], preferred_element_type=F32)
            hmid = (gate * jax.nn.sigmoid(gate) * up).astype(BF16)
            part = jnp.dot(hmid, wd_ref[lo:hi, :], preferred_element_type=F32)
            acc = part if acc is None else acc + part
        out = x + 0.5 * acc

        if post == "final":
            out = _rms(out, gf_ref[...])
        o_refs[s][...] = out
        if post == "proj_slabs":
            proj = jnp.dot(_rms(out, gp_ref[...]).astype(BF16), wp_ref[...], preferred_element_type=F32)
            for k in range(p_refs[s].shape[0]):
                p_refs[s][k] = proj[:, k * LANES:(k + 1) * LANES]

    run(0)
    pl.when(pl.program_id(0) == 0)(lambda: run(1))


def _pick_spec(arr, *idx):
    tail = arr.shape[len(idx):]
    index = tuple(idx) + (0,) * len(tail)
    return pl.BlockSpec((None,) * len(idx) + tail, lambda *_: index, pipeline_mode=pl.Buffered(1))


STAGE_TM = 512


def _whole_spec(shape):
    nd = len(shape)
    return pl.BlockSpec(shape, lambda *_: (0,) * nd)


def _row_specs(xs):
    return [pl.BlockSpec((STAGE_TM, xs[0].shape[1]), lambda i: (i, 0)), _whole_spec(xs[1].shape)]


def _slab_specs(ys):
    return [pl.BlockSpec((SSM_LANE_BLOCKS, STAGE_TM, LANES), lambda i: (0, i, 0)), _whole_spec(ys[1].shape)]


def _stage(xs, W, ffn_w, layer, half, *, pre=None, pre_ins=None, post=None, cast_next=None):
    steps = xs[0].shape[0] // STAGE_TM
    ins, specs = list(xs), _row_specs(xs)
    if pre == "attn":
        ins += list(pre_ins) + [W["o"]]
        specs += _row_specs(pre_ins) + [_pick_spec(W["o"], 0)]
    elif pre == "ssm":
        ins += list(pre_ins) + [W["ssm_w_glu"], W["ssm_b_glu"], W["ssm_w_out"]]
        specs += _slab_specs(pre_ins) + [_pick_spec(W["ssm_w_glu"], 0), _pick_spec(W["ssm_b_glu"], 0),
                                         _pick_spec(W["ssm_w_out"], 0)]
    ins += [W["norms"]] + list(ffn_w)
    specs += [_pick_spec(W["norms"], layer, 2 * half)] + [_pick_spec(w) for w in ffn_w]
    out_specs = _row_specs(xs)
    out_shape = [jax.ShapeDtypeStruct(x.shape, F32) for x in xs]
    if post == "final":
        ins += [W["final_norm"]]
        specs += [_const_spec((1, D_MODEL))]
    elif post == "proj_slabs":
        ins += [W["norms"], W["ssm_w_in"]]
        specs += [_pick_spec(W["norms"], layer, 1), _pick_spec(W["ssm_w_in"], 0)]
    if cast_next:
        nl, nh = cast_next
        rows_per_step = D_MODEL // steps
        steps_per_col = steps // (D_MODEL // LANES)
        assert rows_per_step * steps == D_MODEL and steps_per_col * (D_MODEL // LANES) == steps
        ins += [W[name + "_f32"] for name in FFN_WEIGHTS]
        specs += [pl.BlockSpec((None, None, rows_per_step, D_FF), lambda i: (nl, nh, i, 0))] * 2
        specs += [pl.BlockSpec((None, None, D_FF, LANES), lambda i: (nl, nh, 0, i // steps_per_col))]
    if post == "proj_slabs":
        slabs = [jax.ShapeDtypeStruct((SSM_LANE_BLOCKS, x.shape[0], LANES), F32) for x in xs]
        out_specs += _slab_specs(slabs)
        out_shape += slabs
    if cast_next:
        out_specs += [pl.BlockSpec((rows_per_step, D_FF), lambda i: (i, 0))] * 2
        out_specs += [pl.BlockSpec((D_FF, LANES), lambda i: (0, i // steps_per_col))]
        out_shape += [jax.ShapeDtypeStruct(s, BF16) for s in ((D_MODEL, D_FF), (D_MODEL, D_FF), (D_FF, D_MODEL))]
    outs = list(pl.pallas_call(
        functools.partial(_stage_kernel, pre=pre, post=post, cast_next=bool(cast_next)),
        grid=(steps,),
        in_specs=specs,
        out_specs=out_specs,
        out_shape=out_shape,
        compiler_params=_params(("arbitrary",)),
        name="stage_" + "_".join(s for s in (pre, "ffn", post) if s),
    )(*ins))
    take = lambda n: [outs.pop(0) for _ in range(n)]
    return dict(x=take(N_STREAMS), proj=take(N_STREAMS) if post == "proj_slabs" else None,
                next_w=take(len(FFN_WEIGHTS)) if cast_next else None)


def _qkv_kernel(x0_ref, x1_ref, g_ref, w_ref, c0_ref, c1_ref, c2_ref,
                o0_ref, o1_ref, om_ref, n0_ref, n1_ref, n2_ref, *, t_len):
    def run(x_ref, o_ref):
        xb = _rms(x_ref[...], g_ref[...]).astype(BF16)
        o_ref[...] = jnp.dot(xb, w_ref[...], preferred_element_type=F32)

    i = pl.program_id(0)
    run(x0_ref, o0_ref)
    pl.when(i == 0)(lambda: run(x1_ref, o1_ref))
    qkv_seq = o1_ref.at[pl.ds(pl.multiple_of(i * t_len, t_len), t_len), :]
    _attn_sample_kernel(qkv_seq, c0_ref, c1_ref, c2_ref, om_ref, n0_ref, n1_ref, n2_ref, t_len=t_len)


def _qkv_and_sample_attn(xs, norms, layer, w, caches, *, t_len):
    n = w.shape[-1]
    steps = xs[0].shape[0] // STAGE_TM
    nseq = caches[0].shape[0]
    assert steps == nseq and xs[1].shape[0] == nseq * t_len
    qkv_shapes = [jax.ShapeDtypeStruct((x.shape[0], n), F32) for x in xs]
    cache_specs = [pl.BlockSpec((None,) + c.shape[1:], lambda b: (b, 0, 0)) for c in caches]
    outs = pl.pallas_call(
        functools.partial(_qkv_kernel, t_len=t_len),
        grid=(steps,),
        in_specs=_row_specs(xs) + [_pick_spec(norms, layer, 1), _pick_spec(w, 0)] + cache_specs,
        out_specs=_row_specs(qkv_shapes) + [pl.BlockSpec((t_len, ATTN_WIDTH), lambda b: (b, 0))] + cache_specs,
        out_shape=qkv_shapes
        + [jax.ShapeDtypeStruct((xs[1].shape[0], ATTN_WIDTH), F32)]
        + [jax.ShapeDtypeStruct(c.shape, F32) for c in caches],
        compiler_params=_params(("arbitrary",)),
        name="qkv_proj_sample_attn",
    )(*xs, norms, w, *caches)
    return outs[:N_STREAMS], outs[N_STREAMS], outs[N_STREAMS + 1:]


def _alibi_slope(head):
    return 2.0 ** (-8.0 * (head + 1) / N_HEADS)


def _head_lane_mask(hh):
    lane = lax.broadcasted_iota(jnp.int32, (1, LANES), 1)
    return (lane >= hh * HEAD_DIM) & (lane < (hh + 1) * HEAD_DIM)


ATTN_TILE_TOKENS = 2048
_NT = (((1,), (1,)), ((), ()))


def _merge_groups(outs, lses):
    m = jnp.maximum(jnp.maximum(lses[0], lses[1]), lses[2])
    es = [jnp.exp(l - m) for l in lses]
    tot = es[0] + es[1] + es[2]
    return [o * (e / tot) for o, e in zip(outs, es)]


def _attn_kernel(*refs, tiles_per_seq):
    om_ref, o_scr, l_scr = refs[5 * N_GROUPS:]
    has_prev = (pl.program_id(0) % tiles_per_seq) != 0
    for g in range(N_GROUPS):
        _attn_group(*refs[5 * g:5 * g + 5], o_scr.at[g], l_scr.at[g], group=g, has_prev=has_prev)
    merged = _merge_groups([o_scr[g] for g in range(N_GROUPS)], [l_scr[g] for g in range(N_GROUPS)])
    for g in range(N_GROUPS):
        om_ref[:, g * LANES:(g + 1) * LANES] = merged[g].astype(BF16)


def _attn_group(q_ref, kp_ref, kc_ref, vp_ref, vc_ref, o_ref, l_ref, *, group, has_prev):
    dil = DILATED_GROUPS[group][1]
    span = BAND * dil
    nblk = q_ref.shape[0] // span
    c = pl.program_id(1)

    row = lax.broadcasted_iota(jnp.int32, (2 * BAND, 2 * BAND), 0)
    kj = lax.broadcasted_iota(jnp.int32, (2 * BAND, 2 * BAND), 1)
    delta = BAND + (row & (BAND - 1)) - kj
    head0 = group * HEADS_PER_GROUP
    slope_a = jnp.where(c == 0, _alibi_slope(head0), _alibi_slope(head0 + 2)).astype(F32)
    slope_b = jnp.where(c == 0, _alibi_slope(head0 + 1), _alibi_slope(head0 + 3)).astype(F32)
    slope = jnp.where(row < BAND, slope_a, slope_b)
    band = (delta >= 0) & (delta <= BAND)
    bias = jnp.where(band, -slope * (delta * dil).astype(F32), -jnp.inf)
    bias_first = jnp.where((kj >= BAND) | has_prev, bias, -jnp.inf)
    first_head = _head_lane_mask(0)

    def rows(ref, start):
        return ref[pl.ds(start, BAND), :] if dil == 1 else ref[pl.ds(start, BAND, stride=dil), :]

    for r in range(dil):
        k_prev = rows(kp_ref, r).astype(BF16)
        v_prev = rows(vp_ref, r).astype(BF16)
        for b in range(nblk):
            start = b * span + r
            q = rows(q_ref, start) * (HEAD_DIM ** -0.5)
            k_cur = rows(kc_ref, start).astype(BF16)
            v_cur = rows(vc_ref, start).astype(BF16)
            k = jnp.concatenate([k_prev, k_cur], axis=0)
            v = jnp.concatenate([v_prev, v_cur], axis=0)
            q2 = jnp.concatenate([jnp.where(first_head, q, 0.0), jnp.where(first_head, 0.0, q)], axis=0)
            s = lax.dot_general(q2.astype(BF16), k, _NT, preferred_element_type=F32)
            s = s + (bias_first if b == 0 else bias)
            m = jnp.max(s, axis=-1, keepdims=True)
            p = jnp.exp(s - m)
            den = jnp.sum(p, axis=-1, keepdims=True)
            o2 = jnp.dot(p.astype(BF16), v, preferred_element_type=F32) / den
            l2 = m + jnp.log(den)
            o = jnp.where(first_head, o2[:BAND], o2[BAND:])
            lse = jnp.where(first_head, l2[:BAND], l2[BAND:])
            if dil == 1:
                o_ref[pl.ds(start, BAND), :] = o
                l_ref[pl.ds(start, BAND), :] = lse
            else:
                o_ref[pl.ds(start, BAND, stride=dil), :] = o
                l_ref[pl.ds(start, BAND, stride=dil), :] = lse
            k_prev, v_prev = k_cur, v_cur


MERGED_BLOCK = N_GROUPS * LANES


def _attn_prompt(qkv, *, seq):
    rows = qkv.shape[0]
    tt = ATTN_TILE_TOKENS
    specs = []
    for g, (_, dil) in enumerate(DILATED_GROUPS):
        span = BAND * dil
        nblk = tt // span
        cur = lambda col: pl.BlockSpec((tt, LANES), lambda i, c: (i, col + c))
        prev = lambda col, span=span, nblk=nblk: pl.BlockSpec(
            (span, LANES), lambda i, c: (jnp.maximum(i * nblk - 1, 0), col + c))
        qcol = g * GROUP_WIDTH // LANES
        kcol = (ATTN_WIDTH + g * GROUP_WIDTH) // LANES
        vcol = (2 * ATTN_WIDTH + g * GROUP_WIDTH) // LANES
        specs += [cur(qcol), prev(kcol), cur(kcol), prev(vcol), cur(vcol)]
    return pl.pallas_call(
        functools.partial(_attn_kernel, tiles_per_seq=seq // tt),
        grid=(rows // tt, GROUP_WIDTH // LANES),
        in_specs=specs,
        out_specs=pl.BlockSpec((tt, MERGED_BLOCK), lambda i, c: (i, c)),
        out_shape=jax.ShapeDtypeStruct((rows, ATTN_WIDTH), BF16),
        scratch_shapes=[pltpu.VMEM((N_GROUPS, tt, LANES), F32), pltpu.VMEM((N_GROUPS, tt, LANES), F32)],
        compiler_params=_params(("parallel", "parallel")),
        name="attn_prompt",
    )(*([qkv] * len(specs)))


def _attn_sample_kernel(qkv_ref, c0_ref, c1_ref, c2_ref, om_ref, n0_ref, n1_ref, n2_ref, *, t_len):
    cache_refs = (c0_ref, c1_ref, c2_ref)
    n_refs = (n0_ref, n1_ref, n2_ref)
    outs, lses = [], []
    gw = GROUP_WIDTH
    nrow = HEADS_PER_GROUP * t_len
    row_head = lax.broadcasted_iota(jnp.int32, (nrow, gw), 0) // t_len
    col_head = lax.broadcasted_iota(jnp.int32, (nrow, gw), 1) // HEAD_DIM
    out_col_head = lax.broadcasted_iota(jnp.int32, (t_len, gw), 1) // HEAD_DIM

    def gather_heads(x):
        acc = jnp.zeros((t_len, gw), F32)
        for h in range(HEADS_PER_GROUP):
            acc = acc + jnp.where(out_col_head == h, x[h * t_len:(h + 1) * t_len, :], 0.0)
        return acc

    for g, (win, dil) in enumerate(DILATED_GROUPS):
        cache_ref = cache_refs[g]
        wb = cache_ref.shape[1]
        q0 = g * gw
        k0 = ATTN_WIDTH + g * gw
        v0 = 2 * ATTN_WIDTH + g * gw

        kv_new = jnp.concatenate([qkv_ref[:, k0:k0 + gw], qkv_ref[:, v0:v0 + gw]], axis=1)
        kv_pad = jnp.concatenate([kv_new, jnp.zeros((LANES - t_len, 2 * gw), F32)], axis=0)
        new_t = kv_pad.T

        q = qkv_ref[:, q0:q0 + gw] * (HEAD_DIM ** -0.5)
        q_bd = jnp.where(row_head == col_head, jnp.concatenate([q] * HEADS_PER_GROUP, axis=0), 0.0).astype(BF16)

        def bias(ncol, base):
            r = lax.broadcasted_iota(jnp.int32, (nrow, ncol), 0)
            w = lax.broadcasted_iota(jnp.int32, (nrow, ncol), 1)
            dist = base + (r % t_len) - w
            ok = (dist >= 0) & (dist <= win) & ((dist & (dil - 1)) == 0)
            slope = jnp.zeros((nrow, ncol), F32)
            for h in range(HEADS_PER_GROUP):
                slope = jnp.where(r // t_len == h, _alibi_slope(g * HEADS_PER_GROUP + h), slope)
            return jnp.where(ok, -slope * dist.astype(F32), -jnp.inf)

        s1 = jnp.dot(q_bd, cache_ref[0:gw, :].astype(BF16), preferred_element_type=F32) + bias(wb, wb)
        pad_ok = lax.broadcasted_iota(jnp.int32, (nrow, LANES), 1) < t_len
        s2 = jnp.dot(q_bd, new_t[0:gw, :].astype(BF16), preferred_element_type=F32)
        s2 = jnp.where(pad_ok, s2 + bias(LANES, 0), -jnp.inf)
        m = jnp.maximum(jnp.max(s1, axis=-1, keepdims=True), jnp.max(s2, axis=-1, keepdims=True))
        p1 = jnp.exp(s1 - m)
        p2 = jnp.exp(s2 - m)
        den = jnp.sum(p1, axis=-1, keepdims=True) + jnp.sum(p2, axis=-1, keepdims=True)
        o = (lax.dot_general(p1.astype(BF16), cache_ref[gw:2 * gw, :].astype(BF16), _NT, preferred_element_type=F32)
             + lax.dot_general(p2.astype(BF16), new_t[gw:2 * gw, :].astype(BF16), _NT, preferred_element_type=F32))
        outs.append(gather_heads(o / den))
        lses.append(gather_heads(jnp.broadcast_to(m + jnp.log(den), (nrow, gw))))

        ext = jnp.concatenate([cache_ref[...], new_t], axis=1)
        n_refs[g][...] = pltpu.roll(ext, wb + LANES - t_len, axis=1)[:, 0:wb]

    merged = _merge_groups(outs, lses)
    for g in range(N_GROUPS):
        for c in range(gw // LANES):
            col = c * MERGED_BLOCK + g * LANES
            om_ref[:, col:col + LANES] = merged[g][:, c * LANES:(c + 1) * LANES]


SSM_CHUNK = 16
N_LAM_POWERS = SSM_CHUNK + 1


def _ssm_prep_kernel(a_re_ref, a_im_ref, logdt_ref, bre_ref, bim_ref, pre_ref, pim_ref, bbre_ref, bbim_ref):
    a_re = a_re_ref[...]
    a_im = a_im_ref[...]
    dt = jnp.exp(logdt_ref[...])
    mag = jnp.exp(a_re * dt)
    ang = a_im * dt
    lam_re = mag * jnp.cos(ang)
    lam_im = mag * jnp.sin(ang)
    den = a_re * a_re + a_im * a_im
    num_re = lam_re - 1.0
    coef_re = ((num_re * a_re + lam_im * a_im) / den)[None]
    coef_im = ((lam_im * a_re - num_re * a_im) / den)[None]
    b_re = bre_ref[...]
    b_im = bim_ref[...]
    bbre_ref[...] = coef_re * b_re - coef_im * b_im
    bbim_ref[...] = coef_re * b_im + coef_im * b_re
    p_re, p_im = jnp.ones_like(lam_re), jnp.zeros_like(lam_im)
    for m in range(N_LAM_POWERS):
        pre_ref[m] = p_re
        pim_ref[m] = p_im
        p_re, p_im = p_re * lam_re - p_im * lam_im, p_re * lam_im + p_im * lam_re


def _ssm_prep(a_re, a_im, log_dt, b_re_t, b_im_t):
    g, n = a_re.shape
    pw = jax.ShapeDtypeStruct((N_LAM_POWERS, g, n), F32)
    bb = jax.ShapeDtypeStruct(b_re_t.shape, F32)
    return pl.pallas_call(
        _ssm_prep_kernel,
        out_shape=[pw, pw, bb, bb],
        name="ssm_prep",
    )(a_re, a_im, log_dt.reshape(g, 1), b_re_t, b_im_t)


SCAN_BLOCKS_PER_STEP = 4


def _ssm_rows_kernel(*refs, t_len, nseq):
    for k in range(refs[0].shape[0]):
        _ssm_rows_block(*[r.at[k] for r in refs], t_len=t_len, nseq=nseq)


def _ssm_rows_block(u_ref, bb_ref, ctre_ref, ctim_ref, pw_ref, d_ref, h0_ref, y_ref, fs_ref,
                    bu_scr, xre_scr, xim_scr, ud_scr, *, t_len, nseq):
    sl = STATE_LANES
    for j in range(t_len):
        ud_scr[j * nseq:(j + 1) * nseq, :] = u_ref[pl.ds(j, nseq, stride=t_len), :].astype(BF16)
    bu_scr[...] = jnp.dot(ud_scr[...], bb_ref[...].astype(BF16), preferred_element_type=F32)

    lam_re = jnp.broadcast_to(pw_ref[1:2, :], (nseq, sl))
    lam_im = jnp.broadcast_to(pw_ref[N_LAM_POWERS + 1:N_LAM_POWERS + 2, :], (nseq, sl))
    x_re = h0_ref[:, 0:sl]
    x_im = h0_ref[:, sl:2 * sl]
    for j in range(t_len):
        b_re = bu_scr[j * nseq:(j + 1) * nseq, 0:sl]
        b_im = bu_scr[j * nseq:(j + 1) * nseq, sl:2 * sl]
        x_re, x_im = (lam_re * x_re - lam_im * x_im + b_re, lam_re * x_im + lam_im * x_re + b_im)
        xre_scr[j * nseq:(j + 1) * nseq, :] = x_re.astype(BF16)
        xim_scr[j * nseq:(j + 1) * nseq, :] = x_im.astype(BF16)
    fs_ref[:, 0:sl] = x_re
    fs_ref[:, sl:2 * sl] = x_im

    y = (lax.dot_general(xre_scr[...], ctre_ref[...].astype(BF16), _NT, preferred_element_type=F32)
         - lax.dot_general(xim_scr[...], ctim_ref[...].astype(BF16), _NT, preferred_element_type=F32))
    d = d_ref[...]
    for j in range(t_len):
        y_ref[pl.ds(j, nseq, stride=t_len), :] = (
            y[j * nseq:(j + 1) * nseq, :] + d * u_ref[pl.ds(j, nseq, stride=t_len), :])


def _ssm_scan_rows(u, tabs, h0, *, t_len, nseq):
    nblk, rows, _ = u.shape
    kb = SCAN_BLOCKS_PER_STEP
    sl2 = 2 * STATE_LANES
    blk = lambda arr: pl.BlockSpec((kb,) + arr.shape[1:], lambda k: (k,) + (0,) * (arr.ndim - 1))
    ins = (u, tabs["bb"], tabs["ct_re"], tabs["ct_im"], tabs["pw_rows"], tabs["dvec"], h0)
    return pl.pallas_call(
        functools.partial(_ssm_rows_kernel, t_len=t_len, nseq=nseq),
        grid=(nblk // kb,),
        in_specs=[blk(a) for a in ins],
        out_specs=[blk(u), blk(h0)],
        out_shape=[jax.ShapeDtypeStruct(u.shape, F32), jax.ShapeDtypeStruct(h0.shape, F32)],
        scratch_shapes=[
            pltpu.VMEM((kb, rows, sl2), F32),
            pltpu.VMEM((kb, rows, STATE_LANES), BF16),
            pltpu.VMEM((kb, rows, STATE_LANES), BF16),
            pltpu.VMEM((kb, rows, LANES), BF16),
        ],
        compiler_params=_params(("parallel",)),
        name="ssm_scan_rows",
    )(*ins)


GATHER_STRIDE = 4
assert SSM_CHUNK == GATHER_STRIDE * GATHER_STRIDE


def _ssm_seq_kernel(u_ref, bb_ref, ctre_ref, ctim_ref, pw_ref, d_ref, h0_ref, y_ref, fs_ref,
                    g_scr, kt_scr, clt_scr, ucat_scr, e_scr, hp_scr, s1_scr, s2_scr):
    L = SSM_CHUNK
    sl = STATE_LANES
    seq = u_ref.shape[0]
    nch = seq // L

    @pl.when(pl.program_id(1) == 0)
    def _build_tables():
        lam_re = pw_ref[1:2, :]
        lam_im = pw_ref[N_LAM_POWERS + 1:N_LAM_POWERS + 2, :]
        q_re = ctre_ref[...]
        q_im = ctim_ref[...]
        ccs_t = jnp.concatenate([q_re, -q_im], axis=1).astype(BF16)
        kt_scr[...] = jnp.zeros(kt_scr.shape, BF16)
        g_re = bb_ref[:, 0:sl]
        g_im = bb_ref[:, sl:2 * sl]
        for m in range(L):
            j = L - 1 - m
            g_m = jnp.concatenate([g_re, g_im], axis=1).astype(BF16)
            g_scr[j * LANES:(j + 1) * LANES, :] = g_m
            k_m = lax.dot_general(g_m, ccs_t, _NT, preferred_element_type=F32).astype(BF16)
            for jj in range(L - m):
                i = jj + m
                kt_scr[jj * LANES:(jj + 1) * LANES, i * LANES:(i + 1) * LANES] = k_m
            g_re, g_im = g_re * lam_re - g_im * lam_im, g_re * lam_im + g_im * lam_re
            q_re, q_im = q_re * lam_re - q_im * lam_im, q_re * lam_im + q_im * lam_re
            clt_scr[m * LANES:(m + 1) * LANES, 0:sl] = q_re.astype(BF16)
            clt_scr[m * LANES:(m + 1) * LANES, sl:2 * sl] = (-q_im).astype(BF16)

    qs = GATHER_STRIDE
    for b in range(qs):
        s1_scr[b] = u_ref[pl.ds(b, seq // qs, stride=qs), :]
    for j in range(L):
        a, b = divmod(j, qs)
        ucat_scr[:, j * LANES:(j + 1) * LANES] = s1_scr[b, pl.ds(a, nch, stride=qs), :].astype(BF16)
    e_scr[...] = jnp.dot(ucat_scr[...], g_scr[...], preferred_element_type=F32)

    lc_re = pw_ref[L:L + 1, :]
    lc_im = pw_ref[N_LAM_POWERS + L:N_LAM_POWERS + L + 1, :]

    def chunk_step(c, h):
        h_re, h_im = h
        hp_scr[pl.ds(c, 1), 0:sl] = h_re
        hp_scr[pl.ds(c, 1), sl:2 * sl] = h_im
        e_re = e_scr[pl.ds(c, 1), 0:sl]
        e_im = e_scr[pl.ds(c, 1), sl:2 * sl]
        return (lc_re * h_re - lc_im * h_im + e_re, lc_re * h_im + lc_im * h_re + e_im)

    h_re, h_im = lax.fori_loop(0, nch, chunk_step, (h0_ref[:, 0:sl], h0_ref[:, sl:2 * sl]), unroll=8)
    fs_ref[:, 0:sl] = h_re
    fs_ref[:, sl:2 * sl] = h_im

    h_prev = hp_scr[...].astype(BF16)
    pair = 2 * LANES
    for p in range(L // 2):
        y2 = (jnp.dot(ucat_scr[:, 0:(p + 1) * pair], kt_scr[0:(p + 1) * pair, p * pair:(p + 1) * pair],
                      preferred_element_type=F32)
              + lax.dot_general(h_prev, clt_scr[p * pair:(p + 1) * pair, :], _NT, preferred_element_type=F32))
        for half in range(2):
            a, b = divmod(2 * p + half, qs)
            s2_scr[b, pl.ds(a, nch, stride=qs), :] = y2[:, half * LANES:(half + 1) * LANES]
    d = d_ref[...]
    for b in range(qs):
        y_ref[pl.ds(b, seq // qs, stride=qs), :] = s2_scr[b] + d * s1_scr[b]


def _ssm_scan_seq(u, tabs, h0, *, nseq, seq):
    nblk = u.shape[0]
    L = SSM_CHUNK
    nch = seq // L
    sl2 = 2 * STATE_LANES
    blk = lambda arr: pl.BlockSpec((None,) + arr.shape[1:], lambda k, b: (k,) + (0,) * (arr.ndim - 1))
    row_spec = pl.BlockSpec((None, seq, LANES), lambda k, b: (k, b, 0))
    st_spec = pl.BlockSpec((None, None, 1, sl2), lambda k, b: (k, b, 0, 0))
    return pl.pallas_call(
        _ssm_seq_kernel,
        grid=(nblk, nseq),
        in_specs=[row_spec, blk(tabs["bb"]), blk(tabs["ct_re"]), blk(tabs["ct_im"]), blk(tabs["pw_rows"]),
                  blk(tabs["dvec"]), st_spec],
        out_specs=[row_spec, st_spec],
        out_shape=[jax.ShapeDtypeStruct(u.shape, F32), jax.ShapeDtypeStruct(h0.shape, F32)],
        scratch_shapes=[
            pltpu.VMEM((L * LANES, sl2), BF16),
            pltpu.VMEM((L * LANES, L * LANES), BF16),
            pltpu.VMEM((L * LANES, sl2), BF16),
            pltpu.VMEM((nch, L * LANES), BF16),
            pltpu.VMEM((nch, sl2), F32),
            pltpu.VMEM((nch, sl2), F32),
            pltpu.VMEM((GATHER_STRIDE, seq // GATHER_STRIDE, LANES), F32),
            pltpu.VMEM((GATHER_STRIDE, seq // GATHER_STRIDE, LANES), F32),
        ],
        compiler_params=_params(("arbitrary", "arbitrary")),
        name="ssm_scan_seq",
    )(u, tabs["bb"], tabs["ct_re"], tabs["ct_im"], tabs["pw_rows"], tabs["dvec"], h0)


def _block_diag(x):
    kg, a, b = x.shape
    k = kg // GROUPS_PER_BLOCK
    tiled = jnp.tile(x.reshape(k, GROUPS_PER_BLOCK * a, b), (1, 1, GROUPS_PER_BLOCK))
    row_blk = lax.broadcasted_iota(jnp.int32, tiled.shape, 1) // a
    col_blk = lax.broadcasted_iota(jnp.int32, tiled.shape, 2) // b
    return jnp.where(row_blk == col_blk, tiled, 0)


def _lane_blocks(x):
    lead = x.shape[:-2]
    x = x.reshape(lead + (SSM_LANE_BLOCKS, STATE_LANES))
    return jnp.moveaxis(x, -2, 0)


def _ssm_tables(P, j):
    b_re_t = jnp.transpose(P["ssm_b_re"][j], (2, 0, 1))
    b_im_t = jnp.transpose(P["ssm_b_im"][j], (2, 0, 1))
    p_re, p_im, bb_re, bb_im = _ssm_prep(P["ssm_a_re"][j], P["ssm_a_im"][j], P["ssm_log_dt"][j], b_re_t, b_im_t)
    bb_re = jnp.swapaxes(bb_re, 0, 1)
    bb_im = jnp.swapaxes(bb_im, 0, 1)
    bb = jnp.concatenate([_block_diag(bb_re), _block_diag(bb_im)], axis=2)
    ct_re = _block_diag(P["ssm_c_re"][j])
    ct_im = _block_diag(P["ssm_c_im"][j])
    pw_rows = jnp.concatenate([_lane_blocks(p_re), _lane_blocks(p_im)], axis=1)
    dvec = P["ssm_d"][j].reshape(SSM_LANE_BLOCKS, 1, LANES)
    return dict(bb=bb, ct_re=ct_re, ct_im=ct_im, pw_rows=pw_rows, dvec=dvec)


def _ssm_scan(u, tabs, state, *, nseq, seq):
    if state is None:
        h0 = jnp.zeros((SSM_LANE_BLOCKS, nseq, 1, 2 * STATE_LANES), F32)
        y, fs = _ssm_scan_seq(u, tabs, h0, nseq=nseq, seq=seq)
        fs = fs.reshape(SSM_LANE_BLOCKS, nseq, 2 * STATE_LANES)
    else:
        y, fs = _ssm_scan_rows(u, tabs, _state_to_blocks(state), t_len=seq, nseq=nseq)
    return y, _state_from_blocks(fs)


def _state_from_blocks(fs):
    nblk, nseq, _ = fs.shape
    fs = fs.reshape(nblk, nseq, 2, GROUPS_PER_BLOCK, SSM_STATE)
    fs = jnp.transpose(fs, (1, 0, 3, 4, 2))
    return fs.reshape(nseq, SSM_GROUPS, SSM_STATE, 2)


def _state_to_blocks(state):
    nseq = state.shape[0]
    s = state.reshape(nseq, SSM_LANE_BLOCKS, GROUPS_PER_BLOCK, SSM_STATE, 2)
    return jnp.transpose(s, (1, 0, 4, 2, 3)).reshape(SSM_LANE_BLOCKS, nseq, 2 * STATE_LANES)


def _to_window_minor(c):
    return jnp.transpose(c[0], (0, 2, 3, 4, 1)).reshape(c.shape[1], 2 * GROUP_WIDTH, c.shape[2])


def _from_window_minor(c):
    c = c.reshape(c.shape[0], 2, HEADS_PER_GROUP, HEAD_DIM, c.shape[2])
    return jnp.transpose(c, (0, 4, 1, 2, 3))[None]


def _kv_tail_kernel(k_ref, v_ref, o_ref):
    o_ref[0:GROUP_WIDTH, :] = k_ref[...].T
    o_ref[GROUP_WIDTH:2 * GROUP_WIDTH, :] = v_ref[...].T


def _kv_tail(qkv, *, group, nseq, seq):
    wb = min(DILATED_GROUPS[group][0], seq)
    kcol = (ATTN_WIDTH + group * GROUP_WIDTH) // GROUP_WIDTH
    vcol = (2 * ATTN_WIDTH + group * GROUP_WIDTH) // GROUP_WIDTH
    tail = lambda col: pl.BlockSpec((wb, GROUP_WIDTH), lambda b: ((b + 1) * (seq // wb) - 1, col))
    return pl.pallas_call(
        _kv_tail_kernel,
        grid=(nseq,),
        in_specs=[tail(kcol), tail(vcol)],
        out_specs=pl.BlockSpec((None, 2 * GROUP_WIDTH, wb), lambda b: (b, 0, 0)),
        out_shape=jax.ShapeDtypeStruct((nseq, 2 * GROUP_WIDTH, wb), F32),
        compiler_params=_params(("parallel",)),
        name="kv_tail",
    )(qkv, qkv)


def _trunks(xs, caches, ssm_state, W, *, nseqs, seqs):
    ffn_w = [W[name + "_f32"][0, 0].astype(BF16) for name in FFN_WEIGHTS]

    st = _stage(xs, W, ffn_w, 0, 0, cast_next=(0, 1))
    hs = st["x"]
    qkvs, merged_s, new_caches = _qkv_and_sample_attn(hs, W["norms"], 0, W["qkv"], caches, t_len=seqs[1])
    merged_p = _attn_prompt(qkvs[0], seq=seqs[0])
    st = _stage(hs, W, st["next_w"], 0, 1, pre="attn", pre_ins=(merged_p, merged_s), cast_next=(1, 0))

    st = _stage(st["x"], W, st["next_w"], 1, 0, post="proj_slabs", cast_next=(1, 1))
    us = st["proj"]
    y_p, state_p = _ssm_scan(us[0], W["ssm"], None, nseq=nseqs[0], seq=seqs[0])
    y_s, state_s = _ssm_scan(us[1], W["ssm"], ssm_state, nseq=nseqs[1], seq=seqs[1])
    outs = _stage(st["x"], W, st["next_w"], 1, 1, pre="ssm", pre_ins=(y_p, y_s), post="final")["x"]
    return outs, qkvs[0], new_caches, (state_p, state_s)


def kernel(x_prompt, x_sample, cache_kv_g0, cache_kv_g1, cache_kv_g2, state_ssm, norm_g, final_norm_g,
           ffn_w_gate, ffn_w_up, ffn_w_down, attn_w_qkv, attn_w_o, ssm_w_in, ssm_a_re, ssm_a_im, ssm_log_dt,
           ssm_b_re, ssm_b_im, ssm_c_re, ssm_c_im, ssm_d, ssm_w_glu, ssm_b_glu, ssm_w_out):
    batch, seq, _ = x_prompt.shape
    dec_batch, dec_seq, _ = x_sample.shape
    P = dict(ssm_a_re=ssm_a_re, ssm_a_im=ssm_a_im, ssm_log_dt=ssm_log_dt, ssm_b_re=ssm_b_re,
             ssm_b_im=ssm_b_im, ssm_c_re=ssm_c_re, ssm_c_im=ssm_c_im, ssm_d=ssm_d)
    W = dict(
        norms=norm_g.reshape(norm_g.shape[0], norm_g.shape[1], 1, D_MODEL),
        final_norm=final_norm_g.reshape(1, D_MODEL),
        gate_f32=ffn_w_gate, up_f32=ffn_w_up, down_f32=ffn_w_down,
        qkv=attn_w_qkv.astype(BF16),
        o=jnp.swapaxes(attn_w_o.reshape(-1, N_GROUPS, GROUP_WIDTH // LANES, LANES, D_MODEL), 1, 2)
        .reshape(attn_w_o.shape).astype(BF16),
        ssm_w_in=ssm_w_in.astype(BF16), ssm_w_glu=ssm_w_glu.astype(BF16), ssm_w_out=ssm_w_out.astype(BF16),
        ssm_b_glu=ssm_b_glu.reshape(ssm_b_glu.shape[0], 1, D_MODEL),
        ssm=_ssm_tables(P, 0),
    )

    xs = (x_prompt.reshape(batch * seq, D_MODEL), x_sample.reshape(dec_batch * dec_seq, D_MODEL))
    caches = [_to_window_minor(c) for c in (cache_kv_g0, cache_kv_g1, cache_kv_g2)]
    (y_p, y_s), qkv_p, new_caches, (ssm_p, ssm_s) = _trunks(
        xs, caches, state_ssm[0], W, nseqs=(batch, dec_batch), seqs=(seq, dec_seq))
    kv_prompt = [_from_window_minor(_kv_tail(qkv_p, group=g, nseq=batch, seq=seq)) for g in range(N_GROUPS)]
    kv_sample = [_from_window_minor(nc) for nc in new_caches]

    return (y_p.reshape(batch, seq, D_MODEL), y_s.reshape(dec_batch, dec_seq, D_MODEL),
            kv_prompt[0], kv_prompt[1], kv_prompt[2], ssm_p[None],
            kv_sample[0], kv_sample[1], kv_sample[2], ssm_s[None])
```

```python
import functools
import math

import jax
import jax.numpy as jnp
from jax import lax
from jax.experimental import pallas as pl
from jax.experimental.pallas import tpu as pltpu

F32 = jnp.float32
BF16 = jnp.bfloat16

D_MODEL = 1024
D_FF = 2816
HEAD_DIM = 64
HEADS_PER_GROUP = 4
DILATED_GROUPS = ((128, 1), (512, 4), (2048, 16))
N_GROUPS = len(DILATED_GROUPS)
N_HEADS = N_GROUPS * HEADS_PER_GROUP
GROUP_WIDTH = HEADS_PER_GROUP * HEAD_DIM
ATTN_WIDTH = N_HEADS * HEAD_DIM
BAND = 128
SSM_CH = 16
SSM_GROUPS = D_MODEL // SSM_CH
SSM_STATE = 64
NORM_EPS = 1e-6

LANES = 128
V7X_VMEM_BYTES = 64 * 1024 * 1024
VMEM_LIMIT = V7X_VMEM_BYTES - 8 * 1024 * 1024

SSM_LANE_BLOCKS = D_MODEL // LANES
GROUPS_PER_BLOCK = LANES // SSM_CH
STATE_LANES = GROUPS_PER_BLOCK * SSM_STATE

assert all(w // d == BAND for w, d in DILATED_GROUPS)
assert all(d & (d - 1) == 0 for _, d in DILATED_GROUPS)


def _params(sem, vmem=VMEM_LIMIT):
    return pltpu.CompilerParams(dimension_semantics=sem, vmem_limit_bytes=vmem)


def _const_spec(shape):
    nd = len(shape)
    return pl.BlockSpec(shape, lambda *_: (0,) * nd, pipeline_mode=pl.Buffered(1))


def _rms(x, g):
    return x * lax.rsqrt(jnp.mean(x * x, axis=-1, keepdims=True) + NORM_EPS) * g


MXU_TILE = 256
FFN_CHUNK_BOUNDS = (0, 6 * MXU_TILE, D_FF)


N_STREAMS = 2
STAGE_TM = 512


FFN_WEIGHTS = ("gate", "up", "down")


def _stage_kernel(*refs, pre, post, cast_next):
    refs = list(refs)
    take = lambda n: [refs.pop(0) for _ in range(n)]
    x_refs = take(N_STREAMS)
    if pre == "attn":
        om_refs = take(N_STREAMS)
        (wo_ref,) = take(1)
    elif pre == "ssm":
        y_refs = take(N_STREAMS)
        wglu_ref, bglu_ref, wout_ref = take(3)
    g_ref, wg_ref, wu_ref, wd_ref = take(4)
    if post == "final":
        (gf_ref,) = take(1)
    elif post == "proj_slabs":
        gp_ref, wp_ref = take(2)
    next_f32 = take(len(FFN_WEIGHTS)) if cast_next else []
    o_refs = take(N_STREAMS)
    if post == "proj_slabs":
        p_refs = take(N_STREAMS)
    for src, dst in zip(next_f32, take(len(next_f32))):
        dst[...] = src[...].astype(BF16)

    def run(s):
        x = x_refs[s][...]
        if pre == "attn":
            x = x + jnp.dot(om_refs[s][...].astype(BF16), wo_ref[...], preferred_element_type=F32)
        elif pre == "ssm":
            y = jnp.concatenate([y_refs[s][k] for k in range(y_refs[s].shape[0])], axis=1)
            g = jax.nn.gelu(y, approximate=True)
            gate = jnp.dot(g.astype(BF16), wglu_ref[...], preferred_element_type=F32) + bglu_ref[...]
            z = (g * jax.nn.sigmoid(gate)).astype(BF16)
            x = x + jnp.dot(z, wout_ref[...], preferred_element_type=F32)

        xb = _rms(x, g_ref[...]).astype(BF16)
        acc = None
        for lo, hi in zip(FFN_CHUNK_BOUNDS[:-1], FFN_CHUNK_BOUNDS[1:]):
            gate = jnp.dot(xb, wg_ref[:, lo:hi], preferred_element_type=F32)
            up = jnp.dot(xb, wu_ref[:, lo:hi], preferred_element_type=F32)
            hmid = (gate * jax.nn.sigmoid(gate) * up).astype(BF16)
            part = jnp.dot(hmid, wd_ref[lo:hi, :], preferred_element_type=F32)
            acc = part if acc is None else acc + part
        out = x + 0.5 * acc

        if post == "final":
            out = _rms(out, gf_ref[...])
        o_refs[s][...] = out
        if post == "proj_slabs":
            proj = jnp.dot(_rms(out, gp_ref[...]).astype(BF16), wp_ref[...], preferred_element_type=F32)
            for k in range(p_refs[s].shape[0]):
                p_refs[s][k] = proj[:, k * LANES:(k + 1) * LANES]

    run(0)
    pl.when(pl.program_id(0) == 0)(lambda: run(1))


def _pick_spec(arr, *idx):
    tail = arr.shape[len(idx):]
    index = tuple(idx) + (0,) * len(tail)
    return pl.BlockSpec((None,) * len(idx) + tail, lambda *_: index, pipeline_mode=pl.Buffered(1))


def _whole_spec(shape):
    nd = len(shape)
    return pl.BlockSpec(shape, lambda *_: (0,) * nd)


def _row_specs(xs):
    return [pl.BlockSpec((STAGE_TM, xs[0].shape[1]), lambda i: (i, 0)), _whole_spec(xs[1].shape)]


def _slab_specs(ys):
    return [pl.BlockSpec((SSM_LANE_BLOCKS, STAGE_TM, LANES), lambda i: (0, i, 0)), _whole_spec(ys[1].shape)]


def _stage(xs, W, ffn_w, layer, half, *, pre=None, pre_ins=None, post=None, cast_next=None):
    steps = xs[0].shape[0] // STAGE_TM
    ins, specs = list(xs), _row_specs(xs)
    if pre == "attn":
        ins += list(pre_ins) + [W["o"]]
        specs += _row_specs(pre_ins) + [_pick_spec(W["o"], 0)]
    elif pre == "ssm":
        ins += list(pre_ins) + [W["ssm_w_glu"], W["ssm_b_glu"], W["ssm_w_out"]]
        specs += _slab_specs(pre_ins) + [_pick_spec(W["ssm_w_glu"], 0), _pick_spec(W["ssm_b_glu"], 0),
                                         _pick_spec(W["ssm_w_out"], 0)]
    ins += [W["norms"]] + list(ffn_w)
    specs += [_pick_spec(W["norms"], layer, 2 * half)] + [_pick_spec(w) for w in ffn_w]
    out_specs = _row_specs(xs)
    out_shape = [jax.ShapeDtypeStruct(x.shape, F32) for x in xs]
    if post == "final":
        ins += [W["final_norm"]]
        specs += [_const_spec((1, D_MODEL))]
    elif post == "proj_slabs":
        ins += [W["norms"], W["ssm_w_in"]]
        specs += [_pick_spec(W["norms"], layer, 1), _pick_spec(W["ssm_w_in"], 0)]
    if cast_next:
        nl, nh = cast_next
        rows_per_step = D_MODEL // steps
        steps_per_col = steps // (D_MODEL // LANES)
        assert rows_per_step * steps == D_MODEL and steps_per_col * (D_MODEL // LANES) == steps
        ins += [W[name + "_f32"] for name in FFN_WEIGHTS]
        specs += [pl.BlockSpec((None, None, rows_per_step, D_FF), lambda i: (nl, nh, i, 0))] * 2
        specs += [pl.BlockSpec((None, None, D_FF, LANES), lambda i: (nl, nh, 0, i // steps_per_col))]
    if post == "proj_slabs":
        slabs = [jax.ShapeDtypeStruct((SSM_LANE_BLOCKS, x.shape[0], LANES), F32) for x in xs]
        out_specs += _slab_specs(slabs)
        out_shape += slabs
    if cast_next:
        out_specs += [pl.BlockSpec((rows_per_step, D_FF), lambda i: (i, 0))] * 2
        out_specs += [pl.BlockSpec((D_FF, LANES), lambda i: (0, i // steps_per_col))]
        out_shape += [jax.ShapeDtypeStruct(s, BF16) for s in ((D_MODEL, D_FF), (D_MODEL, D_FF), (D_FF, D_MODEL))]
    outs = list(pl.pallas_call(
        functools.partial(_stage_kernel, pre=pre, post=post, cast_next=bool(cast_next)),
        grid=(steps,),
        in_specs=specs,
        out_specs=out_specs,
        out_shape=out_shape,
        compiler_params=_params(("arbitrary",)),
        name="stage_" + "_".join(s for s in (pre, "ffn", post) if s),
    )(*ins))
    take = lambda n: [outs.pop(0) for _ in range(n)]
    return dict(x=take(N_STREAMS), proj=take(N_STREAMS) if post == "proj_slabs" else None,
                next_w=take(len(FFN_WEIGHTS)) if cast_next else None)


def _qkv_kernel(x0_ref, x1_ref, g_ref, w_ref, c0_ref, c1_ref, c2_ref,
                o0_ref, o1_ref, om_ref, n0_ref, n1_ref, n2_ref, *, t_len):
    def run(x_ref, o_ref):
        xb = _rms(x_ref[...], g_ref[...]).astype(BF16)
        o_ref[...] = jnp.dot(xb, w_ref[...], preferred_element_type=F32)

    i = pl.program_id(0)
    run(x0_ref, o0_ref)
    pl.when(i == 0)(lambda: run(x1_ref, o1_ref))
    qkv_seq = o1_ref.at[pl.ds(pl.multiple_of(i * t_len, t_len), t_len), :]
    _attn_sample_kernel(qkv_seq, c0_ref, c1_ref, c2_ref, om_ref, n0_ref, n1_ref, n2_ref, t_len=t_len)


def _qkv_and_sample_attn(xs, norms, layer, w, caches, *, t_len):
    n = w.shape[-1]
    steps = xs[0].shape[0] // STAGE_TM
    nseq = caches[0].shape[0]
    assert steps == nseq and xs[1].shape[0] == nseq * t_len
    qkv_shapes = [jax.ShapeDtypeStruct((x.shape[0], n), F32) for x in xs]
    cache_specs = [pl.BlockSpec((None,) + c.shape[1:], lambda b: (b, 0, 0)) for c in caches]
    outs = pl.pallas_call(
        functools.partial(_qkv_kernel, t_len=t_len),
        grid=(steps,),
        in_specs=_row_specs(xs) + [_pick_spec(norms, layer, 1), _pick_spec(w, 0)] + cache_specs,
        out_specs=_row_specs(qkv_shapes) + [pl.BlockSpec((t_len, ATTN_WIDTH), lambda b: (b, 0))] + cache_specs,
        out_shape=qkv_shapes
        + [jax.ShapeDtypeStruct((xs[1].shape[0], ATTN_WIDTH), F32)]
        + [jax.ShapeDtypeStruct(c.shape, F32) for c in caches],
        compiler_params=_params(("arbitrary",)),
        name="qkv_proj_sample_attn",
    )(*xs, norms, w, *caches)
    return outs[:N_STREAMS], outs[N_STREAMS], outs[N_STREAMS + 1:]


def _alibi_slope(head):
    return 2.0 ** (-8.0 * (head + 1) / N_HEADS)


def _head_lane_mask(hh):
    lane = lax.broadcasted_iota(jnp.int32, (1, LANES), 1)
    return (lane >= hh * HEAD_DIM) & (lane < (hh + 1) * HEAD_DIM)


ATTN_TILE_TOKENS = 2048
_NT = (((1,), (1,)), ((), ()))


def _merge_groups(outs, lses):
    m = jnp.maximum(jnp.maximum(lses[0], lses[1]), lses[2])
    es = [jnp.exp(l - m) for l in lses]
    tot = es[0] + es[1] + es[2]
    return [o * (e / tot) for o, e in zip(outs, es)]


def _attn_kernel(*refs, tiles_per_seq):
    om_ref, o_scr, l_scr = refs[5 * N_GROUPS:]
    has_prev = (pl.program_id(0) % tiles_per_seq) != 0
    for g in range(N_GROUPS):
        _attn_group(*refs[5 * g:5 * g + 5], o_scr.at[g], l_scr.at[g], group=g, has_prev=has_prev)
    merged = _merge_groups([o_scr[g] for g in range(N_GROUPS)], [l_scr[g] for g in range(N_GROUPS)])
    for g in range(N_GROUPS):
        om_ref[:, g * LANES:(g + 1) * LANES] = merged[g].astype(BF16)


def _attn_group(q_ref, kp_ref, kc_ref, vp_ref, vc_ref, o_ref, l_ref, *, group, has_prev):
    dil = DILATED_GROUPS[group][1]
    span = BAND * dil
    nblk = q_ref.shape[0] // span
    c = pl.program_id(1)

    row = lax.broadcasted_iota(jnp.int32, (2 * BAND, 2 * BAND), 0)
    kj = lax.broadcasted_iota(jnp.int32, (2 * BAND, 2 * BAND), 1)
    delta = BAND + (row & (BAND - 1)) - kj
    head0 = group * HEADS_PER_GROUP
    slope_a = jnp.where(c == 0, _alibi_slope(head0), _alibi_slope(head0 + 2)).astype(F32)
    slope_b = jnp.where(c == 0, _alibi_slope(head0 + 1), _alibi_slope(head0 + 3)).astype(F32)
    slope = jnp.where(row < BAND, slope_a, slope_b)
    band = (delta >= 0) & (delta <= BAND)
    bias = jnp.where(band, -slope * (delta * dil).astype(F32), -jnp.inf)
    bias_first = jnp.where((kj >= BAND) | has_prev, bias, -jnp.inf)
    first_head = _head_lane_mask(0)

    def rows(ref, start):
        return ref[pl.ds(start, BAND), :] if dil == 1 else ref[pl.ds(start, BAND, stride=dil), :]

    for r in range(dil):
        k_prev = rows(kp_ref, r).astype(BF16)
        v_prev = rows(vp_ref, r).astype(BF16)
        for b in range(nblk):
            start = b * span + r
            q = rows(q_ref, start) * (HEAD_DIM ** -0.5)
            k_cur = rows(kc_ref, start).astype(BF16)
            v_cur = rows(vc_ref, start).astype(BF16)
            k = jnp.concatenate([k_prev, k_cur], axis=0)
            v = jnp.concatenate([v_prev, v_cur], axis=0)
            q2 = jnp.concatenate([jnp.where(first_head, q, 0.0), jnp.where(first_head, 0.0, q)], axis=0)
            s = lax.dot_general(q2.astype(BF16), k, _NT, preferred_element_type=F32)
            s = s + (bias_first if b == 0 else bias)
            m = jnp.max(s, axis=-1, keepdims=True)
            p = jnp.exp(s - m)
            den = jnp.sum(p, axis=-1, keepdims=True)
            o2 = jnp.dot(p.astype(BF16), v, preferred_element_type=F32) / den
            l2 = m + jnp.log(den)
            o = jnp.where(first_head, o2[:BAND], o2[BAND:])
            lse = jnp.where(first_head, l2[:BAND], l2[BAND:])
            if dil == 1:
                o_ref[pl.ds(start, BAND), :] = o
                l_ref[pl.ds(start, BAND), :] = lse
            else:
                o_ref[pl.ds(start, BAND, stride=dil), :] = o
                l_ref[pl.ds(start, BAND, stride=dil), :] = lse
            k_prev, v_prev = k_cur, v_cur


MERGED_BLOCK = N_GROUPS * LANES


def _attn_prompt(qkv, *, seq):
    rows = qkv.shape[0]
    tt = ATTN_TILE_TOKENS
    specs = []
    for g, (_, dil) in enumerate(DILATED_GROUPS):
        span = BAND * dil
        nblk = tt // span
        cur = lambda col: pl.BlockSpec((tt, LANES), lambda i, c: (i, col + c))
        prev = lambda col, span=span, nblk=nblk: pl.BlockSpec(
            (span, LANES), lambda i, c: (jnp.maximum(i * nblk - 1, 0), col + c))
        qcol = g * GROUP_WIDTH // LANES
        kcol = (ATTN_WIDTH + g * GROUP_WIDTH) // LANES
        vcol = (2 * ATTN_WIDTH + g * GROUP_WIDTH) // LANES
        specs += [cur(qcol), prev(kcol), cur(kcol), prev(vcol), cur(vcol)]
    return pl.pallas_call(
        functools.partial(_attn_kernel, tiles_per_seq=seq // tt),
        grid=(rows // tt, GROUP_WIDTH // LANES),
        in_specs=specs,
        out_specs=pl.BlockSpec((tt, MERGED_BLOCK), lambda i, c: (i, c)),
        out_shape=jax.ShapeDtypeStruct((rows, ATTN_WIDTH), BF16),
        scratch_shapes=[pltpu.VMEM((N_GROUPS, tt, LANES), F32), pltpu.VMEM((N_GROUPS, tt, LANES), F32)],
        compiler_params=_params(("parallel", "parallel")),
        name="attn_prompt",
    )(*([qkv] * len(specs)))


def _attn_sample_kernel(qkv_ref, c0_ref, c1_ref, c2_ref, om_ref, n0_ref, n1_ref, n2_ref, *, t_len):
    cache_refs = (c0_ref, c1_ref, c2_ref)
    n_refs = (n0_ref, n1_ref, n2_ref)
    outs, lses = [], []
    gw = GROUP_WIDTH
    nrow = HEADS_PER_GROUP * t_len
    row_head = lax.broadcasted_iota(jnp.int32, (nrow, gw), 0) // t_len
    col_head = lax.broadcasted_iota(jnp.int32, (nrow, gw), 1) // HEAD_DIM
    out_col_head = lax.broadcasted_iota(jnp.int32, (t_len, gw), 1) // HEAD_DIM

    def gather_heads(x):
        acc = jnp.zeros((t_len, gw), F32)
        for h in range(HEADS_PER_GROUP):
            acc = acc + jnp.where(out_col_head == h, x[h * t_len:(h + 1) * t_len, :], 0.0)
        return acc

    for g, (win, dil) in enumerate(DILATED_GROUPS):
        cache_ref = cache_refs[g]
        wb = cache_ref.shape[1]
        q0 = g * gw
        k0 = ATTN_WIDTH + g * gw
        v0 = 2 * ATTN_WIDTH + g * gw

        kv_new = jnp.concatenate([qkv_ref[:, k0:k0 + gw], qkv_ref[:, v0:v0 + gw]], axis=1)
        kv_pad = jnp.concatenate([kv_new, jnp.zeros((LANES - t_len, 2 * gw), F32)], axis=0)
        new_t = kv_pad.T

        q = qkv_ref[:, q0:q0 + gw] * (HEAD_DIM ** -0.5)
        q_bd = jnp.where(row_head == col_head, jnp.concatenate([q] * HEADS_PER_GROUP, axis=0), 0.0).astype(BF16)

        def bias(ncol, base):
            r = lax.broadcasted_iota(jnp.int32, (nrow, ncol), 0)
            w = lax.broadcasted_iota(jnp.int32, (nrow, ncol), 1)
            dist = base + (r % t_len) - w
            ok = (dist >= 0) & (dist <= win) & ((dist & (dil - 1)) == 0)
            slope = jnp.zeros((nrow, ncol), F32)
            for h in range(HEADS_PER_GROUP):
                slope = jnp.where(r // t_len == h, _alibi_slope(g * HEADS_PER_GROUP + h), slope)
            return jnp.where(ok, -slope * dist.astype(F32), -jnp.inf)

        s1 = jnp.dot(q_bd, cache_ref[0:gw, :].astype(BF16), preferred_element_type=F32) + bias(wb, wb)
        pad_ok = lax.broadcasted_iota(jnp.int32, (nrow, LANES), 1) < t_len
        s2 = jnp.dot(q_bd, new_t[0:gw, :].astype(BF16), preferred_element_type=F32)
        s2 = jnp.where(pad_ok, s2 + bias(LANES, 0), -jnp.inf)
        m = jnp.maximum(jnp.max(s1, axis=-1, keepdims=True), jnp.max(s2, axis=-1, keepdims=True))
        p1 = jnp.exp(s1 - m)
        p2 = jnp.exp(s2 - m)
        den = jnp.sum(p1, axis=-1, keepdims=True) + jnp.sum(p2, axis=-1, keepdims=True)
        o = (lax.dot_general(p1.astype(BF16), cache_ref[gw:2 * gw, :].astype(BF16), _NT, preferred_element_type=F32)
             + lax.dot_general(p2.astype(BF16), new_t[gw:2 * gw, :].astype(BF16), _NT, preferred_element_type=F32))
        outs.append(gather_heads(o / den))
        lses.append(gather_heads(jnp.broadcast_to(m + jnp.log(den), (nrow, gw))))

        ext = jnp.concatenate([cache_ref[...], new_t], axis=1)
        n_refs[g][...] = pltpu.roll(ext, wb + LANES - t_len, axis=1)[:, 0:wb]

    merged = _merge_groups(outs, lses)
    for g in range(N_GROUPS):
        for c in range(gw // LANES):
            col = c * MERGED_BLOCK + g * LANES
            om_ref[:, col:col + LANES] = merged[g][:, c * LANES:(c + 1) * LANES]


SSM_CHUNK = 16
N_LAM_POWERS = SSM_CHUNK + 1


def _ssm_prep_kernel(a_re_ref, a_im_ref, logdt_ref, bre_ref, bim_ref, pre_ref, pim_ref, bbre_ref, bbim_ref):
    a_re = a_re_ref[...]
    a_im = a_im_ref[...]
    dt = jnp.exp(logdt_ref[...])
    mag = jnp.exp(a_re * dt)
    ang = a_im * dt
    lam_re = mag * jnp.cos(ang)
    lam_im = mag * jnp.sin(ang)
    den = a_re * a_re + a_im * a_im
    num_re = lam_re - 1.0
    coef_re = ((num_re * a_re + lam_im * a_im) / den)[None]
    coef_im = ((lam_im * a_re - num_re * a_im) / den)[None]
    b_re = bre_ref[...]
    b_im = bim_ref[...]
    bbre_ref[...] = coef_re * b_re - coef_im * b_im
    bbim_ref[...] = coef_re * b_im + coef_im * b_re
    p_re, p_im = jnp.ones_like(lam_re), jnp.zeros_like(lam_im)
    for m in range(N_LAM_POWERS):
        pre_ref[m] = p_re
        pim_ref[m] = p_im
        p_re, p_im = p_re * lam_re - p_im * lam_im, p_re * lam_im + p_im * lam_re


def _ssm_prep(a_re, a_im, log_dt, b_re_t, b_im_t):
    g, n = a_re.shape
    pw = jax.ShapeDtypeStruct((N_LAM_POWERS, g, n), F32)
    bb = jax.ShapeDtypeStruct(b_re_t.shape, F32)
    return pl.pallas_call(
        _ssm_prep_kernel,
        out_shape=[pw, pw, bb, bb],
        name="ssm_prep",
    )(a_re, a_im, log_dt.reshape(g, 1), b_re_t, b_im_t)


SCAN_BLOCKS_PER_STEP = 4


def _ssm_rows_kernel(*refs, t_len, nseq):
    for k in range(refs[0].shape[0]):
        _ssm_rows_block(*[r.at[k] for r in refs], t_len=t_len, nseq=nseq)


def _ssm_rows_block(u_ref, bb_ref, ctre_ref, ctim_ref, pw_ref, d_ref, h0_ref, y_ref, fs_ref,
                    bu_scr, xre_scr, xim_scr, ud_scr, *, t_len, nseq):
    sl = STATE_LANES
    for j in range(t_len):
        ud_scr[j * nseq:(j + 1) * nseq, :] = u_ref[pl.ds(j, nseq, stride=t_len), :].astype(BF16)
    bu_scr[...] = jnp.dot(ud_scr[...], bb_ref[...].astype(BF16), preferred_element_type=F32)

    lam_re = jnp.broadcast_to(pw_ref[1:2, :], (nseq, sl))
    lam_im = jnp.broadcast_to(pw_ref[N_LAM_POWERS + 1:N_LAM_POWERS + 2, :], (nseq, sl))
    x_re = h0_ref[:, 0:sl]
    x_im = h0_ref[:, sl:2 * sl]
    for j in range(t_len):
        b_re = bu_scr[j * nseq:(j + 1) * nseq, 0:sl]
        b_im = bu_scr[j * nseq:(j + 1) * nseq, sl:2 * sl]
        x_re, x_im = (lam_re * x_re - lam_im * x_im + b_re, lam_re * x_im + lam_im * x_re + b_im)
        xre_scr[j * nseq:(j + 1) * nseq, :] = x_re.astype(BF16)
        xim_scr[j * nseq:(j + 1) * nseq, :] = x_im.astype(BF16)
    fs_ref[:, 0:sl] = x_re
    fs_ref[:, sl:2 * sl] = x_im

    y = (lax.dot_general(xre_scr[...], ctre_ref[...].astype(BF16), _NT, preferred_element_type=F32)
         - lax.dot_general(xim_scr[...], ctim_ref[...].astype(BF16), _NT, preferred_element_type=F32))
    d = d_ref[...]
    for j in range(t_len):
        y_ref[pl.ds(j, nseq, stride=t_len), :] = (
            y[j * nseq:(j + 1) * nseq, :] + d * u_ref[pl.ds(j, nseq, stride=t_len), :])


def _ssm_scan_rows(u, tabs, h0, *, t_len, nseq):
    nblk, rows, _ = u.shape
    kb = SCAN_BLOCKS_PER_STEP
    sl2 = 2 * STATE_LANES
    blk = lambda arr: pl.BlockSpec((kb,) + arr.shape[1:], lambda k: (k,) + (0,) * (arr.ndim - 1))
    ins = (u, tabs["bb"], tabs["ct_re"], tabs["ct_im"], tabs["pw_rows"], tabs["dvec"], h0)
    return pl.pallas_call(
        functools.partial(_ssm_rows_kernel, t_len=t_len, nseq=nseq),
        grid=(nblk // kb,),
        in_specs=[blk(a) for a in ins],
        out_specs=[blk(u), blk(h0)],
        out_shape=[jax.ShapeDtypeStruct(u.shape, F32), jax.ShapeDtypeStruct(h0.shape, F32)],
        scratch_shapes=[
            pltpu.VMEM((kb, rows, sl2), F32),
            pltpu.VMEM((kb, rows, STATE_LANES), BF16),
            pltpu.VMEM((kb, rows, STATE_LANES), BF16),
            pltpu.VMEM((kb, rows, LANES), BF16),
        ],
        compiler_params=_params(("parallel",)),
        name="ssm_scan_rows",
    )(*ins)


GATHER_STRIDE = 4
assert SSM_CHUNK == GATHER_STRIDE * GATHER_STRIDE


def _ssm_seq_kernel(u_ref, bb_ref, ctre_ref, ctim_ref, pw_ref, d_ref, h0_ref, y_ref, fs_ref,
                    g_scr, kt_scr, clt_scr, ucat_scr, e_scr, hp_scr, s1_scr, s2_scr, y1_scr):
    L = SSM_CHUNK
    sl = STATE_LANES
    seq = u_ref.shape[0]
    nch = seq // L

    @pl.when(pl.program_id(1) == 0)
    def _build_tables():
        lam_re = pw_ref[1:2, :]
        lam_im = pw_ref[N_LAM_POWERS + 1:N_LAM_POWERS + 2, :]
        q_re = ctre_ref[...]
        q_im = ctim_ref[...]
        ccs_t = jnp.concatenate([q_re, -q_im], axis=1).astype(BF16)
        kt_scr[...] = jnp.zeros(kt_scr.shape, BF16)
        g_re = bb_ref[:, 0:sl]
        g_im = bb_ref[:, sl:2 * sl]
        for m in range(L):
            j = L - 1 - m
            g_m = jnp.concatenate([g_re, g_im], axis=1).astype(BF16)
            g_scr[j * LANES:(j + 1) * LANES, :] = g_m
            k_m = lax.dot_general(g_m, ccs_t, _NT, preferred_element_type=F32).astype(BF16)
            for jj in range(L - m):
                i = jj + m
                kt_scr[jj * LANES:(jj + 1) * LANES, i * LANES:(i + 1) * LANES] = k_m
            g_re, g_im = g_re * lam_re - g_im * lam_im, g_re * lam_im + g_im * lam_re
            q_re, q_im = q_re * lam_re - q_im * lam_im, q_re * lam_im + q_im * lam_re
            clt_scr[m * LANES:(m + 1) * LANES, 0:sl] = q_re.astype(BF16)
            clt_scr[m * LANES:(m + 1) * LANES, sl:2 * sl] = (-q_im).astype(BF16)

    qs = GATHER_STRIDE
    for b in range(qs):
        s1_scr[b] = u_ref[pl.ds(b, seq // qs, stride=qs), :]
    for j in range(L):
        a, b = divmod(j, qs)
        ucat_scr[:, j * LANES:(j + 1) * LANES] = s1_scr[b, pl.ds(a, nch, stride=qs), :].astype(BF16)
    e_scr[...] = jnp.dot(ucat_scr[...], g_scr[...], preferred_element_type=F32)

    lc_re = pw_ref[L:L + 1, :]
    lc_im = pw_ref[N_LAM_POWERS + L:N_LAM_POWERS + L + 1, :]

    pair = 2 * LANES
    for p in range(L // 2):
        y1_scr[:, p * pair:(p + 1) * pair] = jnp.dot(
            ucat_scr[:, 0:(p + 1) * pair], kt_scr[0:(p + 1) * pair, p * pair:(p + 1) * pair],
            preferred_element_type=F32)

    h_re, h_im = h0_ref[:, 0:sl], h0_ref[:, sl:2 * sl]
    for c in range(nch):
        hp_scr[c:c + 1, 0:sl] = h_re
        hp_scr[c:c + 1, sl:2 * sl] = h_im
        e_re = e_scr[c:c + 1, 0:sl]
        e_im = e_scr[c:c + 1, sl:2 * sl]
        h_re, h_im = (lc_re * h_re - lc_im * h_im + e_re, lc_re * h_im + lc_im * h_re + e_im)
    fs_ref[:, 0:sl] = h_re
    fs_ref[:, sl:2 * sl] = h_im

    h_prev = hp_scr[...].astype(BF16)
    for p in range(L // 2):
        y2 = y1_scr[:, p * pair:(p + 1) * pair] + lax.dot_general(
            h_prev, clt_scr[p * pair:(p + 1) * pair, :], _NT, preferred_element_type=F32)
        for half in range(2):
            a, b = divmod(2 * p + half, qs)
            s2_scr[b, pl.ds(a, nch, stride=qs), :] = y2[:, half * LANES:(half + 1) * LANES]
    d = d_ref[...]
    for b in range(qs):
        y_ref[pl.ds(b, seq // qs, stride=qs), :] = s2_scr[b] + d * s1_scr[b]


def _ssm_scan_seq(u, tabs, h0, *, nseq, seq):
    nblk = u.shape[0]
    L = SSM_CHUNK
    nch = seq // L
    sl2 = 2 * STATE_LANES
    blk = lambda arr: pl.BlockSpec((None,) + arr.shape[1:], lambda k, b: (k,) + (0,) * (arr.ndim - 1))
    row_spec = pl.BlockSpec((None, seq, LANES), lambda k, b: (k, b, 0))
    st_spec = pl.BlockSpec((None, None, 1, sl2), lambda k, b: (k, b, 0, 0))
    return pl.pallas_call(
        _ssm_seq_kernel,
        grid=(nblk, nseq),
        in_specs=[row_spec, blk(tabs["bb"]), blk(tabs["ct_re"]), blk(tabs["ct_im"]), blk(tabs["pw_rows"]),
                  blk(tabs["dvec"]), st_spec],
        out_specs=[row_spec, st_spec],
        out_shape=[jax.ShapeDtypeStruct(u.shape, F32), jax.ShapeDtypeStruct(h0.shape, F32)],
        scratch_shapes=[
            pltpu.VMEM((L * LANES, sl2), BF16),
            pltpu.VMEM((L * LANES, L * LANES), BF16),
            pltpu.VMEM((L * LANES, sl2), BF16),
            pltpu.VMEM((nch, L * LANES), BF16),
            pltpu.VMEM((nch, sl2), F32),
            pltpu.VMEM((nch, sl2), F32),
            pltpu.VMEM((GATHER_STRIDE, seq // GATHER_STRIDE, LANES), F32),
            pltpu.VMEM((GATHER_STRIDE, seq // GATHER_STRIDE, LANES), F32),
            pltpu.VMEM((nch, L * LANES), F32),
        ],
        compiler_params=_params(("arbitrary", "arbitrary")),
        name="ssm_scan_seq",
    )(u, tabs["bb"], tabs["ct_re"], tabs["ct_im"], tabs["pw_rows"], tabs["dvec"], h0)


def _block_diag(x):
    kg, a, b = x.shape
    k = kg // GROUPS_PER_BLOCK
    tiled = jnp.tile(x.reshape(k, GROUPS_PER_BLOCK * a, b), (1, 1, GROUPS_PER_BLOCK))
    row_blk = lax.broadcasted_iota(jnp.int32, tiled.shape, 1) // a
    col_blk = lax.broadcasted_iota(jnp.int32, tiled.shape, 2) // b
    return jnp.where(row_blk == col_blk, tiled, 0)


def _lane_blocks(x):
    lead = x.shape[:-2]
    x = x.reshape(lead + (SSM_LANE_BLOCKS, STATE_LANES))
    return jnp.moveaxis(x, -2, 0)


def _ssm_tables(P, j):
    b_re_t = jnp.transpose(P["ssm_b_re"][j], (2, 0, 1))
    b_im_t = jnp.transpose(P["ssm_b_im"][j], (2, 0, 1))
    p_re, p_im, bb_re, bb_im = _ssm_prep(P["ssm_a_re"][j], P["ssm_a_im"][j], P["ssm_log_dt"][j], b_re_t, b_im_t)
    bb_re = jnp.swapaxes(bb_re, 0, 1)
    bb_im = jnp.swapaxes(bb_im, 0, 1)
    bb = jnp.concatenate([_block_diag(bb_re), _block_diag(bb_im)], axis=2)
    ct_re = _block_diag(P["ssm_c_re"][j])
    ct_im = _block_diag(P["ssm_c_im"][j])
    pw_rows = jnp.concatenate([_lane_blocks(p_re), _lane_blocks(p_im)], axis=1)
    dvec = P["ssm_d"][j].reshape(SSM_LANE_BLOCKS, 1, LANES)
    return dict(bb=bb, ct_re=ct_re, ct_im=ct_im, pw_rows=pw_rows, dvec=dvec)


def _ssm_scan(u, tabs, state, *, nseq, seq):
    if state is None:
        h0 = jnp.zeros((SSM_LANE_BLOCKS, nseq, 1, 2 * STATE_LANES), F32)
        y, fs = _ssm_scan_seq(u, tabs, h0, nseq=nseq, seq=seq)
        fs = fs.reshape(SSM_LANE_BLOCKS, nseq, 2 * STATE_LANES)
    else:
        y, fs = _ssm_scan_rows(u, tabs, _state_to_blocks(state), t_len=seq, nseq=nseq)
    return y, _state_from_blocks(fs)


def _state_from_blocks(fs):
    nblk, nseq, _ = fs.shape
    fs = fs.reshape(nblk, nseq, 2, GROUPS_PER_BLOCK, SSM_STATE)
    fs = jnp.transpose(fs, (1, 0, 3, 4, 2))
    return fs.reshape(nseq, SSM_GROUPS, SSM_STATE, 2)


def _state_to_blocks(state):
    nseq = state.shape[0]
    s = state.reshape(nseq, SSM_LANE_BLOCKS, GROUPS_PER_BLOCK, SSM_STATE, 2)
    return jnp.transpose(s, (1, 0, 4, 2, 3)).reshape(SSM_LANE_BLOCKS, nseq, 2 * STATE_LANES)


def _to_window_minor(c):
    return jnp.transpose(c[0], (0, 2, 3, 4, 1)).reshape(c.shape[1], 2 * GROUP_WIDTH, c.shape[2])


def _from_window_minor(c):
    c = c.reshape(c.shape[0], 2, HEADS_PER_GROUP, HEAD_DIM, c.shape[2])
    return jnp.transpose(c, (0, 4, 1, 2, 3))[None]


def _kv_tail_kernel(k_ref, v_ref, o_ref):
    o_ref[0:GROUP_WIDTH, :] = k_ref[...].T
    o_ref[GROUP_WIDTH:2 * GROUP_WIDTH, :] = v_ref[...].T


def _kv_tail(qkv, *, group, nseq, seq):
    wb = min(DILATED_GROUPS[group][0], seq)
    kcol = (ATTN_WIDTH + group * GROUP_WIDTH) // GROUP_WIDTH
    vcol = (2 * ATTN_WIDTH + group * GROUP_WIDTH) // GROUP_WIDTH
    tail = lambda col: pl.BlockSpec((wb, GROUP_WIDTH), lambda b: ((b + 1) * (seq // wb) - 1, col))
    return pl.pallas_call(
        _kv_tail_kernel,
        grid=(nseq,),
        in_specs=[tail(kcol), tail(vcol)],
        out_specs=pl.BlockSpec((None, 2 * GROUP_WIDTH, wb), lambda b: (b, 0, 0)),
        out_shape=jax.ShapeDtypeStruct((nseq, 2 * GROUP_WIDTH, wb), F32),
        compiler_params=_params(("parallel",)),
        name="kv_tail",
    )(qkv, qkv)


def _trunks(xs, caches, ssm_state, W, *, nseqs, seqs):
    ffn_w = [W[name + "_f32"][0, 0].astype(BF16) for name in FFN_WEIGHTS]

    st = _stage(xs, W, ffn_w, 0, 0, cast_next=(0, 1))
    hs = st["x"]
    qkvs, merged_s, new_caches = _qkv_and_sample_attn(hs, W["norms"], 0, W["qkv"], caches, t_len=seqs[1])
    merged_p = _attn_prompt(qkvs[0], seq=seqs[0])
    st = _stage(hs, W, st["next_w"], 0, 1, pre="attn", pre_ins=(merged_p, merged_s), cast_next=(1, 0))

    st = _stage(st["x"], W, st["next_w"], 1, 0, post="proj_slabs", cast_next=(1, 1))
    us = st["proj"]
    y_p, state_p = _ssm_scan(us[0], W["ssm"], None, nseq=nseqs[0], seq=seqs[0])
    y_s, state_s = _ssm_scan(us[1], W["ssm"], ssm_state, nseq=nseqs[1], seq=seqs[1])
    outs = _stage(st["x"], W, st["next_w"], 1, 1, pre="ssm", pre_ins=(y_p, y_s), post="final")["x"]
    return outs, qkvs[0], new_caches, (state_p, state_s)


def kernel(x_prompt, x_sample, cache_kv_g0, cache_kv_g1, cache_kv_g2, state_ssm, norm_g, final_norm_g,
           ffn_w_gate, ffn_w_up, ffn_w_down, attn_w_qkv, attn_w_o, ssm_w_in, ssm_a_re, ssm_a_im, ssm_log_dt,
           ssm_b_re, ssm_b_im, ssm_c_re, ssm_c_im, ssm_d, ssm_w_glu, ssm_b_glu, ssm_w_out):
    batch, seq, _ = x_prompt.shape
    dec_batch, dec_seq, _ = x_sample.shape
    P = dict(ssm_a_re=ssm_a_re, ssm_a_im=ssm_a_im, ssm_log_dt=ssm_log_dt, ssm_b_re=ssm_b_re,
             ssm_b_im=ssm_b_im, ssm_c_re=ssm_c_re, ssm_c_im=ssm_c_im, ssm_d=ssm_d)
    W = dict(
        norms=norm_g.reshape(norm_g.shape[0], norm_g.shape[1], 1, D_MODEL),
        final_norm=final_norm_g.reshape(1, D_MODEL),
        gate_f32=ffn_w_gate, up_f32=ffn_w_up, down_f32=ffn_w_down,
        qkv=attn_w_qkv.astype(BF16),
        o=jnp.swapaxes(attn_w_o.reshape(-1, N_GROUPS, GROUP_WIDTH // LANES, LANES, D_MODEL), 1, 2)
        .reshape(attn_w_o.shape).astype(BF16),
        ssm_w_in=ssm_w_in.astype(BF16), ssm_w_glu=ssm_w_glu.astype(BF16), ssm_w_out=ssm_w_out.astype(BF16),
        ssm_b_glu=ssm_b_glu.reshape(ssm_b_glu.shape[0], 1, D_MODEL),
        ssm=_ssm_tables(P, 0),
    )

    xs = (x_prompt.reshape(batch * seq, D_MODEL), x_sample.reshape(dec_batch * dec_seq, D_MODEL))
    caches = [_to_window_minor(c) for c in (cache_kv_g0, cache_kv_g1, cache_kv_g2)]
    (y_p, y_s), qkv_p, new_caches, (ssm_p, ssm_s) = _trunks(
        xs, caches, state_ssm[0], W, nseqs=(batch, dec_batch), seqs=(seq, dec_seq))
    kv_prompt = [_from_window_minor(_kv_tail(qkv_p, group=g, nseq=batch, seq=seq)) for g in range(N_GROUPS)]
    kv_sample = [_from_window_minor(nc) for nc in new_caches]

    return (y_p.reshape(batch, seq, D_MODEL), y_s.reshape(dec_batch, dec_seq, D_MODEL),
            kv_prompt[0], kv_prompt[1], kv_prompt[2], ssm_p[None],
            kv_sample[0], kv_sample[1], kv_sample[2], ssm_s[None])
```

```python
import functools
import math

import jax
import jax.numpy as jnp
from jax import lax
from jax.experimental import pallas as pl
from jax.experimental.pallas import tpu as pltpu

F32 = jnp.float32
BF16 = jnp.bfloat16

D_MODEL = 1024
D_FF = 2816
HEAD_DIM = 64
HEADS_PER_GROUP = 4
DILATED_GROUPS = ((128, 1), (512, 4), (2048, 16))
N_GROUPS = len(DILATED_GROUPS)
N_HEADS = N_GROUPS * HEADS_PER_GROUP
GROUP_WIDTH = HEADS_PER_GROUP * HEAD_DIM
ATTN_WIDTH = N_HEADS * HEAD_DIM
BAND = 128
SSM_CH = 16
SSM_GROUPS = D_MODEL // SSM_CH
SSM_STATE = 64
NORM_EPS = 1e-6

LANES = 128
V7X_VMEM_BYTES = 64 * 1024 * 1024
VMEM_LIMIT = V7X_VMEM_BYTES - 8 * 1024 * 1024

SSM_LANE_BLOCKS = D_MODEL // LANES
GROUPS_PER_BLOCK = LANES // SSM_CH
STATE_LANES = GROUPS_PER_BLOCK * SSM_STATE

assert all(w // d == BAND for w, d in DILATED_GROUPS)
assert all(d & (d - 1) == 0 for _, d in DILATED_GROUPS)


def _params(sem, vmem=VMEM_LIMIT):
    return pltpu.CompilerParams(dimension_semantics=sem, vmem_limit_bytes=vmem)


def _const_spec(shape):
    nd = len(shape)
    return pl.BlockSpec(shape, lambda *_: (0,) * nd, pipeline_mode=pl.Buffered(1))


def _rms(x, g):
    return x * lax.rsqrt(jnp.mean(x * x, axis=-1, keepdims=True) + NORM_EPS) * g


MXU_TILE = 256
FFN_CHUNK_BOUNDS = (0, 6 * MXU_TILE, D_FF)


N_STREAMS = 2
STAGE_TM = 512


FFN_WEIGHTS = ("gate", "up", "down")


def _stage_kernel(*refs, pre, post, n_casts):
    refs = list(refs)
    take = lambda n: [refs.pop(0) for _ in range(n)]
    x_refs = take(N_STREAMS)
    if pre == "attn":
        om_refs = take(N_STREAMS)
        (wo_ref,) = take(1)
    elif pre == "ssm":
        y_refs = take(N_STREAMS)
        wglu_ref, bglu_ref, wout_ref = take(3)
    g_ref, wg_ref, wu_ref, wd_ref = take(4)
    if post == "final":
        (gf_ref,) = take(1)
    elif post == "proj_slabs":
        gp_ref, wp_ref = take(2)
    cast_srcs = take(n_casts)
    o_refs = take(N_STREAMS)
    if post == "proj_slabs":
        p_refs = take(N_STREAMS)
    for src, dst in zip(cast_srcs, take(n_casts)):
        dst[...] = src[...].astype(BF16)

    def run(s):
        x = x_refs[s][...]
        if pre == "attn":
            x = x + jnp.dot(om_refs[s][...].astype(BF16), wo_ref[...], preferred_element_type=F32)
        elif pre == "ssm":
            y = jnp.concatenate([y_refs[s][k] for k in range(y_refs[s].shape[0])], axis=1)
            g = jax.nn.gelu(y, approximate=True)
            gate = jnp.dot(g.astype(BF16), wglu_ref[...], preferred_element_type=F32) + bglu_ref[...]
            z = (g * jax.nn.sigmoid(gate)).astype(BF16)
            x = x + jnp.dot(z, wout_ref[...], preferred_element_type=F32)

        xb = _rms(x, g_ref[...]).astype(BF16)
        acc = None
        for lo, hi in zip(FFN_CHUNK_BOUNDS[:-1], FFN_CHUNK_BOUNDS[1:]):
            gate = jnp.dot(xb, wg_ref[:, lo:hi], preferred_element_type=F32)
            up = jnp.dot(xb, wu_ref[:, lo:hi], preferred_element_type=F32)
            hmid = (gate * jax.nn.sigmoid(gate) * up).astype(BF16)
            part = jnp.dot(hmid, wd_ref[lo:hi, :], preferred_element_type=F32)
            acc = part if acc is None else acc + part
        out = x + 0.5 * acc

        if post == "final":
            out = _rms(out, gf_ref[...])
        o_refs[s][...] = out
        if post == "proj_slabs":
            proj = jnp.dot(_rms(out, gp_ref[...]).astype(BF16), wp_ref[...], preferred_element_type=F32)
            for k in range(p_refs[s].shape[0]):
                p_refs[s][k] = proj[:, k * LANES:(k + 1) * LANES]

    run(0)
    pl.when(pl.program_id(0) == 0)(lambda: run(1))


def _pick_spec(arr, *idx):
    tail = arr.shape[len(idx):]
    index = tuple(idx) + (0,) * len(tail)
    return pl.BlockSpec((None,) * len(idx) + tail, lambda *_: index, pipeline_mode=pl.Buffered(1))


def _whole_spec(shape):
    nd = len(shape)
    return pl.BlockSpec(shape, lambda *_: (0,) * nd)


def _row_specs(xs):
    return [pl.BlockSpec((STAGE_TM, xs[0].shape[1]), lambda i: (i, 0)), _whole_spec(xs[1].shape)]


def _slab_specs(ys):
    return [pl.BlockSpec((SSM_LANE_BLOCKS, STAGE_TM, LANES), lambda i: (0, i, 0)), _whole_spec(ys[1].shape)]


def _stage(xs, W, ffn_w, layer, half, *, pre=None, pre_ins=None, post=None, cast_next=None, side_casts=()):
    steps = xs[0].shape[0] // STAGE_TM
    ins, specs = list(xs), _row_specs(xs)
    if pre == "attn":
        ins += list(pre_ins) + [W["o"]]
        specs += _row_specs(pre_ins) + [_pick_spec(W["o"], 0)]
    elif pre == "ssm":
        ins += list(pre_ins) + [W["ssm_w_glu"], W["ssm_b_glu"], W["ssm_w_out"]]
        specs += _slab_specs(pre_ins) + [_pick_spec(W["ssm_w_glu"], 0), _pick_spec(W["ssm_b_glu"], 0),
                                         _pick_spec(W["ssm_w_out"], 0)]
    ins += [W["norms"]] + list(ffn_w)
    specs += [_pick_spec(W["norms"], layer, 2 * half)] + [_pick_spec(w) for w in ffn_w]
    out_specs = _row_specs(xs)
    out_shape = [jax.ShapeDtypeStruct(x.shape, F32) for x in xs]
    if post == "final":
        ins += [W["final_norm"]]
        specs += [_const_spec((1, D_MODEL))]
    elif post == "proj_slabs":
        ins += [W["norms"], W["ssm_w_in"]]
        specs += [_pick_spec(W["norms"], layer, 1), _pick_spec(W["ssm_w_in"], 0)]
    cast_out_specs, cast_out_shape = [], []
    if cast_next:
        nl, nh = cast_next
        rows_per_step = D_MODEL // steps
        steps_per_col = steps // (D_MODEL // LANES)
        assert rows_per_step * steps == D_MODEL and steps_per_col * (D_MODEL // LANES) == steps
        ins += [W[name + "_f32"] for name in FFN_WEIGHTS]
        specs += [pl.BlockSpec((None, None, rows_per_step, D_FF), lambda i: (nl, nh, i, 0))] * 2
        specs += [pl.BlockSpec((None, None, D_FF, LANES), lambda i: (nl, nh, 0, i // steps_per_col))]
        cast_out_specs += [pl.BlockSpec((rows_per_step, D_FF), lambda i: (i, 0))] * 2
        cast_out_specs += [pl.BlockSpec((D_FF, LANES), lambda i: (0, i // steps_per_col))]
        cast_out_shape += [jax.ShapeDtypeStruct(s, BF16)
                           for s in ((D_MODEL, D_FF), (D_MODEL, D_FF), (D_FF, D_MODEL))]
    for w, nslices in side_casts:
        rows, cols = w.shape
        assert rows % nslices == 0 and steps % nslices == 0
        spec = pl.BlockSpec((rows // nslices, cols), lambda i, hold=steps // nslices: (i // hold, 0))
        ins.append(w)
        specs.append(spec)
        cast_out_specs.append(spec)
        cast_out_shape.append(jax.ShapeDtypeStruct(w.shape, BF16))
    if post == "proj_slabs":
        slabs = [jax.ShapeDtypeStruct((SSM_LANE_BLOCKS, x.shape[0], LANES), F32) for x in xs]
        out_specs += _slab_specs(slabs)
        out_shape += slabs
    out_specs += cast_out_specs
    out_shape += cast_out_shape
    outs = list(pl.pallas_call(
        functools.partial(_stage_kernel, pre=pre, post=post, n_casts=len(cast_out_specs)),
        grid=(steps,),
        in_specs=specs,
        out_specs=out_specs,
        out_shape=out_shape,
        compiler_params=_params(("arbitrary",)),
        name="stage_" + "_".join(s for s in (pre, "ffn", post) if s),
    )(*ins))
    take = lambda n: [outs.pop(0) for _ in range(n)]
    return dict(x=take(N_STREAMS), proj=take(N_STREAMS) if post == "proj_slabs" else None,
                next_w=take(len(FFN_WEIGHTS)) if cast_next else None, side=take(len(side_casts)))


def _qkv_kernel(x0_ref, x1_ref, g_ref, w_ref, c0_ref, c1_ref, c2_ref,
                o0_ref, o1_ref, om_ref, n0_ref, n1_ref, n2_ref, *, t_len):
    def run(x_ref, o_ref):
        xb = _rms(x_ref[...], g_ref[...]).astype(BF16)
        o_ref[...] = jnp.dot(xb, w_ref[...], preferred_element_type=F32)

    i = pl.program_id(0)
    run(x0_ref, o0_ref)
    pl.when(i == 0)(lambda: run(x1_ref, o1_ref))
    qkv_seq = o1_ref.at[pl.ds(pl.multiple_of(i * t_len, t_len), t_len), :]
    _attn_sample_kernel(qkv_seq, c0_ref, c1_ref, c2_ref, om_ref, n0_ref, n1_ref, n2_ref, t_len=t_len)


def _qkv_and_sample_attn(xs, norms, layer, w, caches, *, t_len):
    n = w.shape[-1]
    steps = xs[0].shape[0] // STAGE_TM
    nseq = caches[0].shape[0]
    assert steps == nseq and xs[1].shape[0] == nseq * t_len
    qkv_shapes = [jax.ShapeDtypeStruct((x.shape[0], n), F32) for x in xs]
    cache_specs = [pl.BlockSpec((None,) + c.shape[1:], lambda b: (b, 0, 0)) for c in caches]
    outs = pl.pallas_call(
        functools.partial(_qkv_kernel, t_len=t_len),
        grid=(steps,),
        in_specs=_row_specs(xs) + [_pick_spec(norms, layer, 1), _pick_spec(w, 0)] + cache_specs,
        out_specs=_row_specs(qkv_shapes) + [pl.BlockSpec((t_len, ATTN_WIDTH), lambda b: (b, 0))] + cache_specs,
        out_shape=qkv_shapes
        + [jax.ShapeDtypeStruct((xs[1].shape[0], ATTN_WIDTH), F32)]
        + [jax.ShapeDtypeStruct(c.shape, F32) for c in caches],
        compiler_params=_params(("arbitrary",)),
        name="qkv_proj_sample_attn",
    )(*xs, norms, w, *caches)
    return outs[:N_STREAMS], outs[N_STREAMS], outs[N_STREAMS + 1:]


def _alibi_slope(head):
    return 2.0 ** (-8.0 * (head + 1) / N_HEADS)


def _head_lane_mask(hh):
    lane = lax.broadcasted_iota(jnp.int32, (1, LANES), 1)
    return (lane >= hh * HEAD_DIM) & (lane < (hh + 1) * HEAD_DIM)


ATTN_TILE_TOKENS = 2048
_NT = (((1,), (1,)), ((), ()))


def _merge_groups(outs, lses):
    m = jnp.maximum(jnp.maximum(lses[0], lses[1]), lses[2])
    es = [jnp.exp(l - m) for l in lses]
    tot = es[0] + es[1] + es[2]
    return [o * (e / tot) for o, e in zip(outs, es)]


def _attn_kernel(*refs, tiles_per_seq):
    om_ref, o_scr, l_scr = refs[5 * N_GROUPS:]
    has_prev = (pl.program_id(0) % tiles_per_seq) != 0
    for g in range(N_GROUPS):
        _attn_group(*refs[5 * g:5 * g + 5], o_scr.at[g], l_scr.at[g], group=g, has_prev=has_prev)
    merged = _merge_groups([o_scr[g] for g in range(N_GROUPS)], [l_scr[g] for g in range(N_GROUPS)])
    for g in range(N_GROUPS):
        om_ref[:, g * LANES:(g + 1) * LANES] = merged[g].astype(BF16)


def _attn_group(q_ref, kp_ref, kc_ref, vp_ref, vc_ref, o_ref, l_ref, *, group, has_prev):
    dil = DILATED_GROUPS[group][1]
    span = BAND * dil
    nblk = q_ref.shape[0] // span
    c = pl.program_id(1)

    row = lax.broadcasted_iota(jnp.int32, (2 * BAND, 2 * BAND), 0)
    kj = lax.broadcasted_iota(jnp.int32, (2 * BAND, 2 * BAND), 1)
    delta = BAND + (row & (BAND - 1)) - kj
    head0 = group * HEADS_PER_GROUP
    slope_a = jnp.where(c == 0, _alibi_slope(head0), _alibi_slope(head0 + 2)).astype(F32)
    slope_b = jnp.where(c == 0, _alibi_slope(head0 + 1), _alibi_slope(head0 + 3)).astype(F32)
    slope = jnp.where(row < BAND, slope_a, slope_b)
    band = (delta >= 0) & (delta <= BAND)
    bias = jnp.where(band, -slope * (delta * dil).astype(F32), -jnp.inf)
    bias_first = jnp.where((kj >= BAND) | has_prev, bias, -jnp.inf)
    first_head = _head_lane_mask(0)

    def rows(ref, start):
        return ref[pl.ds(start, BAND), :] if dil == 1 else ref[pl.ds(start, BAND, stride=dil), :]

    for r in range(dil):
        k_prev = rows(kp_ref, r).astype(BF16)
        v_prev = rows(vp_ref, r).astype(BF16)
        for b in range(nblk):
            start = b * span + r
            q = rows(q_ref, start) * (HEAD_DIM ** -0.5)
            k_cur = rows(kc_ref, start).astype(BF16)
            v_cur = rows(vc_ref, start).astype(BF16)
            k = jnp.concatenate([k_prev, k_cur], axis=0)
            v = jnp.concatenate([v_prev, v_cur], axis=0)
            q2 = jnp.concatenate([jnp.where(first_head, q, 0.0), jnp.where(first_head, 0.0, q)], axis=0)
            s = lax.dot_general(q2.astype(BF16), k, _NT, preferred_element_type=F32)
            s = s + (bias_first if b == 0 else bias)
            m = jnp.max(s, axis=-1, keepdims=True)
            p = jnp.exp(s - m)
            den = jnp.sum(p, axis=-1, keepdims=True)
            o2 = jnp.dot(p.astype(BF16), v, preferred_element_type=F32) / den
            l2 = m + jnp.log(den)
            o = jnp.where(first_head, o2[:BAND], o2[BAND:])
            lse = jnp.where(first_head, l2[:BAND], l2[BAND:])
            if dil == 1:
                o_ref[pl.ds(start, BAND), :] = o
                l_ref[pl.ds(start, BAND), :] = lse
            else:
                o_ref[pl.ds(start, BAND, stride=dil), :] = o
                l_ref[pl.ds(start, BAND, stride=dil), :] = lse
            k_prev, v_prev = k_cur, v_cur


MERGED_BLOCK = N_GROUPS * LANES


def _attn_prompt(qkv, *, seq):
    rows = qkv.shape[0]
    tt = ATTN_TILE_TOKENS
    specs = []
    for g, (_, dil) in enumerate(DILATED_GROUPS):
        span = BAND * dil
        nblk = tt // span
        cur = lambda col: pl.BlockSpec((tt, LANES), lambda i, c: (i, col + c))
        prev = lambda col, span=span, nblk=nblk: pl.BlockSpec(
            (span, LANES), lambda i, c: (jnp.maximum(i * nblk - 1, 0), col + c))
        qcol = g * GROUP_WIDTH // LANES
        kcol = (ATTN_WIDTH + g * GROUP_WIDTH) // LANES
        vcol = (2 * ATTN_WIDTH + g * GROUP_WIDTH) // LANES
        specs += [cur(qcol), prev(kcol), cur(kcol), prev(vcol), cur(vcol)]
    return pl.pallas_call(
        functools.partial(_attn_kernel, tiles_per_seq=seq // tt),
        grid=(rows // tt, GROUP_WIDTH // LANES),
        in_specs=specs,
        out_specs=pl.BlockSpec((tt, MERGED_BLOCK), lambda i, c: (i, c)),
        out_shape=jax.ShapeDtypeStruct((rows, ATTN_WIDTH), BF16),
        scratch_shapes=[pltpu.VMEM((N_GROUPS, tt, LANES), F32), pltpu.VMEM((N_GROUPS, tt, LANES), F32)],
        compiler_params=_params(("parallel", "parallel")),
        name="attn_prompt",
    )(*([qkv] * len(specs)))


def _attn_sample_kernel(qkv_ref, c0_ref, c1_ref, c2_ref, om_ref, n0_ref, n1_ref, n2_ref, *, t_len):
    cache_refs = (c0_ref, c1_ref, c2_ref)
    n_refs = (n0_ref, n1_ref, n2_ref)
    outs, lses = [], []
    gw = GROUP_WIDTH
    nrow = HEADS_PER_GROUP * t_len
    row_head = lax.broadcasted_iota(jnp.int32, (nrow, gw), 0) // t_len
    col_head = lax.broadcasted_iota(jnp.int32, (nrow, gw), 1) // HEAD_DIM
    out_col_head = lax.broadcasted_iota(jnp.int32, (t_len, gw), 1) // HEAD_DIM

    def gather_heads(x):
        acc = jnp.zeros((t_len, gw), F32)
        for h in range(HEADS_PER_GROUP):
            acc = acc + jnp.where(out_col_head == h, x[h * t_len:(h + 1) * t_len, :], 0.0)
        return acc

    for g, (win, dil) in enumerate(DILATED_GROUPS):
        cache_ref = cache_refs[g]
        wb = cache_ref.shape[1]
        q0 = g * gw
        k0 = ATTN_WIDTH + g * gw
        v0 = 2 * ATTN_WIDTH + g * gw

        kv_new = jnp.concatenate([qkv_ref[:, k0:k0 + gw], qkv_ref[:, v0:v0 + gw]], axis=1)
        kv_pad = jnp.concatenate([kv_new, jnp.zeros((LANES - t_len, 2 * gw), F32)], axis=0)
        new_t = kv_pad.T

        q = qkv_ref[:, q0:q0 + gw] * (HEAD_DIM ** -0.5)
        q_bd = jnp.where(row_head == col_head, jnp.concatenate([q] * HEADS_PER_GROUP, axis=0), 0.0).astype(BF16)

        def bias(ncol, base):
            r = lax.broadcasted_iota(jnp.int32, (nrow, ncol), 0)
            w = lax.broadcasted_iota(jnp.int32, (nrow, ncol), 1)
            dist = base + (r % t_len) - w
            ok = (dist >= 0) & (dist <= win) & ((dist & (dil - 1)) == 0)
            slope = jnp.zeros((nrow, ncol), F32)
            for h in range(HEADS_PER_GROUP):
                slope = jnp.where(r // t_len == h, _alibi_slope(g * HEADS_PER_GROUP + h), slope)
            return jnp.where(ok, -slope * dist.astype(F32), -jnp.inf)

        s1 = jnp.dot(q_bd, cache_ref[0:gw, :].astype(BF16), preferred_element_type=F32) + bias(wb, wb)
        pad_ok = lax.broadcasted_iota(jnp.int32, (nrow, LANES), 1) < t_len
        s2 = jnp.dot(q_bd, new_t[0:gw, :].astype(BF16), preferred_element_type=F32)
        s2 = jnp.where(pad_ok, s2 + bias(LANES, 0), -jnp.inf)
        m = jnp.maximum(jnp.max(s1, axis=-1, keepdims=True), jnp.max(s2, axis=-1, keepdims=True))
        p1 = jnp.exp(s1 - m)
        p2 = jnp.exp(s2 - m)
        den = jnp.sum(p1, axis=-1, keepdims=True) + jnp.sum(p2, axis=-1, keepdims=True)
        o = (lax.dot_general(p1.astype(BF16), cache_ref[gw:2 * gw, :].astype(BF16), _NT, preferred_element_type=F32)
             + lax.dot_general(p2.astype(BF16), new_t[gw:2 * gw, :].astype(BF16), _NT, preferred_element_type=F32))
        outs.append(gather_heads(o / den))
        lses.append(gather_heads(jnp.broadcast_to(m + jnp.log(den), (nrow, gw))))

        ext = jnp.concatenate([cache_ref[...], new_t], axis=1)
        n_refs[g][...] = pltpu.roll(ext, wb + LANES - t_len, axis=1)[:, 0:wb]

    merged = _merge_groups(outs, lses)
    for g in range(N_GROUPS):
        for c in range(gw // LANES):
            col = c * MERGED_BLOCK + g * LANES
            om_ref[:, col:col + LANES] = merged[g][:, c * LANES:(c + 1) * LANES]


SSM_CHUNK = 16
N_LAM_POWERS = SSM_CHUNK + 1


def _ssm_prep_kernel(a_re_ref, a_im_ref, logdt_ref, bre_ref, bim_ref, pre_ref, pim_ref, bbre_ref, bbim_ref):
    a_re = a_re_ref[...]
    a_im = a_im_ref[...]
    dt = jnp.exp(logdt_ref[...])
    mag = jnp.exp(a_re * dt)
    ang = a_im * dt
    lam_re = mag * jnp.cos(ang)
    lam_im = mag * jnp.sin(ang)
    den = a_re * a_re + a_im * a_im
    num_re = lam_re - 1.0
    coef_re = ((num_re * a_re + lam_im * a_im) / den)[None]
    coef_im = ((lam_im * a_re - num_re * a_im) / den)[None]
    b_re = bre_ref[...]
    b_im = bim_ref[...]
    bbre_ref[...] = coef_re * b_re - coef_im * b_im
    bbim_ref[...] = coef_re * b_im + coef_im * b_re
    p_re, p_im = jnp.ones_like(lam_re), jnp.zeros_like(lam_im)
    for m in range(N_LAM_POWERS):
        pre_ref[m] = p_re
        pim_ref[m] = p_im
        p_re, p_im = p_re * lam_re - p_im * lam_im, p_re * lam_im + p_im * lam_re


def _ssm_prep(a_re, a_im, log_dt, b_re_t, b_im_t):
    g, n = a_re.shape
    pw = jax.ShapeDtypeStruct((N_LAM_POWERS, g, n), F32)
    bb = jax.ShapeDtypeStruct(b_re_t.shape, F32)
    return pl.pallas_call(
        _ssm_prep_kernel,
        out_shape=[pw, pw, bb, bb],
        name="ssm_prep",
    )(a_re, a_im, log_dt.reshape(g, 1), b_re_t, b_im_t)


SCAN_BLOCKS_PER_STEP = 4


def _ssm_rows_kernel(*refs, t_len, nseq):
    for k in range(refs[0].shape[0]):
        _ssm_rows_block(*[r.at[k] for r in refs], t_len=t_len, nseq=nseq)


def _expand_groups(x):
    width = x.shape[1]
    tiled = jnp.concatenate([x] * GROUPS_PER_BLOCK, axis=0)
    row_grp = lax.broadcasted_iota(jnp.int32, tiled.shape, 0) // SSM_CH
    lane_grp = lax.broadcasted_iota(jnp.int32, tiled.shape, 1) // (width // GROUPS_PER_BLOCK)
    return jnp.where(row_grp == lane_grp, tiled, 0.0)


def _ssm_rows_block(u_ref, bb_ref, ct_ref, pw_ref, d_ref, h0_ref, y_ref, fs_ref,
                    bu_scr, xre_scr, xim_scr, ud_scr, *, t_len, nseq):
    sl = STATE_LANES
    for j in range(t_len):
        ud_scr[j * nseq:(j + 1) * nseq, :] = u_ref[pl.ds(j, nseq, stride=t_len), :].astype(BF16)
    bb = jnp.concatenate([_expand_groups(bb_ref[:, 0:sl]), _expand_groups(bb_ref[:, sl:2 * sl])], axis=1)
    bu_scr[...] = jnp.dot(ud_scr[...], bb.astype(BF16), preferred_element_type=F32)

    lam_re = jnp.broadcast_to(pw_ref[1:2, :], (nseq, sl))
    lam_im = jnp.broadcast_to(pw_ref[N_LAM_POWERS + 1:N_LAM_POWERS + 2, :], (nseq, sl))
    x_re = h0_ref[:, 0:sl]
    x_im = h0_ref[:, sl:2 * sl]
    for j in range(t_len):
        b_re = bu_scr[j * nseq:(j + 1) * nseq, 0:sl]
        b_im = bu_scr[j * nseq:(j + 1) * nseq, sl:2 * sl]
        x_re, x_im = (lam_re * x_re - lam_im * x_im + b_re, lam_re * x_im + lam_im * x_re + b_im)
        xre_scr[j * nseq:(j + 1) * nseq, :] = x_re.astype(BF16)
        xim_scr[j * nseq:(j + 1) * nseq, :] = x_im.astype(BF16)
    fs_ref[:, 0:sl] = x_re
    fs_ref[:, sl:2 * sl] = x_im

    ct_re = _expand_groups(ct_ref[:, 0:sl]).astype(BF16)
    ct_im = _expand_groups(ct_ref[:, sl:2 * sl]).astype(BF16)
    y = (lax.dot_general(xre_scr[...], ct_re, _NT, preferred_element_type=F32)
         - lax.dot_general(xim_scr[...], ct_im, _NT, preferred_element_type=F32))
    d = d_ref[...]
    for j in range(t_len):
        y_ref[pl.ds(j, nseq, stride=t_len), :] = (
            y[j * nseq:(j + 1) * nseq, :] + d * u_ref[pl.ds(j, nseq, stride=t_len), :])


def _ssm_scan_rows(u, tabs, h0, *, t_len, nseq):
    nblk, rows, _ = u.shape
    kb = SCAN_BLOCKS_PER_STEP
    sl2 = 2 * STATE_LANES
    blk = lambda arr: pl.BlockSpec((kb,) + arr.shape[1:], lambda k: (k,) + (0,) * (arr.ndim - 1))
    ins = (u, tabs["bb"], tabs["ct"], tabs["pw_rows"], tabs["dvec"], h0)
    return pl.pallas_call(
        functools.partial(_ssm_rows_kernel, t_len=t_len, nseq=nseq),
        grid=(nblk // kb,),
        in_specs=[blk(a) for a in ins],
        out_specs=[blk(u), blk(h0)],
        out_shape=[jax.ShapeDtypeStruct(u.shape, F32), jax.ShapeDtypeStruct(h0.shape, F32)],
        scratch_shapes=[
            pltpu.VMEM((kb, rows, sl2), F32),
            pltpu.VMEM((kb, rows, STATE_LANES), BF16),
            pltpu.VMEM((kb, rows, STATE_LANES), BF16),
            pltpu.VMEM((kb, rows, LANES), BF16),
        ],
        compiler_params=_params(("parallel",)),
        name="ssm_scan_rows",
    )(*ins)


GATHER_STRIDE = 4
assert SSM_CHUNK == GATHER_STRIDE * GATHER_STRIDE


def _ssm_seq_kernel(u_ref, bb_ref, ct_ref, pw_ref, d_ref, h0_ref, y_ref, fs_ref,
                    g_scr, kt_scr, clt_scr, ucat_scr, e_scr, hp_scr, s1_scr, s2_scr, y1_scr):
    L = SSM_CHUNK
    sl = STATE_LANES
    seq = u_ref.shape[0]
    nch = seq // L

    @pl.when(pl.program_id(1) == 0)
    def _build_tables():
        lam_re = pw_ref[1:2, :]
        lam_im = pw_ref[N_LAM_POWERS + 1:N_LAM_POWERS + 2, :]
        q_re = ct_ref[:, 0:sl]
        q_im = ct_ref[:, sl:2 * sl]
        ccs_t = jnp.concatenate([_expand_groups(q_re), -_expand_groups(q_im)], axis=1).astype(BF16)
        kt_scr[...] = jnp.zeros(kt_scr.shape, BF16)
        g_re = bb_ref[:, 0:sl]
        g_im = bb_ref[:, sl:2 * sl]
        for m in range(L):
            j = L - 1 - m
            g_m = jnp.concatenate([_expand_groups(g_re), _expand_groups(g_im)], axis=1).astype(BF16)
            g_scr[j * LANES:(j + 1) * LANES, :] = g_m
            k_m = lax.dot_general(g_m, ccs_t, _NT, preferred_element_type=F32).astype(BF16)
            for jj in range(L - m):
                i = jj + m
                kt_scr[jj * LANES:(jj + 1) * LANES, i * LANES:(i + 1) * LANES] = k_m
            g_re, g_im = g_re * lam_re - g_im * lam_im, g_re * lam_im + g_im * lam_re
            q_re, q_im = q_re * lam_re - q_im * lam_im, q_re * lam_im + q_im * lam_re
            clt_scr[m * LANES:(m + 1) * LANES, 0:sl] = _expand_groups(q_re).astype(BF16)
            clt_scr[m * LANES:(m + 1) * LANES, sl:2 * sl] = _expand_groups(-q_im).astype(BF16)

    qs = GATHER_STRIDE
    for b in range(qs):
        s1_scr[b] = u_ref[pl.ds(b, seq // qs, stride=qs), :]
    for j in range(L):
        a, b = divmod(j, qs)
        ucat_scr[:, j * LANES:(j + 1) * LANES] = s1_scr[b, pl.ds(a, nch, stride=qs), :].astype(BF16)
    e_scr[...] = jnp.dot(ucat_scr[...], g_scr[...], preferred_element_type=F32)

    lc_re = pw_ref[L:L + 1, :]
    lc_im = pw_ref[N_LAM_POWERS + L:N_LAM_POWERS + L + 1, :]

    pair = 2 * LANES
    for p in range(L // 2):
        y1_scr[:, p * pair:(p + 1) * pair] = jnp.dot(
            ucat_scr[:, 0:(p + 1) * pair], kt_scr[0:(p + 1) * pair, p * pair:(p + 1) * pair],
            preferred_element_type=F32)

    h_re, h_im = h0_ref[:, 0:sl], h0_ref[:, sl:2 * sl]
    for c in range(nch):
        hp_scr[c:c + 1, 0:sl] = h_re
        hp_scr[c:c + 1, sl:2 * sl] = h_im
        e_re = e_scr[c:c + 1, 0:sl]
        e_im = e_scr[c:c + 1, sl:2 * sl]
        h_re, h_im = (lc_re * h_re - lc_im * h_im + e_re, lc_re * h_im + lc_im * h_re + e_im)
    fs_ref[:, 0:sl] = h_re
    fs_ref[:, sl:2 * sl] = h_im

    h_prev = hp_scr[...].astype(BF16)
    for p in range(L // 2):
        y2 = y1_scr[:, p * pair:(p + 1) * pair] + lax.dot_general(
            h_prev, clt_scr[p * pair:(p + 1) * pair, :], _NT, preferred_element_type=F32)
        for half in range(2):
            a, b = divmod(2 * p + half, qs)
            s2_scr[b, pl.ds(a, nch, stride=qs), :] = y2[:, half * LANES:(half + 1) * LANES]
    d = d_ref[...]
    for b in range(qs):
        y_ref[pl.ds(b, seq // qs, stride=qs), :] = s2_scr[b] + d * s1_scr[b]


def _ssm_scan_seq(u, tabs, h0, *, nseq, seq):
    nblk = u.shape[0]
    L = SSM_CHUNK
    nch = seq // L
    sl2 = 2 * STATE_LANES
    blk = lambda arr: pl.BlockSpec((None,) + arr.shape[1:], lambda k, b: (k,) + (0,) * (arr.ndim - 1))
    row_spec = pl.BlockSpec((None, seq, LANES), lambda k, b: (k, b, 0))
    st_spec = pl.BlockSpec((None, None, 1, sl2), lambda k, b: (k, b, 0, 0))
    return pl.pallas_call(
        _ssm_seq_kernel,
        grid=(nblk, nseq),
        in_specs=[row_spec, blk(tabs["bb"]), blk(tabs["ct"]), blk(tabs["pw_rows"]),
                  blk(tabs["dvec"]), st_spec],
        out_specs=[row_spec, st_spec],
        out_shape=[jax.ShapeDtypeStruct(u.shape, F32), jax.ShapeDtypeStruct(h0.shape, F32)],
        scratch_shapes=[
            pltpu.VMEM((L * LANES, sl2), BF16),
            pltpu.VMEM((L * LANES, L * LANES), BF16),
            pltpu.VMEM((L * LANES, sl2), BF16),
            pltpu.VMEM((nch, L * LANES), BF16),
            pltpu.VMEM((nch, sl2), F32),
            pltpu.VMEM((nch, sl2), F32),
            pltpu.VMEM((GATHER_STRIDE, seq // GATHER_STRIDE, LANES), F32),
            pltpu.VMEM((GATHER_STRIDE, seq // GATHER_STRIDE, LANES), F32),
            pltpu.VMEM((nch, L * LANES), F32),
        ],
        compiler_params=_params(("arbitrary", "arbitrary")),
        name="ssm_scan_seq",
    )(u, tabs["bb"], tabs["ct"], tabs["pw_rows"], tabs["dvec"], h0)


def _lane_blocks(x):
    lead = x.shape[:-2]
    x = x.reshape(lead + (SSM_LANE_BLOCKS, STATE_LANES))
    return jnp.moveaxis(x, -2, 0)


def _ssm_tables(P, j):
    b_re_t = jnp.transpose(P["ssm_b_re"][j], (2, 0, 1))
    b_im_t = jnp.transpose(P["ssm_b_im"][j], (2, 0, 1))
    p_re, p_im, bb_re, bb_im = _ssm_prep(P["ssm_a_re"][j], P["ssm_a_im"][j], P["ssm_log_dt"][j], b_re_t, b_im_t)
    bb = jnp.concatenate([_lane_blocks(bb_re), _lane_blocks(bb_im)], axis=2)
    ct = jnp.concatenate([_lane_blocks(jnp.swapaxes(P["ssm_c_re"][j], 0, 1)),
                          _lane_blocks(jnp.swapaxes(P["ssm_c_im"][j], 0, 1))], axis=2)
    pw_rows = jnp.concatenate([_lane_blocks(p_re), _lane_blocks(p_im)], axis=1)
    dvec = P["ssm_d"][j].reshape(SSM_LANE_BLOCKS, 1, LANES)
    return dict(bb=bb, ct=ct, pw_rows=pw_rows, dvec=dvec)


def _ssm_scan(u, tabs, state, *, nseq, seq):
    if state is None:
        h0 = jnp.zeros((SSM_LANE_BLOCKS, nseq, 1, 2 * STATE_LANES), F32)
        y, fs = _ssm_scan_seq(u, tabs, h0, nseq=nseq, seq=seq)
        fs = fs.reshape(SSM_LANE_BLOCKS, nseq, 2 * STATE_LANES)
    else:
        y, fs = _ssm_scan_rows(u, tabs, _state_to_blocks(state), t_len=seq, nseq=nseq)
    return y, _state_from_blocks(fs)


def _state_from_blocks(fs):
    nblk, nseq, _ = fs.shape
    fs = fs.reshape(nblk, nseq, 2, GROUPS_PER_BLOCK, SSM_STATE)
    fs = jnp.transpose(fs, (1, 0, 3, 4, 2))
    return fs.reshape(nseq, SSM_GROUPS, SSM_STATE, 2)


def _state_to_blocks(state):
    nseq = state.shape[0]
    s = state.reshape(nseq, SSM_LANE_BLOCKS, GROUPS_PER_BLOCK, SSM_STATE, 2)
    return jnp.transpose(s, (1, 0, 4, 2, 3)).reshape(SSM_LANE_BLOCKS, nseq, 2 * STATE_LANES)


def _to_window_minor(c):
    return jnp.transpose(c[0], (0, 2, 3, 4, 1)).reshape(c.shape[1], 2 * GROUP_WIDTH, c.shape[2])


def _from_window_minor(c):
    c = c.reshape(c.shape[0], 2, HEADS_PER_GROUP, HEAD_DIM, c.shape[2])
    return jnp.transpose(c, (0, 4, 1, 2, 3))[None]


def _kv_tail_kernel(k_ref, v_ref, o_ref):
    o_ref[0:GROUP_WIDTH, :] = k_ref[...].T
    o_ref[GROUP_WIDTH:2 * GROUP_WIDTH, :] = v_ref[...].T


def _kv_tail(qkv, *, group, nseq, seq):
    wb = min(DILATED_GROUPS[group][0], seq)
    kcol = (ATTN_WIDTH + group * GROUP_WIDTH) // GROUP_WIDTH
    vcol = (2 * ATTN_WIDTH + group * GROUP_WIDTH) // GROUP_WIDTH
    tail = lambda col: pl.BlockSpec((wb, GROUP_WIDTH), lambda b: ((b + 1) * (seq // wb) - 1, col))
    return pl.pallas_call(
        _kv_tail_kernel,
        grid=(nseq,),
        in_specs=[tail(kcol), tail(vcol)],
        out_specs=pl.BlockSpec((None, 2 * GROUP_WIDTH, wb), lambda b: (b, 0, 0)),
        out_shape=jax.ShapeDtypeStruct((nseq, 2 * GROUP_WIDTH, wb), F32),
        compiler_params=_params(("parallel",)),
        name="kv_tail",
    )(qkv, qkv)


SIDE_CAST_SLICES = dict(qkv=32, o=8, ssm_w_in=32, ssm_w_glu=32, ssm_w_out=32)


def _trunks(xs, caches, ssm_state, W, *, nseqs, seqs):
    ffn_w = [W[name + "_f32"][0, 0].astype(BF16) for name in FFN_WEIGHTS]

    small = ("qkv", "o", "ssm_w_in", "ssm_w_glu", "ssm_w_out")
    st = _stage(xs, W, ffn_w, 0, 0, cast_next=(0, 1),
                side_casts=[(W[name + "_f32"], SIDE_CAST_SLICES[name]) for name in small])
    W = dict(W, **{name: w[None] for name, w in zip(small, st["side"])})
    hs = st["x"]
    qkvs, merged_s, new_caches = _qkv_and_sample_attn(hs, W["norms"], 0, W["qkv"], caches, t_len=seqs[1])
    merged_p = _attn_prompt(qkvs[0], seq=seqs[0])
    st = _stage(hs, W, st["next_w"], 0, 1, pre="attn", pre_ins=(merged_p, merged_s), cast_next=(1, 0))

    st = _stage(st["x"], W, st["next_w"], 1, 0, post="proj_slabs", cast_next=(1, 1))
    us = st["proj"]
    y_p, state_p = _ssm_scan(us[0], W["ssm"], None, nseq=nseqs[0], seq=seqs[0])
    y_s, state_s = _ssm_scan(us[1], W["ssm"], ssm_state, nseq=nseqs[1], seq=seqs[1])
    outs = _stage(st["x"], W, st["next_w"], 1, 1, pre="ssm", pre_ins=(y_p, y_s), post="final")["x"]
    return outs, qkvs[0], new_caches, (state_p, state_s)


def kernel(x_prompt, x_sample, cache_kv_g0, cache_kv_g1, cache_kv_g2, state_ssm, norm_g, final_norm_g,
           ffn_w_gate, ffn_w_up, ffn_w_down, attn_w_qkv, attn_w_o, ssm_w_in, ssm_a_re, ssm_a_im, ssm_log_dt,
           ssm_b_re, ssm_b_im, ssm_c_re, ssm_c_im, ssm_d, ssm_w_glu, ssm_b_glu, ssm_w_out):
    batch, seq, _ = x_prompt.shape
    dec_batch, dec_seq, _ = x_sample.shape
    P = dict(ssm_a_re=ssm_a_re, ssm_a_im=ssm_a_im, ssm_log_dt=ssm_log_dt, ssm_b_re=ssm_b_re,
             ssm_b_im=ssm_b_im, ssm_c_re=ssm_c_re, ssm_c_im=ssm_c_im, ssm_d=ssm_d)
    W = dict(
        norms=norm_g.reshape(norm_g.shape[0], norm_g.shape[1], 1, D_MODEL),
        final_norm=final_norm_g.reshape(1, D_MODEL),
        gate_f32=ffn_w_gate, up_f32=ffn_w_up, down_f32=ffn_w_down,
        qkv_f32=attn_w_qkv[0],
        o_f32=jnp.swapaxes(attn_w_o[0].reshape(N_GROUPS, GROUP_WIDTH // LANES, LANES, D_MODEL), 0, 1)
        .reshape(ATTN_WIDTH, D_MODEL),
        ssm_w_in_f32=ssm_w_in[0], ssm_w_glu_f32=ssm_w_glu[0], ssm_w_out_f32=ssm_w_out[0],
        ssm_b_glu=ssm_b_glu.reshape(ssm_b_glu.shape[0], 1, D_MODEL),
        ssm=_ssm_tables(P, 0),
    )

    xs = (x_prompt.reshape(batch * seq, D_MODEL), x_sample.reshape(dec_batch * dec_seq, D_MODEL))
    caches = [_to_window_minor(c) for c in (cache_kv_g0, cache_kv_g1, cache_kv_g2)]
    (y_p, y_s), qkv_p, new_caches, (ssm_p, ssm_s) = _trunks(
        xs, caches, state_ssm[0], W, nseqs=(batch, dec_batch), seqs=(seq, dec_seq))
    kv_prompt = [_from_window_minor(_kv_tail(qkv_p, group=g, nseq=batch, seq=seq)) for g in range(N_GROUPS)]
    kv_sample = [_from_window_minor(nc) for nc in new_caches]

    return (y_p.reshape(batch, seq, D_MODEL), y_s.reshape(dec_batch, dec_seq, D_MODEL),
            kv_prompt[0], kv_prompt[1], kv_prompt[2], ssm_p[None],
            kv_sample[0], kv_sample[1], kv_sample[2], ssm_s[None])
```

```python
import functools

import jax
import jax.numpy as jnp
from jax import lax
from jax.experimental import pallas as pl
from jax.experimental.pallas import tpu as pltpu

F32 = jnp.float32
BF16 = jnp.bfloat16

D_MODEL = 1024
D_FF = 2816
HEAD_DIM = 64
HEADS_PER_GROUP = 4
DILATED_GROUPS = ((128, 1), (512, 4), (2048, 16))
N_GROUPS = len(DILATED_GROUPS)
N_HEADS = N_GROUPS * HEADS_PER_GROUP
GROUP_WIDTH = HEADS_PER_GROUP * HEAD_DIM
ATTN_WIDTH = N_HEADS * HEAD_DIM
BAND = 128
SSM_CH = 16
SSM_GROUPS = D_MODEL // SSM_CH
SSM_STATE = 64
NORM_EPS = 1e-6

LANES = 128
V7X_VMEM_BYTES = 64 * 1024 * 1024
VMEM_LIMIT = V7X_VMEM_BYTES - 8 * 1024 * 1024

SSM_LANE_BLOCKS = D_MODEL // LANES
GROUPS_PER_BLOCK = LANES // SSM_CH
STATE_LANES = GROUPS_PER_BLOCK * SSM_STATE

assert all(w // d == BAND for w, d in DILATED_GROUPS)
assert all(d & (d - 1) == 0 for _, d in DILATED_GROUPS)


def _params(sem, vmem=VMEM_LIMIT):
    return pltpu.CompilerParams(dimension_semantics=sem, vmem_limit_bytes=vmem)


def _const_spec(shape):
    nd = len(shape)
    return pl.BlockSpec(shape, lambda *_: (0,) * nd, pipeline_mode=pl.Buffered(1))


def _rms(x, g):
    return x * lax.rsqrt(jnp.mean(x * x, axis=-1, keepdims=True) + NORM_EPS) * g


MXU_TILE = 256
FFN_CHUNK_BOUNDS = (0, 6 * MXU_TILE, D_FF)


N_STREAMS = 2
STAGE_TM = 512


FFN_WEIGHTS = ("gate", "up", "down")


def _stage_kernel(*refs, pre, post, n_casts):
    refs = list(refs)
    take = lambda n: [refs.pop(0) for _ in range(n)]
    x_refs = take(N_STREAMS)
    if pre == "attn":
        om_refs = take(N_STREAMS)
        (wo_ref,) = take(1)
    elif pre == "ssm":
        y_refs = take(N_STREAMS)
        wglu_ref, bglu_ref, wout_ref = take(3)
    g_ref, wg_ref, wu_ref, wd_ref = take(4)
    if post == "final":
        (gf_ref,) = take(1)
    elif post == "proj_slabs":
        gp_ref, wp_ref = take(2)
    cast_srcs = take(n_casts)
    o_refs = take(N_STREAMS)
    if post == "proj_slabs":
        p_refs = take(N_STREAMS)
    for src, dst in zip(cast_srcs, take(n_casts)):
        dst[...] = src[...].astype(BF16)

    def run(s):
        x = x_refs[s][...]
        if pre == "attn":
            x = x + jnp.dot(om_refs[s][...].astype(BF16), wo_ref[...], preferred_element_type=F32)
        elif pre == "ssm":
            y = jnp.concatenate([y_refs[s][k] for k in range(y_refs[s].shape[0])], axis=1)
            g = jax.nn.gelu(y, approximate=True)
            gate = jnp.dot(g.astype(BF16), wglu_ref[...], preferred_element_type=F32) + bglu_ref[...]
            z = (g * jax.nn.sigmoid(gate)).astype(BF16)
            x = x + jnp.dot(z, wout_ref[...], preferred_element_type=F32)

        xb = _rms(x, g_ref[...]).astype(BF16)
        acc = None
        for lo, hi in zip(FFN_CHUNK_BOUNDS[:-1], FFN_CHUNK_BOUNDS[1:]):
            gate = jnp.dot(xb, wg_ref[:, lo:hi], preferred_element_type=F32)
            up = jnp.dot(xb, wu_ref[:, lo:hi], preferred_element_type=F32)
            hmid = (gate * jax.nn.sigmoid(gate) * up).astype(BF16)
            part = jnp.dot(hmid, wd_ref[lo:hi, :], preferred_element_type=F32)
            acc = part if acc is None else acc + part
        out = x + 0.5 * acc

        if post == "final":
            out = _rms(out, gf_ref[...])
        o_refs[s][...] = out
        if post == "proj_slabs":
            proj = jnp.dot(_rms(out, gp_ref[...]).astype(BF16), wp_ref[...], preferred_element_type=F32)
            for k in range(p_refs[s].shape[0]):
                p_refs[s][k] = proj[:, k * LANES:(k + 1) * LANES]

    run(0)
    pl.when(pl.program_id(0) == 0)(lambda: run(1))


def _pick_spec(arr, *idx):
    tail = arr.shape[len(idx):]
    index = tuple(idx) + (0,) * len(tail)
    return pl.BlockSpec((None,) * len(idx) + tail, lambda *_: index, pipeline_mode=pl.Buffered(1))


def _whole_spec(shape):
    nd = len(shape)
    return pl.BlockSpec(shape, lambda *_: (0,) * nd)


def _row_specs(xs):
    return [pl.BlockSpec((STAGE_TM, xs[0].shape[1]), lambda i: (i, 0)), _whole_spec(xs[1].shape)]


def _slab_specs(ys):
    return [pl.BlockSpec((SSM_LANE_BLOCKS, STAGE_TM, LANES), lambda i: (0, i, 0)), _whole_spec(ys[1].shape)]


def _stage(xs, W, ffn_w, layer, half, *, pre=None, pre_ins=None, post=None, cast_next=None, side_casts=()):
    steps = xs[0].shape[0] // STAGE_TM
    ins, specs = list(xs), _row_specs(xs)
    if pre == "attn":
        ins += list(pre_ins) + [W["o"]]
        specs += _row_specs(pre_ins) + [_pick_spec(W["o"], 0)]
    elif pre == "ssm":
        ins += list(pre_ins) + [W["ssm_w_glu"], W["ssm_b_glu"], W["ssm_w_out"]]
        specs += _slab_specs(pre_ins) + [_pick_spec(W["ssm_w_glu"], 0), _pick_spec(W["ssm_b_glu"], 0),
                                         _pick_spec(W["ssm_w_out"], 0)]
    ins += [W["norms"]] + list(ffn_w)
    specs += [_pick_spec(W["norms"], layer, 2 * half)] + [_pick_spec(w) for w in ffn_w]
    out_specs = _row_specs(xs)
    out_shape = [jax.ShapeDtypeStruct(x.shape, F32) for x in xs]
    if post == "final":
        ins += [W["final_norm"]]
        specs += [_const_spec((1, D_MODEL))]
    elif post == "proj_slabs":
        ins += [W["norms"], W["ssm_w_in"]]
        specs += [_pick_spec(W["norms"], layer, 1), _pick_spec(W["ssm_w_in"], 0)]
    cast_out_specs, cast_out_shape = [], []
    if cast_next:
        nl, nh = cast_next
        rows_per_step = D_MODEL // steps
        steps_per_col = steps // (D_MODEL // LANES)
        assert rows_per_step * steps == D_MODEL and steps_per_col * (D_MODEL // LANES) == steps
        ins += [W[name + "_f32"] for name in FFN_WEIGHTS]
        specs += [pl.BlockSpec((None, None, rows_per_step, D_FF), lambda i: (nl, nh, i, 0))] * 2
        specs += [pl.BlockSpec((None, None, D_FF, LANES), lambda i: (nl, nh, 0, i // steps_per_col))]
        cast_out_specs += [pl.BlockSpec((rows_per_step, D_FF), lambda i: (i, 0))] * 2
        cast_out_specs += [pl.BlockSpec((D_FF, LANES), lambda i: (0, i // steps_per_col))]
        cast_out_shape += [jax.ShapeDtypeStruct(s, BF16)
                           for s in ((D_MODEL, D_FF), (D_MODEL, D_FF), (D_FF, D_MODEL))]
    for w, nslices in side_casts:
        rows, cols = w.shape
        assert rows % nslices == 0 and steps % nslices == 0
        spec = pl.BlockSpec((rows // nslices, cols), lambda i, hold=steps // nslices: (i // hold, 0))
        ins.append(w)
        specs.append(spec)
        cast_out_specs.append(spec)
        cast_out_shape.append(jax.ShapeDtypeStruct(w.shape, BF16))
    if post == "proj_slabs":
        slabs = [jax.ShapeDtypeStruct((SSM_LANE_BLOCKS, x.shape[0], LANES), F32) for x in xs]
        out_specs += _slab_specs(slabs)
        out_shape += slabs
    out_specs += cast_out_specs
    out_shape += cast_out_shape
    outs = list(pl.pallas_call(
        functools.partial(_stage_kernel, pre=pre, post=post, n_casts=len(cast_out_specs)),
        grid=(steps,),
        in_specs=specs,
        out_specs=out_specs,
        out_shape=out_shape,
        compiler_params=_params(("arbitrary",)),
        name="stage_" + "_".join(s for s in (pre, "ffn", post) if s),
    )(*ins))
    take = lambda n: [outs.pop(0) for _ in range(n)]
    return dict(x=take(N_STREAMS), proj=take(N_STREAMS) if post == "proj_slabs" else None,
                next_w=take(len(FFN_WEIGHTS)) if cast_next else None, side=take(len(side_casts)))


def _qkv_kernel(x0_ref, x1_ref, g_ref, w_ref, c0_ref, c1_ref, c2_ref,
                o0_ref, o1_ref, om_ref, n0_ref, n1_ref, n2_ref, *, t_len):
    def run(x_ref, o_ref):
        xb = _rms(x_ref[...], g_ref[...]).astype(BF16)
        o_ref[...] = jnp.dot(xb, w_ref[...], preferred_element_type=F32)

    i = pl.program_id(0)
    run(x0_ref, o0_ref)
    pl.when(i == 0)(lambda: run(x1_ref, o1_ref))
    qkv_seq = o1_ref.at[pl.ds(pl.multiple_of(i * t_len, t_len), t_len), :]
    _attn_sample_kernel(qkv_seq, c0_ref, c1_ref, c2_ref, om_ref, n0_ref, n1_ref, n2_ref, t_len=t_len)


def _qkv_and_sample_attn(xs, norms, layer, w, caches, *, t_len):
    n = w.shape[-1]
    steps = xs[0].shape[0] // STAGE_TM
    nseq = caches[0].shape[0]
    assert steps == nseq and xs[1].shape[0] == nseq * t_len
    qkv_shapes = [jax.ShapeDtypeStruct((x.shape[0], n), F32) for x in xs]
    cache_specs = [pl.BlockSpec((None,) + c.shape[1:], lambda b: (b, 0, 0)) for c in caches]
    outs = pl.pallas_call(
        functools.partial(_qkv_kernel, t_len=t_len),
        grid=(steps,),
        in_specs=_row_specs(xs) + [_pick_spec(norms, layer, 1), _pick_spec(w, 0)] + cache_specs,
        out_specs=_row_specs(qkv_shapes) + [pl.BlockSpec((t_len, ATTN_WIDTH), lambda b: (b, 0))] + cache_specs,
        out_shape=qkv_shapes
        + [jax.ShapeDtypeStruct((xs[1].shape[0], ATTN_WIDTH), F32)]
        + [jax.ShapeDtypeStruct(c.shape, F32) for c in caches],
        compiler_params=_params(("arbitrary",)),
        name="qkv_proj_sample_attn",
    )(*xs, norms, w, *caches)
    return outs[:N_STREAMS], outs[N_STREAMS], outs[N_STREAMS + 1:]


def _alibi_slope(head):
    return 2.0 ** (-8.0 * (head + 1) / N_HEADS)


def _head_lane_mask(hh):
    lane = lax.broadcasted_iota(jnp.int32, (1, LANES), 1)
    return (lane >= hh * HEAD_DIM) & (lane < (hh + 1) * HEAD_DIM)


ATTN_TILE_TOKENS = 2048
_NT = (((1,), (1,)), ((), ()))


def _merge_groups(outs, lses):
    m = jnp.maximum(jnp.maximum(lses[0], lses[1]), lses[2])
    es = [jnp.exp(l - m) for l in lses]
    tot = es[0] + es[1] + es[2]
    return [o * (e / tot) for o, e in zip(outs, es)]


def _attn_kernel(*refs, tiles_per_seq):
    om_ref, o_scr, l_scr = refs[5 * N_GROUPS:]
    has_prev = (pl.program_id(0) % tiles_per_seq) != 0
    for g in range(N_GROUPS):
        _attn_group(*refs[5 * g:5 * g + 5], o_scr.at[g], l_scr.at[g], group=g, has_prev=has_prev)
    merged = _merge_groups([o_scr[g] for g in range(N_GROUPS)], [l_scr[g] for g in range(N_GROUPS)])
    for g in range(N_GROUPS):
        om_ref[:, g * LANES:(g + 1) * LANES] = merged[g].astype(BF16)


def _attn_group(q_ref, kp_ref, kc_ref, vp_ref, vc_ref, o_ref, l_ref, *, group, has_prev):
    dil = DILATED_GROUPS[group][1]
    span = BAND * dil
    nblk = q_ref.shape[0] // span
    c = pl.program_id(1)

    row = lax.broadcasted_iota(jnp.int32, (2 * BAND, 2 * BAND), 0)
    kj = lax.broadcasted_iota(jnp.int32, (2 * BAND, 2 * BAND), 1)
    delta = BAND + (row & (BAND - 1)) - kj
    head0 = group * HEADS_PER_GROUP
    slope_a = jnp.where(c == 0, _alibi_slope(head0), _alibi_slope(head0 + 2)).astype(F32)
    slope_b = jnp.where(c == 0, _alibi_slope(head0 + 1), _alibi_slope(head0 + 3)).astype(F32)
    slope = jnp.where(row < BAND, slope_a, slope_b)
    band = (delta >= 0) & (delta <= BAND)
    bias = jnp.where(band, -slope * (delta * dil).astype(F32), -jnp.inf)
    bias_first = jnp.where((kj >= BAND) | has_prev, bias, -jnp.inf)
    first_head = _head_lane_mask(0)

    def rows(ref, start):
        return ref[pl.ds(start, BAND), :] if dil == 1 else ref[pl.ds(start, BAND, stride=dil), :]

    for r in range(dil):
        k_prev = rows(kp_ref, r).astype(BF16)
        v_prev = rows(vp_ref, r).astype(BF16)
        for b in range(nblk):
            start = b * span + r
            q = rows(q_ref, start) * (HEAD_DIM ** -0.5)
            k_cur = rows(kc_ref, start).astype(BF16)
            v_cur = rows(vc_ref, start).astype(BF16)
            k = jnp.concatenate([k_prev, k_cur], axis=0)
            v = jnp.concatenate([v_prev, v_cur], axis=0)
            q2 = jnp.concatenate([jnp.where(first_head, q, 0.0), jnp.where(first_head, 0.0, q)], axis=0)
            s = lax.dot_general(q2.astype(BF16), k, _NT, preferred_element_type=F32)
            s = s + (bias_first if b == 0 else bias)
            m = jnp.max(s, axis=-1, keepdims=True)
            p = jnp.exp(s - m)
            den = jnp.sum(p, axis=-1, keepdims=True)
            o2 = jnp.dot(p.astype(BF16), v, preferred_element_type=F32) / den
            l2 = m + jnp.log(den)
            o = jnp.where(first_head, o2[:BAND], o2[BAND:])
            lse = jnp.where(first_head, l2[:BAND], l2[BAND:])
            if dil == 1:
                o_ref[pl.ds(start, BAND), :] = o
                l_ref[pl.ds(start, BAND), :] = lse
            else:
                o_ref[pl.ds(start, BAND, stride=dil), :] = o
                l_ref[pl.ds(start, BAND, stride=dil), :] = lse
            k_prev, v_prev = k_cur, v_cur


MERGED_BLOCK = N_GROUPS * LANES


def _attn_prompt(qkv, *, seq):
    rows = qkv.shape[0]
    tt = ATTN_TILE_TOKENS
    specs = []
    for g, (_, dil) in enumerate(DILATED_GROUPS):
        span = BAND * dil
        nblk = tt // span
        cur = lambda col: pl.BlockSpec((tt, LANES), lambda i, c: (i, col + c))
        prev = lambda col, span=span, nblk=nblk: pl.BlockSpec(
            (span, LANES), lambda i, c: (jnp.maximum(i * nblk - 1, 0), col + c))
        qcol = g * GROUP_WIDTH // LANES
        kcol = (ATTN_WIDTH + g * GROUP_WIDTH) // LANES
        vcol = (2 * ATTN_WIDTH + g * GROUP_WIDTH) // LANES
        specs += [cur(qcol), prev(kcol), cur(kcol), prev(vcol), cur(vcol)]
    return pl.pallas_call(
        functools.partial(_attn_kernel, tiles_per_seq=seq // tt),
        grid=(rows // tt, GROUP_WIDTH // LANES),
        in_specs=specs,
        out_specs=pl.BlockSpec((tt, MERGED_BLOCK), lambda i, c: (i, c)),
        out_shape=jax.ShapeDtypeStruct((rows, ATTN_WIDTH), BF16),
        scratch_shapes=[pltpu.VMEM((N_GROUPS, tt, LANES), F32), pltpu.VMEM((N_GROUPS, tt, LANES), F32)],
        compiler_params=_params(("parallel", "parallel")),
        name="attn_prompt",
    )(*([qkv] * len(specs)))


def _attn_sample_kernel(qkv_ref, c0_ref, c1_ref, c2_ref, om_ref, n0_ref, n1_ref, n2_ref, *, t_len):
    cache_refs = (c0_ref, c1_ref, c2_ref)
    n_refs = (n0_ref, n1_ref, n2_ref)
    outs, lses = [], []
    gw = GROUP_WIDTH
    nrow = HEADS_PER_GROUP * t_len
    row_head = lax.broadcasted_iota(jnp.int32, (nrow, gw), 0) // t_len
    col_head = lax.broadcasted_iota(jnp.int32, (nrow, gw), 1) // HEAD_DIM
    out_col_head = lax.broadcasted_iota(jnp.int32, (t_len, gw), 1) // HEAD_DIM

    def gather_heads(x):
        acc = jnp.zeros((t_len, gw), F32)
        for h in range(HEADS_PER_GROUP):
            acc = acc + jnp.where(out_col_head == h, x[h * t_len:(h + 1) * t_len, :], 0.0)
        return acc

    for g, (win, dil) in enumerate(DILATED_GROUPS):
        cache_ref = cache_refs[g]
        wb = cache_ref.shape[1]
        q0 = g * gw
        k0 = ATTN_WIDTH + g * gw
        v0 = 2 * ATTN_WIDTH + g * gw

        kv_new = jnp.concatenate([qkv_ref[:, k0:k0 + gw], qkv_ref[:, v0:v0 + gw]], axis=1)
        kv_pad = jnp.concatenate([kv_new, jnp.zeros((LANES - t_len, 2 * gw), F32)], axis=0)
        new_t = kv_pad.T

        q = qkv_ref[:, q0:q0 + gw] * (HEAD_DIM ** -0.5)
        q_bd = jnp.where(row_head == col_head, jnp.concatenate([q] * HEADS_PER_GROUP, axis=0), 0.0).astype(BF16)

        def bias(ncol, base):
            r = lax.broadcasted_iota(jnp.int32, (nrow, ncol), 0)
            w = lax.broadcasted_iota(jnp.int32, (nrow, ncol), 1)
            dist = base + (r % t_len) - w
            ok = (dist >= 0) & (dist <= win) & ((dist & (dil - 1)) == 0)
            slope = jnp.zeros((nrow, ncol), F32)
            for h in range(HEADS_PER_GROUP):
                slope = jnp.where(r // t_len == h, _alibi_slope(g * HEADS_PER_GROUP + h), slope)
            return jnp.where(ok, -slope * dist.astype(F32), -jnp.inf)

        s1 = jnp.dot(q_bd, cache_ref[0:gw, :].astype(BF16), preferred_element_type=F32) + bias(wb, wb)
        pad_ok = lax.broadcasted_iota(jnp.int32, (nrow, LANES), 1) < t_len
        s2 = jnp.dot(q_bd, new_t[0:gw, :].astype(BF16), preferred_element_type=F32)
        s2 = jnp.where(pad_ok, s2 + bias(LANES, 0), -jnp.inf)
        m = jnp.maximum(jnp.max(s1, axis=-1, keepdims=True), jnp.max(s2, axis=-1, keepdims=True))
        p1 = jnp.exp(s1 - m)
        p2 = jnp.exp(s2 - m)
        den = jnp.sum(p1, axis=-1, keepdims=True) + jnp.sum(p2, axis=-1, keepdims=True)
        o = (lax.dot_general(p1.astype(BF16), cache_ref[gw:2 * gw, :].astype(BF16), _NT, preferred_element_type=F32)
             + lax.dot_general(p2.astype(BF16), new_t[gw:2 * gw, :].astype(BF16), _NT, preferred_element_type=F32))
        outs.append(gather_heads(o / den))
        lses.append(gather_heads(jnp.broadcast_to(m + jnp.log(den), (nrow, gw))))

        ext = jnp.concatenate([cache_ref[...], new_t], axis=1)
        n_refs[g][...] = pltpu.roll(ext, wb + LANES - t_len, axis=1)[:, 0:wb]

    merged = _merge_groups(outs, lses)
    for g in range(N_GROUPS):
        for c in range(gw // LANES):
            col = c * MERGED_BLOCK + g * LANES
            om_ref[:, col:col + LANES] = merged[g][:, c * LANES:(c + 1) * LANES]


SSM_CHUNK = 16
N_LAM_POWERS = SSM_CHUNK + 1


def _ssm_prep_kernel(a_re_ref, a_im_ref, logdt_ref, bre_ref, bim_ref, pre_ref, pim_ref, bbre_ref, bbim_ref):
    a_re = a_re_ref[...]
    a_im = a_im_ref[...]
    dt = jnp.exp(logdt_ref[...])
    mag = jnp.exp(a_re * dt)
    ang = a_im * dt
    lam_re = mag * jnp.cos(ang)
    lam_im = mag * jnp.sin(ang)
    den = a_re * a_re + a_im * a_im
    num_re = lam_re - 1.0
    coef_re = ((num_re * a_re + lam_im * a_im) / den)[None]
    coef_im = ((lam_im * a_re - num_re * a_im) / den)[None]
    b_re = bre_ref[...]
    b_im = bim_ref[...]
    bbre_ref[...] = coef_re * b_re - coef_im * b_im
    bbim_ref[...] = coef_re * b_im + coef_im * b_re
    p_re, p_im = jnp.ones_like(lam_re), jnp.zeros_like(lam_im)
    for m in range(N_LAM_POWERS):
        pre_ref[m] = p_re
        pim_ref[m] = p_im
        p_re, p_im = p_re * lam_re - p_im * lam_im, p_re * lam_im + p_im * lam_re


def _ssm_prep(a_re, a_im, log_dt, b_re_t, b_im_t):
    g, n = a_re.shape
    pw = jax.ShapeDtypeStruct((N_LAM_POWERS, g, n), F32)
    bb = jax.ShapeDtypeStruct(b_re_t.shape, F32)
    return pl.pallas_call(
        _ssm_prep_kernel,
        out_shape=[pw, pw, bb, bb],
        name="ssm_prep",
    )(a_re, a_im, log_dt.reshape(g, 1), b_re_t, b_im_t)


SCAN_BLOCKS_PER_STEP = 4


def _ssm_rows_kernel(*refs, t_len, nseq):
    for k in range(refs[0].shape[0]):
        _ssm_rows_block(*[r.at[k] for r in refs], t_len=t_len, nseq=nseq)


def _expand_groups(x):
    width = x.shape[1]
    tiled = jnp.concatenate([x] * GROUPS_PER_BLOCK, axis=0)
    row_grp = lax.broadcasted_iota(jnp.int32, tiled.shape, 0) // SSM_CH
    lane_grp = lax.broadcasted_iota(jnp.int32, tiled.shape, 1) // (width // GROUPS_PER_BLOCK)
    return jnp.where(row_grp == lane_grp, tiled, 0.0)


def _ssm_rows_block(u_ref, bb_ref, ct_ref, pw_ref, d_ref, h0_ref, y_ref, fs_ref,
                    bu_scr, xre_scr, xim_scr, ud_scr, *, t_len, nseq):
    sl = STATE_LANES
    for j in range(t_len):
        ud_scr[j * nseq:(j + 1) * nseq, :] = u_ref[pl.ds(j, nseq, stride=t_len), :].astype(BF16)
    bb = jnp.concatenate([_expand_groups(bb_ref[:, 0:sl]), _expand_groups(bb_ref[:, sl:2 * sl])], axis=1)
    bu_scr[...] = jnp.dot(ud_scr[...], bb.astype(BF16), preferred_element_type=F32)

    lam_re = jnp.broadcast_to(pw_ref[1:2, :], (nseq, sl))
    lam_im = jnp.broadcast_to(pw_ref[N_LAM_POWERS + 1:N_LAM_POWERS + 2, :], (nseq, sl))
    x_re = h0_ref[:, 0:sl]
    x_im = h0_ref[:, sl:2 * sl]
    for j in range(t_len):
        b_re = bu_scr[j * nseq:(j + 1) * nseq, 0:sl]
        b_im = bu_scr[j * nseq:(j + 1) * nseq, sl:2 * sl]
        x_re, x_im = (lam_re * x_re - lam_im * x_im + b_re, lam_re * x_im + lam_im * x_re + b_im)
        xre_scr[j * nseq:(j + 1) * nseq, :] = x_re.astype(BF16)
        xim_scr[j * nseq:(j + 1) * nseq, :] = x_im.astype(BF16)
    fs_ref[:, 0:sl] = x_re
    fs_ref[:, sl:2 * sl] = x_im

    ct_re = _expand_groups(ct_ref[:, 0:sl]).astype(BF16)
    ct_im = _expand_groups(ct_ref[:, sl:2 * sl]).astype(BF16)
    y = (lax.dot_general(xre_scr[...], ct_re, _NT, preferred_element_type=F32)
         - lax.dot_general(xim_scr[...], ct_im, _NT, preferred_element_type=F32))
    d = d_ref[...]
    for j in range(t_len):
        y_ref[pl.ds(j, nseq, stride=t_len), :] = (
            y[j * nseq:(j + 1) * nseq, :] + d * u_ref[pl.ds(j, nseq, stride=t_len), :])


def _ssm_scan_rows(u, tabs, h0, *, t_len, nseq):
    nblk, rows, _ = u.shape
    kb = SCAN_BLOCKS_PER_STEP
    sl2 = 2 * STATE_LANES
    blk = lambda arr: pl.BlockSpec((kb,) + arr.shape[1:], lambda k: (k,) + (0,) * (arr.ndim - 1))
    ins = (u, tabs["bb"], tabs["ct"], tabs["pw_rows"], tabs["dvec"], h0)
    return pl.pallas_call(
        functools.partial(_ssm_rows_kernel, t_len=t_len, nseq=nseq),
        grid=(nblk // kb,),
        in_specs=[blk(a) for a in ins],
        out_specs=[blk(u), blk(h0)],
        out_shape=[jax.ShapeDtypeStruct(u.shape, F32), jax.ShapeDtypeStruct(h0.shape, F32)],
        scratch_shapes=[
            pltpu.VMEM((kb, rows, sl2), F32),
            pltpu.VMEM((kb, rows, STATE_LANES), BF16),
            pltpu.VMEM((kb, rows, STATE_LANES), BF16),
            pltpu.VMEM((kb, rows, LANES), BF16),
        ],
        compiler_params=_params(("parallel",)),
        name="ssm_scan_rows",
    )(*ins)


GATHER_STRIDE = 4
assert SSM_CHUNK == GATHER_STRIDE * GATHER_STRIDE


def _ssm_seq_kernel(u_ref, bb_ref, ct_ref, pw_ref, d_ref, h0_ref, y_ref, fs_ref,
                    g_scr, kt_scr, clt_scr, ucat_scr, e_scr, hp_scr, s1_scr, s2_scr, y1_scr):
    L = SSM_CHUNK
    sl = STATE_LANES
    seq = u_ref.shape[0]
    nch = seq // L

    @pl.when(pl.program_id(1) == 0)
    def _build_tables():
        lam_re = pw_ref[1:2, :]
        lam_im = pw_ref[N_LAM_POWERS + 1:N_LAM_POWERS + 2, :]
        q_re = ct_ref[:, 0:sl]
        q_im = ct_ref[:, sl:2 * sl]
        ccs_t = jnp.concatenate([_expand_groups(q_re), -_expand_groups(q_im)], axis=1).astype(BF16)
        kt_scr[...] = jnp.zeros(kt_scr.shape, BF16)
        g_re = bb_ref[:, 0:sl]
        g_im = bb_ref[:, sl:2 * sl]
        for m in range(L):
            j = L - 1 - m
            g_m = jnp.concatenate([_expand_groups(g_re), _expand_groups(g_im)], axis=1).astype(BF16)
            g_scr[j * LANES:(j + 1) * LANES, :] = g_m
            k_m = lax.dot_general(g_m, ccs_t, _NT, preferred_element_type=F32).astype(BF16)
            for jj in range(L - m):
                i = jj + m
                kt_scr[jj * LANES:(jj + 1) * LANES, i * LANES:(i + 1) * LANES] = k_m
            g_re, g_im = g_re * lam_re - g_im * lam_im, g_re * lam_im + g_im * lam_re
            q_re, q_im = q_re * lam_re - q_im * lam_im, q_re * lam_im + q_im * lam_re
            clt_scr[m * LANES:(m + 1) * LANES, 0:sl] = _expand_groups(q_re).astype(BF16)
            clt_scr[m * LANES:(m + 1) * LANES, sl:2 * sl] = _expand_groups(-q_im).astype(BF16)

    qs = GATHER_STRIDE
    for b in range(qs):
        s1_scr[b] = u_ref[pl.ds(b, seq // qs, stride=qs), :]
    for j in range(L):
        a, b = divmod(j, qs)
        ucat_scr[:, j * LANES:(j + 1) * LANES] = s1_scr[b, pl.ds(a, nch, stride=qs), :].astype(BF16)
    e_scr[...] = jnp.dot(ucat_scr[...], g_scr[...], preferred_element_type=F32)

    lc_re = pw_ref[L:L + 1, :]
    lc_im = pw_ref[N_LAM_POWERS + L:N_LAM_POWERS + L + 1, :]

    pair = 2 * LANES
    for p in range(L // 2):
        y1_scr[:, p * pair:(p + 1) * pair] = jnp.dot(
            ucat_scr[:, 0:(p + 1) * pair], kt_scr[0:(p + 1) * pair, p * pair:(p + 1) * pair],
            preferred_element_type=F32)

    h_re, h_im = h0_ref[:, 0:sl], h0_ref[:, sl:2 * sl]
    for c in range(nch):
        hp_scr[c:c + 1, 0:sl] = h_re
        hp_scr[c:c + 1, sl:2 * sl] = h_im
        e_re = e_scr[c:c + 1, 0:sl]
        e_im = e_scr[c:c + 1, sl:2 * sl]
        h_re, h_im = (lc_re * h_re - lc_im * h_im + e_re, lc_re * h_im + lc_im * h_re + e_im)
    fs_ref[:, 0:sl] = h_re
    fs_ref[:, sl:2 * sl] = h_im

    h_prev = hp_scr[...].astype(BF16)
    for p in range(L // 2):
        y2 = y1_scr[:, p * pair:(p + 1) * pair] + lax.dot_general(
            h_prev, clt_scr[p * pair:(p + 1) * pair, :], _NT, preferred_element_type=F32)
        for half in range(2):
            a, b = divmod(2 * p + half, qs)
            s2_scr[b, pl.ds(a, nch, stride=qs), :] = y2[:, half * LANES:(half + 1) * LANES]
    d = d_ref[...]
    for b in range(qs):
        y_ref[pl.ds(b, seq // qs, stride=qs), :] = s2_scr[b] + d * s1_scr[b]


def _ssm_scan_seq(u, tabs, h0, *, nseq, seq):
    nblk = u.shape[0]
    L = SSM_CHUNK
    nch = seq // L
    sl2 = 2 * STATE_LANES
    blk = lambda arr: pl.BlockSpec((None,) + arr.shape[1:], lambda k, b: (k,) + (0,) * (arr.ndim - 1))
    row_spec = pl.BlockSpec((None, seq, LANES), lambda k, b: (k, b, 0))
    st_spec = pl.BlockSpec((None, None, 1, sl2), lambda k, b: (k, b, 0, 0))
    return pl.pallas_call(
        _ssm_seq_kernel,
        grid=(nblk, nseq),
        in_specs=[row_spec, blk(tabs["bb"]), blk(tabs["ct"]), blk(tabs["pw_rows"]),
                  blk(tabs["dvec"]), st_spec],
        out_specs=[row_spec, st_spec],
        out_shape=[jax.ShapeDtypeStruct(u.shape, F32), jax.ShapeDtypeStruct(h0.shape, F32)],
        scratch_shapes=[
            pltpu.VMEM((L * LANES, sl2), BF16),
            pltpu.VMEM((L * LANES, L * LANES), BF16),
            pltpu.VMEM((L * LANES, sl2), BF16),
            pltpu.VMEM((nch, L * LANES), BF16),
            pltpu.VMEM((nch, sl2), F32),
            pltpu.VMEM((nch, sl2), F32),
            pltpu.VMEM((GATHER_STRIDE, seq // GATHER_STRIDE, LANES), F32),
            pltpu.VMEM((GATHER_STRIDE, seq // GATHER_STRIDE, LANES), F32),
            pltpu.VMEM((nch, L * LANES), F32),
        ],
        compiler_params=_params(("arbitrary", "arbitrary")),
        name="ssm_scan_seq",
    )(u, tabs["bb"], tabs["ct"], tabs["pw_rows"], tabs["dvec"], h0)


def _lane_blocks(x):
    lead = x.shape[:-2]
    x = x.reshape(lead + (SSM_LANE_BLOCKS, STATE_LANES))
    return jnp.moveaxis(x, -2, 0)


def _ssm_tables(P, j):
    b_re_t = jnp.transpose(P["ssm_b_re"][j], (2, 0, 1))
    b_im_t = jnp.transpose(P["ssm_b_im"][j], (2, 0, 1))
    p_re, p_im, bb_re, bb_im = _ssm_prep(P["ssm_a_re"][j], P["ssm_a_im"][j], P["ssm_log_dt"][j], b_re_t, b_im_t)
    bb = jnp.concatenate([_lane_blocks(bb_re), _lane_blocks(bb_im)], axis=2)
    ct = jnp.concatenate([_lane_blocks(jnp.swapaxes(P["ssm_c_re"][j], 0, 1)),
                          _lane_blocks(jnp.swapaxes(P["ssm_c_im"][j], 0, 1))], axis=2)
    pw_rows = jnp.concatenate([_lane_blocks(p_re), _lane_blocks(p_im)], axis=1)
    dvec = P["ssm_d"][j].reshape(SSM_LANE_BLOCKS, 1, LANES)
    return dict(bb=bb, ct=ct, pw_rows=pw_rows, dvec=dvec)


def _ssm_scan(u, tabs, state, *, nseq, seq):
    if state is None:
        h0 = jnp.zeros((SSM_LANE_BLOCKS, nseq, 1, 2 * STATE_LANES), F32)
        y, fs = _ssm_scan_seq(u, tabs, h0, nseq=nseq, seq=seq)
        fs = fs.reshape(SSM_LANE_BLOCKS, nseq, 2 * STATE_LANES)
    else:
        y, fs = _ssm_scan_rows(u, tabs, _state_to_blocks(state), t_len=seq, nseq=nseq)
    return y, _state_from_blocks(fs)


def _state_from_blocks(fs):
    nblk, nseq, _ = fs.shape
    fs = fs.reshape(nblk, nseq, 2, GROUPS_PER_BLOCK, SSM_STATE)
    fs = jnp.transpose(fs, (1, 0, 3, 4, 2))
    return fs.reshape(nseq, SSM_GROUPS, SSM_STATE, 2)


def _state_to_blocks(state):
    nseq = state.shape[0]
    s = state.reshape(nseq, SSM_LANE_BLOCKS, GROUPS_PER_BLOCK, SSM_STATE, 2)
    return jnp.transpose(s, (1, 0, 4, 2, 3)).reshape(SSM_LANE_BLOCKS, nseq, 2 * STATE_LANES)


def _to_window_minor(c):
    return jnp.transpose(c[0], (0, 2, 3, 4, 1)).reshape(c.shape[1], 2 * GROUP_WIDTH, c.shape[2])


def _from_window_minor(c):
    c = c.reshape(c.shape[0], 2, HEADS_PER_GROUP, HEAD_DIM, c.shape[2])
    return jnp.transpose(c, (0, 4, 1, 2, 3))[None]


def _kv_tail_kernel(k_ref, v_ref, o_ref):
    o_ref[0:GROUP_WIDTH, :] = k_ref[...].T
    o_ref[GROUP_WIDTH:2 * GROUP_WIDTH, :] = v_ref[...].T


def _kv_tail(qkv, *, group, nseq, seq):
    wb = min(DILATED_GROUPS[group][0], seq)
    kcol = (ATTN_WIDTH + group * GROUP_WIDTH) // GROUP_WIDTH
    vcol = (2 * ATTN_WIDTH + group * GROUP_WIDTH) // GROUP_WIDTH
    tail = lambda col: pl.BlockSpec((wb, GROUP_WIDTH), lambda b: ((b + 1) * (seq // wb) - 1, col))
    return pl.pallas_call(
        _kv_tail_kernel,
        grid=(nseq,),
        in_specs=[tail(kcol), tail(vcol)],
        out_specs=pl.BlockSpec((None, 2 * GROUP_WIDTH, wb), lambda b: (b, 0, 0)),
        out_shape=jax.ShapeDtypeStruct((nseq, 2 * GROUP_WIDTH, wb), F32),
        compiler_params=_params(("parallel",)),
        name="kv_tail",
    )(qkv, qkv)


SIDE_CAST_SLICES = dict(qkv=32, o=8, ssm_w_in=32, ssm_w_glu=32, ssm_w_out=32)


def _trunks(xs, caches, ssm_state, W, *, nseqs, seqs):
    ffn_w = [W[name + "_f32"][0, 0].astype(BF16) for name in FFN_WEIGHTS]

    small = ("qkv", "o", "ssm_w_in", "ssm_w_glu", "ssm_w_out")
    st = _stage(xs, W, ffn_w, 0, 0, cast_next=(0, 1),
                side_casts=[(W[name + "_f32"], SIDE_CAST_SLICES[name]) for name in small])
    W = dict(W, **{name: w[None] for name, w in zip(small, st["side"])})
    hs = st["x"]
    qkvs, merged_s, new_caches = _qkv_and_sample_attn(hs, W["norms"], 0, W["qkv"], caches, t_len=seqs[1])
    merged_p = _attn_prompt(qkvs[0], seq=seqs[0])
    st = _stage(hs, W, st["next_w"], 0, 1, pre="attn", pre_ins=(merged_p, merged_s), cast_next=(1, 0))

    st = _stage(st["x"], W, st["next_w"], 1, 0, post="proj_slabs", cast_next=(1, 1))
    us = st["proj"]
    y_p, state_p = _ssm_scan(us[0], W["ssm"], None, nseq=nseqs[0], seq=seqs[0])
    y_s, state_s = _ssm_scan(us[1], W["ssm"], ssm_state, nseq=nseqs[1], seq=seqs[1])
    outs = _stage(st["x"], W, st["next_w"], 1, 1, pre="ssm", pre_ins=(y_p, y_s), post="final")["x"]
    return outs, qkvs[0], new_caches, (state_p, state_s)


def kernel(x_prompt, x_sample, cache_kv_g0, cache_kv_g1, cache_kv_g2, state_ssm, norm_g, final_norm_g,
           ffn_w_gate, ffn_w_up, ffn_w_down, attn_w_qkv, attn_w_o, ssm_w_in, ssm_a_re, ssm_a_im, ssm_log_dt,
           ssm_b_re, ssm_b_im, ssm_c_re, ssm_c_im, ssm_d, ssm_w_glu, ssm_b_glu, ssm_w_out):
    batch, seq, _ = x_prompt.shape
    dec_batch, dec_seq, _ = x_sample.shape
    P = dict(ssm_a_re=ssm_a_re, ssm_a_im=ssm_a_im, ssm_log_dt=ssm_log_dt, ssm_b_re=ssm_b_re,
             ssm_b_im=ssm_b_im, ssm_c_re=ssm_c_re, ssm_c_im=ssm_c_im, ssm_d=ssm_d)
    W = dict(
        norms=norm_g.reshape(norm_g.shape[0], norm_g.shape[1], 1, D_MODEL),
        final_norm=final_norm_g.reshape(1, D_MODEL),
        gate_f32=ffn_w_gate, up_f32=ffn_w_up, down_f32=ffn_w_down,
        qkv_f32=attn_w_qkv[0],
        o_f32=jnp.swapaxes(attn_w_o[0].reshape(N_GROUPS, GROUP_WIDTH // LANES, LANES, D_MODEL), 0, 1)
        .reshape(ATTN_WIDTH, D_MODEL),
        ssm_w_in_f32=ssm_w_in[0], ssm_w_glu_f32=ssm_w_glu[0], ssm_w_out_f32=ssm_w_out[0],
        ssm_b_glu=ssm_b_glu.reshape(ssm_b_glu.shape[0], 1, D_MODEL),
        ssm=_ssm_tables(P, 0),
    )

    xs = (x_prompt.reshape(batch * seq, D_MODEL), x_sample.reshape(dec_batch * dec_seq, D_MODEL))
    caches = [_to_window_minor(c) for c in (cache_kv_g0, cache_kv_g1, cache_kv_g2)]
    (y_p, y_s), qkv_p, new_caches, (ssm_p, ssm_s) = _trunks(
        xs, caches, state_ssm[0], W, nseqs=(batch, dec_batch), seqs=(seq, dec_seq))
    kv_prompt = [_from_window_minor(_kv_tail(qkv_p, group=g, nseq=batch, seq=seq)) for g in range(N_GROUPS)]
    kv_sample = [_from_window_minor(nc) for nc in new_caches]

    return (y_p.reshape(batch, seq, D_MODEL), y_s.reshape(dec_batch, dec_seq, D_MODEL),
            kv_prompt[0], kv_prompt[1], kv_prompt[2], ssm_p[None],
            kv_sample[0], kv_sample[1], kv_sample[2], ssm_s[None])
```

```python
import functools

import jax
import jax.numpy as jnp
from jax import lax
from jax.experimental import pallas as pl
from jax.experimental.pallas import tpu as pltpu

F32 = jnp.float32
BF16 = jnp.bfloat16

D_MODEL = 1024
D_FF = 2816
HEAD_DIM = 64
HEADS_PER_GROUP = 4
DILATED_GROUPS = ((128, 1), (512, 4), (2048, 16))
N_GROUPS = len(DILATED_GROUPS)
N_HEADS = N_GROUPS * HEADS_PER_GROUP
GROUP_WIDTH = HEADS_PER_GROUP * HEAD_DIM
ATTN_WIDTH = N_HEADS * HEAD_DIM
BAND = 128
SSM_CH = 16
SSM_GROUPS = D_MODEL // SSM_CH
SSM_STATE = 64
NORM_EPS = 1e-6

LANES = 128
V7X_VMEM_BYTES = 64 * 1024 * 1024
VMEM_LIMIT = V7X_VMEM_BYTES - 8 * 1024 * 1024

SSM_LANE_BLOCKS = D_MODEL // LANES
GROUPS_PER_BLOCK = LANES // SSM_CH
STATE_LANES = GROUPS_PER_BLOCK * SSM_STATE

assert all(w // d == BAND for w, d in DILATED_GROUPS)
assert all(d & (d - 1) == 0 for _, d in DILATED_GROUPS)


def _params(sem, vmem=VMEM_LIMIT):
    return pltpu.CompilerParams(dimension_semantics=sem, vmem_limit_bytes=vmem)


def _const_spec(shape):
    nd = len(shape)
    return pl.BlockSpec(shape, lambda *_: (0,) * nd, pipeline_mode=pl.Buffered(1))


def _rms(x, g):
    return x * lax.rsqrt(jnp.mean(x * x, axis=-1, keepdims=True) + NORM_EPS) * g


MXU_TILE = 256
FFN_CHUNK_BOUNDS = (0, 6 * MXU_TILE, D_FF)


N_STREAMS = 2
STAGE_TM = 512


FFN_WEIGHTS = ("gate", "up", "down")


def _stage_kernel(*refs, pre, post, n_casts):
    refs = list(refs)
    take = lambda n: [refs.pop(0) for _ in range(n)]
    x_refs = take(N_STREAMS)
    if pre == "attn":
        om_refs = take(N_STREAMS)
        (wo_ref,) = take(1)
    elif pre == "ssm":
        y_refs = take(N_STREAMS)
        wglu_ref, bglu_ref, wout_ref = take(3)
    g_ref, wg_ref, wu_ref, wd_ref = take(4)
    if post == "final":
        (gf_ref,) = take(1)
    elif post == "proj_slabs":
        gp_ref, wp_ref = take(2)
    cast_srcs = take(n_casts)
    o_refs = take(N_STREAMS)
    if post == "proj_slabs":
        p_refs = take(N_STREAMS)
    for src, dst in zip(cast_srcs, take(n_casts)):
        dst[...] = src[...].astype(BF16)

    def run(s):
        x = x_refs[s][...]
        if pre == "attn":
            x = x + jnp.dot(om_refs[s][...].astype(BF16), wo_ref[...], preferred_element_type=F32)
        elif pre == "ssm":
            y = jnp.concatenate([y_refs[s][k] for k in range(y_refs[s].shape[0])], axis=1)
            g = jax.nn.gelu(y, approximate=True)
            gate = jnp.dot(g.astype(BF16), wglu_ref[...], preferred_element_type=F32) + bglu_ref[...]
            z = (g * jax.nn.sigmoid(gate)).astype(BF16)
            x = x + jnp.dot(z, wout_ref[...], preferred_element_type=F32)

        xb = _rms(x, g_ref[...]).astype(BF16)
        acc = None
        for lo, hi in zip(FFN_CHUNK_BOUNDS[:-1], FFN_CHUNK_BOUNDS[1:]):
            gate = jnp.dot(xb, wg_ref[:, lo:hi], preferred_element_type=F32)
            up = jnp.dot(xb, wu_ref[:, lo:hi], preferred_element_type=F32)
            hmid = (gate * jax.nn.sigmoid(gate) * up).astype(BF16)
            part = jnp.dot(hmid, wd_ref[lo:hi, :], preferred_element_type=F32)
            acc = part if acc is None else acc + part
        out = x + 0.5 * acc

        if post == "final":
            out = _rms(out, gf_ref[...])
        o_refs[s][...] = out
        if post == "proj_slabs":
            proj = jnp.dot(_rms(out, gp_ref[...]).astype(BF16), wp_ref[...], preferred_element_type=F32)
            for k in range(p_refs[s].shape[0]):
                p_refs[s][k] = proj[:, k * LANES:(k + 1) * LANES]

    run(0)
    pl.when(pl.program_id(0) == 0)(lambda: run(1))


def _pick_spec(arr, *idx):
    tail = arr.shape[len(idx):]
    index = tuple(idx) + (0,) * len(tail)
    return pl.BlockSpec((None,) * len(idx) + tail, lambda *_: index, pipeline_mode=pl.Buffered(1))


def _whole_spec(shape):
    nd = len(shape)
    return pl.BlockSpec(shape, lambda *_: (0,) * nd)


def _row_specs(xs):
    return [pl.BlockSpec((STAGE_TM, xs[0].shape[1]), lambda i: (i, 0)), _whole_spec(xs[1].shape)]


def _slab_specs(ys):
    return [pl.BlockSpec((SSM_LANE_BLOCKS, STAGE_TM, LANES), lambda i: (0, i, 0)), _whole_spec(ys[1].shape)]


def _stage(xs, W, ffn_w, layer, half, *, pre=None, pre_ins=None, post=None, cast_next=None, side_casts=()):
    steps = xs[0].shape[0] // STAGE_TM
    ins, specs = list(xs), _row_specs(xs)
    if pre == "attn":
        ins += list(pre_ins) + [W["o"]]
        specs += _row_specs(pre_ins) + [_pick_spec(W["o"], 0)]
    elif pre == "ssm":
        ins += list(pre_ins) + [W["ssm_w_glu"], W["ssm_b_glu"], W["ssm_w_out"]]
        specs += _slab_specs(pre_ins) + [_pick_spec(W["ssm_w_glu"], 0), _pick_spec(W["ssm_b_glu"], 0),
                                         _pick_spec(W["ssm_w_out"], 0)]
    ins += [W["norms"]] + list(ffn_w)
    specs += [_pick_spec(W["norms"], layer, 2 * half)] + [_pick_spec(w) for w in ffn_w]
    out_specs = _row_specs(xs)
    out_shape = [jax.ShapeDtypeStruct(x.shape, F32) for x in xs]
    if post == "final":
        ins += [W["final_norm"]]
        specs += [_const_spec((1, D_MODEL))]
    elif post == "proj_slabs":
        ins += [W["norms"], W["ssm_w_in"]]
        specs += [_pick_spec(W["norms"], layer, 1), _pick_spec(W["ssm_w_in"], 0)]
    cast_out_specs, cast_out_shape = [], []
    if cast_next:
        nl, nh = cast_next
        rows_per_step = D_MODEL // steps
        steps_per_col = steps // (D_MODEL // LANES)
        assert rows_per_step * steps == D_MODEL and steps_per_col * (D_MODEL // LANES) == steps
        ins += [W[name + "_f32"] for name in FFN_WEIGHTS]
        specs += [pl.BlockSpec((None, None, rows_per_step, D_FF), lambda i: (nl, nh, i, 0))] * 2
        specs += [pl.BlockSpec((None, None, D_FF, LANES), lambda i: (nl, nh, 0, i // steps_per_col))]
        cast_out_specs += [pl.BlockSpec((rows_per_step, D_FF), lambda i: (i, 0))] * 2
        cast_out_specs += [pl.BlockSpec((D_FF, LANES), lambda i: (0, i // steps_per_col))]
        cast_out_shape += [jax.ShapeDtypeStruct(s, BF16)
                           for s in ((D_MODEL, D_FF), (D_MODEL, D_FF), (D_FF, D_MODEL))]
    for w, nslices in side_casts:
        rows, cols = w.shape
        assert rows % nslices == 0 and steps % nslices == 0
        spec = pl.BlockSpec((rows // nslices, cols), lambda i, hold=steps // nslices: (i // hold, 0))
        ins.append(w)
        specs.append(spec)
        cast_out_specs.append(spec)
        cast_out_shape.append(jax.ShapeDtypeStruct(w.shape, BF16))
    if post == "proj_slabs":
        slabs = [jax.ShapeDtypeStruct((SSM_LANE_BLOCKS, x.shape[0], LANES), F32) for x in xs]
        out_specs += _slab_specs(slabs)
        out_shape += slabs
    out_specs += cast_out_specs
    out_shape += cast_out_shape
    outs = list(pl.pallas_call(
        functools.partial(_stage_kernel, pre=pre, post=post, n_casts=len(cast_out_specs)),
        grid=(steps,),
        in_specs=specs,
        out_specs=out_specs,
        out_shape=out_shape,
        compiler_params=_params(("arbitrary",)),
        name="stage_" + "_".join(s for s in (pre, "ffn", post) if s),
    )(*ins))
    take = lambda n: [outs.pop(0) for _ in range(n)]
    return dict(x=take(N_STREAMS), proj=take(N_STREAMS) if post == "proj_slabs" else None,
                next_w=take(len(FFN_WEIGHTS)) if cast_next else None, side=take(len(side_casts)))


def _tail_tiles(group, seq):
    wb = min(DILATED_GROUPS[group][0], seq)
    return max(wb // STAGE_TM, 1), min(wb, STAGE_TM)


def _qkv_kernel(x0_ref, x1_ref, g_ref, w_ref, c0_ref, c1_ref, c2_ref,
                o0_ref, o1_ref, om_ref, n0_ref, n1_ref, n2_ref, t0_ref, t1_ref, t2_ref, *, t_len, seq):
    def run(x_ref, o_ref):
        xb = _rms(x_ref[...], g_ref[...]).astype(BF16)
        o_ref[...] = jnp.dot(xb, w_ref[...], preferred_element_type=F32)

    i = pl.program_id(0)
    run(x0_ref, o0_ref)
    tiles_per_seq = seq // STAGE_TM
    for g, tail_ref in enumerate((t0_ref, t1_ref, t2_ref)):
        ntail, nrows = _tail_tiles(g, seq)

        @pl.when(i % tiles_per_seq >= tiles_per_seq - ntail)
        def _(g=g, tail_ref=tail_ref, nrows=nrows):
            k0 = ATTN_WIDTH + g * GROUP_WIDTH
            v0 = 2 * ATTN_WIDTH + g * GROUP_WIDTH
            tail_ref[0:GROUP_WIDTH, :] = o0_ref[STAGE_TM - nrows:STAGE_TM, k0:k0 + GROUP_WIDTH].T
            tail_ref[GROUP_WIDTH:2 * GROUP_WIDTH, :] = o0_ref[STAGE_TM - nrows:STAGE_TM, v0:v0 + GROUP_WIDTH].T

    pl.when(i == 0)(lambda: run(x1_ref, o1_ref))
    qkv_seq = o1_ref.at[pl.ds(pl.multiple_of(i * t_len, t_len), t_len), :]
    _attn_sample_kernel(qkv_seq, c0_ref, c1_ref, c2_ref, om_ref, n0_ref, n1_ref, n2_ref, t_len=t_len)


def _qkv_and_sample_attn(xs, norms, layer, w, caches, *, t_len, nseq0, seq0):
    n = w.shape[-1]
    steps = xs[0].shape[0] // STAGE_TM
    nseq = caches[0].shape[0]
    assert steps == nseq and xs[1].shape[0] == nseq * t_len
    tiles_per_seq = seq0 // STAGE_TM
    qkv_shapes = [jax.ShapeDtypeStruct((x.shape[0], n), F32) for x in xs]
    cache_specs = [pl.BlockSpec((None,) + c.shape[1:], lambda b: (b, 0, 0)) for c in caches]
    tail_specs, tail_shapes = [], []
    for g in range(N_GROUPS):
        ntail, nrows = _tail_tiles(g, seq0)
        tail_specs.append(pl.BlockSpec(
            (None, 2 * GROUP_WIDTH, nrows),
            lambda i, first=tiles_per_seq - ntail: (i // tiles_per_seq, 0,
                                                    jnp.maximum(i % tiles_per_seq - first, 0))))
        tail_shapes.append(jax.ShapeDtypeStruct((nseq0, 2 * GROUP_WIDTH, ntail * nrows), F32))
    outs = pl.pallas_call(
        functools.partial(_qkv_kernel, t_len=t_len, seq=seq0),
        grid=(steps,),
        in_specs=_row_specs(xs) + [_pick_spec(norms, layer, 1), _pick_spec(w, 0)] + cache_specs,
        out_specs=(_row_specs(qkv_shapes) + [pl.BlockSpec((t_len, ATTN_WIDTH), lambda b: (b, 0))]
                   + cache_specs + tail_specs),
        out_shape=qkv_shapes
        + [jax.ShapeDtypeStruct((xs[1].shape[0], ATTN_WIDTH), F32)]
        + [jax.ShapeDtypeStruct(c.shape, F32) for c in caches] + tail_shapes,
        compiler_params=_params(("arbitrary",)),
        name="qkv_proj_sample_attn",
    )(*xs, norms, w, *caches)
    outs = list(outs)
    return (outs[:N_STREAMS], outs[N_STREAMS], outs[N_STREAMS + 1:N_STREAMS + 1 + N_GROUPS],
            outs[N_STREAMS + 1 + N_GROUPS:])


def _alibi_slope(head):
    return 2.0 ** (-8.0 * (head + 1) / N_HEADS)


def _head_lane_mask(hh):
    lane = lax.broadcasted_iota(jnp.int32, (1, LANES), 1)
    return (lane >= hh * HEAD_DIM) & (lane < (hh + 1) * HEAD_DIM)


ATTN_TILE_TOKENS = 2048
_NT = (((1,), (1,)), ((), ()))


def _merge_groups(outs, lses):
    m = jnp.maximum(jnp.maximum(lses[0], lses[1]), lses[2])
    es = [jnp.exp(l - m) for l in lses]
    tot = es[0] + es[1] + es[2]
    return [o * (e / tot) for o, e in zip(outs, es)]


def _attn_kernel(*refs, tiles_per_seq):
    om_ref, o_scr, l_scr = refs[5 * N_GROUPS:]
    has_prev = (pl.program_id(0) % tiles_per_seq) != 0
    for g in range(N_GROUPS):
        _attn_group(*refs[5 * g:5 * g + 5], o_scr.at[g], l_scr.at[g], group=g, has_prev=has_prev)
    merged = _merge_groups([o_scr[g] for g in range(N_GROUPS)], [l_scr[g] for g in range(N_GROUPS)])
    for g in range(N_GROUPS):
        om_ref[:, g * LANES:(g + 1) * LANES] = merged[g].astype(BF16)


def _attn_group(q_ref, kp_ref, kc_ref, vp_ref, vc_ref, o_ref, l_ref, *, group, has_prev):
    dil = DILATED_GROUPS[group][1]
    span = BAND * dil
    nblk = q_ref.shape[0] // span
    c = pl.program_id(1)

    row = lax.broadcasted_iota(jnp.int32, (2 * BAND, 2 * BAND), 0)
    kj = lax.broadcasted_iota(jnp.int32, (2 * BAND, 2 * BAND), 1)
    delta = BAND + (row & (BAND - 1)) - kj
    head0 = group * HEADS_PER_GROUP
    slope_a = jnp.where(c == 0, _alibi_slope(head0), _alibi_slope(head0 + 2)).astype(F32)
    slope_b = jnp.where(c == 0, _alibi_slope(head0 + 1), _alibi_slope(head0 + 3)).astype(F32)
    slope = jnp.where(row < BAND, slope_a, slope_b)
    band = (delta >= 0) & (delta <= BAND)
    bias = jnp.where(band, -slope * (delta * dil).astype(F32), -jnp.inf)
    bias_first = jnp.where((kj >= BAND) | has_prev, bias, -jnp.inf)
    first_head = _head_lane_mask(0)

    def rows(ref, start):
        return ref[pl.ds(start, BAND), :] if dil == 1 else ref[pl.ds(start, BAND, stride=dil), :]

    for r in range(dil):
        k_prev = rows(kp_ref, r).astype(BF16)
        v_prev = rows(vp_ref, r).astype(BF16)
        for b in range(nblk):
            start = b * span + r
            q = rows(q_ref, start) * (HEAD_DIM ** -0.5)
            k_cur = rows(kc_ref, start).astype(BF16)
            v_cur = rows(vc_ref, start).astype(BF16)
            k = jnp.concatenate([k_prev, k_cur], axis=0)
            v = jnp.concatenate([v_prev, v_cur], axis=0)
            q2 = jnp.concatenate([jnp.where(first_head, q, 0.0), jnp.where(first_head, 0.0, q)], axis=0)
            s = lax.dot_general(q2.astype(BF16), k, _NT, preferred_element_type=F32)
            s = s + (bias_first if b == 0 else bias)
            m = jnp.max(s, axis=-1, keepdims=True)
            p = jnp.exp(s - m)
            den = jnp.sum(p, axis=-1, keepdims=True)
            o2 = jnp.dot(p.astype(BF16), v, preferred_element_type=F32) / den
            l2 = m + jnp.log(den)
            o = jnp.where(first_head, o2[:BAND], o2[BAND:])
            lse = jnp.where(first_head, l2[:BAND], l2[BAND:])
            if dil == 1:
                o_ref[pl.ds(start, BAND), :] = o
                l_ref[pl.ds(start, BAND), :] = lse
            else:
                o_ref[pl.ds(start, BAND, stride=dil), :] = o
                l_ref[pl.ds(start, BAND, stride=dil), :] = lse
            k_prev, v_prev = k_cur, v_cur


MERGED_BLOCK = N_GROUPS * LANES


def _attn_prompt(qkv, *, seq):
    rows = qkv.shape[0]
    tt = ATTN_TILE_TOKENS
    specs = []
    for g, (_, dil) in enumerate(DILATED_GROUPS):
        span = BAND * dil
        nblk = tt // span
        cur = lambda col: pl.BlockSpec((tt, LANES), lambda i, c: (i, col + c))
        prev = lambda col, span=span, nblk=nblk: pl.BlockSpec(
            (span, LANES), lambda i, c: (jnp.maximum(i * nblk - 1, 0), col + c))
        qcol = g * GROUP_WIDTH // LANES
        kcol = (ATTN_WIDTH + g * GROUP_WIDTH) // LANES
        vcol = (2 * ATTN_WIDTH + g * GROUP_WIDTH) // LANES
        specs += [cur(qcol), prev(kcol), cur(kcol), prev(vcol), cur(vcol)]
    return pl.pallas_call(
        functools.partial(_attn_kernel, tiles_per_seq=seq // tt),
        grid=(rows // tt, GROUP_WIDTH // LANES),
        in_specs=specs,
        out_specs=pl.BlockSpec((tt, MERGED_BLOCK), lambda i, c: (i, c)),
        out_shape=jax.ShapeDtypeStruct((rows, ATTN_WIDTH), BF16),
        scratch_shapes=[pltpu.VMEM((N_GROUPS, tt, LANES), F32), pltpu.VMEM((N_GROUPS, tt, LANES), F32)],
        compiler_params=_params(("parallel", "parallel")),
        name="attn_prompt",
    )(*([qkv] * len(specs)))


def _attn_sample_kernel(qkv_ref, c0_ref, c1_ref, c2_ref, om_ref, n0_ref, n1_ref, n2_ref, *, t_len):
    cache_refs = (c0_ref, c1_ref, c2_ref)
    n_refs = (n0_ref, n1_ref, n2_ref)
    outs, lses = [], []
    gw = GROUP_WIDTH
    nrow = HEADS_PER_GROUP * t_len
    row_head = lax.broadcasted_iota(jnp.int32, (nrow, gw), 0) // t_len
    col_head = lax.broadcasted_iota(jnp.int32, (nrow, gw), 1) // HEAD_DIM
    out_col_head = lax.broadcasted_iota(jnp.int32, (t_len, gw), 1) // HEAD_DIM

    def gather_heads(x):
        acc = jnp.zeros((t_len, gw), F32)
        for h in range(HEADS_PER_GROUP):
            acc = acc + jnp.where(out_col_head == h, x[h * t_len:(h + 1) * t_len, :], 0.0)
        return acc

    for g, (win, dil) in enumerate(DILATED_GROUPS):
        cache_ref = cache_refs[g]
        wb = cache_ref.shape[1]
        q0 = g * gw
        k0 = ATTN_WIDTH + g * gw
        v0 = 2 * ATTN_WIDTH + g * gw

        kv_new = jnp.concatenate([qkv_ref[:, k0:k0 + gw], qkv_ref[:, v0:v0 + gw]], axis=1)
        kv_pad = jnp.concatenate([kv_new, jnp.zeros((LANES - t_len, 2 * gw), F32)], axis=0)
        new_t = kv_pad.T

        q = qkv_ref[:, q0:q0 + gw] * (HEAD_DIM ** -0.5)
        q_bd = jnp.where(row_head == col_head, jnp.concatenate([q] * HEADS_PER_GROUP, axis=0), 0.0).astype(BF16)

        def bias(ncol, base):
            r = lax.broadcasted_iota(jnp.int32, (nrow, ncol), 0)
            w = lax.broadcasted_iota(jnp.int32, (nrow, ncol), 1)
            dist = base + (r % t_len) - w
            ok = (dist >= 0) & (dist <= win) & ((dist & (dil - 1)) == 0)
            slope = jnp.zeros((nrow, ncol), F32)
            for h in range(HEADS_PER_GROUP):
                slope = jnp.where(r // t_len == h, _alibi_slope(g * HEADS_PER_GROUP + h), slope)
            return jnp.where(ok, -slope * dist.astype(F32), -jnp.inf)

        s1 = jnp.dot(q_bd, cache_ref[0:gw, :].astype(BF16), preferred_element_type=F32) + bias(wb, wb)
        pad_ok = lax.broadcasted_iota(jnp.int32, (nrow, LANES), 1) < t_len
        s2 = jnp.dot(q_bd, new_t[0:gw, :].astype(BF16), preferred_element_type=F32)
        s2 = jnp.where(pad_ok, s2 + bias(LANES, 0), -jnp.inf)
        m = jnp.maximum(jnp.max(s1, axis=-1, keepdims=True), jnp.max(s2, axis=-1, keepdims=True))
        p1 = jnp.exp(s1 - m)
        p2 = jnp.exp(s2 - m)
        den = jnp.sum(p1, axis=-1, keepdims=True) + jnp.sum(p2, axis=-1, keepdims=True)
        o = (lax.dot_general(p1.astype(BF16), cache_ref[gw:2 * gw, :].astype(BF16), _NT, preferred_element_type=F32)
             + lax.dot_general(p2.astype(BF16), new_t[gw:2 * gw, :].astype(BF16), _NT, preferred_element_type=F32))
        outs.append(gather_heads(o / den))
        lses.append(gather_heads(jnp.broadcast_to(m + jnp.log(den), (nrow, gw))))

        ext = jnp.concatenate([cache_ref[...], new_t], axis=1)
        n_refs[g][...] = pltpu.roll(ext, wb + LANES - t_len, axis=1)[:, 0:wb]

    merged = _merge_groups(outs, lses)
    for g in range(N_GROUPS):
        for c in range(gw // LANES):
            col = c * MERGED_BLOCK + g * LANES
            om_ref[:, col:col + LANES] = merged[g][:, c * LANES:(c + 1) * LANES]


SSM_CHUNK = 16
N_LAM_POWERS = SSM_CHUNK + 1


def _ssm_prep_kernel(a_re_ref, a_im_ref, logdt_ref, bre_ref, bim_ref, pre_ref, pim_ref, bbre_ref, bbim_ref):
    a_re = a_re_ref[...]
    a_im = a_im_ref[...]
    dt = jnp.exp(logdt_ref[...])
    mag = jnp.exp(a_re * dt)
    ang = a_im * dt
    lam_re = mag * jnp.cos(ang)
    lam_im = mag * jnp.sin(ang)
    den = a_re * a_re + a_im * a_im
    num_re = lam_re - 1.0
    coef_re = ((num_re * a_re + lam_im * a_im) / den)[None]
    coef_im = ((lam_im * a_re - num_re * a_im) / den)[None]
    b_re = bre_ref[...]
    b_im = bim_ref[...]
    bbre_ref[...] = coef_re * b_re - coef_im * b_im
    bbim_ref[...] = coef_re * b_im + coef_im * b_re
    p_re, p_im = jnp.ones_like(lam_re), jnp.zeros_like(lam_im)
    for m in range(N_LAM_POWERS):
        pre_ref[m] = p_re
        pim_ref[m] = p_im
        p_re, p_im = p_re * lam_re - p_im * lam_im, p_re * lam_im + p_im * lam_re


def _ssm_prep(a_re, a_im, log_dt, b_re_t, b_im_t):
    g, n = a_re.shape
    pw = jax.ShapeDtypeStruct((N_LAM_POWERS, g, n), F32)
    bb = jax.ShapeDtypeStruct(b_re_t.shape, F32)
    return pl.pallas_call(
        _ssm_prep_kernel,
        out_shape=[pw, pw, bb, bb],
        name="ssm_prep",
    )(a_re, a_im, log_dt.reshape(g, 1), b_re_t, b_im_t)


SCAN_BLOCKS_PER_STEP = 4


def _ssm_rows_kernel(*refs, t_len, nseq):
    for k in range(refs[0].shape[0]):
        _ssm_rows_block(*[r.at[k] for r in refs], t_len=t_len, nseq=nseq)


def _expand_groups(x):
    width = x.shape[1]
    tiled = jnp.concatenate([x] * GROUPS_PER_BLOCK, axis=0)
    row_grp = lax.broadcasted_iota(jnp.int32, tiled.shape, 0) // SSM_CH
    lane_grp = lax.broadcasted_iota(jnp.int32, tiled.shape, 1) // (width // GROUPS_PER_BLOCK)
    return jnp.where(row_grp == lane_grp, tiled, 0.0)


def _ssm_rows_block(u_ref, bb_ref, ct_ref, pw_ref, d_ref, h0_ref, y_ref, fs_ref,
                    bu_scr, xre_scr, xim_scr, ud_scr, *, t_len, nseq):
    sl = STATE_LANES
    for j in range(t_len):
        ud_scr[j * nseq:(j + 1) * nseq, :] = u_ref[pl.ds(j, nseq, stride=t_len), :].astype(BF16)
    bb = jnp.concatenate([_expand_groups(bb_ref[:, 0:sl]), _expand_groups(bb_ref[:, sl:2 * sl])], axis=1)
    bu_scr[...] = jnp.dot(ud_scr[...], bb.astype(BF16), preferred_element_type=F32)

    lam_re = jnp.broadcast_to(pw_ref[1:2, :], (nseq, sl))
    lam_im = jnp.broadcast_to(pw_ref[N_LAM_POWERS + 1:N_LAM_POWERS + 2, :], (nseq, sl))
    x_re = h0_ref[:, 0:sl]
    x_im = h0_ref[:, sl:2 * sl]
    for j in range(t_len):
        b_re = bu_scr[j * nseq:(j + 1) * nseq, 0:sl]
        b_im = bu_scr[j * nseq:(j + 1) * nseq, sl:2 * sl]
        x_re, x_im = (lam_re * x_re - lam_im * x_im + b_re, lam_re * x_im + lam_im * x_re + b_im)
        xre_scr[j * nseq:(j + 1) * nseq, :] = x_re.astype(BF16)
        xim_scr[j * nseq:(j + 1) * nseq, :] = x_im.astype(BF16)
    fs_ref[:, 0:sl] = x_re
    fs_ref[:, sl:2 * sl] = x_im

    ct_re = _expand_groups(ct_ref[:, 0:sl]).astype(BF16)
    ct_im = _expand_groups(ct_ref[:, sl:2 * sl]).astype(BF16)
    y = (lax.dot_general(xre_scr[...], ct_re, _NT, preferred_element_type=F32)
         - lax.dot_general(xim_scr[...], ct_im, _NT, preferred_element_type=F32))
    d = d_ref[...]
    for j in range(t_len):
        y_ref[pl.ds(j, nseq, stride=t_len), :] = (
            y[j * nseq:(j + 1) * nseq, :] + d * u_ref[pl.ds(j, nseq, stride=t_len), :])


def _ssm_scan_rows(u, tabs, h0, *, t_len, nseq):
    nblk, rows, _ = u.shape
    kb = SCAN_BLOCKS_PER_STEP
    sl2 = 2 * STATE_LANES
    blk = lambda arr: pl.BlockSpec((kb,) + arr.shape[1:], lambda k: (k,) + (0,) * (arr.ndim - 1))
    ins = (u, tabs["bb"], tabs["ct"], tabs["pw_rows"], tabs["dvec"], h0)
    return pl.pallas_call(
        functools.partial(_ssm_rows_kernel, t_len=t_len, nseq=nseq),
        grid=(nblk // kb,),
        in_specs=[blk(a) for a in ins],
        out_specs=[blk(u), blk(h0)],
        out_shape=[jax.ShapeDtypeStruct(u.shape, F32), jax.ShapeDtypeStruct(h0.shape, F32)],
        scratch_shapes=[
            pltpu.VMEM((kb, rows, sl2), F32),
            pltpu.VMEM((kb, rows, STATE_LANES), BF16),
            pltpu.VMEM((kb, rows, STATE_LANES), BF16),
            pltpu.VMEM((kb, rows, LANES), BF16),
        ],
        compiler_params=_params(("parallel",)),
        name="ssm_scan_rows",
    )(*ins)


GATHER_STRIDE = 4
assert SSM_CHUNK == GATHER_STRIDE * GATHER_STRIDE


def _ssm_seq_kernel(u_ref, bb_ref, ct_ref, pw_ref, d_ref, h0_ref, y_ref, fs_ref,
                    g_scr, kt_scr, clt_scr, ucat_scr, e_scr, hp_scr, s1_scr, s2_scr, y1_scr):
    L = SSM_CHUNK
    sl = STATE_LANES
    seq = u_ref.shape[0]
    nch = seq // L

    @pl.when(pl.program_id(1) == 0)
    def _build_tables():
        lam_re = pw_ref[1:2, :]
        lam_im = pw_ref[N_LAM_POWERS + 1:N_LAM_POWERS + 2, :]
        q_re = ct_ref[:, 0:sl]
        q_im = ct_ref[:, sl:2 * sl]
        ccs_t = jnp.concatenate([_expand_groups(q_re), -_expand_groups(q_im)], axis=1).astype(BF16)
        kt_scr[...] = jnp.zeros(kt_scr.shape, BF16)
        g_re = bb_ref[:, 0:sl]
        g_im = bb_ref[:, sl:2 * sl]
        for m in range(L):
            j = L - 1 - m
            g_m = jnp.concatenate([_expand_groups(g_re), _expand_groups(g_im)], axis=1).astype(BF16)
            g_scr[j * LANES:(j + 1) * LANES, :] = g_m
            k_m = lax.dot_general(g_m, ccs_t, _NT, preferred_element_type=F32).astype(BF16)
            for jj in range(L - m):
                i = jj + m
                kt_scr[jj * LANES:(jj + 1) * LANES, i * LANES:(i + 1) * LANES] = k_m
            g_re, g_im = g_re * lam_re - g_im * lam_im, g_re * lam_im + g_im * lam_re
            q_re, q_im = q_re * lam_re - q_im * lam_im, q_re * lam_im + q_im * lam_re
            clt_scr[m * LANES:(m + 1) * LANES, 0:sl] = _expand_groups(q_re).astype(BF16)
            clt_scr[m * LANES:(m + 1) * LANES, sl:2 * sl] = _expand_groups(-q_im).astype(BF16)

    qs = GATHER_STRIDE
    for b in range(qs):
        s1_scr[b] = u_ref[pl.ds(b, seq // qs, stride=qs), :]
    for j in range(L):
        a, b = divmod(j, qs)
        ucat_scr[:, j * LANES:(j + 1) * LANES] = s1_scr[b, pl.ds(a, nch, stride=qs), :].astype(BF16)
    e_scr[...] = jnp.dot(ucat_scr[...], g_scr[...], preferred_element_type=F32)

    lc_re = pw_ref[L:L + 1, :]
    lc_im = pw_ref[N_LAM_POWERS + L:N_LAM_POWERS + L + 1, :]

    pair = 2 * LANES
    for p in range(L // 2):
        y1_scr[:, p * pair:(p + 1) * pair] = jnp.dot(
            ucat_scr[:, 0:(p + 1) * pair], kt_scr[0:(p + 1) * pair, p * pair:(p + 1) * pair],
            preferred_element_type=F32)

    h_re, h_im = h0_ref[:, 0:sl], h0_ref[:, sl:2 * sl]
    for c in range(nch):
        hp_scr[c:c + 1, 0:sl] = h_re
        hp_scr[c:c + 1, sl:2 * sl] = h_im
        e_re = e_scr[c:c + 1, 0:sl]
        e_im = e_scr[c:c + 1, sl:2 * sl]
        h_re, h_im = (lc_re * h_re - lc_im * h_im + e_re, lc_re * h_im + lc_im * h_re + e_im)
    fs_ref[:, 0:sl] = h_re
    fs_ref[:, sl:2 * sl] = h_im

    h_prev = hp_scr[...].astype(BF16)
    for p in range(L // 2):
        y2 = y1_scr[:, p * pair:(p + 1) * pair] + lax.dot_general(
            h_prev, clt_scr[p * pair:(p + 1) * pair, :], _NT, preferred_element_type=F32)
        for half in range(2):
            a, b = divmod(2 * p + half, qs)
            s2_scr[b, pl.ds(a, nch, stride=qs), :] = y2[:, half * LANES:(half + 1) * LANES]
    d = d_ref[...]
    for b in range(qs):
        y_ref[pl.ds(b, seq // qs, stride=qs), :] = s2_scr[b] + d * s1_scr[b]


def _ssm_scan_seq(u, tabs, h0, *, nseq, seq):
    nblk = u.shape[0]
    L = SSM_CHUNK
    nch = seq // L
    sl2 = 2 * STATE_LANES
    blk = lambda arr: pl.BlockSpec((None,) + arr.shape[1:], lambda k, b: (k,) + (0,) * (arr.ndim - 1))
    row_spec = pl.BlockSpec((None, seq, LANES), lambda k, b: (k, b, 0))
    st_spec = pl.BlockSpec((None, None, 1, sl2), lambda k, b: (k, b, 0, 0))
    return pl.pallas_call(
        _ssm_seq_kernel,
        grid=(nblk, nseq),
        in_specs=[row_spec, blk(tabs["bb"]), blk(tabs["ct"]), blk(tabs["pw_rows"]),
                  blk(tabs["dvec"]), st_spec],
        out_specs=[row_spec, st_spec],
        out_shape=[jax.ShapeDtypeStruct(u.shape, F32), jax.ShapeDtypeStruct(h0.shape, F32)],
        scratch_shapes=[
            pltpu.VMEM((L * LANES, sl2), BF16),
            pltpu.VMEM((L * LANES, L * LANES), BF16),
            pltpu.VMEM((L * LANES, sl2), BF16),
            pltpu.VMEM((nch, L * LANES), BF16),
            pltpu.VMEM((nch, sl2), F32),
            pltpu.VMEM((nch, sl2), F32),
            pltpu.VMEM((GATHER_STRIDE, seq // GATHER_STRIDE, LANES), F32),
            pltpu.VMEM((GATHER_STRIDE, seq // GATHER_STRIDE, LANES), F32),
            pltpu.VMEM((nch, L * LANES), F32),
        ],
        compiler_params=_params(("arbitrary", "arbitrary")),
        name="ssm_scan_seq",
    )(u, tabs["bb"], tabs["ct"], tabs["pw_rows"], tabs["dvec"], h0)


def _lane_blocks(x):
    lead = x.shape[:-2]
    x = x.reshape(lead + (SSM_LANE_BLOCKS, STATE_LANES))
    return jnp.moveaxis(x, -2, 0)


def _ssm_tables(P, j):
    b_re_t = jnp.transpose(P["ssm_b_re"][j], (2, 0, 1))
    b_im_t = jnp.transpose(P["ssm_b_im"][j], (2, 0, 1))
    p_re, p_im, bb_re, bb_im = _ssm_prep(P["ssm_a_re"][j], P["ssm_a_im"][j], P["ssm_log_dt"][j], b_re_t, b_im_t)
    bb = jnp.concatenate([_lane_blocks(bb_re), _lane_blocks(bb_im)], axis=2)
    ct = jnp.concatenate([_lane_blocks(jnp.swapaxes(P["ssm_c_re"][j], 0, 1)),
                          _lane_blocks(jnp.swapaxes(P["ssm_c_im"][j], 0, 1))], axis=2)
    pw_rows = jnp.concatenate([_lane_blocks(p_re), _lane_blocks(p_im)], axis=1)
    dvec = P["ssm_d"][j].reshape(SSM_LANE_BLOCKS, 1, LANES)
    return dict(bb=bb, ct=ct, pw_rows=pw_rows, dvec=dvec)


def _ssm_scan(u, tabs, state, *, nseq, seq):
    if state is None:
        h0 = jnp.zeros((SSM_LANE_BLOCKS, nseq, 1, 2 * STATE_LANES), F32)
        y, fs = _ssm_scan_seq(u, tabs, h0, nseq=nseq, seq=seq)
        fs = fs.reshape(SSM_LANE_BLOCKS, nseq, 2 * STATE_LANES)
    else:
        y, fs = _ssm_scan_rows(u, tabs, _state_to_blocks(state), t_len=seq, nseq=nseq)
    return y, _state_from_blocks(fs)


def _state_from_blocks(fs):
    nblk, nseq, _ = fs.shape
    fs = fs.reshape(nblk, nseq, 2, GROUPS_PER_BLOCK, SSM_STATE)
    fs = jnp.transpose(fs, (1, 0, 3, 4, 2))
    return fs.reshape(nseq, SSM_GROUPS, SSM_STATE, 2)


def _state_to_blocks(state):
    nseq = state.shape[0]
    s = state.reshape(nseq, SSM_LANE_BLOCKS, GROUPS_PER_BLOCK, SSM_STATE, 2)
    return jnp.transpose(s, (1, 0, 4, 2, 3)).reshape(SSM_LANE_BLOCKS, nseq, 2 * STATE_LANES)


def _to_window_minor(c):
    return jnp.transpose(c[0], (0, 2, 3, 4, 1)).reshape(c.shape[1], 2 * GROUP_WIDTH, c.shape[2])


def _from_window_minor(c):
    c = c.reshape(c.shape[0], 2, HEADS_PER_GROUP, HEAD_DIM, c.shape[2])
    return jnp.transpose(c, (0, 4, 1, 2, 3))[None]


SIDE_CAST_SLICES = dict(qkv=32, o=8, ssm_w_in=32, ssm_w_glu=32, ssm_w_out=32)


def _trunks(xs, caches, ssm_state, W, *, nseqs, seqs):
    ffn_w = [W[name + "_f32"][0, 0].astype(BF16) for name in FFN_WEIGHTS]

    small = ("qkv", "o", "ssm_w_in", "ssm_w_glu", "ssm_w_out")
    st = _stage(xs, W, ffn_w, 0, 0, cast_next=(0, 1),
                side_casts=[(W[name + "_f32"], SIDE_CAST_SLICES[name]) for name in small])
    W = dict(W, **{name: w[None] for name, w in zip(small, st["side"])})
    hs = st["x"]
    qkvs, merged_s, new_caches, tails = _qkv_and_sample_attn(
        hs, W["norms"], 0, W["qkv"], caches, t_len=seqs[1], nseq0=nseqs[0], seq0=seqs[0])
    merged_p = _attn_prompt(qkvs[0], seq=seqs[0])
    st = _stage(hs, W, st["next_w"], 0, 1, pre="attn", pre_ins=(merged_p, merged_s), cast_next=(1, 0))

    st = _stage(st["x"], W, st["next_w"], 1, 0, post="proj_slabs", cast_next=(1, 1))
    us = st["proj"]
    y_p, state_p = _ssm_scan(us[0], W["ssm"], None, nseq=nseqs[0], seq=seqs[0])
    y_s, state_s = _ssm_scan(us[1], W["ssm"], ssm_state, nseq=nseqs[1], seq=seqs[1])
    outs = _stage(st["x"], W, st["next_w"], 1, 1, pre="ssm", pre_ins=(y_p, y_s), post="final")["x"]
    return outs, tails, new_caches, (state_p, state_s)


def kernel(x_prompt, x_sample, cache_kv_g0, cache_kv_g1, cache_kv_g2, state_ssm, norm_g, final_norm_g,
           ffn_w_gate, ffn_w_up, ffn_w_down, attn_w_qkv, attn_w_o, ssm_w_in, ssm_a_re, ssm_a_im, ssm_log_dt,
           ssm_b_re, ssm_b_im, ssm_c_re, ssm_c_im, ssm_d, ssm_w_glu, ssm_b_glu, ssm_w_out):
    batch, seq, _ = x_prompt.shape
    dec_batch, dec_seq, _ = x_sample.shape
    P = dict(ssm_a_re=ssm_a_re, ssm_a_im=ssm_a_im, ssm_log_dt=ssm_log_dt, ssm_b_re=ssm_b_re,
             ssm_b_im=ssm_b_im, ssm_c_re=ssm_c_re, ssm_c_im=ssm_c_im, ssm_d=ssm_d)
    W = dict(
        norms=norm_g.reshape(norm_g.shape[0], norm_g.shape[1], 1, D_MODEL),
        final_norm=final_norm_g.reshape(1, D_MODEL),
        gate_f32=ffn_w_gate, up_f32=ffn_w_up, down_f32=ffn_w_down,
        qkv_f32=attn_w_qkv[0],
        o_f32=jnp.swapaxes(attn_w_o[0].reshape(N_GROUPS, GROUP_WIDTH // LANES, LANES, D_MODEL), 0, 1)
        .reshape(ATTN_WIDTH, D_MODEL),
        ssm_w_in_f32=ssm_w_in[0], ssm_w_glu_f32=ssm_w_glu[0], ssm_w_out_f32=ssm_w_out[0],
        ssm_b_glu=ssm_b_glu.reshape(ssm_b_glu.shape[0], 1, D_MODEL),
        ssm=_ssm_tables(P, 0),
    )

    xs = (x_prompt.reshape(batch * seq, D_MODEL), x_sample.reshape(dec_batch * dec_seq, D_MODEL))
    caches = [_to_window_minor(c) for c in (cache_kv_g0, cache_kv_g1, cache_kv_g2)]
    (y_p, y_s), tails, new_caches, (ssm_p, ssm_s) = _trunks(
        xs, caches, state_ssm[0], W, nseqs=(batch, dec_batch), seqs=(seq, dec_seq))
    kv_prompt = [_from_window_minor(t) for t in tails]
    kv_sample = [_from_window_minor(nc) for nc in new_caches]

    return (y_p.reshape(batch, seq, D_MODEL), y_s.reshape(dec_batch, dec_seq, D_MODEL),
            kv_prompt[0], kv_prompt[1], kv_prompt[2], ssm_p[None],
            kv_sample[0], kv_sample[1], kv_sample[2], ssm_s[None])
```

```python
import functools

import jax
import jax.numpy as jnp
from jax import lax
from jax.experimental import pallas as pl
from jax.experimental.pallas import tpu as pltpu

F32 = jnp.float32
BF16 = jnp.bfloat16

D_MODEL = 1024
D_FF = 2816
HEAD_DIM = 64
HEADS_PER_GROUP = 4
DILATED_GROUPS = ((128, 1), (512, 4), (2048, 16))
N_GROUPS = len(DILATED_GROUPS)
N_HEADS = N_GROUPS * HEADS_PER_GROUP
GROUP_WIDTH = HEADS_PER_GROUP * HEAD_DIM
ATTN_WIDTH = N_HEADS * HEAD_DIM
BAND = 128
SSM_CH = 16
SSM_GROUPS = D_MODEL // SSM_CH
SSM_STATE = 64
NORM_EPS = 1e-6

LANES = 128
V7X_VMEM_BYTES = 64 * 1024 * 1024
VMEM_LIMIT = V7X_VMEM_BYTES - 8 * 1024 * 1024

SSM_LANE_BLOCKS = D_MODEL // LANES
GROUPS_PER_BLOCK = LANES // SSM_CH
STATE_LANES = GROUPS_PER_BLOCK * SSM_STATE

assert all(w // d == BAND for w, d in DILATED_GROUPS)
assert all(d & (d - 1) == 0 for _, d in DILATED_GROUPS)


def _params(sem, vmem=VMEM_LIMIT):
    return pltpu.CompilerParams(dimension_semantics=sem, vmem_limit_bytes=vmem)


def _const_spec(shape):
    nd = len(shape)
    return pl.BlockSpec(shape, lambda *_: (0,) * nd, pipeline_mode=pl.Buffered(1))


def _rms(x, g):
    return x * lax.rsqrt(jnp.mean(x * x, axis=-1, keepdims=True) + NORM_EPS) * g


MXU_TILE = 256
FFN_CHUNK_BOUNDS = (0, 6 * MXU_TILE, D_FF)


N_STREAMS = 2
STAGE_TM = 512


FFN_WEIGHTS = ("gate", "up", "down")


WEIGHT_STREAM_CHUNKS = 8


def _stream_weights_at_step0(layer_half, srcs, dsts, stagings, sem):
    layer, half = layer_half
    jobs = []
    for src, dst, stg in zip(srcs, dsts, stagings):
        rows = dst.shape[0] // WEIGHT_STREAM_CHUNKS
        for c in range(WEIGHT_STREAM_CHUNKS):
            jobs.append((src.at[layer, half, pl.ds(c * rows, rows), :], dst, stg, c * rows, rows))

    def copy(k):
        src, _, stg, _, _ = jobs[k]
        return pltpu.make_async_copy(src, stg.at[k % 2], sem.at[k % 2])

    @pl.when(pl.program_id(0) == 0)
    def _():
        copy(0).start()
        for k, (_, dst, stg, r0, rows) in enumerate(jobs):
            if k + 1 < len(jobs):
                copy(k + 1).start()
            copy(k).wait()
            dst[r0:r0 + rows, :] = stg[k % 2].astype(BF16)


def _stage_kernel(*refs, pre, post, n_casts, stream_f32=None):
    refs = list(refs)
    take = lambda n: [refs.pop(0) for _ in range(n)]
    x_refs = take(N_STREAMS)
    if pre == "attn":
        om_refs = take(N_STREAMS)
        (wo_ref,) = take(1)
    elif pre == "ssm":
        y_refs = take(N_STREAMS)
        wglu_ref, bglu_ref, wout_ref = take(3)
    g_ref, wg_ref, wu_ref, wd_ref = take(4)
    if stream_f32 is not None:
        sem = refs.pop()
        stg_down, stg_gate_up = refs.pop(), refs.pop()
        wd_scr, wu_scr, wg_scr = refs.pop(), refs.pop(), refs.pop()
        _stream_weights_at_step0(stream_f32, (wg_ref, wu_ref, wd_ref), (wg_scr, wu_scr, wd_scr),
                                 (stg_gate_up, stg_gate_up, stg_down), sem)
        wg_ref, wu_ref, wd_ref = wg_scr, wu_scr, wd_scr
    if post == "final":
        (gf_ref,) = take(1)
    elif post == "proj_slabs":
        gp_ref, wp_ref = take(2)
    cast_srcs = take(n_casts)
    o_refs = take(N_STREAMS)
    if post == "proj_slabs":
        p_refs = take(N_STREAMS)
    for src, dst in zip(cast_srcs, take(n_casts)):
        dst[...] = src[...].astype(BF16)

    def run(s):
        x = x_refs[s][...]
        if pre == "attn":
            x = x + jnp.dot(om_refs[s][...].astype(BF16), wo_ref[...], preferred_element_type=F32)
        elif pre == "ssm":
            y = jnp.concatenate([y_refs[s][k] for k in range(y_refs[s].shape[0])], axis=1)
            g = jax.nn.gelu(y, approximate=True)
            gate = jnp.dot(g.astype(BF16), wglu_ref[...], preferred_element_type=F32) + bglu_ref[...]
            z = (g * jax.nn.sigmoid(gate)).astype(BF16)
            x = x + jnp.dot(z, wout_ref[...], preferred_element_type=F32)

        xb = _rms(x, g_ref[...]).astype(BF16)
        acc = None
        for lo, hi in zip(FFN_CHUNK_BOUNDS[:-1], FFN_CHUNK_BOUNDS[1:]):
            gate = jnp.dot(xb, wg_ref[:, lo:hi], preferred_element_type=F32)
            up = jnp.dot(xb, wu_ref[:, lo:hi], preferred_element_type=F32)
            hmid = (gate * jax.nn.sigmoid(gate) * up).astype(BF16)
            part = jnp.dot(hmid, wd_ref[lo:hi, :], preferred_element_type=F32)
            acc = part if acc is None else acc + part
        out = x + 0.5 * acc

        if post == "final":
            out = _rms(out, gf_ref[...])
        o_refs[s][...] = out
        if post == "proj_slabs":
            proj = jnp.dot(_rms(out, gp_ref[...]).astype(BF16), wp_ref[...], preferred_element_type=F32)
            for k in range(p_refs[s].shape[0]):
                p_refs[s][k] = proj[:, k * LANES:(k + 1) * LANES]

    run(0)
    pl.when(pl.program_id(0) == 0)(lambda: run(1))


def _pick_spec(arr, *idx):
    tail = arr.shape[len(idx):]
    index = tuple(idx) + (0,) * len(tail)
    return pl.BlockSpec((None,) * len(idx) + tail, lambda *_: index, pipeline_mode=pl.Buffered(1))


def _whole_spec(shape):
    nd = len(shape)
    return pl.BlockSpec(shape, lambda *_: (0,) * nd)


def _row_specs(xs):
    return [pl.BlockSpec((STAGE_TM, xs[0].shape[1]), lambda i: (i, 0)), _whole_spec(xs[1].shape)]


def _slab_specs(ys):
    return [pl.BlockSpec((SSM_LANE_BLOCKS, STAGE_TM, LANES), lambda i: (0, i, 0)), _whole_spec(ys[1].shape)]


def _stage(xs, W, ffn_w, layer, half, *, pre=None, pre_ins=None, post=None, cast_next=None, side_casts=()):
    steps = xs[0].shape[0] // STAGE_TM
    ins, specs = list(xs), _row_specs(xs)
    if pre == "attn":
        ins += list(pre_ins) + [W["o"]]
        specs += _row_specs(pre_ins) + [_pick_spec(W["o"], 0)]
    elif pre == "ssm":
        ins += list(pre_ins) + [W["ssm_w_glu"], W["ssm_b_glu"], W["ssm_w_out"]]
        specs += _slab_specs(pre_ins) + [_pick_spec(W["ssm_w_glu"], 0), _pick_spec(W["ssm_b_glu"], 0),
                                         _pick_spec(W["ssm_w_out"], 0)]
    scratch = []
    if ffn_w is None:
        stream_f32 = (layer, half)
        ins += [W["norms"]] + [W[name + "_f32"] for name in FFN_WEIGHTS]
        specs += [_pick_spec(W["norms"], layer, 2 * half)] + [pl.BlockSpec(memory_space=pl.ANY)] * len(FFN_WEIGHTS)
        scratch = [pltpu.VMEM((D_MODEL, D_FF), BF16), pltpu.VMEM((D_MODEL, D_FF), BF16),
                   pltpu.VMEM((D_FF, D_MODEL), BF16),
                   pltpu.VMEM((2, D_MODEL // WEIGHT_STREAM_CHUNKS, D_FF), F32),
                   pltpu.VMEM((2, D_FF // WEIGHT_STREAM_CHUNKS, D_MODEL), F32),
                   pltpu.SemaphoreType.DMA((2,))]
    else:
        stream_f32 = None
        ins += [W["norms"]] + list(ffn_w)
        specs += [_pick_spec(W["norms"], layer, 2 * half)] + [_pick_spec(w) for w in ffn_w]
    out_specs = _row_specs(xs)
    out_shape = [jax.ShapeDtypeStruct(x.shape, F32) for x in xs]
    if post == "final":
        ins += [W["final_norm"]]
        specs += [_const_spec((1, D_MODEL))]
    elif post == "proj_slabs":
        ins += [W["norms"], W["ssm_w_in"]]
        specs += [_pick_spec(W["norms"], layer, 1), _pick_spec(W["ssm_w_in"], 0)]
    cast_out_specs, cast_out_shape = [], []
    if cast_next:
        nl, nh = cast_next
        rows_per_step = D_MODEL // steps
        steps_per_col = steps // (D_MODEL // LANES)
        assert rows_per_step * steps == D_MODEL and steps_per_col * (D_MODEL // LANES) == steps
        ins += [W[name + "_f32"] for name in FFN_WEIGHTS]
        specs += [pl.BlockSpec((None, None, rows_per_step, D_FF), lambda i: (nl, nh, i, 0))] * 2
        specs += [pl.BlockSpec((None, None, D_FF, LANES), lambda i: (nl, nh, 0, i // steps_per_col))]
        cast_out_specs += [pl.BlockSpec((rows_per_step, D_FF), lambda i: (i, 0))] * 2
        cast_out_specs += [pl.BlockSpec((D_FF, LANES), lambda i: (0, i // steps_per_col))]
        cast_out_shape += [jax.ShapeDtypeStruct(s, BF16)
                           for s in ((D_MODEL, D_FF), (D_MODEL, D_FF), (D_FF, D_MODEL))]
    for w, nslices in side_casts:
        rows, cols = w.shape
        assert rows % nslices == 0 and steps % nslices == 0
        spec = pl.BlockSpec((rows // nslices, cols), lambda i, hold=steps // nslices: (i // hold, 0))
        ins.append(w)
        specs.append(spec)
        cast_out_specs.append(spec)
        cast_out_shape.append(jax.ShapeDtypeStruct(w.shape, BF16))
    if post == "proj_slabs":
        slabs = [jax.ShapeDtypeStruct((SSM_LANE_BLOCKS, x.shape[0], LANES), F32) for x in xs]
        out_specs += _slab_specs(slabs)
        out_shape += slabs
    out_specs += cast_out_specs
    out_shape += cast_out_shape
    outs = list(pl.pallas_call(
        functools.partial(_stage_kernel, pre=pre, post=post, n_casts=len(cast_out_specs), stream_f32=stream_f32),
        grid=(steps,),
        in_specs=specs,
        out_specs=out_specs,
        out_shape=out_shape,
        scratch_shapes=scratch,
        compiler_params=_params(("arbitrary",)),
        name="stage_" + "_".join(s for s in (pre, "ffn", post) if s),
    )(*ins))
    take = lambda n: [outs.pop(0) for _ in range(n)]
    return dict(x=take(N_STREAMS), proj=take(N_STREAMS) if post == "proj_slabs" else None,
                next_w=take(len(FFN_WEIGHTS)) if cast_next else None, side=take(len(side_casts)))


def _tail_tiles(group, seq):
    wb = min(DILATED_GROUPS[group][0], seq)
    return max(wb // STAGE_TM, 1), min(wb, STAGE_TM)


def _qkv_kernel(x0_ref, x1_ref, g_ref, w_ref, c0_ref, c1_ref, c2_ref,
                o0_ref, o1_ref, om_ref, n0_ref, n1_ref, n2_ref, t0_ref, t1_ref, t2_ref, *, t_len, seq):
    def run(x_ref, o_ref):
        xb = _rms(x_ref[...], g_ref[...]).astype(BF16)
        o_ref[...] = jnp.dot(xb, w_ref[...], preferred_element_type=F32)

    i = pl.program_id(0)
    run(x0_ref, o0_ref)
    tiles_per_seq = seq // STAGE_TM
    for g, tail_ref in enumerate((t0_ref, t1_ref, t2_ref)):
        ntail, nrows = _tail_tiles(g, seq)

        @pl.when(i % tiles_per_seq >= tiles_per_seq - ntail)
        def _(g=g, tail_ref=tail_ref, nrows=nrows):
            k0 = ATTN_WIDTH + g * GROUP_WIDTH
            v0 = 2 * ATTN_WIDTH + g * GROUP_WIDTH
            tail_ref[0:GROUP_WIDTH, :] = o0_ref[STAGE_TM - nrows:STAGE_TM, k0:k0 + GROUP_WIDTH].T
            tail_ref[GROUP_WIDTH:2 * GROUP_WIDTH, :] = o0_ref[STAGE_TM - nrows:STAGE_TM, v0:v0 + GROUP_WIDTH].T

    pl.when(i == 0)(lambda: run(x1_ref, o1_ref))
    qkv_seq = o1_ref.at[pl.ds(pl.multiple_of(i * t_len, t_len), t_len), :]
    _attn_sample_kernel(qkv_seq, c0_ref, c1_ref, c2_ref, om_ref, n0_ref, n1_ref, n2_ref, t_len=t_len)


def _qkv_and_sample_attn(xs, norms, layer, w, caches, *, t_len, nseq0, seq0):
    n = w.shape[-1]
    steps = xs[0].shape[0] // STAGE_TM
    nseq = caches[0].shape[0]
    assert steps == nseq and xs[1].shape[0] == nseq * t_len
    tiles_per_seq = seq0 // STAGE_TM
    qkv_shapes = [jax.ShapeDtypeStruct((x.shape[0], n), F32) for x in xs]
    cache_specs = [pl.BlockSpec((None,) + c.shape[1:], lambda b: (b, 0, 0)) for c in caches]
    tail_specs, tail_shapes = [], []
    for g in range(N_GROUPS):
        ntail, nrows = _tail_tiles(g, seq0)
        tail_specs.append(pl.BlockSpec(
            (None, 2 * GROUP_WIDTH, nrows),
            lambda i, first=tiles_per_seq - ntail: (i // tiles_per_seq, 0,
                                                    jnp.maximum(i % tiles_per_seq - first, 0))))
        tail_shapes.append(jax.ShapeDtypeStruct((nseq0, 2 * GROUP_WIDTH, ntail * nrows), F32))
    outs = pl.pallas_call(
        functools.partial(_qkv_kernel, t_len=t_len, seq=seq0),
        grid=(steps,),
        in_specs=_row_specs(xs) + [_pick_spec(norms, layer, 1), _pick_spec(w, 0)] + cache_specs,
        out_specs=(_row_specs(qkv_shapes) + [pl.BlockSpec((t_len, ATTN_WIDTH), lambda b: (b, 0))]
                   + cache_specs + tail_specs),
        out_shape=qkv_shapes
        + [jax.ShapeDtypeStruct((xs[1].shape[0], ATTN_WIDTH), F32)]
        + [jax.ShapeDtypeStruct(c.shape, F32) for c in caches] + tail_shapes,
        compiler_params=_params(("arbitrary",)),
        name="qkv_proj_sample_attn",
    )(*xs, norms, w, *caches)
    outs = list(outs)
    return (outs[:N_STREAMS], outs[N_STREAMS], outs[N_STREAMS + 1:N_STREAMS + 1 + N_GROUPS],
            outs[N_STREAMS + 1 + N_GROUPS:])


def _alibi_slope(head):
    return 2.0 ** (-8.0 * (head + 1) / N_HEADS)


def _head_lane_mask(hh):
    lane = lax.broadcasted_iota(jnp.int32, (1, LANES), 1)
    return (lane >= hh * HEAD_DIM) & (lane < (hh + 1) * HEAD_DIM)


ATTN_TILE_TOKENS = 2048
_NT = (((1,), (1,)), ((), ()))


def _merge_groups(outs, lses):
    m = jnp.maximum(jnp.maximum(lses[0], lses[1]), lses[2])
    es = [jnp.exp(l - m) for l in lses]
    tot = es[0] + es[1] + es[2]
    return [o * (e / tot) for o, e in zip(outs, es)]


def _attn_kernel(*refs, tiles_per_seq):
    om_ref, o_scr, l_scr = refs[5 * N_GROUPS:]
    has_prev = (pl.program_id(0) % tiles_per_seq) != 0
    for g in range(N_GROUPS):
        _attn_group(*refs[5 * g:5 * g + 5], o_scr.at[g], l_scr.at[g], group=g, has_prev=has_prev)
    merged = _merge_groups([o_scr[g] for g in range(N_GROUPS)], [l_scr[g] for g in range(N_GROUPS)])
    for g in range(N_GROUPS):
        om_ref[:, g * LANES:(g + 1) * LANES] = merged[g].astype(BF16)


def _attn_group(q_ref, kp_ref, kc_ref, vp_ref, vc_ref, o_ref, l_ref, *, group, has_prev):
    dil = DILATED_GROUPS[group][1]
    span = BAND * dil
    nblk = q_ref.shape[0] // span
    c = pl.program_id(1)

    row = lax.broadcasted_iota(jnp.int32, (2 * BAND, 2 * BAND), 0)
    kj = lax.broadcasted_iota(jnp.int32, (2 * BAND, 2 * BAND), 1)
    delta = BAND + (row & (BAND - 1)) - kj
    head0 = group * HEADS_PER_GROUP
    slope_a = jnp.where(c == 0, _alibi_slope(head0), _alibi_slope(head0 + 2)).astype(F32)
    slope_b = jnp.where(c == 0, _alibi_slope(head0 + 1), _alibi_slope(head0 + 3)).astype(F32)
    slope = jnp.where(row < BAND, slope_a, slope_b)
    band = (delta >= 0) & (delta <= BAND)
    bias = jnp.where(band, -slope * (delta * dil).astype(F32), -jnp.inf)
    bias_first = jnp.where((kj >= BAND) | has_prev, bias, -jnp.inf)
    first_head = _head_lane_mask(0)

    def rows(ref, start):
        return ref[pl.ds(start, BAND), :] if dil == 1 else ref[pl.ds(start, BAND, stride=dil), :]

    for r in range(dil):
        k_prev = rows(kp_ref, r).astype(BF16)
        v_prev = rows(vp_ref, r).astype(BF16)
        for b in range(nblk):
            start = b * span + r
            q = rows(q_ref, start) * (HEAD_DIM ** -0.5)
            k_cur = rows(kc_ref, start).astype(BF16)
            v_cur = rows(vc_ref, start).astype(BF16)
            k = jnp.concatenate([k_prev, k_cur], axis=0)
            v = jnp.concatenate([v_prev, v_cur], axis=0)
            q2 = jnp.concatenate([jnp.where(first_head, q, 0.0), jnp.where(first_head, 0.0, q)], axis=0)
            s = lax.dot_general(q2.astype(BF16), k, _NT, preferred_element_type=F32)
            s = s + (bias_first if b == 0 else bias)
            m = jnp.max(s, axis=-1, keepdims=True)
            p = jnp.exp(s - m)
            den = jnp.sum(p, axis=-1, keepdims=True)
            o2 = jnp.dot(p.astype(BF16), v, preferred_element_type=F32) / den
            l2 = m + jnp.log(den)
            o = jnp.where(first_head, o2[:BAND], o2[BAND:])
            lse = jnp.where(first_head, l2[:BAND], l2[BAND:])
            if dil == 1:
                o_ref[pl.ds(start, BAND), :] = o
                l_ref[pl.ds(start, BAND), :] = lse
            else:
                o_ref[pl.ds(start, BAND, stride=dil), :] = o
                l_ref[pl.ds(start, BAND, stride=dil), :] = lse
            k_prev, v_prev = k_cur, v_cur


MERGED_BLOCK = N_GROUPS * LANES


def _attn_prompt(qkv, *, seq):
    rows = qkv.shape[0]
    tt = ATTN_TILE_TOKENS
    specs = []
    for g, (_, dil) in enumerate(DILATED_GROUPS):
        span = BAND * dil
        nblk = tt // span
        cur = lambda col: pl.BlockSpec((tt, LANES), lambda i, c: (i, col + c))
        prev = lambda col, span=span, nblk=nblk: pl.BlockSpec(
            (span, LANES), lambda i, c: (jnp.maximum(i * nblk - 1, 0), col + c))
        qcol = g * GROUP_WIDTH // LANES
        kcol = (ATTN_WIDTH + g * GROUP_WIDTH) // LANES
        vcol = (2 * ATTN_WIDTH + g * GROUP_WIDTH) // LANES
        specs += [cur(qcol), prev(kcol), cur(kcol), prev(vcol), cur(vcol)]
    return pl.pallas_call(
        functools.partial(_attn_kernel, tiles_per_seq=seq // tt),
        grid=(rows // tt, GROUP_WIDTH // LANES),
        in_specs=specs,
        out_specs=pl.BlockSpec((tt, MERGED_BLOCK), lambda i, c: (i, c)),
        out_shape=jax.ShapeDtypeStruct((rows, ATTN_WIDTH), BF16),
        scratch_shapes=[pltpu.VMEM((N_GROUPS, tt, LANES), F32), pltpu.VMEM((N_GROUPS, tt, LANES), F32)],
        compiler_params=_params(("parallel", "parallel")),
        name="attn_prompt",
    )(*([qkv] * len(specs)))


def _attn_sample_kernel(qkv_ref, c0_ref, c1_ref, c2_ref, om_ref, n0_ref, n1_ref, n2_ref, *, t_len):
    cache_refs = (c0_ref, c1_ref, c2_ref)
    n_refs = (n0_ref, n1_ref, n2_ref)
    outs, lses = [], []
    gw = GROUP_WIDTH
    nrow = HEADS_PER_GROUP * t_len
    row_head = lax.broadcasted_iota(jnp.int32, (nrow, gw), 0) // t_len
    col_head = lax.broadcasted_iota(jnp.int32, (nrow, gw), 1) // HEAD_DIM
    out_col_head = lax.broadcasted_iota(jnp.int32, (t_len, gw), 1) // HEAD_DIM

    def gather_heads(x):
        acc = jnp.zeros((t_len, gw), F32)
        for h in range(HEADS_PER_GROUP):
            acc = acc + jnp.where(out_col_head == h, x[h * t_len:(h + 1) * t_len, :], 0.0)
        return acc

    for g, (win, dil) in enumerate(DILATED_GROUPS):
        cache_ref = cache_refs[g]
        wb = cache_ref.shape[1]
        q0 = g * gw
        k0 = ATTN_WIDTH + g * gw
        v0 = 2 * ATTN_WIDTH + g * gw

        kv_new = jnp.concatenate([qkv_ref[:, k0:k0 + gw], qkv_ref[:, v0:v0 + gw]], axis=1)
        kv_pad = jnp.concatenate([kv_new, jnp.zeros((LANES - t_len, 2 * gw), F32)], axis=0)
        new_t = kv_pad.T

        q = qkv_ref[:, q0:q0 + gw] * (HEAD_DIM ** -0.5)
        q_bd = jnp.where(row_head == col_head, jnp.concatenate([q] * HEADS_PER_GROUP, axis=0), 0.0).astype(BF16)

        def bias(ncol, base):
            r = lax.broadcasted_iota(jnp.int32, (nrow, ncol), 0)
            w = lax.broadcasted_iota(jnp.int32, (nrow, ncol), 1)
            dist = base + (r % t_len) - w
            ok = (dist >= 0) & (dist <= win) & ((dist & (dil - 1)) == 0)
            slope = jnp.zeros((nrow, ncol), F32)
            for h in range(HEADS_PER_GROUP):
                slope = jnp.where(r // t_len == h, _alibi_slope(g * HEADS_PER_GROUP + h), slope)
            return jnp.where(ok, -slope * dist.astype(F32), -jnp.inf)

        s1 = jnp.dot(q_bd, cache_ref[0:gw, :].astype(BF16), preferred_element_type=F32) + bias(wb, wb)
        pad_ok = lax.broadcasted_iota(jnp.int32, (nrow, LANES), 1) < t_len
        s2 = jnp.dot(q_bd, new_t[0:gw, :].astype(BF16), preferred_element_type=F32)
        s2 = jnp.where(pad_ok, s2 + bias(LANES, 0), -jnp.inf)
        m = jnp.maximum(jnp.max(s1, axis=-1, keepdims=True), jnp.max(s2, axis=-1, keepdims=True))
        p1 = jnp.exp(s1 - m)
        p2 = jnp.exp(s2 - m)
        den = jnp.sum(p1, axis=-1, keepdims=True) + jnp.sum(p2, axis=-1, keepdims=True)
        o = (lax.dot_general(p1.astype(BF16), cache_ref[gw:2 * gw, :].astype(BF16), _NT, preferred_element_type=F32)
             + lax.dot_general(p2.astype(BF16), new_t[gw:2 * gw, :].astype(BF16), _NT, preferred_element_type=F32))
        outs.append(gather_heads(o / den))
        lses.append(gather_heads(jnp.broadcast_to(m + jnp.log(den), (nrow, gw))))

        ext = jnp.concatenate([cache_ref[...], new_t], axis=1)
        n_refs[g][...] = pltpu.roll(ext, wb + LANES - t_len, axis=1)[:, 0:wb]

    merged = _merge_groups(outs, lses)
    for g in range(N_GROUPS):
        for c in range(gw // LANES):
            col = c * MERGED_BLOCK + g * LANES
            om_ref[:, col:col + LANES] = merged[g][:, c * LANES:(c + 1) * LANES]


SSM_CHUNK = 16
N_LAM_POWERS = SSM_CHUNK + 1


def _ssm_prep_kernel(a_re_ref, a_im_ref, logdt_ref, bre_ref, bim_ref, pre_ref, pim_ref, bbre_ref, bbim_ref):
    a_re = a_re_ref[...]
    a_im = a_im_ref[...]
    dt = jnp.exp(logdt_ref[...])
    mag = jnp.exp(a_re * dt)
    ang = a_im * dt
    lam_re = mag * jnp.cos(ang)
    lam_im = mag * jnp.sin(ang)
    den = a_re * a_re + a_im * a_im
    num_re = lam_re - 1.0
    coef_re = ((num_re * a_re + lam_im * a_im) / den)[None]
    coef_im = ((lam_im * a_re - num_re * a_im) / den)[None]
    b_re = bre_ref[...]
    b_im = bim_ref[...]
    bbre_ref[...] = coef_re * b_re - coef_im * b_im
    bbim_ref[...] = coef_re * b_im + coef_im * b_re
    p_re, p_im = jnp.ones_like(lam_re), jnp.zeros_like(lam_im)
    for m in range(N_LAM_POWERS):
        pre_ref[m] = p_re
        pim_ref[m] = p_im
        p_re, p_im = p_re * lam_re - p_im * lam_im, p_re * lam_im + p_im * lam_re


def _ssm_prep(a_re, a_im, log_dt, b_re_t, b_im_t):
    g, n = a_re.shape
    pw = jax.ShapeDtypeStruct((N_LAM_POWERS, g, n), F32)
    bb = jax.ShapeDtypeStruct(b_re_t.shape, F32)
    return pl.pallas_call(
        _ssm_prep_kernel,
        out_shape=[pw, pw, bb, bb],
        name="ssm_prep",
    )(a_re, a_im, log_dt.reshape(g, 1), b_re_t, b_im_t)


SCAN_BLOCKS_PER_STEP = 4


def _ssm_rows_kernel(*refs, t_len, nseq):
    for k in range(refs[0].shape[0]):
        _ssm_rows_block(*[r.at[k] for r in refs], t_len=t_len, nseq=nseq)


def _expand_groups(x):
    width = x.shape[1]
    tiled = jnp.concatenate([x] * GROUPS_PER_BLOCK, axis=0)
    row_grp = lax.broadcasted_iota(jnp.int32, tiled.shape, 0) // SSM_CH
    lane_grp = lax.broadcasted_iota(jnp.int32, tiled.shape, 1) // (width // GROUPS_PER_BLOCK)
    return jnp.where(row_grp == lane_grp, tiled, 0.0)


def _ssm_rows_block(u_ref, bb_ref, ct_ref, pw_ref, d_ref, h0_ref, y_ref, fs_ref,
                    bu_scr, xre_scr, xim_scr, ud_scr, *, t_len, nseq):
    sl = STATE_LANES
    for j in range(t_len):
        ud_scr[j * nseq:(j + 1) * nseq, :] = u_ref[pl.ds(j, nseq, stride=t_len), :].astype(BF16)
    bb = jnp.concatenate([_expand_groups(bb_ref[:, 0:sl]), _expand_groups(bb_ref[:, sl:2 * sl])], axis=1)
    bu_scr[...] = jnp.dot(ud_scr[...], bb.astype(BF16), preferred_element_type=F32)

    lam_re = jnp.broadcast_to(pw_ref[1:2, :], (nseq, sl))
    lam_im = jnp.broadcast_to(pw_ref[N_LAM_POWERS + 1:N_LAM_POWERS + 2, :], (nseq, sl))
    x_re = h0_ref[:, 0:sl]
    x_im = h0_ref[:, sl:2 * sl]
    for j in range(t_len):
        b_re = bu_scr[j * nseq:(j + 1) * nseq, 0:sl]
        b_im = bu_scr[j * nseq:(j + 1) * nseq, sl:2 * sl]
        x_re, x_im = (lam_re * x_re - lam_im * x_im + b_re, lam_re * x_im + lam_im * x_re + b_im)
        xre_scr[j * nseq:(j + 1) * nseq, :] = x_re.astype(BF16)
        xim_scr[j * nseq:(j + 1) * nseq, :] = x_im.astype(BF16)
    fs_ref[:, 0:sl] = x_re
    fs_ref[:, sl:2 * sl] = x_im

    ct_re = _expand_groups(ct_ref[:, 0:sl]).astype(BF16)
    ct_im = _expand_groups(ct_ref[:, sl:2 * sl]).astype(BF16)
    y = (lax.dot_general(xre_scr[...], ct_re, _NT, preferred_element_type=F32)
         - lax.dot_general(xim_scr[...], ct_im, _NT, preferred_element_type=F32))
    d = d_ref[...]
    for j in range(t_len):
        y_ref[pl.ds(j, nseq, stride=t_len), :] = (
            y[j * nseq:(j + 1) * nseq, :] + d * u_ref[pl.ds(j, nseq, stride=t_len), :])


def _ssm_scan_rows(u, tabs, h0, *, t_len, nseq):
    nblk, rows, _ = u.shape
    kb = SCAN_BLOCKS_PER_STEP
    sl2 = 2 * STATE_LANES
    blk = lambda arr: pl.BlockSpec((kb,) + arr.shape[1:], lambda k: (k,) + (0,) * (arr.ndim - 1))
    ins = (u, tabs["bb"], tabs["ct"], tabs["pw_rows"], tabs["dvec"], h0)
    return pl.pallas_call(
        functools.partial(_ssm_rows_kernel, t_len=t_len, nseq=nseq),
        grid=(nblk // kb,),
        in_specs=[blk(a) for a in ins],
        out_specs=[blk(u), blk(h0)],
        out_shape=[jax.ShapeDtypeStruct(u.shape, F32), jax.ShapeDtypeStruct(h0.shape, F32)],
        scratch_shapes=[
            pltpu.VMEM((kb, rows, sl2), F32),
            pltpu.VMEM((kb, rows, STATE_LANES), BF16),
            pltpu.VMEM((kb, rows, STATE_LANES), BF16),
            pltpu.VMEM((kb, rows, LANES), BF16),
        ],
        compiler_params=_params(("parallel",)),
        name="ssm_scan_rows",
    )(*ins)


GATHER_STRIDE = 4
assert SSM_CHUNK == GATHER_STRIDE * GATHER_STRIDE


def _ssm_seq_kernel(u_ref, bb_ref, ct_ref, pw_ref, d_ref, h0_ref, y_ref, fs_ref,
                    g_scr, kt_scr, clt_scr, ucat_scr, e_scr, hp_scr, s1_scr, s2_scr, y1_scr):
    L = SSM_CHUNK
    sl = STATE_LANES
    seq = u_ref.shape[0]
    nch = seq // L

    @pl.when(pl.program_id(1) == 0)
    def _build_tables():
        lam_re = pw_ref[1:2, :]
        lam_im = pw_ref[N_LAM_POWERS + 1:N_LAM_POWERS + 2, :]
        q_re = ct_ref[:, 0:sl]
        q_im = ct_ref[:, sl:2 * sl]
        ccs_t = jnp.concatenate([_expand_groups(q_re), -_expand_groups(q_im)], axis=1).astype(BF16)
        kt_scr[...] = jnp.zeros(kt_scr.shape, BF16)
        g_re = bb_ref[:, 0:sl]
        g_im = bb_ref[:, sl:2 * sl]
        for m in range(L):
            j = L - 1 - m
            g_m = jnp.concatenate([_expand_groups(g_re), _expand_groups(g_im)], axis=1).astype(BF16)
            g_scr[j * LANES:(j + 1) * LANES, :] = g_m
            k_m = lax.dot_general(g_m, ccs_t, _NT, preferred_element_type=F32).astype(BF16)
            for jj in range(L - m):
                i = jj + m
                kt_scr[jj * LANES:(jj + 1) * LANES, i * LANES:(i + 1) * LANES] = k_m
            g_re, g_im = g_re * lam_re - g_im * lam_im, g_re * lam_im + g_im * lam_re
            q_re, q_im = q_re * lam_re - q_im * lam_im, q_re * lam_im + q_im * lam_re
            clt_scr[m * LANES:(m + 1) * LANES, 0:sl] = _expand_groups(q_re).astype(BF16)
            clt_scr[m * LANES:(m + 1) * LANES, sl:2 * sl] = _expand_groups(-q_im).astype(BF16)

    qs = GATHER_STRIDE
    for b in range(qs):
        s1_scr[b] = u_ref[pl.ds(b, seq // qs, stride=qs), :]
    for j in range(L):
        a, b = divmod(j, qs)
        ucat_scr[:, j * LANES:(j + 1) * LANES] = s1_scr[b, pl.ds(a, nch, stride=qs), :].astype(BF16)
    e_scr[...] = jnp.dot(ucat_scr[...], g_scr[...], preferred_element_type=F32)

    lc_re = pw_ref[L:L + 1, :]
    lc_im = pw_ref[N_LAM_POWERS + L:N_LAM_POWERS + L + 1, :]

    pair = 2 * LANES
    for p in range(L // 2):
        y1_scr[:, p * pair:(p + 1) * pair] = jnp.dot(
            ucat_scr[:, 0:(p + 1) * pair], kt_scr[0:(p + 1) * pair, p * pair:(p + 1) * pair],
            preferred_element_type=F32)

    h_re, h_im = h0_ref[:, 0:sl], h0_ref[:, sl:2 * sl]
    for c in range(nch):
        hp_scr[c:c + 1, 0:sl] = h_re
        hp_scr[c:c + 1, sl:2 * sl] = h_im
        e_re = e_scr[c:c + 1, 0:sl]
        e_im = e_scr[c:c + 1, sl:2 * sl]
        h_re, h_im = (lc_re * h_re - lc_im * h_im + e_re, lc_re * h_im + lc_im * h_re + e_im)
    fs_ref[:, 0:sl] = h_re
    fs_ref[:, sl:2 * sl] = h_im

    h_prev = hp_scr[...].astype(BF16)
    for p in range(L // 2):
        y2 = y1_scr[:, p * pair:(p + 1) * pair] + lax.dot_general(
            h_prev, clt_scr[p * pair:(p + 1) * pair, :], _NT, preferred_element_type=F32)
        for half in range(2):
            a, b = divmod(2 * p + half, qs)
            s2_scr[b, pl.ds(a, nch, stride=qs), :] = y2[:, half * LANES:(half + 1) * LANES]
    d = d_ref[...]
    for b in range(qs):
        y_ref[pl.ds(b, seq // qs, stride=qs), :] = s2_scr[b] + d * s1_scr[b]


def _ssm_scan_seq(u, tabs, h0, *, nseq, seq):
    nblk = u.shape[0]
    L = SSM_CHUNK
    nch = seq // L
    sl2 = 2 * STATE_LANES
    blk = lambda arr: pl.BlockSpec((None,) + arr.shape[1:], lambda k, b: (k,) + (0,) * (arr.ndim - 1))
    row_spec = pl.BlockSpec((None, seq, LANES), lambda k, b: (k, b, 0))
    st_spec = pl.BlockSpec((None, None, 1, sl2), lambda k, b: (k, b, 0, 0))
    return pl.pallas_call(
        _ssm_seq_kernel,
        grid=(nblk, nseq),
        in_specs=[row_spec, blk(tabs["bb"]), blk(tabs["ct"]), blk(tabs["pw_rows"]),
                  blk(tabs["dvec"]), st_spec],
        out_specs=[row_spec, st_spec],
        out_shape=[jax.ShapeDtypeStruct(u.shape, F32), jax.ShapeDtypeStruct(h0.shape, F32)],
        scratch_shapes=[
            pltpu.VMEM((L * LANES, sl2), BF16),
            pltpu.VMEM((L * LANES, L * LANES), BF16),
            pltpu.VMEM((L * LANES, sl2), BF16),
            pltpu.VMEM((nch, L * LANES), BF16),
            pltpu.VMEM((nch, sl2), F32),
            pltpu.VMEM((nch, sl2), F32),
            pltpu.VMEM((GATHER_STRIDE, seq // GATHER_STRIDE, LANES), F32),
            pltpu.VMEM((GATHER_STRIDE, seq // GATHER_STRIDE, LANES), F32),
            pltpu.VMEM((nch, L * LANES), F32),
        ],
        compiler_params=_params(("arbitrary", "arbitrary")),
        name="ssm_scan_seq",
    )(u, tabs["bb"], tabs["ct"], tabs["pw_rows"], tabs["dvec"], h0)


def _lane_blocks(x):
    lead = x.shape[:-2]
    x = x.reshape(lead + (SSM_LANE_BLOCKS, STATE_LANES))
    return jnp.moveaxis(x, -2, 0)


def _ssm_tables(P, j):
    b_re_t = jnp.transpose(P["ssm_b_re"][j], (2, 0, 1))
    b_im_t = jnp.transpose(P["ssm_b_im"][j], (2, 0, 1))
    p_re, p_im, bb_re, bb_im = _ssm_prep(P["ssm_a_re"][j], P["ssm_a_im"][j], P["ssm_log_dt"][j], b_re_t, b_im_t)
    bb = jnp.concatenate([_lane_blocks(bb_re), _lane_blocks(bb_im)], axis=2)
    ct = jnp.concatenate([_lane_blocks(jnp.swapaxes(P["ssm_c_re"][j], 0, 1)),
                          _lane_blocks(jnp.swapaxes(P["ssm_c_im"][j], 0, 1))], axis=2)
    pw_rows = jnp.concatenate([_lane_blocks(p_re), _lane_blocks(p_im)], axis=1)
    dvec = P["ssm_d"][j].reshape(SSM_LANE_BLOCKS, 1, LANES)
    return dict(bb=bb, ct=ct, pw_rows=pw_rows, dvec=dvec)


def _ssm_scan(u, tabs, state, *, nseq, seq):
    if state is None:
        h0 = jnp.zeros((SSM_LANE_BLOCKS, nseq, 1, 2 * STATE_LANES), F32)
        y, fs = _ssm_scan_seq(u, tabs, h0, nseq=nseq, seq=seq)
        fs = fs.reshape(SSM_LANE_BLOCKS, nseq, 2 * STATE_LANES)
    else:
        y, fs = _ssm_scan_rows(u, tabs, _state_to_blocks(state), t_len=seq, nseq=nseq)
    return y, _state_from_blocks(fs)


def _state_from_blocks(fs):
    nblk, nseq, _ = fs.shape
    fs = fs.reshape(nblk, nseq, 2, GROUPS_PER_BLOCK, SSM_STATE)
    fs = jnp.transpose(fs, (1, 0, 3, 4, 2))
    return fs.reshape(nseq, SSM_GROUPS, SSM_STATE, 2)


def _state_to_blocks(state):
    nseq = state.shape[0]
    s = state.reshape(nseq, SSM_LANE_BLOCKS, GROUPS_PER_BLOCK, SSM_STATE, 2)
    return jnp.transpose(s, (1, 0, 4, 2, 3)).reshape(SSM_LANE_BLOCKS, nseq, 2 * STATE_LANES)


def _to_window_minor(c):
    return jnp.transpose(c[0], (0, 2, 3, 4, 1)).reshape(c.shape[1], 2 * GROUP_WIDTH, c.shape[2])


def _from_window_minor(c):
    c = c.reshape(c.shape[0], 2, HEADS_PER_GROUP, HEAD_DIM, c.shape[2])
    return jnp.transpose(c, (0, 4, 1, 2, 3))[None]


SIDE_CAST_SLICES = dict(qkv=32, o=8, ssm_w_in=32, ssm_w_glu=32, ssm_w_out=32)


def _trunks(xs, caches, ssm_state, W, *, nseqs, seqs):
    ffn_w = None

    small = ("qkv", "o", "ssm_w_in", "ssm_w_glu", "ssm_w_out")
    st = _stage(xs, W, ffn_w, 0, 0, cast_next=(0, 1),
                side_casts=[(W[name + "_f32"], SIDE_CAST_SLICES[name]) for name in small])
    W = dict(W, **{name: w[None] for name, w in zip(small, st["side"])})
    hs = st["x"]
    qkvs, merged_s, new_caches, tails = _qkv_and_sample_attn(
        hs, W["norms"], 0, W["qkv"], caches, t_len=seqs[1], nseq0=nseqs[0], seq0=seqs[0])
    merged_p = _attn_prompt(qkvs[0], seq=seqs[0])
    st = _stage(hs, W, st["next_w"], 0, 1, pre="attn", pre_ins=(merged_p, merged_s), cast_next=(1, 0))

    st = _stage(st["x"], W, st["next_w"], 1, 0, post="proj_slabs", cast_next=(1, 1))
    us = st["proj"]
    y_p, state_p = _ssm_scan(us[0], W["ssm"], None, nseq=nseqs[0], seq=seqs[0])
    y_s, state_s = _ssm_scan(us[1], W["ssm"], ssm_state, nseq=nseqs[1], seq=seqs[1])
    outs = _stage(st["x"], W, st["next_w"], 1, 1, pre="ssm", pre_ins=(y_p, y_s), post="final")["x"]
    return outs, tails, new_caches, (state_p, state_s)


def kernel(x_prompt, x_sample, cache_kv_g0, cache_kv_g1, cache_kv_g2, state_ssm, norm_g, final_norm_g,
           ffn_w_gate, ffn_w_up, ffn_w_down, attn_w_qkv, attn_w_o, ssm_w_in, ssm_a_re, ssm_a_im, ssm_log_dt,
           ssm_b_re, ssm_b_im, ssm_c_re, ssm_c_im, ssm_d, ssm_w_glu, ssm_b_glu, ssm_w_out):
    batch, seq, _ = x_prompt.shape
    dec_batch, dec_seq, _ = x_sample.shape
    P = dict(ssm_a_re=ssm_a_re, ssm_a_im=ssm_a_im, ssm_log_dt=ssm_log_dt, ssm_b_re=ssm_b_re,
             ssm_b_im=ssm_b_im, ssm_c_re=ssm_c_re, ssm_c_im=ssm_c_im, ssm_d=ssm_d)
    W = dict(
        norms=norm_g.reshape(norm_g.shape[0], norm_g.shape[1], 1, D_MODEL),
        final_norm=final_norm_g.reshape(1, D_MODEL),
        gate_f32=ffn_w_gate, up_f32=ffn_w_up, down_f32=ffn_w_down,
        qkv_f32=attn_w_qkv[0],
        o_f32=jnp.swapaxes(attn_w_o[0].reshape(N_GROUPS, GROUP_WIDTH // LANES, LANES, D_MODEL), 0, 1)
        .reshape(ATTN_WIDTH, D_MODEL),
        ssm_w_in_f32=ssm_w_in[0], ssm_w_glu_f32=ssm_w_glu[0], ssm_w_out_f32=ssm_w_out[0],
        ssm_b_glu=ssm_b_glu.reshape(ssm_b_glu.shape[0], 1, D_MODEL),
        ssm=_ssm_tables(P, 0),
    )

    xs = (x_prompt.reshape(batch * seq, D_MODEL), x_sample.reshape(dec_batch * dec_seq, D_MODEL))
    caches = [_to_window_minor(c) for c in (cache_kv_g0, cache_kv_g1, cache_kv_g2)]
    (y_p, y_s), tails, new_caches, (ssm_p, ssm_s) = _trunks(
        xs, caches, state_ssm[0], W, nseqs=(batch, dec_batch), seqs=(seq, dec_seq))
    kv_prompt = [_from_window_minor(t) for t in tails]
    kv_sample = [_from_window_minor(nc) for nc in new_caches]

    return (y_p.reshape(batch, seq, D_MODEL), y_s.reshape(dec_batch, dec_seq, D_MODEL),
            kv_prompt[0], kv_prompt[1], kv_prompt[2], ssm_p[None],
            kv_sample[0], kv_sample[1], kv_sample[2], ssm_s[None])
```

```python
import functools

import jax
import jax.numpy as jnp
from jax import lax
from jax.experimental import pallas as pl
from jax.experimental.pallas import tpu as pltpu

F32 = jnp.float32
BF16 = jnp.bfloat16

D_MODEL = 1024
D_FF = 2816
HEAD_DIM = 64
HEADS_PER_GROUP = 4
DILATED_GROUPS = ((128, 1), (512, 4), (2048, 16))
N_GROUPS = len(DILATED_GROUPS)
N_HEADS = N_GROUPS * HEADS_PER_GROUP
GROUP_WIDTH = HEADS_PER_GROUP * HEAD_DIM
ATTN_WIDTH = N_HEADS * HEAD_DIM
BAND = 128
SSM_CH = 16
SSM_GROUPS = D_MODEL // SSM_CH
SSM_STATE = 64
NORM_EPS = 1e-6

LANES = 128
V7X_VMEM_BYTES = 64 * 1024 * 1024
VMEM_LIMIT = V7X_VMEM_BYTES - 8 * 1024 * 1024

SSM_LANE_BLOCKS = D_MODEL // LANES
GROUPS_PER_BLOCK = LANES // SSM_CH
STATE_LANES = GROUPS_PER_BLOCK * SSM_STATE

assert all(w // d == BAND for w, d in DILATED_GROUPS)
assert all(d & (d - 1) == 0 for _, d in DILATED_GROUPS)


def _params(sem, vmem=VMEM_LIMIT):
    return pltpu.CompilerParams(dimension_semantics=sem, vmem_limit_bytes=vmem)


def _const_spec(shape):
    nd = len(shape)
    return pl.BlockSpec(shape, lambda *_: (0,) * nd, pipeline_mode=pl.Buffered(1))


def _rms(x, g):
    return x * lax.rsqrt(jnp.mean(x * x, axis=-1, keepdims=True) + NORM_EPS) * g


MXU_TILE = 256
FFN_CHUNK_BOUNDS = (0, 6 * MXU_TILE, D_FF)


N_STREAMS = 2
STAGE_TM = 512


FFN_WEIGHTS = ("gate", "up", "down")


WEIGHT_STREAM_CHUNKS = 8
WEIGHT_STREAM_SLOTS = 4


def _stream_weights_at_step0(layer_half, srcs, dsts, stagings, sem):
    layer, half = layer_half
    jobs = []
    for src, dst, stg in zip(srcs, dsts, stagings):
        rows = dst.shape[0] // WEIGHT_STREAM_CHUNKS
        for c in range(WEIGHT_STREAM_CHUNKS):
            jobs.append((src.at[layer, half, pl.ds(c * rows, rows), :], dst, stg, c * rows, rows))

    ahead = WEIGHT_STREAM_SLOTS - 1

    def copy(k):
        src, _, stg, _, _ = jobs[k]
        return pltpu.make_async_copy(src, stg.at[k % WEIGHT_STREAM_SLOTS], sem.at[k % WEIGHT_STREAM_SLOTS])

    @pl.when(pl.program_id(0) == 0)
    def _():
        for k in range(ahead):
            copy(k).start()
        for k, (_, dst, stg, r0, rows) in enumerate(jobs):
            if k + ahead < len(jobs):
                copy(k + ahead).start()
            copy(k).wait()
            dst[r0:r0 + rows, :] = stg[k % WEIGHT_STREAM_SLOTS].astype(BF16)


def _stage_kernel(*refs, pre, post, n_casts, stream_f32=None):
    refs = list(refs)
    take = lambda n: [refs.pop(0) for _ in range(n)]
    x_refs = take(N_STREAMS)
    if pre == "attn":
        om_refs = take(N_STREAMS)
        (wo_ref,) = take(1)
    elif pre == "ssm":
        y_refs = take(N_STREAMS)
        wglu_ref, bglu_ref, wout_ref = take(3)
    g_ref, wg_ref, wu_ref, wd_ref = take(4)
    if stream_f32 is not None:
        sem = refs.pop()
        stg_down, stg_gate_up = refs.pop(), refs.pop()
        wd_scr, wu_scr, wg_scr = refs.pop(), refs.pop(), refs.pop()
        _stream_weights_at_step0(stream_f32, (wg_ref, wu_ref, wd_ref), (wg_scr, wu_scr, wd_scr),
                                 (stg_gate_up, stg_gate_up, stg_down), sem)
        wg_ref, wu_ref, wd_ref = wg_scr, wu_scr, wd_scr
    if post == "final":
        (gf_ref,) = take(1)
    elif post == "proj_slabs":
        gp_ref, wp_ref = take(2)
    cast_srcs = take(n_casts)
    o_refs = take(N_STREAMS)
    if post == "proj_slabs":
        p_refs = take(N_STREAMS)
    for src, dst in zip(cast_srcs, take(n_casts)):
        dst[...] = src[...].astype(BF16)

    def run(s):
        x = x_refs[s][...]
        if pre == "attn":
            x = x + jnp.dot(om_refs[s][...].astype(BF16), wo_ref[...], preferred_element_type=F32)
        elif pre == "ssm":
            y = jnp.concatenate([y_refs[s][k] for k in range(y_refs[s].shape[0])], axis=1)
            g = jax.nn.gelu(y, approximate=True)
            gate = jnp.dot(g.astype(BF16), wglu_ref[...], preferred_element_type=F32) + bglu_ref[...]
            z = (g * jax.nn.sigmoid(gate)).astype(BF16)
            x = x + jnp.dot(z, wout_ref[...], preferred_element_type=F32)

        xb = _rms(x, g_ref[...]).astype(BF16)
        acc = None
        for lo, hi in zip(FFN_CHUNK_BOUNDS[:-1], FFN_CHUNK_BOUNDS[1:]):
            gate = jnp.dot(xb, wg_ref[:, lo:hi], preferred_element_type=F32)
            up = jnp.dot(xb, wu_ref[:, lo:hi], preferred_element_type=F32)
            hmid = (gate * jax.nn.sigmoid(gate) * up).astype(BF16)
            part = jnp.dot(hmid, wd_ref[lo:hi, :], preferred_element_type=F32)
            acc = part if acc is None else acc + part
        out = x + 0.5 * acc

        if post == "final":
            out = _rms(out, gf_ref[...])
        o_refs[s][...] = out
        if post == "proj_slabs":
            proj = jnp.dot(_rms(out, gp_ref[...]).astype(BF16), wp_ref[...], preferred_element_type=F32)
            for k in range(p_refs[s].shape[0]):
                p_refs[s][k] = proj[:, k * LANES:(k + 1) * LANES]

    run(0)
    pl.when(pl.program_id(0) == 0)(lambda: run(1))


def _pick_spec(arr, *idx):
    tail = arr.shape[len(idx):]
    index = tuple(idx) + (0,) * len(tail)
    return pl.BlockSpec((None,) * len(idx) + tail, lambda *_: index, pipeline_mode=pl.Buffered(1))


def _whole_spec(shape):
    nd = len(shape)
    return pl.BlockSpec(shape, lambda *_: (0,) * nd)


def _row_specs(xs):
    return [pl.BlockSpec((STAGE_TM, xs[0].shape[1]), lambda i: (i, 0)), _whole_spec(xs[1].shape)]


def _slab_specs(ys):
    return [pl.BlockSpec((SSM_LANE_BLOCKS, STAGE_TM, LANES), lambda i: (0, i, 0)), _whole_spec(ys[1].shape)]


def _stage(xs, W, ffn_w, layer, half, *, pre=None, pre_ins=None, post=None, cast_next=None, side_casts=()):
    steps = xs[0].shape[0] // STAGE_TM
    ins, specs = list(xs), _row_specs(xs)
    if pre == "attn":
        ins += list(pre_ins) + [W["o"]]
        specs += _row_specs(pre_ins) + [_pick_spec(W["o"], 0)]
    elif pre == "ssm":
        ins += list(pre_ins) + [W["ssm_w_glu"], W["ssm_b_glu"], W["ssm_w_out"]]
        specs += _slab_specs(pre_ins) + [_pick_spec(W["ssm_w_glu"], 0), _pick_spec(W["ssm_b_glu"], 0),
                                         _pick_spec(W["ssm_w_out"], 0)]
    scratch = []
    if ffn_w is None:
        stream_f32 = (layer, half)
        ins += [W["norms"]] + [W[name + "_f32"] for name in FFN_WEIGHTS]
        specs += [_pick_spec(W["norms"], layer, 2 * half)] + [pl.BlockSpec(memory_space=pl.ANY)] * len(FFN_WEIGHTS)
        scratch = [pltpu.VMEM((D_MODEL, D_FF), BF16), pltpu.VMEM((D_MODEL, D_FF), BF16),
                   pltpu.VMEM((D_FF, D_MODEL), BF16),
                   pltpu.VMEM((WEIGHT_STREAM_SLOTS, D_MODEL // WEIGHT_STREAM_CHUNKS, D_FF), F32),
                   pltpu.VMEM((WEIGHT_STREAM_SLOTS, D_FF // WEIGHT_STREAM_CHUNKS, D_MODEL), F32),
                   pltpu.SemaphoreType.DMA((WEIGHT_STREAM_SLOTS,))]
    else:
        stream_f32 = None
        ins += [W["norms"]] + list(ffn_w)
        specs += [_pick_spec(W["norms"], layer, 2 * half)] + [_pick_spec(w) for w in ffn_w]
    out_specs = _row_specs(xs)
    out_shape = [jax.ShapeDtypeStruct(x.shape, F32) for x in xs]
    if post == "final":
        ins += [W["final_norm"]]
        specs += [_const_spec((1, D_MODEL))]
    elif post == "proj_slabs":
        ins += [W["norms"], W["ssm_w_in"]]
        specs += [_pick_spec(W["norms"], layer, 1), _pick_spec(W["ssm_w_in"], 0)]
    cast_out_specs, cast_out_shape = [], []
    if cast_next:
        nl, nh = cast_next
        rows_per_step = D_MODEL // steps
        steps_per_col = steps // (D_MODEL // LANES)
        assert rows_per_step * steps == D_MODEL and steps_per_col * (D_MODEL // LANES) == steps
        ins += [W[name + "_f32"] for name in FFN_WEIGHTS]
        specs += [pl.BlockSpec((None, None, rows_per_step, D_FF), lambda i: (nl, nh, i, 0))] * 2
        specs += [pl.BlockSpec((None, None, D_FF, LANES), lambda i: (nl, nh, 0, i // steps_per_col))]
        cast_out_specs += [pl.BlockSpec((rows_per_step, D_FF), lambda i: (i, 0))] * 2
        cast_out_specs += [pl.BlockSpec((D_FF, LANES), lambda i: (0, i // steps_per_col))]
        cast_out_shape += [jax.ShapeDtypeStruct(s, BF16)
                           for s in ((D_MODEL, D_FF), (D_MODEL, D_FF), (D_FF, D_MODEL))]
    for w, nslices in side_casts:
        rows, cols = w.shape
        assert rows % nslices == 0 and steps % nslices == 0
        spec = pl.BlockSpec((rows // nslices, cols), lambda i, hold=steps // nslices: (i // hold, 0))
        ins.append(w)
        specs.append(spec)
        cast_out_specs.append(spec)
        cast_out_shape.append(jax.ShapeDtypeStruct(w.shape, BF16))
    if post == "proj_slabs":
        slabs = [jax.ShapeDtypeStruct((SSM_LANE_BLOCKS, x.shape[0], LANES), F32) for x in xs]
        out_specs += _slab_specs(slabs)
        out_shape += slabs
    out_specs += cast_out_specs
    out_shape += cast_out_shape
    outs = list(pl.pallas_call(
        functools.partial(_stage_kernel, pre=pre, post=post, n_casts=len(cast_out_specs), stream_f32=stream_f32),
        grid=(steps,),
        in_specs=specs,
        out_specs=out_specs,
        out_shape=out_shape,
        scratch_shapes=scratch,
        compiler_params=_params(("arbitrary",)),
        name="stage_" + "_".join(s for s in (pre, "ffn", post) if s),
    )(*ins))
    take = lambda n: [outs.pop(0) for _ in range(n)]
    return dict(x=take(N_STREAMS), proj=take(N_STREAMS) if post == "proj_slabs" else None,
                next_w=take(len(FFN_WEIGHTS)) if cast_next else None, side=take(len(side_casts)))


def _tail_tiles(group, seq):
    wb = min(DILATED_GROUPS[group][0], seq)
    return max(wb // STAGE_TM, 1), min(wb, STAGE_TM)


def _qkv_kernel(x0_ref, x1_ref, g_ref, w_ref, c0_ref, c1_ref, c2_ref,
                o0_ref, o1_ref, om_ref, n0_ref, n1_ref, n2_ref, t0_ref, t1_ref, t2_ref, *, t_len, seq):
    def run(x_ref, o_ref):
        xb = _rms(x_ref[...], g_ref[...]).astype(BF16)
        o_ref[...] = jnp.dot(xb, w_ref[...], preferred_element_type=F32)

    i = pl.program_id(0)
    run(x0_ref, o0_ref)
    tiles_per_seq = seq // STAGE_TM
    for g, tail_ref in enumerate((t0_ref, t1_ref, t2_ref)):
        ntail, nrows = _tail_tiles(g, seq)

        @pl.when(i % tiles_per_seq >= tiles_per_seq - ntail)
        def _(g=g, tail_ref=tail_ref, nrows=nrows):
            k0 = ATTN_WIDTH + g * GROUP_WIDTH
            v0 = 2 * ATTN_WIDTH + g * GROUP_WIDTH
            tail_ref[0:GROUP_WIDTH, :] = o0_ref[STAGE_TM - nrows:STAGE_TM, k0:k0 + GROUP_WIDTH].T
            tail_ref[GROUP_WIDTH:2 * GROUP_WIDTH, :] = o0_ref[STAGE_TM - nrows:STAGE_TM, v0:v0 + GROUP_WIDTH].T

    pl.when(i == 0)(lambda: run(x1_ref, o1_ref))
    qkv_seq = o1_ref.at[pl.ds(pl.multiple_of(i * t_len, t_len), t_len), :]
    _attn_sample_kernel(qkv_seq, c0_ref, c1_ref, c2_ref, om_ref, n0_ref, n1_ref, n2_ref, t_len=t_len)


def _qkv_and_sample_attn(xs, norms, layer, w, caches, *, t_len, nseq0, seq0):
    n = w.shape[-1]
    steps = xs[0].shape[0] // STAGE_TM
    nseq = caches[0].shape[0]
    assert steps == nseq and xs[1].shape[0] == nseq * t_len
    tiles_per_seq = seq0 // STAGE_TM
    qkv_shapes = [jax.ShapeDtypeStruct((x.shape[0], n), F32) for x in xs]
    cache_specs = [pl.BlockSpec((None,) + c.shape[1:], lambda b: (b, 0, 0)) for c in caches]
    tail_specs, tail_shapes = [], []
    for g in range(N_GROUPS):
        ntail, nrows = _tail_tiles(g, seq0)
        tail_specs.append(pl.BlockSpec(
            (None, 2 * GROUP_WIDTH, nrows),
            lambda i, first=tiles_per_seq - ntail: (i // tiles_per_seq, 0,
                                                    jnp.maximum(i % tiles_per_seq - first, 0))))
        tail_shapes.append(jax.ShapeDtypeStruct((nseq0, 2 * GROUP_WIDTH, ntail * nrows), F32))
    outs = pl.pallas_call(
        functools.partial(_qkv_kernel, t_len=t_len, seq=seq0),
        grid=(steps,),
        in_specs=_row_specs(xs) + [_pick_spec(norms, layer, 1), _pick_spec(w, 0)] + cache_specs,
        out_specs=(_row_specs(qkv_shapes) + [pl.BlockSpec((t_len, ATTN_WIDTH), lambda b: (b, 0))]
                   + cache_specs + tail_specs),
        out_shape=qkv_shapes
        + [jax.ShapeDtypeStruct((xs[1].shape[0], ATTN_WIDTH), F32)]
        + [jax.ShapeDtypeStruct(c.shape, F32) for c in caches] + tail_shapes,
        compiler_params=_params(("arbitrary",)),
        name="qkv_proj_sample_attn",
    )(*xs, norms, w, *caches)
    outs = list(outs)
    return (outs[:N_STREAMS], outs[N_STREAMS], outs[N_STREAMS + 1:N_STREAMS + 1 + N_GROUPS],
            outs[N_STREAMS + 1 + N_GROUPS:])


def _alibi_slope(head):
    return 2.0 ** (-8.0 * (head + 1) / N_HEADS)


def _head_lane_mask(hh):
    lane = lax.broadcasted_iota(jnp.int32, (1, LANES), 1)
    return (lane >= hh * HEAD_DIM) & (lane < (hh + 1) * HEAD_DIM)


ATTN_TILE_TOKENS = 2048
_NT = (((1,), (1,)), ((), ()))


def _merge_groups(outs, lses):
    m = jnp.maximum(jnp.maximum(lses[0], lses[1]), lses[2])
    es = [jnp.exp(l - m) for l in lses]
    tot = es[0] + es[1] + es[2]
    return [o * (e / tot) for o, e in zip(outs, es)]


def _attn_kernel(*refs, tiles_per_seq):
    om_ref, o_scr, l_scr = refs[5 * N_GROUPS:]
    has_prev = (pl.program_id(0) % tiles_per_seq) != 0
    for g in range(N_GROUPS):
        _attn_group(*refs[5 * g:5 * g + 5], o_scr.at[g], l_scr.at[g], group=g, has_prev=has_prev)
    merged = _merge_groups([o_scr[g] for g in range(N_GROUPS)], [l_scr[g] for g in range(N_GROUPS)])
    for g in range(N_GROUPS):
        om_ref[:, g * LANES:(g + 1) * LANES] = merged[g].astype(BF16)


def _attn_group(q_ref, kp_ref, kc_ref, vp_ref, vc_ref, o_ref, l_ref, *, group, has_prev):
    dil = DILATED_GROUPS[group][1]
    span = BAND * dil
    nblk = q_ref.shape[0] // span
    c = pl.program_id(1)

    row = lax.broadcasted_iota(jnp.int32, (2 * BAND, 2 * BAND), 0)
    kj = lax.broadcasted_iota(jnp.int32, (2 * BAND, 2 * BAND), 1)
    delta = BAND + (row & (BAND - 1)) - kj
    head0 = group * HEADS_PER_GROUP
    slope_a = jnp.where(c == 0, _alibi_slope(head0), _alibi_slope(head0 + 2)).astype(F32)
    slope_b = jnp.where(c == 0, _alibi_slope(head0 + 1), _alibi_slope(head0 + 3)).astype(F32)
    slope = jnp.where(row < BAND, slope_a, slope_b)
    band = (delta >= 0) & (delta <= BAND)
    bias = jnp.where(band, -slope * (delta * dil).astype(F32), -jnp.inf)
    bias_first = jnp.where((kj >= BAND) | has_prev, bias, -jnp.inf)
    first_head = _head_lane_mask(0)

    def rows(ref, start):
        return ref[pl.ds(start, BAND), :] if dil == 1 else ref[pl.ds(start, BAND, stride=dil), :]

    for r in range(dil):
        k_prev = rows(kp_ref, r).astype(BF16)
        v_prev = rows(vp_ref, r).astype(BF16)
        for b in range(nblk):
            start = b * span + r
            q = rows(q_ref, start) * (HEAD_DIM ** -0.5)
            k_cur = rows(kc_ref, start).astype(BF16)
            v_cur = rows(vc_ref, start).astype(BF16)
            k = jnp.concatenate([k_prev, k_cur], axis=0)
            v = jnp.concatenate([v_prev, v_cur], axis=0)
            q2 = jnp.concatenate([jnp.where(first_head, q, 0.0), jnp.where(first_head, 0.0, q)], axis=0)
            s = lax.dot_general(q2.astype(BF16), k, _NT, preferred_element_type=F32)
            s = s + (bias_first if b == 0 else bias)
            m = jnp.max(s, axis=-1, keepdims=True)
            p = jnp.exp(s - m)
            den = jnp.sum(p, axis=-1, keepdims=True)
            o2 = jnp.dot(p.astype(BF16), v, preferred_element_type=F32) / den
            l2 = m + jnp.log(den)
            o = jnp.where(first_head, o2[:BAND], o2[BAND:])
            lse = jnp.where(first_head, l2[:BAND], l2[BAND:])
            if dil == 1:
                o_ref[pl.ds(start, BAND), :] = o
                l_ref[pl.ds(start, BAND), :] = lse
            else:
                o_ref[pl.ds(start, BAND, stride=dil), :] = o
                l_ref[pl.ds(start, BAND, stride=dil), :] = lse
            k_prev, v_prev = k_cur, v_cur


MERGED_BLOCK = N_GROUPS * LANES


def _attn_prompt(qkv, *, seq):
    rows = qkv.shape[0]
    tt = ATTN_TILE_TOKENS
    specs = []
    for g, (_, dil) in enumerate(DILATED_GROUPS):
        span = BAND * dil
        nblk = tt // span
        cur = lambda col: pl.BlockSpec((tt, LANES), lambda i, c: (i, col + c))
        prev = lambda col, span=span, nblk=nblk: pl.BlockSpec(
            (span, LANES), lambda i, c: (jnp.maximum(i * nblk - 1, 0), col + c))
        qcol = g * GROUP_WIDTH // LANES
        kcol = (ATTN_WIDTH + g * GROUP_WIDTH) // LANES
        vcol = (2 * ATTN_WIDTH + g * GROUP_WIDTH) // LANES
        specs += [cur(qcol), prev(kcol), cur(kcol), prev(vcol), cur(vcol)]
    return pl.pallas_call(
        functools.partial(_attn_kernel, tiles_per_seq=seq // tt),
        grid=(rows // tt, GROUP_WIDTH // LANES),
        in_specs=specs,
        out_specs=pl.BlockSpec((tt, MERGED_BLOCK), lambda i, c: (i, c)),
        out_shape=jax.ShapeDtypeStruct((rows, ATTN_WIDTH), BF16),
        scratch_shapes=[pltpu.VMEM((N_GROUPS, tt, LANES), F32), pltpu.VMEM((N_GROUPS, tt, LANES), F32)],
        compiler_params=_params(("parallel", "parallel")),
        name="attn_prompt",
    )(*([qkv] * len(specs)))


def _attn_sample_kernel(qkv_ref, c0_ref, c1_ref, c2_ref, om_ref, n0_ref, n1_ref, n2_ref, *, t_len):
    cache_refs = (c0_ref, c1_ref, c2_ref)
    n_refs = (n0_ref, n1_ref, n2_ref)
    outs, lses = [], []
    gw = GROUP_WIDTH
    nrow = HEADS_PER_GROUP * t_len
    row_head = lax.broadcasted_iota(jnp.int32, (nrow, gw), 0) // t_len
    col_head = lax.broadcasted_iota(jnp.int32, (nrow, gw), 1) // HEAD_DIM
    out_col_head = lax.broadcasted_iota(jnp.int32, (t_len, gw), 1) // HEAD_DIM

    def gather_heads(x):
        acc = jnp.zeros((t_len, gw), F32)
        for h in range(HEADS_PER_GROUP):
            acc = acc + jnp.where(out_col_head == h, x[h * t_len:(h + 1) * t_len, :], 0.0)
        return acc

    for g, (win, dil) in enumerate(DILATED_GROUPS):
        cache_ref = cache_refs[g]
        wb = cache_ref.shape[1]
        q0 = g * gw
        k0 = ATTN_WIDTH + g * gw
        v0 = 2 * ATTN_WIDTH + g * gw

        kv_new = jnp.concatenate([qkv_ref[:, k0:k0 + gw], qkv_ref[:, v0:v0 + gw]], axis=1)
        kv_pad = jnp.concatenate([kv_new, jnp.zeros((LANES - t_len, 2 * gw), F32)], axis=0)
        new_t = kv_pad.T

        q = qkv_ref[:, q0:q0 + gw] * (HEAD_DIM ** -0.5)
        q_bd = jnp.where(row_head == col_head, jnp.concatenate([q] * HEADS_PER_GROUP, axis=0), 0.0).astype(BF16)

        def bias(ncol, base):
            r = lax.broadcasted_iota(jnp.int32, (nrow, ncol), 0)
            w = lax.broadcasted_iota(jnp.int32, (nrow, ncol), 1)
            dist = base + (r % t_len) - w
            ok = (dist >= 0) & (dist <= win) & ((dist & (dil - 1)) == 0)
            slope = jnp.zeros((nrow, ncol), F32)
            for h in range(HEADS_PER_GROUP):
                slope = jnp.where(r // t_len == h, _alibi_slope(g * HEADS_PER_GROUP + h), slope)
            return jnp.where(ok, -slope * dist.astype(F32), -jnp.inf)

        s1 = jnp.dot(q_bd, cache_ref[0:gw, :].astype(BF16), preferred_element_type=F32) + bias(wb, wb)
        pad_ok = lax.broadcasted_iota(jnp.int32, (nrow, LANES), 1) < t_len
        s2 = jnp.dot(q_bd, new_t[0:gw, :].astype(BF16), preferred_element_type=F32)
        s2 = jnp.where(pad_ok, s2 + bias(LANES, 0), -jnp.inf)
        m = jnp.maximum(jnp.max(s1, axis=-1, keepdims=True), jnp.max(s2, axis=-1, keepdims=True))
        p1 = jnp.exp(s1 - m)
        p2 = jnp.exp(s2 - m)
        den = jnp.sum(p1, axis=-1, keepdims=True) + jnp.sum(p2, axis=-1, keepdims=True)
        o = (lax.dot_general(p1.astype(BF16), cache_ref[gw:2 * gw, :].astype(BF16), _NT, preferred_element_type=F32)
             + lax.dot_general(p2.astype(BF16), new_t[gw:2 * gw, :].astype(BF16), _NT, preferred_element_type=F32))
        outs.append(gather_heads(o / den))
        lses.append(gather_heads(jnp.broadcast_to(m + jnp.log(den), (nrow, gw))))

        ext = jnp.concatenate([cache_ref[...], new_t], axis=1)
        n_refs[g][...] = pltpu.roll(ext, wb + LANES - t_len, axis=1)[:, 0:wb]

    merged = _merge_groups(outs, lses)
    for g in range(N_GROUPS):
        for c in range(gw // LANES):
            col = c * MERGED_BLOCK + g * LANES
            om_ref[:, col:col + LANES] = merged[g][:, c * LANES:(c + 1) * LANES]


SSM_CHUNK = 16
N_LAM_POWERS = SSM_CHUNK + 1


def _ssm_prep_kernel(a_re_ref, a_im_ref, logdt_ref, bre_ref, bim_ref, pre_ref, pim_ref, bbre_ref, bbim_ref):
    a_re = a_re_ref[...]
    a_im = a_im_ref[...]
    dt = jnp.exp(logdt_ref[...])
    mag = jnp.exp(a_re * dt)
    ang = a_im * dt
    lam_re = mag * jnp.cos(ang)
    lam_im = mag * jnp.sin(ang)
    den = a_re * a_re + a_im * a_im
    num_re = lam_re - 1.0
    coef_re = ((num_re * a_re + lam_im * a_im) / den)[None]
    coef_im = ((lam_im * a_re - num_re * a_im) / den)[None]
    b_re = bre_ref[...]
    b_im = bim_ref[...]
    bbre_ref[...] = coef_re * b_re - coef_im * b_im
    bbim_ref[...] = coef_re * b_im + coef_im * b_re
    p_re, p_im = jnp.ones_like(lam_re), jnp.zeros_like(lam_im)
    for m in range(N_LAM_POWERS):
        pre_ref[m] = p_re
        pim_ref[m] = p_im
        p_re, p_im = p_re * lam_re - p_im * lam_im, p_re * lam_im + p_im * lam_re


def _ssm_prep(a_re, a_im, log_dt, b_re_t, b_im_t):
    g, n = a_re.shape
    pw = jax.ShapeDtypeStruct((N_LAM_POWERS, g, n), F32)
    bb = jax.ShapeDtypeStruct(b_re_t.shape, F32)
    return pl.pallas_call(
        _ssm_prep_kernel,
        out_shape=[pw, pw, bb, bb],
        name="ssm_prep",
    )(a_re, a_im, log_dt.reshape(g, 1), b_re_t, b_im_t)


SCAN_BLOCKS_PER_STEP = 4


def _ssm_rows_kernel(*refs, t_len, nseq):
    for k in range(refs[0].shape[0]):
        _ssm_rows_block(*[r.at[k] for r in refs], t_len=t_len, nseq=nseq)


def _expand_groups(x):
    width = x.shape[1]
    tiled = jnp.concatenate([x] * GROUPS_PER_BLOCK, axis=0)
    row_grp = lax.broadcasted_iota(jnp.int32, tiled.shape, 0) // SSM_CH
    lane_grp = lax.broadcasted_iota(jnp.int32, tiled.shape, 1) // (width // GROUPS_PER_BLOCK)
    return jnp.where(row_grp == lane_grp, tiled, 0.0)


def _ssm_rows_block(u_ref, bb_ref, ct_ref, pw_ref, d_ref, h0_ref, y_ref, fs_ref,
                    bu_scr, xre_scr, xim_scr, ud_scr, *, t_len, nseq):
    sl = STATE_LANES
    for j in range(t_len):
        ud_scr[j * nseq:(j + 1) * nseq, :] = u_ref[pl.ds(j, nseq, stride=t_len), :].astype(BF16)
    bb = jnp.concatenate([_expand_groups(bb_ref[:, 0:sl]), _expand_groups(bb_ref[:, sl:2 * sl])], axis=1)
    bu_scr[...] = jnp.dot(ud_scr[...], bb.astype(BF16), preferred_element_type=F32)

    lam_re = jnp.broadcast_to(pw_ref[1:2, :], (nseq, sl))
    lam_im = jnp.broadcast_to(pw_ref[N_LAM_POWERS + 1:N_LAM_POWERS + 2, :], (nseq, sl))
    x_re = h0_ref[:, 0:sl]
    x_im = h0_ref[:, sl:2 * sl]
    for j in range(t_len):
        b_re = bu_scr[j * nseq:(j + 1) * nseq, 0:sl]
        b_im = bu_scr[j * nseq:(j + 1) * nseq, sl:2 * sl]
        x_re, x_im = (lam_re * x_re - lam_im * x_im + b_re, lam_re * x_im + lam_im * x_re + b_im)
        xre_scr[j * nseq:(j + 1) * nseq, :] = x_re.astype(BF16)
        xim_scr[j * nseq:(j + 1) * nseq, :] = x_im.astype(BF16)
    fs_ref[:, 0:sl] = x_re
    fs_ref[:, sl:2 * sl] = x_im

    ct_re = _expand_groups(ct_ref[:, 0:sl]).astype(BF16)
    ct_im = _expand_groups(ct_ref[:, sl:2 * sl]).astype(BF16)
    y = (lax.dot_general(xre_scr[...], ct_re, _NT, preferred_element_type=F32)
         - lax.dot_general(xim_scr[...], ct_im, _NT, preferred_element_type=F32))
    d = d_ref[...]
    for j in range(t_len):
        y_ref[pl.ds(j, nseq, stride=t_len), :] = (
            y[j * nseq:(j + 1) * nseq, :] + d * u_ref[pl.ds(j, nseq, stride=t_len), :])


def _ssm_scan_rows(u, tabs, h0, *, t_len, nseq):
    nblk, rows, _ = u.shape
    kb = SCAN_BLOCKS_PER_STEP
    sl2 = 2 * STATE_LANES
    blk = lambda arr: pl.BlockSpec((kb,) + arr.shape[1:], lambda k: (k,) + (0,) * (arr.ndim - 1))
    ins = (u, tabs["bb"], tabs["ct"], tabs["pw_rows"], tabs["dvec"], h0)
    return pl.pallas_call(
        functools.partial(_ssm_rows_kernel, t_len=t_len, nseq=nseq),
        grid=(nblk // kb,),
        in_specs=[blk(a) for a in ins],
        out_specs=[blk(u), blk(h0)],
        out_shape=[jax.ShapeDtypeStruct(u.shape, F32), jax.ShapeDtypeStruct(h0.shape, F32)],
        scratch_shapes=[
            pltpu.VMEM((kb, rows, sl2), F32),
            pltpu.VMEM((kb, rows, STATE_LANES), BF16),
            pltpu.VMEM((kb, rows, STATE_LANES), BF16),
            pltpu.VMEM((kb, rows, LANES), BF16),
        ],
        compiler_params=_params(("parallel",)),
        name="ssm_scan_rows",
    )(*ins)


GATHER_STRIDE = 4
assert SSM_CHUNK == GATHER_STRIDE * GATHER_STRIDE


def _ssm_seq_kernel(u_ref, bb_ref, ct_ref, pw_ref, d_ref, h0_ref, y_ref, fs_ref,
                    g_scr, kt_scr, clt_scr, ucat_scr, e_scr, hp_scr, s1_scr, s2_scr, y1_scr):
    L = SSM_CHUNK
    sl = STATE_LANES
    seq = u_ref.shape[0]
    nch = seq // L

    @pl.when(pl.program_id(1) == 0)
    def _build_tables():
        lam_re = pw_ref[1:2, :]
        lam_im = pw_ref[N_LAM_POWERS + 1:N_LAM_POWERS + 2, :]
        q_re = ct_ref[:, 0:sl]
        q_im = ct_ref[:, sl:2 * sl]
        ccs_t = jnp.concatenate([_expand_groups(q_re), -_expand_groups(q_im)], axis=1).astype(BF16)
        kt_scr[...] = jnp.zeros(kt_scr.shape, BF16)
        g_re = bb_ref[:, 0:sl]
        g_im = bb_ref[:, sl:2 * sl]
        for m in range(L):
            j = L - 1 - m
            g_m = jnp.concatenate([_expand_groups(g_re), _expand_groups(g_im)], axis=1).astype(BF16)
            g_scr[j * LANES:(j + 1) * LANES, :] = g_m
            k_m = lax.dot_general(g_m, ccs_t, _NT, preferred_element_type=F32).astype(BF16)
            for jj in range(L - m):
                i = jj + m
                kt_scr[jj * LANES:(jj + 1) * LANES, i * LANES:(i + 1) * LANES] = k_m
            g_re, g_im = g_re * lam_re - g_im * lam_im, g_re * lam_im + g_im * lam_re
            q_re, q_im = q_re * lam_re - q_im * lam_im, q_re * lam_im + q_im * lam_re
            clt_scr[m * LANES:(m + 1) * LANES, 0:sl] = _expand_groups(q_re).astype(BF16)
            clt_scr[m * LANES:(m + 1) * LANES, sl:2 * sl] = _expand_groups(-q_im).astype(BF16)

    qs = GATHER_STRIDE
    for b in range(qs):
        s1_scr[b] = u_ref[pl.ds(b, seq // qs, stride=qs), :]
    for j in range(L):
        a, b = divmod(j, qs)
        ucat_scr[:, j * LANES:(j + 1) * LANES] = s1_scr[b, pl.ds(a, nch, stride=qs), :].astype(BF16)
    e_scr[...] = jnp.dot(ucat_scr[...], g_scr[...], preferred_element_type=F32)

    lc_re = pw_ref[L:L + 1, :]
    lc_im = pw_ref[N_LAM_POWERS + L:N_LAM_POWERS + L + 1, :]

    pair = 2 * LANES
    for p in range(L // 2):
        y1_scr[:, p * pair:(p + 1) * pair] = jnp.dot(
            ucat_scr[:, 0:(p + 1) * pair], kt_scr[0:(p + 1) * pair, p * pair:(p + 1) * pair],
            preferred_element_type=F32)

    h_re, h_im = h0_ref[:, 0:sl], h0_ref[:, sl:2 * sl]
    for c in range(nch):
        hp_scr[c:c + 1, 0:sl] = h_re
        hp_scr[c:c + 1, sl:2 * sl] = h_im
        e_re = e_scr[c:c + 1, 0:sl]
        e_im = e_scr[c:c + 1, sl:2 * sl]
        h_re, h_im = (lc_re * h_re - lc_im * h_im + e_re, lc_re * h_im + lc_im * h_re + e_im)
    fs_ref[:, 0:sl] = h_re
    fs_ref[:, sl:2 * sl] = h_im

    h_prev = hp_scr[...].astype(BF16)
    for p in range(L // 2):
        y2 = y1_scr[:, p * pair:(p + 1) * pair] + lax.dot_general(
            h_prev, clt_scr[p * pair:(p + 1) * pair, :], _NT, preferred_element_type=F32)
        for half in range(2):
            a, b = divmod(2 * p + half, qs)
            s2_scr[b, pl.ds(a, nch, stride=qs), :] = y2[:, half * LANES:(half + 1) * LANES]
    d = d_ref[...]
    for b in range(qs):
        y_ref[pl.ds(b, seq // qs, stride=qs), :] = s2_scr[b] + d * s1_scr[b]


def _ssm_scan_seq(u, tabs, h0, *, nseq, seq):
    nblk = u.shape[0]
    L = SSM_CHUNK
    nch = seq // L
    sl2 = 2 * STATE_LANES
    blk = lambda arr: pl.BlockSpec((None,) + arr.shape[1:], lambda k, b: (k,) + (0,) * (arr.ndim - 1))
    row_spec = pl.BlockSpec((None, seq, LANES), lambda k, b: (k, b, 0))
    st_spec = pl.BlockSpec((None, None, 1, sl2), lambda k, b: (k, b, 0, 0))
    return pl.pallas_call(
        _ssm_seq_kernel,
        grid=(nblk, nseq),
        in_specs=[row_spec, blk(tabs["bb"]), blk(tabs["ct"]), blk(tabs["pw_rows"]),
                  blk(tabs["dvec"]), st_spec],
        out_specs=[row_spec, st_spec],
        out_shape=[jax.ShapeDtypeStruct(u.shape, F32), jax.ShapeDtypeStruct(h0.shape, F32)],
        scratch_shapes=[
            pltpu.VMEM((L * LANES, sl2), BF16),
            pltpu.VMEM((L * LANES, L * LANES), BF16),
            pltpu.VMEM((L * LANES, sl2), BF16),
            pltpu.VMEM((nch, L * LANES), BF16),
            pltpu.VMEM((nch, sl2), F32),
            pltpu.VMEM((nch, sl2), F32),
            pltpu.VMEM((GATHER_STRIDE, seq // GATHER_STRIDE, LANES), F32),
            pltpu.VMEM((GATHER_STRIDE, seq // GATHER_STRIDE, LANES), F32),
            pltpu.VMEM((nch, L * LANES), F32),
        ],
        compiler_params=_params(("arbitrary", "arbitrary")),
        name="ssm_scan_seq",
    )(u, tabs["bb"], tabs["ct"], tabs["pw_rows"], tabs["dvec"], h0)


def _lane_blocks(x):
    lead = x.shape[:-2]
    x = x.reshape(lead + (SSM_LANE_BLOCKS, STATE_LANES))
    return jnp.moveaxis(x, -2, 0)


def _ssm_tables(P, j):
    b_re_t = jnp.transpose(P["ssm_b_re"][j], (2, 0, 1))
    b_im_t = jnp.transpose(P["ssm_b_im"][j], (2, 0, 1))
    p_re, p_im, bb_re, bb_im = _ssm_prep(P["ssm_a_re"][j], P["ssm_a_im"][j], P["ssm_log_dt"][j], b_re_t, b_im_t)
    bb = jnp.concatenate([_lane_blocks(bb_re), _lane_blocks(bb_im)], axis=2)
    ct = jnp.concatenate([_lane_blocks(jnp.swapaxes(P["ssm_c_re"][j], 0, 1)),
                          _lane_blocks(jnp.swapaxes(P["ssm_c_im"][j], 0, 1))], axis=2)
    pw_rows = jnp.concatenate([_lane_blocks(p_re), _lane_blocks(p_im)], axis=1)
    dvec = P["ssm_d"][j].reshape(SSM_LANE_BLOCKS, 1, LANES)
    return dict(bb=bb, ct=ct, pw_rows=pw_rows, dvec=dvec)


def _ssm_scan(u, tabs, state, *, nseq, seq):
    if state is None:
        h0 = jnp.zeros((SSM_LANE_BLOCKS, nseq, 1, 2 * STATE_LANES), F32)
        y, fs = _ssm_scan_seq(u, tabs, h0, nseq=nseq, seq=seq)
        fs = fs.reshape(SSM_LANE_BLOCKS, nseq, 2 * STATE_LANES)
    else:
        y, fs = _ssm_scan_rows(u, tabs, _state_to_blocks(state), t_len=seq, nseq=nseq)
    return y, _state_from_blocks(fs)


def _state_from_blocks(fs):
    nblk, nseq, _ = fs.shape
    fs = fs.reshape(nblk, nseq, 2, GROUPS_PER_BLOCK, SSM_STATE)
    fs = jnp.transpose(fs, (1, 0, 3, 4, 2))
    return fs.reshape(nseq, SSM_GROUPS, SSM_STATE, 2)


def _state_to_blocks(state):
    nseq = state.shape[0]
    s = state.reshape(nseq, SSM_LANE_BLOCKS, GROUPS_PER_BLOCK, SSM_STATE, 2)
    return jnp.transpose(s, (1, 0, 4, 2, 3)).reshape(SSM_LANE_BLOCKS, nseq, 2 * STATE_LANES)


def _to_window_minor(c):
    return jnp.transpose(c[0], (0, 2, 3, 4, 1)).reshape(c.shape[1], 2 * GROUP_WIDTH, c.shape[2])


def _from_window_minor(c):
    c = c.reshape(c.shape[0], 2, HEADS_PER_GROUP, HEAD_DIM, c.shape[2])
    return jnp.transpose(c, (0, 4, 1, 2, 3))[None]


SIDE_CAST_SLICES = dict(qkv=32, o=8, ssm_w_in=32, ssm_w_glu=32, ssm_w_out=32)


def _trunks(xs, caches, ssm_state, W, *, nseqs, seqs):
    ffn_w = None

    small = ("qkv", "o", "ssm_w_in", "ssm_w_glu", "ssm_w_out")
    st = _stage(xs, W, ffn_w, 0, 0, cast_next=(0, 1),
                side_casts=[(W[name + "_f32"], SIDE_CAST_SLICES[name]) for name in small])
    W = dict(W, **{name: w[None] for name, w in zip(small, st["side"])})
    hs = st["x"]
    qkvs, merged_s, new_caches, tails = _qkv_and_sample_attn(
        hs, W["norms"], 0, W["qkv"], caches, t_len=seqs[1], nseq0=nseqs[0], seq0=seqs[0])
    merged_p = _attn_prompt(qkvs[0], seq=seqs[0])
    st = _stage(hs, W, st["next_w"], 0, 1, pre="attn", pre_ins=(merged_p, merged_s), cast_next=(1, 0))

    st = _stage(st["x"], W, st["next_w"], 1, 0, post="proj_slabs", cast_next=(1, 1))
    us = st["proj"]
    y_p, state_p = _ssm_scan(us[0], W["ssm"], None, nseq=nseqs[0], seq=seqs[0])
    y_s, state_s = _ssm_scan(us[1], W["ssm"], ssm_state, nseq=nseqs[1], seq=seqs[1])
    outs = _stage(st["x"], W, st["next_w"], 1, 1, pre="ssm", pre_ins=(y_p, y_s), post="final")["x"]
    return outs, tails, new_caches, (state_p, state_s)


def kernel(x_prompt, x_sample, cache_kv_g0, cache_kv_g1, cache_kv_g2, state_ssm, norm_g, final_norm_g,
           ffn_w_gate, ffn_w_up, ffn_w_down, attn_w_qkv, attn_w_o, ssm_w_in, ssm_a_re, ssm_a_im, ssm_log_dt,
           ssm_b_re, ssm_b_im, ssm_c_re, ssm_c_im, ssm_d, ssm_w_glu, ssm_b_glu, ssm_w_out):
    batch, seq, _ = x_prompt.shape
    dec_batch, dec_seq, _ = x_sample.shape
    P = dict(ssm_a_re=ssm_a_re, ssm_a_im=ssm_a_im, ssm_log_dt=ssm_log_dt, ssm_b_re=ssm_b_re,
             ssm_b_im=ssm_b_im, ssm_c_re=ssm_c_re, ssm_c_im=ssm_c_im, ssm_d=ssm_d)
    W = dict(
        norms=norm_g.reshape(norm_g.shape[0], norm_g.shape[1], 1, D_MODEL),
        final_norm=final_norm_g.reshape(1, D_MODEL),
        gate_f32=ffn_w_gate, up_f32=ffn_w_up, down_f32=ffn_w_down,
        qkv_f32=attn_w_qkv[0],
        o_f32=jnp.swapaxes(attn_w_o[0].reshape(N_GROUPS, GROUP_WIDTH // LANES, LANES, D_MODEL), 0, 1)
        .reshape(ATTN_WIDTH, D_MODEL),
        ssm_w_in_f32=ssm_w_in[0], ssm_w_glu_f32=ssm_w_glu[0], ssm_w_out_f32=ssm_w_out[0],
        ssm_b_glu=ssm_b_glu.reshape(ssm_b_glu.shape[0], 1, D_MODEL),
        ssm=_ssm_tables(P, 0),
    )

    xs = (x_prompt.reshape(batch * seq, D_MODEL), x_sample.reshape(dec_batch * dec_seq, D_MODEL))
    caches = [_to_window_minor(c) for c in (cache_kv_g0, cache_kv_g1, cache_kv_g2)]
    (y_p, y_s), tails, new_caches, (ssm_p, ssm_s) = _trunks(
        xs, caches, state_ssm[0], W, nseqs=(batch, dec_batch), seqs=(seq, dec_seq))
    kv_prompt = [_from_window_minor(t) for t in tails]
    kv_sample = [_from_window_minor(nc) for nc in new_caches]

    return (y_p.reshape(batch, seq, D_MODEL), y_s.reshape(dec_batch, dec_seq, D_MODEL),
            kv_prompt[0], kv_prompt[1], kv_prompt[2], ssm_p[None],
            kv_sample[0], kv_sample[1], kv_sample[2], ssm_s[None])
```
